```python
import math
import jax, jax.numpy as jnp
from jax import lax
import numpy as np

D_MODEL = 2048
BATCH = 4
SEQ = 2048
DEPTH = 1
DEC_BATCH = 8
DEC_SEQ = 8
PAST_LEN = 16384
PAGE_SIZE = 128

N_HEADS = 16
HEAD_DIM = 64
N_KV = 4
GROUP = N_HEADS // N_KV
CMP_LEN = 32
CMP_STRIDE = 16
SEL_BLOCK = 64
SEL_TOP = 16
WINDOW = 512
Q_CHUNK = 64
WIN_QBLK = 128
SSM_WIDTH = D_MODEL // 2
SSM_GROUP = 16
SSM_NGROUPS = SSM_WIDTH // SSM_GROUP
SSM_STATE = 64
D_FF = ((-(-8 * D_MODEL // 3)) + 255) // 256 * 256

Q_COLS = N_HEADS * HEAD_DIM
KV_COLS = N_KV * HEAD_DIM
IN_COLS = Q_COLS + 6 * KV_COLS + 3 * N_HEADS + SSM_WIDTH + 2 * D_MODEL
EPS = 1e-6
NEG = -1e30
FORCE_SCORE = 1e4

kernel_name = 'nsa_s5_gated_hybrid_step'


def _rmsnorm(x, g):
    xf = x.astype(jnp.float32)
    y = xf * lax.rsqrt(jnp.mean(xf * xf, axis=-1, keepdims=True) + EPS)
    return (y * g.astype(jnp.float32)).astype(x.dtype)


def _alibi_slopes():
    return jnp.exp2(-8.0 * jnp.arange(1, N_HEADS + 1, dtype=jnp.float32) / N_HEADS).reshape(N_KV, GROUP)


def _in_proj(x, g_mix, w_in):
    B, T = x.shape[:2]
    z = _rmsnorm(x, g_mix) @ w_in
    cuts = np.cumsum([Q_COLS, 6 * KV_COLS, 3 * N_HEADS, SSM_WIDTH]).tolist()
    q, kv, gt, u, mg = jnp.split(z, cuts, axis=-1)
    return (q.reshape(B, T, N_HEADS, HEAD_DIM),
            kv.reshape(B, T, 6, N_KV, HEAD_DIM),
            jax.nn.sigmoid(gt.reshape(B, T, 3, N_HEADS)),
            u,
            mg.reshape(B, T, 2, D_MODEL))


def _compress(x, pe, pool, phi):
    B, Tp = x.shape[:2]
    r = CMP_LEN // CMP_STRIDE
    nsub = Tp // CMP_STRIDE
    nc = nsub - r + 1
    sub = x.reshape(B, nsub, CMP_STRIDE, N_KV, HEAD_DIM)
    parts = jnp.einsum('bnsgd,ms->bmngd', sub, pool.reshape(r, CMP_STRIDE))
    pooled = parts[:, 0, :nc]
    for m in range(1, r):
        pooled = pooled + parts[:, m, m:m + nc]
    pooled = pooled + jnp.einsum('l,ld->d', pool, pe)
    return jax.nn.gelu(pooled) @ phi


def _cmp_sel_attend(q, q_pos, kv4, pe, pool, phi, slopes):
    B, Tq = q.shape[:2]
    Tk = kv4.shape[1]
    Tp = -(-Tk // SEL_BLOCK) * SEL_BLOCK
    kv4 = jnp.pad(kv4, ((0, 0), (0, Tp - Tk), (0, 0), (0, 0), (0, 0)))
    k_cmp = _compress(kv4[:, :, 0], pe[0], pool[0], phi[0])
    v_cmp = _compress(kv4[:, :, 1], pe[1], pool[1], phi[1])
    nc = k_cmp.shape[1]
    cmp_end = jnp.arange(nc, dtype=jnp.int32) * CMP_STRIDE + (CMP_LEN - 1)
    ns = Tp // SEL_BLOCK
    sel_start = jnp.arange(ns, dtype=jnp.int32) * SEL_BLOCK
    overlap = ((cmp_end[:, None] - (CMP_LEN - 1) < sel_start[None, :] + SEL_BLOCK)
               & (cmp_end[:, None] >= sel_start[None, :])).astype(jnp.float32)
    k_blk = kv4[:, :, 2].reshape(B, ns, SEL_BLOCK, N_KV, HEAD_DIM).transpose(0, 3, 1, 2, 4)
    v_blk = kv4[:, :, 3].reshape(B, ns, SEL_BLOCK, N_KV, HEAD_DIM).transpose(0, 3, 1, 2, 4)
    n_top = min(SEL_TOP, ns)
    scale = HEAD_DIM ** -0.5
    bi = jnp.arange(B)[:, None, None, None]
    gi = jnp.arange(N_KV)[None, :, None, None]
    blk_ids = jnp.arange(ns, dtype=jnp.int32)
    qc_len = math.gcd(Tq, Q_CHUNK)
    nch = Tq // qc_len

    def chunk(args):
        qc, pc = args
        qg = qc.reshape(B, qc_len, N_KV, GROUP, HEAD_DIM)
        s = jnp.einsum('bqgnd,bcgd->bgnqc', qg, k_cmp, preferred_element_type=jnp.float32) * scale
        dist = (pc[:, None] - cmp_end[None, :]).astype(jnp.float32)
        valid = cmp_end[None, :] <= pc[:, None]
        s = s - slopes[:, :, None, None] * dist
        p = jax.nn.softmax(jnp.where(valid, s, NEG), axis=-1) * valid
        o_c = jnp.einsum('bgnqc,bcgd->bqgnd', p.astype(v_cmp.dtype), v_cmp)
        imp = jnp.einsum('bgnqc,cs->bgqs', p, overlap)
        cur = pc // SEL_BLOCK
        forced = (blk_ids[None, :] == 0) | (blk_ids[None, :] == cur[:, None]) | (blk_ids[None, :] == cur[:, None] - 1)
        imp = jnp.where(forced, FORCE_SCORE, imp)
        imp = jnp.where(blk_ids[None, :] <= cur[:, None], imp, -1.0)
        _, idx = lax.top_k(imp, n_top)
        kg = k_blk[bi, gi, idx]
        vg = v_blk[bi, gi, idx]
        kpos = idx[..., None] * SEL_BLOCK + jnp.arange(SEL_BLOCK, dtype=jnp.int32)
        dsel = pc[None, None, :, None, None] - kpos
        s2 = jnp.einsum('bqgnd,bgqksd->bgnqks', qg, kg, preferred_element_type=jnp.float32) * scale
        s2 = s2 - slopes[None, :, :, None, None, None] * dsel[:, :, None].astype(jnp.float32)
        s2 = jnp.where(dsel[:, :, None] >= 0, s2, NEG)
        sh = s2.shape
        p2 = jax.nn.softmax(s2.reshape(*sh[:-2], -1), axis=-1).reshape(sh)
        o_s = jnp.einsum('bgnqks,bgqksd->bqgnd', p2.astype(vg.dtype), vg)
        return (o_c.reshape(B, qc_len, N_HEADS, HEAD_DIM), o_s.reshape(B, qc_len, N_HEADS, HEAD_DIM))

    qs = q.reshape(B, nch, qc_len, N_HEADS, HEAD_DIM).transpose(1, 0, 2, 3, 4)
    ps = q_pos.reshape(nch, qc_len)
    o_c, o_s = lax.map(chunk, (qs, ps))
    o_c = o_c.transpose(1, 0, 2, 3, 4).reshape(B, Tq, N_HEADS, HEAD_DIM)
    o_s = o_s.transpose(1, 0, 2, 3, 4).reshape(B, Tq, N_HEADS, HEAD_DIM)
    return o_c, o_s


def _window_attend(q, q_pos, kw, vw, k_pos, slopes):
    B, Tq = q.shape[:2]
    qg = q.reshape(B, Tq, N_KV, GROUP, HEAD_DIM)
    s = jnp.einsum('bqgnd,bkgd->bgnqk', qg, kw, preferred_element_type=jnp.float32) * (HEAD_DIM ** -0.5)
    dist = q_pos[:, None] - k_pos[None, :]
    valid = (dist >= 0) & (dist < WINDOW) & (k_pos[None, :] >= 0)
    s = s - slopes[:, :, None, None] * dist.astype(jnp.float32)
    p = jax.nn.softmax(jnp.where(valid, s, NEG), axis=-1)
    o = jnp.einsum('bgnqk,bkgd->bqgnd', p.astype(vw.dtype), vw)
    return o.reshape(B, Tq, N_HEADS, HEAD_DIM)


def _window_prompt(q, kvw, slopes):
    B, T = q.shape[:2]
    kpad = jnp.pad(kvw, ((0, 0), (WINDOW, 0), (0, 0), (0, 0), (0, 0)))
    span = WINDOW + WIN_QBLK

    def blk(i):
        q0 = i * WIN_QBLK
        qb = lax.dynamic_slice_in_dim(q, q0, WIN_QBLK, axis=1)
        kb = lax.dynamic_slice_in_dim(kpad, q0, span, axis=1)
        qpos = q0 + jnp.arange(WIN_QBLK, dtype=jnp.int32)
        kpos = q0 - WINDOW + jnp.arange(span, dtype=jnp.int32)
        return _window_attend(qb, qpos, kb[:, :, 0], kb[:, :, 1], kpos, slopes)

    o = lax.map(blk, jnp.arange(T // WIN_QBLK, dtype=jnp.int32))
    return o.transpose(1, 0, 2, 3, 4).reshape(B, T, N_HEADS, HEAD_DIM)


def _cplx_affine_combine(e1, e2):
    a1r, a1i, b1r, b1i = e1
    a2r, a2i, b2r, b2i = e2
    return (a2r * a1r - a2i * a1i, a2r * a1i + a2i * a1r,
            a2r * b1r - a2i * b1i + b2r, a2r * b1i + a2i * b1r + b2i)


def _ssm_scan(u, h0_re, h0_im, lam_re, lam_im, log_dt, b_re, b_im, c_re, c_im, d_skip):
    f32 = jnp.float32
    B, T = u.shape[:2]
    uf = u.astype(f32).reshape(B, T, SSM_NGROUPS, SSM_GROUP)
    dt = jnp.exp(log_dt.astype(f32))[:, None]
    lr = lam_re.astype(f32)
    li = lam_im.astype(f32)
    mag = jnp.exp(lr * dt)
    ar = mag * jnp.cos(li * dt)
    ai = mag * jnp.sin(li * dt)
    den = lr * lr + li * li
    fr = ((ar - 1.0) * lr + ai * li) / den
    fi = (ai * lr - (ar - 1.0) * li) / den
    br = b_re.astype(f32)
    bim = b_im.astype(f32)
    bbr = fr[..., None] * br - fi[..., None] * bim
    bbi = fr[..., None] * bim + fi[..., None] * br
    xr = jnp.einsum('btgc,gpc->btgp', uf, bbr)
    xi = jnp.einsum('btgc,gpc->btgp', uf, bbi)
    xr = jnp.concatenate([h0_re.astype(f32)[:, None], xr], axis=1)
    xi = jnp.concatenate([h0_im.astype(f32)[:, None], xi], axis=1)
    a_r = jnp.broadcast_to(ar, xr.shape)
    a_i = jnp.broadcast_to(ai, xr.shape)
    _, _, hr, hi = lax.associative_scan(_cplx_affine_combine, (a_r, a_i, xr, xi), axis=1)
    hr = hr[:, 1:]
    hi = hi[:, 1:]
    y = (jnp.einsum('btgp,gcp->btgc', hr, c_re.astype(f32))
         - jnp.einsum('btgp,gcp->btgc', hi, c_im.astype(f32))
         + d_skip.astype(f32).reshape(SSM_NGROUPS, SSM_GROUP) * uf)
    return y.reshape(B, T, SSM_WIDTH).astype(u.dtype), hr[:, -1], hi[:, -1]


def _merge_ffn(x, o_heads, y_ssm, mg, w_attn_out, w_glu, w_out, g_ffn, w_gate, w_up, w_down):
    B, T = x.shape[:2]
    ya = o_heads.reshape(B, T, Q_COLS) @ w_attn_out
    yg = jax.nn.gelu(y_ssm) @ w_glu
    yb = yg[..., :D_MODEL] * jax.nn.sigmoid(yg[..., D_MODEL:])
    m = jax.nn.sigmoid(mg[:, :, 0]) * ya + jax.nn.sigmoid(mg[:, :, 1]) * yb
    x = x + m @ w_out
    h = _rmsnorm(x, g_ffn)
    return x + (jax.nn.silu(h @ w_gate) * (h @ w_up)) @ w_down


def _layer_prompt(x, lw, slopes):
    (g_mix, w_in, pe, pool, phi, w_attn_out, lam_re, lam_im, log_dt, b_re, b_im, c_re, c_im,
     d_skip, w_glu, w_out, g_ffn, w_gate, w_up, w_down) = lw
    B, T = x.shape[:2]
    q, kv, gates, u, mg = _in_proj(x, g_mix, w_in)
    pos = jnp.arange(T, dtype=jnp.int32)
    o_c, o_s = _cmp_sel_attend(q, pos, kv[:, :, :4], pe, pool, phi, slopes)
    o_w = _window_prompt(q, kv[:, :, 4:], slopes)
    o = gates[:, :, 0, :, None] * o_c + gates[:, :, 1, :, None] * o_s + gates[:, :, 2, :, None] * o_w
    h0 = jnp.zeros((B, SSM_NGROUPS, SSM_STATE), jnp.float32)
    y_ssm, h_re, h_im = _ssm_scan(u, h0, h0, lam_re, lam_im, log_dt, b_re, b_im, c_re, c_im, d_skip)
    y = _merge_ffn(x, o, y_ssm, mg, w_attn_out, w_glu, w_out, g_ffn, w_gate, w_up, w_down)
    n_win = min(WINDOW, T)
    return y, kv[:, :, :4], kv[:, T - n_win:, 4:], h_re, h_im


def _layer_sample(x, kv_past, win_buf, h_re, h_im, lw, slopes):
    (g_mix, w_in, pe, pool, phi, w_attn_out, lam_re, lam_im, log_dt, b_re, b_im, c_re, c_im,
     d_skip, w_glu, w_out, g_ffn, w_gate, w_up, w_down) = lw
    B, T = x.shape[:2]
    past_len = kv_past.shape[1]
    q, kv, gates, u, mg = _in_proj(x, g_mix, w_in)
    pos = past_len + jnp.arange(T, dtype=jnp.int32)
    kv4 = jnp.concatenate([kv_past, kv[:, :, :4].astype(kv_past.dtype)], axis=1)
    o_c, o_s = _cmp_sel_attend(q, pos, kv4, pe, pool, phi, slopes)
    n_buf = win_buf.shape[1]
    kvw = jnp.concatenate([win_buf, kv[:, :, 4:].astype(win_buf.dtype)], axis=1)
    kpos = past_len - n_buf + jnp.arange(n_buf + T, dtype=jnp.int32)
    o_w = _window_attend(q, pos, kvw[:, :, 0], kvw[:, :, 1], kpos, slopes)
    o = gates[:, :, 0, :, None] * o_c + gates[:, :, 1, :, None] * o_s + gates[:, :, 2, :, None] * o_w
    y_ssm, hr, hi = _ssm_scan(u, h_re, h_im, lam_re, lam_im, log_dt, b_re, b_im, c_re, c_im, d_skip)
    y = _merge_ffn(x, o, y_ssm, mg, w_attn_out, w_glu, w_out, g_ffn, w_gate, w_up, w_down)
    n_keep = min(WINDOW, n_buf + T)
    return y, kv[:, :, :4], kvw[:, n_buf + T - n_keep:], hr, hi


def setup_inputs(seed: int = 0) -> dict:
    key = jax.random.key(seed)
    ks = iter(jax.random.split(key, 32))
    f32 = jnp.float32

    def nrm(shape, s=1.0):
        return jax.random.normal(next(ks), shape, f32) * s

    L = DEPTH
    n_pages = PAST_LEN // PAGE_SIZE
    n_pool = (DEC_BATCH * n_pages * 5) // 4
    n_win = min(WINDOW, PAST_LEN)
    x_prompt = nrm((BATCH, SEQ, D_MODEL))
    x_sample = nrm((DEC_BATCH, DEC_SEQ, D_MODEL))
    cache_kv = nrm((L, n_pool, PAGE_SIZE, 4, N_KV, HEAD_DIM))
    state_win = nrm((L, DEC_BATCH, n_win, 2, N_KV, HEAD_DIM))
    state_ssm_re = nrm((L, DEC_BATCH, SSM_NGROUPS, SSM_STATE), 0.1)
    state_ssm_im = nrm((L, DEC_BATCH, SSM_NGROUPS, SSM_STATE), 0.1)
    perm = jax.random.permutation(next(ks), n_pool)
    page_table = perm[:DEC_BATCH * n_pages].reshape(DEC_BATCH, n_pages).astype(jnp.int32)
    g_mix = 1.0 + nrm((L, D_MODEL), 0.01)
    w_in = nrm((L, D_MODEL, IN_COLS), D_MODEL ** -0.5)
    w_cmp_pe = nrm((L, 2, CMP_LEN, HEAD_DIM), 0.1)
    w_cmp_pool = (1.0 + nrm((L, 2, CMP_LEN), 0.1)) * CMP_LEN ** -0.5
    w_cmp_phi = nrm((L, 2, HEAD_DIM, HEAD_DIM), HEAD_DIM ** -0.5)
    w_attn_out = nrm((L, Q_COLS, D_MODEL), Q_COLS ** -0.5)
    ssm_lam_re = -0.5 + nrm((L, SSM_NGROUPS, SSM_STATE), 0.01)
    ssm_lam_im = jnp.pi * jnp.arange(SSM_STATE, dtype=f32) + nrm((L, SSM_NGROUPS, SSM_STATE), 0.01)
    ssm_log_dt = jax.random.uniform(next(ks), (L, SSM_NGROUPS), f32, math.log(1e-3), math.log(1e-1))
    ssm_b_re = nrm((L, SSM_NGROUPS, SSM_STATE, SSM_GROUP), (2.0 * SSM_GROUP) ** -0.5)
    ssm_b_im = nrm((L, SSM_NGROUPS, SSM_STATE, SSM_GROUP), (2.0 * SSM_GROUP) ** -0.5)
    ssm_c_re = nrm((L, SSM_NGROUPS, SSM_GROUP, SSM_STATE), (2.0 * SSM_STATE) ** -0.5)
    ssm_c_im = nrm((L, SSM_NGROUPS, SSM_GROUP, SSM_STATE), (2.0 * SSM_STATE) ** -0.5)
    ssm_d = nrm((L, SSM_WIDTH))
    w_glu = nrm((L, SSM_WIDTH, 2 * D_MODEL), SSM_WIDTH ** -0.5)
    w_out = nrm((L, D_MODEL, D_MODEL), D_MODEL ** -0.5)
    g_ffn = 1.0 + nrm((L, D_MODEL), 0.01)
    w_gate = nrm((L, D_MODEL, D_FF), D_MODEL ** -0.5)
    w_up = nrm((L, D_MODEL, D_FF), D_MODEL ** -0.5)
    w_down = nrm((L, D_FF, D_MODEL), D_FF ** -0.5)
    g_final = 1.0 + nrm((D_MODEL,), 0.01)
    return {'x_prompt': x_prompt, 'x_sample': x_sample, 'cache_kv': cache_kv, 'state_win': state_win,
            'state_ssm_re': state_ssm_re, 'state_ssm_im': state_ssm_im, 'page_table': page_table,
            'g_mix': g_mix, 'w_in': w_in, 'w_cmp_pe': w_cmp_pe, 'w_cmp_pool': w_cmp_pool, 'w_cmp_phi': w_cmp_phi,
            'w_attn_out': w_attn_out, 'ssm_lam_re': ssm_lam_re, 'ssm_lam_im': ssm_lam_im, 'ssm_log_dt': ssm_log_dt,
            'ssm_b_re': ssm_b_re, 'ssm_b_im': ssm_b_im, 'ssm_c_re': ssm_c_re, 'ssm_c_im': ssm_c_im, 'ssm_d': ssm_d,
            'w_glu': w_glu, 'w_out': w_out, 'g_ffn': g_ffn, 'w_gate': w_gate, 'w_up': w_up, 'w_down': w_down,
            'g_final': g_final}


def reference(x_prompt, x_sample, cache_kv, state_win, state_ssm_re, state_ssm_im, page_table,
              g_mix, w_in, w_cmp_pe, w_cmp_pool, w_cmp_phi, w_attn_out, ssm_lam_re, ssm_lam_im, ssm_log_dt,
              ssm_b_re, ssm_b_im, ssm_c_re, ssm_c_im, ssm_d, w_glu, w_out, g_ffn, w_gate, w_up, w_down, g_final):
    slopes = _alibi_slopes()
    n_seq, n_pages = page_table.shape
    page = cache_kv.shape[2]
    xp, xs = x_prompt, x_sample
    kvp_l, winp_l, hrp_l, hip_l = [], [], [], []
    kvs_l, wins_l, hrs_l, his_l = [], [], [], []
    for l in range(DEPTH):
        lw = (g_mix[l], w_in[l], w_cmp_pe[l], w_cmp_pool[l], w_cmp_phi[l], w_attn_out[l],
              ssm_lam_re[l], ssm_lam_im[l], ssm_log_dt[l], ssm_b_re[l], ssm_b_im[l], ssm_c_re[l], ssm_c_im[l],
              ssm_d[l], w_glu[l], w_out[l], g_ffn[l], w_gate[l], w_up[l], w_down[l])
        xp, kvp, winp, hrp, hip = _layer_prompt(xp, lw, slopes)
        kv_past = cache_kv[l][page_table].reshape(n_seq, n_pages * page, 4, N_KV, HEAD_DIM)
        xs, kvs, wins, hrs, his = _layer_sample(xs, kv_past, state_win[l], state_ssm_re[l], state_ssm_im[l], lw, slopes)
        kvp_l.append(kvp); winp_l.append(winp); hrp_l.append(hrp); hip_l.append(hip)
        kvs_l.append(kvs); wins_l.append(wins); hrs_l.append(hrs); his_l.append(his)
    y_prompt = _rmsnorm(xp, g_final)
    y_sample = _rmsnorm(xs, g_final)
    kv_rows_prompt = jnp.stack(kvp_l)
    win_prompt = jnp.stack(winp_l)
    ssm_re_prompt = jnp.stack(hrp_l)
    ssm_im_prompt = jnp.stack(hip_l)
    kv_rows_sample = jnp.stack(kvs_l)
    win_sample = jnp.stack(wins_l)
    ssm_re_sample = jnp.stack(hrs_l)
    ssm_im_sample = jnp.stack(his_l)
    return (y_prompt, y_sample, kv_rows_prompt, win_prompt, ssm_re_prompt, ssm_im_prompt,
            kv_rows_sample, win_sample, ssm_re_sample, ssm_im_sample)
```

```python
import functools
import math

import numpy as np
import jax
import jax.numpy as jnp
from jax import lax
from jax.experimental import pallas as pl
from jax.experimental.pallas import tpu as pltpu

F32 = jnp.float32
BF16 = jnp.bfloat16

N_HEADS = 16
HEAD_DIM = 64
N_KV = 4
GROUP = N_HEADS // N_KV
CMP_LEN = 32
CMP_STRIDE = 16
SEL_BLOCK = 64
SEL_TOP = 16
WINDOW = 512
SSM_GROUP = 16
SSM_STATE = 64
EPS = 1e-6
NEG = -1e30
FORCE_SCORE = 1e4

KV_COLS = N_KV * HEAD_DIM
Q_COLS = N_HEADS * HEAD_DIM
SUB_PER_PAGE = 8
PAGES_PER_STEP = 16
VMEM_LIMIT = 56 * 1024 * 1024


def _dot(a, b):
    return jnp.dot(a, b, preferred_element_type=F32)


def _dot_nt(a, b):
    return lax.dot_general(a, b, (((1,), (1,)), ((), ())), preferred_element_type=F32)


def _split(a):
    hi = a.astype(BF16)
    lo = (a - hi.astype(F32)).astype(BF16)
    return hi, lo


def _dot3(a, b):
    ah, al = _split(a)
    bh, bl = _split(b)
    return _dot(ah, bh) + _dot(ah, bl) + _dot(al, bh)


def _dot_nt_hl(w_bf16, a):
    ah, al = _split(a)
    return _dot_nt(w_bf16, ah) + _dot_nt(w_bf16, al)


def _gelu(x):
    return 0.5 * x * (1.0 + jnp.tanh(math.sqrt(2.0 / math.pi) * (x + 0.044715 * (x * x * x))))


def _iota(shape, dim):
    return lax.broadcasted_iota(jnp.int32, shape, dim)


def _log2(n):
    assert n > 0 and n & (n - 1) == 0, n
    return n.bit_length() - 1


def _div_pow2(x, n):
    return jnp.right_shift(x, _log2(n))


def _mod_pow2(x, n):
    return jnp.bitwise_and(x, (1 << _log2(n)) - 1)


def _softmax_rows(s):
    m = jnp.max(s, axis=-1, keepdims=True)
    e = jnp.exp(s - m)
    return e / jnp.sum(e, axis=-1, keepdims=True)


IN_TN = 512
_Q_T, _KV_T, _U_T, _MG_T = 2, 3, 2, 8
_KV_0 = _Q_T
_U_0 = _KV_0 + _KV_T
_MG_0 = _U_0 + _U_T
_GT_0 = _MG_0 + _MG_T
IN_TILES = _GT_0 + 1


def _inproj_kernel(x_ref, g_ref, w_ref, q_ref, kv_ref, u_ref, sg_ref, gt_ref, xn_ref):
    j = pl.program_id(1)

    @pl.when(j == 0)
    def _():
        x = x_ref[...]
        r = lax.rsqrt(jnp.mean(x * x, axis=-1, keepdims=True) + EPS)
        xn_ref[...] = (x * r * g_ref[...]).astype(BF16)

    z = _dot(xn_ref[...], w_ref[...])

    @pl.when(j < _KV_0)
    def _():
        q_ref[...] = (z * (HEAD_DIM ** -0.5)).astype(BF16)

    @pl.when((j >= _KV_0) & (j < _U_0))
    def _():
        kv_ref[...] = z

    @pl.when((j >= _U_0) & (j < _MG_0))
    def _():
        u_ref[...] = z

    @pl.when((j >= _MG_0) & (j < _GT_0))
    def _():
        sg_ref[...] = jax.nn.sigmoid(z).astype(BF16)

    @pl.when(j == _GT_0)
    def _():
        gt_ref[...] = jax.nn.sigmoid(z)


def _in_proj(x, g_mix, w_cat, tm):
    rows, d = x.shape

    def col(lo, n):
        return lambda i, j: (i, jnp.clip(j - lo, 0, n - 1))

    return pl.pallas_call(
        _inproj_kernel,
        grid=(rows // tm, IN_TILES),
        in_specs=[
            pl.BlockSpec((tm, d), lambda i, j: (i, 0)),
            pl.BlockSpec((1, d), lambda i, j: (0, 0)),
            pl.BlockSpec((d, IN_TN), lambda i, j: (0, j)),
        ],
        out_specs=[
            pl.BlockSpec((tm, IN_TN), col(0, _Q_T)),
            pl.BlockSpec((tm, IN_TN), col(_KV_0, _KV_T)),
            pl.BlockSpec((tm, IN_TN), col(_U_0, _U_T)),
            pl.BlockSpec((tm, IN_TN), col(_MG_0, _MG_T)),
            pl.BlockSpec((tm, IN_TN), lambda i, j: (i, 0)),
        ],
        out_shape=[
            jax.ShapeDtypeStruct((rows, _Q_T * IN_TN), BF16),
            jax.ShapeDtypeStruct((rows, _KV_T * IN_TN), F32),
            jax.ShapeDtypeStruct((rows, _U_T * IN_TN), F32),
            jax.ShapeDtypeStruct((rows, _MG_T * IN_TN), BF16),
            jax.ShapeDtypeStruct((rows, IN_TN), F32),
        ],
        scratch_shapes=[pltpu.VMEM((tm, d), BF16)],
        compiler_params=pltpu.CompilerParams(
            dimension_semantics=("parallel", "arbitrary"), vmem_limit_bytes=VMEM_LIMIT),
        name="in_proj",
    )(x, g_mix.reshape(1, d), w_cat)


def _pool_kernel(pt_ref, *refs, npg):
    page_refs = refs[:npg]
    pm_ref, out_ref, carry_ref = refs[npg], refs[npg + 1], refs[npg + 2]

    @pl.when(pl.program_id(1) == 0)
    def _():
        carry_ref[...] = jnp.zeros_like(carry_ref)

    pm = pm_ref[...]
    first = _iota((SUB_PER_PAGE, 2 * KV_COLS), 0) == 0
    carry = carry_ref[...]
    for j in range(npg):
        x = page_refs[j][0].astype(BF16)
        r = _dot(pm, x)
        body = jnp.concatenate([r[0:8, :KV_COLS], r[16:24, KV_COLS:]], axis=1)
        out_ref[0, 8 * j:8 * j + 8, :] = body + jnp.where(first, carry, 0.0)
        carry = jnp.concatenate([r[8:9, :KV_COLS], r[24:25, KV_COLS:]], axis=1)
    carry_ref[...] = carry


def _pool_matrix(pool):
    r = np.arange(SUB_PER_PAGE)[:, None]
    j = np.arange(128)[None, :]
    idx = j - CMP_STRIDE * (r - 1)
    ok = (idx >= 0) & (idx < CMP_LEN)
    idx = np.clip(idx, 0, CMP_LEN - 1)
    cidx = j[0] - (128 - CMP_STRIDE)
    cok = cidx >= 0
    cidx = np.clip(cidx, 0, CMP_LEN - 1)
    mats = []
    for s in range(2):
        body = jnp.where(ok, pool[s][idx], 0.0)
        carry = jnp.where(cok, pool[s][cidx], 0.0)[None, :]
        mats += [body, carry, jnp.zeros((7, 128), F32)]
    return jnp.concatenate(mats, axis=0).astype(BF16)


def _pool_pages(pages, page_table, pmat):
    nseq, npages = page_table.shape
    npg = PAGES_PER_STEP
    nstep = npages // npg
    page_specs = [
        pl.BlockSpec((1, 128, 2 * KV_COLS), lambda b, s, pt, j=j: (pt[b, s * npg + j], 0, 0))
        for j in range(npg)
    ]
    return pl.pallas_call(
        functools.partial(_pool_kernel, npg=npg),
        grid_spec=pltpu.PrefetchScalarGridSpec(
            num_scalar_prefetch=1,
            grid=(nseq, nstep),
            in_specs=page_specs + [pl.BlockSpec((32, 128), lambda b, s, pt: (0, 0))],
            out_specs=pl.BlockSpec((1, npg * SUB_PER_PAGE, 2 * KV_COLS), lambda b, s, pt: (b, s, 0)),
            scratch_shapes=[pltpu.VMEM((1, 2 * KV_COLS), F32)],
        ),
        out_shape=jax.ShapeDtypeStruct((nseq, npages * SUB_PER_PAGE, 2 * KV_COLS), F32),
        compiler_params=pltpu.CompilerParams(
            dimension_semantics=("parallel", "arbitrary"), vmem_limit_bytes=VMEM_LIMIT),
        name="cmp_pool",
    )(page_table, *([pages] * npg), pmat)


def _cmp_finish_kernel(pooled_ref, poolw_ref, pe_ref, phi_ref, kc_ref, vc_ref):
    bias = jnp.sum(poolw_ref[...] * pe_ref[...], axis=0, keepdims=True)
    x = _gelu(pooled_ref[0] + bias)
    kc_ref[0] = _dot3(x[:, :KV_COLS], phi_ref[0]).astype(BF16)
    vc_ref[0] = _dot3(x[:, KV_COLS:], phi_ref[1]).astype(BF16)


def _cmp_finish(pooled, pool, pe, phi):
    nseq, n, _ = pooled.shape
    poolw = jnp.concatenate([jnp.broadcast_to(pool[s][:, None], (CMP_LEN, KV_COLS)) for s in range(2)], axis=1)
    pe_t = jnp.concatenate([jnp.tile(pe[s], (1, N_KV)) for s in range(2)], axis=1)
    phi_bd = jnp.stack([jnp.kron(jnp.eye(N_KV, dtype=F32), phi[s]) for s in range(2)])
    spec = pl.BlockSpec((1, n, KV_COLS), lambda b: (b, 0, 0))
    return pl.pallas_call(
        _cmp_finish_kernel,
        grid=(nseq,),
        in_specs=[
            pl.BlockSpec((1, n, 2 * KV_COLS), lambda b: (b, 0, 0)),
            pl.BlockSpec((CMP_LEN, 2 * KV_COLS), lambda b: (0, 0)),
            pl.BlockSpec((CMP_LEN, 2 * KV_COLS), lambda b: (0, 0)),
            pl.BlockSpec((2, KV_COLS, KV_COLS), lambda b: (0, 0, 0)),
        ],
        out_specs=[spec, spec],
        out_shape=[jax.ShapeDtypeStruct((nseq, n, KV_COLS), BF16)] * 2,
        compiler_params=pltpu.CompilerParams(dimension_semantics=("parallel",)),
        name="cmp_finish",
    )(pooled, poolw, pe_t, phi_bd)


ATT_TQ = 128
ATT_TK = 512
NBLK_PAD = 128


def _attn_prompt_kernel(slopes_ref, q_ref, kc_ref, vc_ref, ks_ref, vs_ref, kw_ref, vw_ref, gt_ref,
                        ovt_ref, e_ref, o_ref, m_ref, l_ref, acc_ref, *, tq, tk, seq, n_sel, n_top):
    g = pl.program_id(1)
    q0 = pl.program_id(2) * tq
    q = q_ref[0, 0].reshape(GROUP * tq, HEAD_DIM)
    tcol = q0 + _iota((tq, 1), 0)
    qend = (q0 + tq).astype(F32)
    slope = [slopes_ref[GROUP * g + n] for n in range(GROUP)]
    rows = [slice(n * tq, (n + 1) * tq) for n in range(GROUP)]

    ncmp = kc_ref.shape[2]
    nrow = _iota((1, ncmp), 1)
    cend = CMP_STRIDE * nrow + (CMP_STRIDE - 1)
    valid = (nrow >= 1) & (cend <= tcol)
    crel = cend.astype(F32) - qend
    sc = _dot_nt(q, kc_ref[0, 0])
    ps = []
    for n in range(GROUP):
        s = jnp.where(valid, sc[rows[n]] + slope[n] * crel, NEG)
        ps.append(jnp.where(valid, _softmax_rows(s), 0.0))
    o_c = _dot(jnp.concatenate(ps, axis=0).astype(BF16), vc_ref[0, 0])

    psum = (ps[0] + ps[1]) + (ps[2] + ps[3])
    imp = _dot_nt_hl(ovt_ref[...], psum)[0:n_sel]
    blk = _iota((n_sel, 1), 0)
    cur = _div_pow2(q0 + _iota((1, tq), 1), SEL_BLOCK)
    forced = (blk == 0) | (blk == cur) | (blk == cur - 1)
    imp = jnp.where(forced, FORCE_SCORE, imp)
    imp = jnp.where(blk <= cur, imp, -1.0)
    cnt = jnp.zeros((n_sel, tq), F32)
    for j in range(n_sel):
        vj = imp[j:j + 1, :]
        beats = (vj > imp) | ((vj == imp) & (blk > j))
        cnt = cnt + jnp.where(beats, 1.0, 0.0)
    sel_t = jnp.where(cnt < n_top, 1.0, 0.0)
    sel_t = jnp.concatenate([sel_t, jnp.zeros((NBLK_PAD - n_sel, tq), F32)], axis=0)
    sel = sel_t.T.astype(BF16)

    m_ref[...] = jnp.full_like(m_ref, NEG)
    l_ref[...] = jnp.zeros_like(l_ref)
    acc_ref[...] = jnp.zeros_like(acc_ref)

    def body(kt, carry):
        k0 = pl.multiple_of(kt * tk, tk)
        k = ks_ref[0, 0, 0, pl.ds(k0, tk), :]
        v = vs_ref[0, 0, 0, pl.ds(k0, tk), :]
        s = _dot_nt(q, k)
        kpos = k0 + _iota((1, tk), 1)
        chosen = _dot(sel, e_ref[:, pl.ds(k0, tk)])
        allowed = (chosen > 0.5) & (kpos <= tcol)
        bm = jnp.where(allowed, 0.0, NEG)
        krel = kpos.astype(F32) - qend
        for n in range(GROUP):
            sn = s[rows[n]] + slope[n] * krel + bm
            m_old = m_ref[rows[n]]
            m_new = jnp.maximum(m_old, jnp.max(sn, axis=-1, keepdims=True))
            alpha = jnp.exp(m_old - m_new)
            p = jnp.exp(sn - m_new)
            l_ref[rows[n]] = alpha * l_ref[rows[n]] + jnp.sum(p, axis=-1, keepdims=True)
            acc_ref[rows[n]] = alpha * acc_ref[rows[n]] + _dot(p.astype(BF16), v)
            m_ref[rows[n]] = m_new
        return carry

    lax.fori_loop(0, (q0 + tq + tk - 1) // tk, body, 0)
    o_s = acc_ref[...] / l_ref[...]

    span = WINDOW + tq
    w0 = pl.multiple_of(jnp.maximum(q0 - WINDOW, 0), tq)
    kw = kw_ref[0, 0, 0, pl.ds(w0, span), :]
    vw = vw_ref[0, 0, 0, pl.ds(w0, span), :]
    sw = _dot_nt(q, kw)
    kposw = w0 + _iota((1, span), 1)
    dist = tcol - kposw
    bmw = jnp.where((dist >= 0) & (dist < WINDOW), 0.0, NEG)
    krelw = kposw.astype(F32) - qend
    pw = [_softmax_rows(sw[rows[n]] + slope[n] * krelw + bmw) for n in range(GROUP)]
    o_w = _dot(jnp.concatenate(pw, axis=0).astype(BF16), vw)

    gt = gt_ref[0, 0]
    for n in range(GROUP):
        o = (gt[:, n:n + 1] * o_c[rows[n]] + gt[:, GROUP + n:GROUP + n + 1] * o_s[rows[n]]
             + gt[:, 2 * GROUP + n:2 * GROUP + n + 1] * o_w[rows[n]])
        o_ref[0, 0, n] = o.astype(BF16)


def _overlap_t(n_blk_rows, n_cmp):
    s = np.arange(n_blk_rows)[:, None]
    n = np.arange(n_cmp)[None, :]
    r = SEL_BLOCK // CMP_STRIDE
    return jnp.asarray(((n >= r * s) & (n <= r * s + r)).astype(np.float32), dtype=BF16)


def _block_expand(n_blk_rows, n_keys):
    s = np.arange(n_blk_rows)[:, None]
    k = np.arange(n_keys)[None, :]
    return jnp.asarray((k // SEL_BLOCK == s).astype(np.float32), dtype=BF16)


def _attn_prompt(slopes, q5, kc4, vc4, kvt, gates):
    b, g, _, t, _ = q5.shape
    ncmp = kc4.shape[2]
    tq, tk = ATT_TQ, ATT_TK
    n_sel = t // SEL_BLOCK
    ovt = _overlap_t(NBLK_PAD, ncmp)
    emat = _block_expand(NBLK_PAD, t)

    def kv_spec(slot):
        return pl.BlockSpec((1, 1, 1, t, HEAD_DIM), lambda bi, gi, qi, slot=slot: (slot, bi, gi, 0, 0))

    cmp_spec = pl.BlockSpec((1, 1, ncmp, HEAD_DIM), lambda bi, gi, qi: (bi, gi, 0, 0))
    q_spec = pl.BlockSpec((1, 1, GROUP, tq, HEAD_DIM), lambda bi, gi, qi: (bi, gi, 0, qi, 0))
    return pl.pallas_call(
        functools.partial(_attn_prompt_kernel, tq=tq, tk=tk, seq=t, n_sel=n_sel, n_top=min(SEL_TOP, n_sel)),
        grid=(b, g, t // tq),
        in_specs=[
            pl.BlockSpec(memory_space=pltpu.SMEM),
            q_spec, cmp_spec, cmp_spec, kv_spec(0), kv_spec(1), kv_spec(2), kv_spec(3),
            pl.BlockSpec((1, 1, tq, 16), lambda bi, gi, qi: (bi, gi, qi, 0)),
            pl.BlockSpec((NBLK_PAD, ncmp), lambda bi, gi, qi: (0, 0)),
            pl.BlockSpec((NBLK_PAD, t), lambda bi, gi, qi: (0, 0)),
        ],
        out_specs=q_spec,
        out_shape=jax.ShapeDtypeStruct(q5.shape, BF16),
        scratch_shapes=[
            pltpu.VMEM((GROUP * tq, 1), F32),
            pltpu.VMEM((GROUP * tq, 1), F32),
            pltpu.VMEM((GROUP * tq, HEAD_DIM), F32),
        ],
        compiler_params=pltpu.CompilerParams(
            dimension_semantics=("parallel", "parallel", "arbitrary"), vmem_limit_bytes=VMEM_LIMIT),
        name="attn_prompt",
    )(slopes, q5, kc4, vc4, kvt, kvt, kvt, kvt, gates, ovt, emat)


def _diag_blocks(o, maskbd):
    o = o * maskbd
    return (o[:, 0:64] + o[:, 64:128]) + (o[:, 128:192] + o[:, 192:256])


def _attn_dec1_kernel(qbd_ref, kc_ref, vc_ref, win_ref, kvnew_ref, gates_ref, slope_ref, ovt_ref,
                      maskbd_ref, opart_ref, sel_ref, *, past, tdec, n_blk, n_top):
    qb = qbd_ref[0]
    nrows = qb.shape[0]
    tcol = past + _mod_pow2(_iota((nrows, 1), 0), tdec)
    pref = float(past + tdec)
    slope = slope_ref[...]
    maskbd = maskbd_ref[...]

    ncmp = kc_ref.shape[1]
    nrow = _iota((1, ncmp), 1)
    cend = CMP_STRIDE * nrow + (CMP_STRIDE - 1)
    valid = (nrow >= 1) & (cend <= tcol)
    s = _dot_nt(qb, kc_ref[0])
    s = jnp.where(valid, s + slope * (cend.astype(F32) - pref), NEG)
    p = jnp.where(valid, _softmax_rows(s), 0.0)
    o_c = _diag_blocks(_dot(p.astype(BF16), vc_ref[0]), maskbd)

    per_g = GROUP * tdec
    psum = jnp.concatenate(
        [sum(p[gi * per_g + n * tdec: gi * per_g + (n + 1) * tdec] for n in range(GROUP)) for gi in range(N_KV)],
        axis=0)
    imp = _dot_nt_hl(ovt_ref[...], psum)
    nb_pad = imp.shape[0]
    blk = _iota((nb_pad, 1), 0)
    cur = _div_pow2(past + _mod_pow2(_iota((1, N_KV * tdec), 1), tdec), SEL_BLOCK)
    forced = (blk == 0) | (blk == cur) | (blk == cur - 1)
    imp = jnp.where(forced, FORCE_SCORE, imp)
    imp = jnp.where(blk <= cur, imp, -1.0)
    imp = jnp.where(blk < n_blk, imp, -2.0)
    sel = jnp.zeros(imp.shape, F32)
    for _ in range(n_top):
        mx = jnp.max(imp, axis=0, keepdims=True)
        first = jnp.min(jnp.where(imp == mx, blk, nb_pad), axis=0, keepdims=True)
        pick = blk == first
        sel = jnp.where(pick, 1.0, sel)
        imp = jnp.where(pick, -jnp.inf, imp)
    sel_ref[0] = sel

    nbuf = win_ref.shape[1]
    span = nbuf + 128
    knew = kvnew_ref[0]
    zpad = jnp.zeros((128 - tdec, KV_COLS), F32)
    kw = jnp.concatenate([win_ref[0, :, :KV_COLS], knew[:, 4 * KV_COLS:5 * KV_COLS], zpad], axis=0).astype(BF16)
    vw = jnp.concatenate([win_ref[0, :, KV_COLS:], knew[:, 5 * KV_COLS:], zpad], axis=0).astype(BF16)
    idx = _iota((1, span), 1)
    kposw = past - nbuf + idx
    dist = tcol - kposw
    validw = (dist >= 0) & (dist < WINDOW) & (kposw >= 0) & (idx < nbuf + tdec)
    sw = _dot_nt(qb, kw)
    sw = jnp.where(validw, sw + slope * (kposw.astype(F32) - pref), NEG)
    o_w = _diag_blocks(_dot(_softmax_rows(sw).astype(BF16), vw), maskbd)

    gates = gates_ref[0]
    opart_ref[0] = gates[:, 0:1] * o_c + gates[:, 2:3] * o_w


def _attn_dec1(qbd, kc, vc, win, kvnew, gates_rows, slopecol, maskbd, past, tdec, n_blk):
    nseq, nrows, _ = qbd.shape
    ncmp = kc.shape[1]
    nb_pad = -(-n_blk // 8) * 8
    ovt = _overlap_t(nb_pad, ncmp)
    nbuf = win.shape[1]
    return pl.pallas_call(
        functools.partial(_attn_dec1_kernel, past=past, tdec=tdec, n_blk=n_blk, n_top=min(SEL_TOP, n_blk)),
        grid=(nseq,),
        in_specs=[
            pl.BlockSpec((1, nrows, KV_COLS), lambda b: (b, 0, 0)),
            pl.BlockSpec((1, ncmp, KV_COLS), lambda b: (b, 0, 0)),
            pl.BlockSpec((1, ncmp, KV_COLS), lambda b: (b, 0, 0)),
            pl.BlockSpec((1, nbuf, 2 * KV_COLS), lambda b: (b, 0, 0)),
            pl.BlockSpec((1, tdec, 6 * KV_COLS), lambda b: (b, 0, 0)),
            pl.BlockSpec((1, nrows, 4), lambda b: (b, 0, 0)),
            pl.BlockSpec((nrows, 1), lambda b: (0, 0)),
            pl.BlockSpec((nb_pad, ncmp), lambda b: (0, 0)),
            pl.BlockSpec((nrows, KV_COLS), lambda b: (0, 0)),
        ],
        out_specs=[
            pl.BlockSpec((1, nrows, HEAD_DIM), lambda b: (b, 0, 0)),
            pl.BlockSpec((1, nb_pad, N_KV * tdec), lambda b: (b, 0, 0)),
        ],
        out_shape=[
            jax.ShapeDtypeStruct((nseq, nrows, HEAD_DIM), F32),
            jax.ShapeDtypeStruct((nseq, nb_pad, N_KV * tdec), F32),
        ],
        compiler_params=pltpu.CompilerParams(dimension_semantics=("parallel",), vmem_limit_bytes=VMEM_LIMIT),
        name="attn_dec_select",
    )(qbd, kc, vc, win, kvnew, gates_rows, slopecol, ovt, maskbd)


def _attn_dec2_kernel(pt_ref, *refs, npg, past, tdec):
    page_refs = refs[:npg]
    (qbd_ref, selrow_ref, sellast_ref, e_ref, slope_ref, knew_ref, opart_ref, gates_ref, maskbd_ref,
     o_ref, m_ref, l_ref, acc_ref) = refs[npg:]
    step = pl.program_id(1)
    nkeys = npg * 128
    qb = qbd_ref[0]
    nrows = qb.shape[0]
    pref = float(past + tdec)
    slope = slope_ref[...]

    @pl.when(step == 0)
    def _():
        m_ref[...] = jnp.full_like(m_ref, NEG)
        l_ref[...] = jnp.zeros_like(l_ref)
        acc_ref[...] = jnp.zeros_like(acc_ref)

    def update(s, pv_fn):
        m_old = m_ref[...]
        m_new = jnp.maximum(m_old, jnp.max(s, axis=-1, keepdims=True))
        alpha = jnp.exp(m_old - m_new)
        p = jnp.exp(s - m_new)
        l_ref[...] = alpha * l_ref[...] + jnp.sum(p, axis=-1, keepdims=True)
        acc_ref[...] = alpha * acc_ref[...] + pv_fn(p.astype(BF16))
        m_ref[...] = m_new

    pages = [page_refs[j][0] for j in range(npg)]
    s = jnp.concatenate([_dot_nt(qb, pg[:, :KV_COLS].astype(BF16)) for pg in pages], axis=1)
    kpos = step * nkeys + _iota((1, nkeys), 1)
    chosen = _dot(selrow_ref[0, 0], e_ref[...])
    s = s + slope * (kpos.astype(F32) - pref) + jnp.where(chosen > 0.5, 0.0, NEG)

    def pv_pages(p):
        acc = _dot(p[:, 0:128], pages[0][:, KV_COLS:].astype(BF16))
        for j in range(1, npg):
            acc = acc + _dot(p[:, 128 * j:128 * (j + 1)], pages[j][:, KV_COLS:].astype(BF16))
        return acc

    update(s, pv_pages)

    @pl.when(step == pl.num_programs(1) - 1)
    def _():
        tcol = past + _mod_pow2(_iota((nrows, 1), 0), tdec)
        kn = knew_ref[0]
        kposn = past + _iota((1, 128), 1)
        sn = _dot_nt(qb, kn[:, :KV_COLS].astype(BF16))
        ok = (sellast_ref[0, 0][:, 0:1].astype(F32) > 0.5) & (kposn <= tcol)
        sn = sn + slope * (kposn.astype(F32) - pref) + jnp.where(ok, 0.0, NEG)
        update(sn, lambda p: _dot(p, kn[:, KV_COLS:].astype(BF16)))
        o_s = _diag_blocks(acc_ref[...] / l_ref[...], maskbd_ref[...])
        o_ref[0] = opart_ref[0] + gates_ref[0][:, 1:2] * o_s


def _attn_dec2(cache, page_table, qbd, selrows, slopecol, knew, opart, gates_rows, maskbd, past, tdec):
    nseq, npages = page_table.shape
    npg = PAGES_PER_STEP
    nstep = npages // npg
    nrows = qbd.shape[1]
    blk_per_step = npg * 128 // SEL_BLOCK
    emat = _block_expand(blk_per_step, npg * 128)
    page_specs = [
        pl.BlockSpec((1, 128, 2 * KV_COLS), lambda b, s, pt, j=j: (pt[b, s * npg + j], 0, 1))
        for j in range(npg)
    ]
    per_seq = lambda shape: pl.BlockSpec((1,) + shape, lambda b, s, pt: (b,) + (0,) * len(shape))
    const = lambda shape: pl.BlockSpec(shape, lambda b, s, pt: (0,) * len(shape))
    return pl.pallas_call(
        functools.partial(_attn_dec2_kernel, npg=npg, past=past, tdec=tdec),
        grid_spec=pltpu.PrefetchScalarGridSpec(
            num_scalar_prefetch=1,
            grid=(nseq, nstep),
            in_specs=page_specs + [
                per_seq((nrows, KV_COLS)),
                pl.BlockSpec((1, 1, nrows, blk_per_step), lambda b, s, pt: (b, s, 0, 0)),
                pl.BlockSpec((1, 1, nrows, blk_per_step), lambda b, s, pt: (b, nstep, 0, 0)),
                const((blk_per_step, npg * 128)),
                const((nrows, 1)),
                per_seq((128, 2 * KV_COLS)),
                per_seq((nrows, HEAD_DIM)),
                per_seq((nrows, 4)),
                const((nrows, KV_COLS)),
            ],
            out_specs=per_seq((nrows, HEAD_DIM)),
            scratch_shapes=[
                pltpu.VMEM((nrows, 1), F32),
                pltpu.VMEM((nrows, 1), F32),
                pltpu.VMEM((nrows, KV_COLS), F32),
            ],
        ),
        out_shape=jax.ShapeDtypeStruct((nseq, nrows, HEAD_DIM), F32),
        compiler_params=pltpu.CompilerParams(
            dimension_semantics=("parallel", "arbitrary"), vmem_limit_bytes=VMEM_LIMIT),
        name="attn_dec_selected",
    )(page_table, *([cache] * npg), qbd, selrows, selrows, emat, slopecol, knew, opart, gates_rows, maskbd)


S5_GS = 2


def _s5_weights(lam_re, lam_im, log_dt, b_re, b_im, c_re, c_im, d_skip, L):
    hp = lax.Precision.HIGHEST
    ng, p = lam_re.shape
    c = SSM_GROUP
    dt = jnp.exp(log_dt)[:, None]
    lr, li = lam_re, lam_im
    mag = jnp.exp(lr * dt)
    ar = mag * jnp.cos(li * dt)
    ai = mag * jnp.sin(li * dt)
    den = lr * lr + li * li
    fr = ((ar - 1.0) * lr + ai * li) / den
    fi = (ai * lr - (ar - 1.0) * li) / den
    bbr = fr[..., None] * b_re - fi[..., None] * b_im
    bbi = fr[..., None] * b_im + fi[..., None] * b_re
    j = jnp.arange(L + 1, dtype=F32)[:, None, None]
    pmag = jnp.exp(j * (lr * dt))
    pr = pmag * jnp.cos(j * (li * dt))
    pi = pmag * jnp.sin(j * (li * dt))
    abr = pr[..., None] * bbr - pi[..., None] * bbi
    abi = pr[..., None] * bbi + pi[..., None] * bbr
    kd = (jnp.einsum('gcp,jgpk->jgck', c_re, abr, precision=hp)
          - jnp.einsum('gcp,jgpk->jgck', c_im, abi, precision=hp))
    kd = kd.at[0].add(d_skip.reshape(ng, c)[:, :, None] * jnp.eye(c, dtype=F32))
    tau = np.arange(L)
    delta = tau[None, :] - tau[:, None]
    wt = jnp.where((delta >= 0)[:, :, None, None, None], kd[np.clip(delta, 0, L)], 0.0)
    wt = wt.transpose(2, 0, 4, 1, 3).reshape(ng, L * c, L * c)
    rev = L - 1 - tau
    wp = jnp.concatenate([abr[rev].transpose(1, 0, 3, 2).reshape(ng, L * c, p),
                          abi[rev].transpose(1, 0, 3, 2).reshape(ng, L * c, p)], axis=-1)
    cr = c_re[None] * pr[1:, :, None, :] - c_im[None] * pi[1:, :, None, :]
    ci = -(c_re[None] * pi[1:, :, None, :] + c_im[None] * pr[1:, :, None, :])
    wo = jnp.concatenate([cr.transpose(1, 3, 0, 2).reshape(ng, p, L * c),
                          ci.transpose(1, 3, 0, 2).reshape(ng, p, L * c)], axis=1)
    al = jnp.stack([pr[L].reshape(-1), pi[L].reshape(-1)])
    return wt, wp, wo, al


def _s5_local_kernel(u_ref, wp_ref, pre_ref, pim_ref):
    r = [_dot3(u_ref[i], wp_ref[i]) for i in range(S5_GS)]
    pre_ref[...] = jnp.concatenate([x[:, :SSM_STATE] for x in r], axis=1)
    pim_ref[...] = jnp.concatenate([x[:, SSM_STATE:] for x in r], axis=1)


def _s5_scan_kernel(pre_ref, pim_ref, al_ref, h0r_ref, h0i_ref, hsr_ref, hsi_ref, hfr_ref, hfi_ref, *, nbatch):
    ntile = pre_ref.shape[0] // 8
    ar = al_ref[0:1, :]
    ai = al_ref[1:2, :]
    low = _iota((8, pre_ref.shape[1]), 0) < nbatch

    def step(cr, ci, xr, xi):
        return ar * cr - ai * ci + xr, ar * ci + ai * cr + xi

    def body(j, carry):
        cr, ci = carry
        r0 = pl.multiple_of(j * 8, 8)
        xr = pre_ref[pl.ds(r0, 8), :]
        xi = pim_ref[pl.ds(r0, 8), :]
        if nbatch == 8:
            hsr_ref[pl.ds(r0, 8), :] = cr
            hsi_ref[pl.ds(r0, 8), :] = ci
            return step(cr, ci, xr, xi)
        t1r, t1i = step(cr, ci, xr, xi)
        t1r = pltpu.roll(t1r, 4, 0)
        t1i = pltpu.roll(t1i, 4, 0)
        hsr_ref[pl.ds(r0, 8), :] = jnp.where(low, cr, t1r)
        hsi_ref[pl.ds(r0, 8), :] = jnp.where(low, ci, t1i)
        t2r, t2i = step(t1r, t1i, xr, xi)
        return pltpu.roll(t2r, 4, 0), pltpu.roll(t2i, 4, 0)

    cr, ci = lax.fori_loop(0, ntile, body, (h0r_ref[...], h0i_ref[...]))
    hfr_ref[...] = cr
    hfi_ref[...] = ci


def _s5_out_kernel(u_ref, wt_ref, wo_ref, hsr_ref, hsi_ref, y_ref):
    for i in range(S5_GS):
        lanes = slice(i * SSM_STATE, (i + 1) * SSM_STATE)
        hs = jnp.concatenate([hsr_ref[:, lanes], hsi_ref[:, lanes]], axis=1)
        y_ref[i] = _dot3(u_ref[i], wt_ref[i]) + _dot3(hs, wo_ref[i])


def _s5(u, h0r, h0i, weights, nbatch, nchunk, L):
    wt, wp, wo, al = weights
    ng = wt.shape[0]
    d = L * SSM_GROUP
    nb = nchunk * nbatch
    gp = ng * SSM_STATE
    assert nbatch in (4, 8) and nb % 8 == 0
    ug = (u.reshape(nbatch, nchunk, L, ng, SSM_GROUP).transpose(3, 1, 0, 2, 4).reshape(ng, nb, d))
    gs = S5_GS
    lane_blk = gs * SSM_STATE
    cp = pltpu.CompilerParams(dimension_semantics=("parallel",), vmem_limit_bytes=VMEM_LIMIT)
    pre, pim = pl.pallas_call(
        _s5_local_kernel,
        grid=(ng // gs,),
        in_specs=[pl.BlockSpec((gs, nb, d), lambda g: (g, 0, 0)),
                  pl.BlockSpec((gs, d, 2 * SSM_STATE), lambda g: (g, 0, 0))],
        out_specs=[pl.BlockSpec((nb, lane_blk), lambda g: (0, g))] * 2,
        out_shape=[jax.ShapeDtypeStruct((nb, gp), F32)] * 2,
        compiler_params=cp,
        name="s5_local",
    )(ug, wp)

    pad = lambda h: jnp.pad(h, ((0, 8 - nbatch), (0, 0)))
    lb = 512
    col = lambda r: pl.BlockSpec((r, lb), lambda c: (0, c))
    hsr, hsi, hfr, hfi = pl.pallas_call(
        functools.partial(_s5_scan_kernel, nbatch=nbatch),
        grid=(gp // lb,),
        in_specs=[col(nb), col(nb), col(2), col(8), col(8)],
        out_specs=[col(nb), col(nb), col(8), col(8)],
        out_shape=[jax.ShapeDtypeStruct((nb, gp), F32)] * 2 + [jax.ShapeDtypeStruct((8, gp), F32)] * 2,
        compiler_params=cp,
        name="s5_scan",
    )(pre, pim, al, pad(h0r), pad(h0i))

    yg = pl.pallas_call(
        _s5_out_kernel,
        grid=(ng // gs,),
        in_specs=[pl.BlockSpec((gs, nb, d), lambda g: (g, 0, 0)),
                  pl.BlockSpec((gs, d, d), lambda g: (g, 0, 0)),
                  pl.BlockSpec((gs, 2 * SSM_STATE, d), lambda g: (g, 0, 0)),
                  pl.BlockSpec((nb, lane_blk), lambda g: (0, g)),
                  pl.BlockSpec((nb, lane_blk), lambda g: (0, g))],
        out_specs=pl.BlockSpec((gs, nb, d), lambda g: (g, 0, 0)),
        out_shape=jax.ShapeDtypeStruct((ng, nb, d), F32),
        compiler_params=cp,
        name="s5_out",
    )(ug, wt, wo, hsr, hsi)
    y = yg.reshape(ng, nchunk, nbatch, L, SSM_GROUP).transpose(2, 1, 3, 0, 4).reshape(u.shape)
    return y, hfr[:nbatch], hfi[:nbatch]


def _merge_kernel(o_ref, y_ref, wa_ref, wg1_ref, wg2_ref, sga_ref, sgb_ref, m_ref, gy_ref):
    @pl.when(pl.program_id(1) == 0)
    def _():
        gy_ref[...] = _gelu(y_ref[...]).astype(BF16)

    ya = _dot(o_ref[...], wa_ref[...])
    gy = gy_ref[...]
    yb = _dot(gy, wg1_ref[...]) * jax.nn.sigmoid(_dot(gy, wg2_ref[...]))
    m = sga_ref[...].astype(F32) * ya + sgb_ref[...].astype(F32) * yb
    m_ref[...] = m.astype(BF16)


def _merge(o, y_ssm, w_attn_out, w_glu, sg, tm, tn=512):
    rows, d = o.shape[0], w_attn_out.shape[1]
    nj = d // tn
    kq, ks = w_attn_out.shape[0], w_glu.shape[0]
    return pl.pallas_call(
        _merge_kernel,
        grid=(rows // tm, nj),
        in_specs=[
            pl.BlockSpec((tm, kq), lambda i, j: (i, 0)),
            pl.BlockSpec((tm, ks), lambda i, j: (i, 0)),
            pl.BlockSpec((kq, tn), lambda i, j: (0, j)),
            pl.BlockSpec((ks, tn), lambda i, j: (0, j)),
            pl.BlockSpec((ks, tn), lambda i, j: (0, j + nj)),
            pl.BlockSpec((tm, tn), lambda i, j: (i, j)),
            pl.BlockSpec((tm, tn), lambda i, j: (i, j + nj)),
        ],
        out_specs=pl.BlockSpec((tm, tn), lambda i, j: (i, j)),
        out_shape=jax.ShapeDtypeStruct((rows, d), BF16),
        scratch_shapes=[pltpu.VMEM((tm, ks), BF16)],
        compiler_params=pltpu.CompilerParams(
            dimension_semantics=("parallel", "arbitrary"), vmem_limit_bytes=VMEM_LIMIT),
        name="merge",
    )(o, y_ssm, w_attn_out, w_glu, w_glu, sg, sg)


def _outproj_kernel(m_ref, x_ref, w_ref, g_ref, x1_ref, h_ref):
    x1 = x_ref[...] + _dot(m_ref[...], w_ref[...])
    x1_ref[...] = x1
    r = lax.rsqrt(jnp.mean(x1 * x1, axis=-1, keepdims=True) + EPS)
    h_ref[...] = (x1 * r * g_ref[...]).astype(BF16)


def _outproj(m, x, w_out, g_ffn, tm):
    rows, d = x.shape
    row = pl.BlockSpec((tm, d), lambda i: (i, 0))
    return pl.pallas_call(
        _outproj_kernel,
        grid=(rows // tm,),
        in_specs=[row, row, pl.BlockSpec((d, d), lambda i: (0, 0)), pl.BlockSpec((1, d), lambda i: (0, 0))],
        out_specs=[row, row],
        out_shape=[jax.ShapeDtypeStruct((rows, d), F32), jax.ShapeDtypeStruct((rows, d), BF16)],
        compiler_params=pltpu.CompilerParams(dimension_semantics=("parallel",), vmem_limit_bytes=VMEM_LIMIT),
        name="out_proj",
    )(m, x, w_out, g_ffn.reshape(1, d))


def _ffn_kernel(h_ref, x1_ref, wg_ref, wu_ref, wd_ref, gf_ref, y_ref, acc_ref):
    f = pl.program_id(1)

    @pl.when(f == 0)
    def _():
        acc_ref[...] = jnp.zeros_like(acc_ref)

    h = h_ref[...]
    a = _dot(h, wg_ref[...])
    a = (a * jax.nn.sigmoid(a)) * _dot(h, wu_ref[...])
    acc_ref[...] += _dot(a.astype(BF16), wd_ref[...])

    @pl.when(f == pl.num_programs(1) - 1)
    def _():
        y = x1_ref[...] + acc_ref[...]
        r = lax.rsqrt(jnp.mean(y * y, axis=-1, keepdims=True) + EPS)
        y_ref[...] = y * r * gf_ref[...]


def _ffn(h, x1, w_gate, w_up, w_down, g_final, tm, tf=512):
    rows, d = x1.shape
    dff = w_gate.shape[1]
    row = pl.BlockSpec((tm, d), lambda i, f: (i, 0))
    return pl.pallas_call(
        _ffn_kernel,
        grid=(rows // tm, dff // tf),
        in_specs=[row, row,
                  pl.BlockSpec((d, tf), lambda i, f: (0, f)),
                  pl.BlockSpec((d, tf), lambda i, f: (0, f)),
                  pl.BlockSpec((tf, d), lambda i, f: (f, 0)),
                  pl.BlockSpec((1, d), lambda i, f: (0, 0))],
        out_specs=row,
        out_shape=jax.ShapeDtypeStruct((rows, d), F32),
        scratch_shapes=[pltpu.VMEM((tm, d), F32)],
        compiler_params=pltpu.CompilerParams(
            dimension_semantics=("parallel", "arbitrary"), vmem_limit_bytes=VMEM_LIMIT),
        name="ffn",
    )(h, x1, w_gate, w_up, w_down, g_final.reshape(1, d))


def _alibi_slopes():
    return jnp.exp2(-8.0 * jnp.arange(1, N_HEADS + 1, dtype=F32) / N_HEADS)


def _pack_w_in(w_in, d_model, ssm_width):
    cuts = np.cumsum([Q_COLS, 6 * KV_COLS, 3 * N_HEADS, ssm_width]).tolist()
    wq, wkv, wgt, wu, wmg = jnp.split(w_in, cuts, axis=1)
    assert wq.shape[1] == _Q_T * IN_TN and wkv.shape[1] == _KV_T * IN_TN
    assert wu.shape[1] == _U_T * IN_TN and wmg.shape[1] == _MG_T * IN_TN
    wgt = jnp.pad(wgt, ((0, 0), (0, IN_TN - wgt.shape[1])))
    return jnp.concatenate([wq, wkv, wu, wmg, wgt], axis=1).astype(BF16)


def _tail(x, o, y_ssm, sg, lw, g_final, tm):
    m = _merge(o, y_ssm, lw['w_attn_out'], lw['w_glu'], sg, tm)
    x1, h = _outproj(m, x, lw['w_out'], lw['g_ffn'], min(tm, 256))
    return _ffn(h, x1, lw['w_gate'], lw['w_up'], lw['w_down'], g_final, tm)


def _layer_prompt(x, lw, slopes, g_out):
    b, t, d = x.shape
    rows = b * t
    q, kv, u, sg, gt = _in_proj(x.reshape(rows, d), lw['g_mix'], lw['w_in'], 512)

    npages = t // 128
    ident = jnp.arange(b * npages, dtype=jnp.int32).reshape(b, npages)
    pooled = _pool_pages(kv.reshape(b * npages, 128, 6 * KV_COLS), ident, lw['pool_mat'])
    kc, vc = _cmp_finish(pooled, lw['pool'], lw['pe'], lw['phi'])
    ncmp = kc.shape[1]
    kc4 = kc.reshape(b, ncmp, N_KV, HEAD_DIM).transpose(0, 2, 1, 3)
    vc4 = vc.reshape(b, ncmp, N_KV, HEAD_DIM).transpose(0, 2, 1, 3)

    q5 = q.reshape(b, t, N_KV, GROUP, HEAD_DIM).transpose(0, 2, 3, 1, 4)
    kv6 = kv.reshape(b, t, 6, N_KV, HEAD_DIM)
    kvt = kv6[:, :, 2:].astype(BF16).transpose(2, 0, 3, 1, 4)
    gates = gt[:, :3 * N_HEADS].reshape(b, t, 3, N_KV, GROUP).transpose(0, 3, 1, 2, 4).reshape(b, N_KV, t, 3 * GROUP)
    gates = jnp.pad(gates, ((0, 0), (0, 0), (0, 0), (0, 16 - 3 * GROUP)))
    o5 = _attn_prompt(slopes, q5, kc4, vc4, kvt, gates)
    o = o5.transpose(0, 3, 1, 2, 4).reshape(rows, Q_COLS)

    L = 16
    gp = lw['s5_16'][3].shape[1]
    h0 = jnp.zeros((b, gp), F32)
    y_ssm, hr, hi = _s5(u, h0, h0, lw['s5_16'], b, t // L, L)

    y = _tail(x.reshape(rows, d), o, y_ssm, sg, lw, g_out, 512).reshape(b, t, d)
    n_win = min(WINDOW, t)
    ng = gp // SSM_STATE
    return (y, kv6[:, :, :4], kv6[:, t - n_win:, 4:],
            hr.reshape(b, ng, SSM_STATE), hi.reshape(b, ng, SSM_STATE))


def _layer_sample(x, cache, page_table, win_buf, h_re, h_im, lw, slopes, g_out):
    b, t, d = x.shape
    rows = b * t
    npages = page_table.shape[1]
    past = npages * cache.shape[1]
    n_buf = win_buf.shape[1]
    assert past % CMP_STRIDE == 0 and t < CMP_STRIDE and past % SEL_BLOCK == 0 and t <= SEL_BLOCK
    q, kv, u, sg, gt = _in_proj(x.reshape(rows, d), lw['g_mix'], lw['w_in'], rows)

    pooled = _pool_pages(cache.reshape(cache.shape[0], cache.shape[1], -1), page_table, lw['pool_mat'])
    kc, vc = _cmp_finish(pooled, lw['pool'], lw['pe'], lw['phi'])

    nrows = N_HEADS * t
    eye = jnp.eye(N_KV, dtype=BF16)
    q5 = q.reshape(b, t, N_KV, GROUP, HEAD_DIM).transpose(0, 2, 3, 1, 4)
    qbd = (q5[:, :, :, :, None, :] * eye[None, :, None, None, :, None]).reshape(b, nrows, KV_COLS)
    maskbd = jnp.repeat(jnp.repeat(jnp.eye(N_KV, dtype=F32), GROUP * t, axis=0), HEAD_DIM, axis=1)
    slopecol = jnp.repeat(slopes, t).reshape(nrows, 1)
    g3 = gt[:, :3 * N_HEADS].reshape(b, t, 3, N_KV, GROUP).transpose(0, 3, 4, 1, 2).reshape(b, nrows, 3)
    gates_rows = jnp.pad(g3, ((0, 0), (0, 0), (0, 1)))
    kv3 = kv.reshape(b, t, 6 * KV_COLS)
    n_blk = -(-(past + t) // SEL_BLOCK)

    opart, sel_t = _attn_dec1(qbd, kc, vc, win_buf.reshape(b, n_buf, 2 * KV_COLS), kv3, gates_rows,
                              slopecol, maskbd, past, t, n_blk)
    bps = PAGES_PER_STEP * 128 // SEL_BLOCK
    nstep = npages // PAGES_PER_STEP
    nb_all = (nstep + 1) * bps
    sel = jnp.pad(sel_t[:, :n_blk], ((0, 0), (0, nb_all - n_blk), (0, 0)))
    sel = sel.reshape(b, nstep + 1, bps, N_KV, 1, t).transpose(0, 1, 3, 4, 5, 2)
    selrows = jnp.broadcast_to(sel, (b, nstep + 1, N_KV, GROUP, t, bps)).reshape(b, nstep + 1, nrows, bps)
    knew = jnp.pad(kv3[:, :, 2 * KV_COLS:4 * KV_COLS], ((0, 0), (0, 128 - t), (0, 0)))
    o_rows = _attn_dec2(cache.reshape(cache.shape[0], cache.shape[1], -1), page_table, qbd,
                        selrows.astype(BF16), slopecol, knew, opart, gates_rows, maskbd, past, t)
    o = (o_rows.reshape(b, N_KV, GROUP, t, HEAD_DIM).transpose(0, 3, 1, 2, 4).reshape(rows, Q_COLS).astype(BF16))

    gp = h_re.shape[1] * h_re.shape[2]
    y_ssm, hr, hi = _s5(u, h_re.reshape(b, gp), h_im.reshape(b, gp), lw['s5_dec'], b, 1, t)

    y = _tail(x.reshape(rows, d), o, y_ssm, sg, lw, g_out, rows).reshape(b, t, d)
    kv6 = kv.reshape(b, t, 6, N_KV, HEAD_DIM)
    kvw = jnp.concatenate([win_buf, kv6[:, :, 4:]], axis=1)
    n_keep = min(WINDOW, n_buf + t)
    return y, kv6[:, :, :4], kvw[:, n_buf + t - n_keep:], hr.reshape(h_re.shape), hi.reshape(h_im.shape)


def kernel(x_prompt, x_sample, cache_kv, state_win, state_ssm_re, state_ssm_im, page_table, g_mix, w_in, w_cmp_pe, w_cmp_pool, w_cmp_phi, w_attn_out, ssm_lam_re, ssm_lam_im, ssm_log_dt, ssm_b_re, ssm_b_im, ssm_c_re, ssm_c_im, ssm_d, w_glu, w_out, g_ffn, w_gate, w_up, w_down, g_final):
    depth = g_mix.shape[0]
    assert depth == 1, "final norm is fused into the last layer's FFN; one layer supported"
    d_model = x_prompt.shape[-1]
    ssm_width = ssm_d.shape[-1]
    tdec = x_sample.shape[1]
    slopes = _alibi_slopes()
    outs = [[] for _ in range(8)]
    xp, xs = x_prompt, x_sample
    for l in range(depth):
        s5p = (ssm_lam_re[l], ssm_lam_im[l], ssm_log_dt[l], ssm_b_re[l], ssm_b_im[l], ssm_c_re[l], ssm_c_im[l], ssm_d[l])
        lw = dict(
            g_mix=g_mix[l], w_in=_pack_w_in(w_in[l], d_model, ssm_width),
            pool=w_cmp_pool[l], pe=w_cmp_pe[l], phi=w_cmp_phi[l], pool_mat=_pool_matrix(w_cmp_pool[l]),
            w_attn_out=w_attn_out[l].astype(BF16), w_glu=w_glu[l].astype(BF16), w_out=w_out[l].astype(BF16),
            g_ffn=g_ffn[l], w_gate=w_gate[l].astype(BF16), w_up=w_up[l].astype(BF16), w_down=w_down[l].astype(BF16),
            s5_16=_s5_weights(*s5p, 16), s5_dec=_s5_weights(*s5p, tdec),
        )
        xp, kvp, winp, hrp, hip = _layer_prompt(xp, lw, slopes, g_final)
        xs, kvs, wins, hrs, his = _layer_sample(xs, cache_kv[l], page_table, state_win[l], state_ssm_re[l],
                                                state_ssm_im[l], lw, slopes, g_final)
        for lst, v in zip(outs, (kvp, winp, hrp, hip, kvs, wins, hrs, his)):
            lst.append(v)
    st = [jnp.stack(v) for v in outs]
    return (xp, xs, st[0], st[1], st[2], st[3], st[4], st[5], st[6], st[7])
```

```python
import functools
import math

import numpy as np
import jax
import jax.numpy as jnp
from jax import lax
from jax.experimental import pallas as pl
from jax.experimental.pallas import tpu as pltpu

F32 = jnp.float32
BF16 = jnp.bfloat16

N_HEADS = 16
HEAD_DIM = 64
N_KV = 4
GROUP = N_HEADS // N_KV
CMP_LEN = 32
CMP_STRIDE = 16
SEL_BLOCK = 64
SEL_TOP = 16
WINDOW = 512
SSM_GROUP = 16
SSM_STATE = 64
EPS = 1e-6
NEG = -1e30
FORCE_SCORE = 1e4

KV_COLS = N_KV * HEAD_DIM
Q_COLS = N_HEADS * HEAD_DIM
PAGE = 128
CMP_PER_PAGE = PAGE // CMP_STRIDE
PAGES_PER_STEP = 16
S5_L = 8
S5_SLAB = 128 // SSM_GROUP
VMEM_LIMIT = 56 * 1024 * 1024


def _dot(a, b):
    return jnp.dot(a, b, preferred_element_type=F32)


def _dot_nt(a, b):
    return lax.dot_general(a, b, (((1,), (1,)), ((), ())), preferred_element_type=F32)


def _split(a):
    hi = a.astype(BF16)
    lo = (a - hi.astype(F32)).astype(BF16)
    return hi, lo


def _dot3(a, b):
    ah, al = _split(a)
    bh, bl = _split(b)
    return _dot(ah, bh) + _dot(ah, bl) + _dot(al, bh)


def _dot_nt_hl(w_bf16, a):
    ah, al = _split(a)
    return _dot_nt(w_bf16, ah) + _dot_nt(w_bf16, al)


def _gelu(x):
    return 0.5 * x * (1.0 + jnp.tanh(math.sqrt(2.0 / math.pi) * (x + 0.044715 * (x * x * x))))


def _iota(shape, dim):
    return lax.broadcasted_iota(jnp.int32, shape, dim)


def _log2(n):
    assert n > 0 and n & (n - 1) == 0, n
    return n.bit_length() - 1


def _div_pow2(x, n):
    return jnp.right_shift(x, _log2(n))


def _mod_pow2(x, n):
    return jnp.bitwise_and(x, (1 << _log2(n)) - 1)


def _softmax_rows(s):
    m = jnp.max(s, axis=-1, keepdims=True)
    e = jnp.exp(s - m)
    return e / jnp.sum(e, axis=-1, keepdims=True)


IN_TN = 512
_Q_T, _U_T, _MG_T, _KV_T = 2, 2, 8, 3
_U_0 = _Q_T
_MG_0 = _U_0 + _U_T
_GT_0 = _MG_0 + _MG_T
_KV_0 = _GT_0 + 1
IN_TILES = _KV_0 + _KV_T


def _inproj_kernel(x_ref, g_ref, w_ref, wkv_ref, q_ref, u_ref, sg_ref, gt_ref, kv_ref, xn_ref, *, kv_feature_major):
    j = pl.program_id(1)

    @pl.when(j == 0)
    def _():
        x = x_ref[...]
        r = lax.rsqrt(jnp.mean(x * x, axis=-1, keepdims=True) + EPS)
        xn_ref[...] = (x * r * g_ref[...]).astype(BF16)

    @pl.when(j < _KV_0)
    def _():
        z = _dot(xn_ref[...], w_ref[...])

        @pl.when(j < _U_0)
        def _():
            q_ref[...] = (z * (HEAD_DIM ** -0.5)).astype(BF16)

        @pl.when((j >= _U_0) & (j < _MG_0))
        def _():
            for s in range(IN_TN // 128):
                u_ref[s] = z[:, 128 * s:128 * (s + 1)].reshape(u_ref.shape[1:])

        @pl.when((j >= _MG_0) & (j < _GT_0))
        def _():
            sg_ref[...] = jax.nn.sigmoid(z).astype(BF16)

        @pl.when(j == _GT_0)
        def _():
            gt_ref[...] = jax.nn.sigmoid(z)

    @pl.when(j >= _KV_0)
    def _():
        if kv_feature_major:
            kv_ref[0] = _dot_nt(wkv_ref[...], xn_ref[...])
        else:
            kv_ref[...] = _dot_nt(xn_ref[...], wkv_ref[...])


def _in_proj(x, g_mix, w_cat, w_kv_t, tm, seq_len=None):
    rows, d = x.shape
    feature_major = seq_len is not None

    def col(lo, n):
        return lambda i, j: (i, jnp.clip(j - lo, 0, n - 1))

    if feature_major:
        tiles_per_seq = seq_len // tm
        kv_spec = pl.BlockSpec((1, IN_TN, tm),
                               lambda i, j: (i // tiles_per_seq, jnp.clip(j - _KV_0, 0, _KV_T - 1), i % tiles_per_seq))
        kv_shape = jax.ShapeDtypeStruct((rows // seq_len, _KV_T * IN_TN, seq_len), F32)
    else:
        kv_spec = pl.BlockSpec((tm, IN_TN), col(_KV_0, _KV_T))
        kv_shape = jax.ShapeDtypeStruct((rows, _KV_T * IN_TN), F32)
    n_slab = IN_TN // 128
    return pl.pallas_call(
        functools.partial(_inproj_kernel, kv_feature_major=feature_major),
        grid=(rows // tm, IN_TILES),
        in_specs=[
            pl.BlockSpec((tm, d), lambda i, j: (i, 0)),
            pl.BlockSpec((1, d), lambda i, j: (0, 0)),
            pl.BlockSpec((d, IN_TN), lambda i, j: (0, jnp.minimum(j, _KV_0 - 1))),
            pl.BlockSpec((IN_TN, d), lambda i, j: (jnp.clip(j - _KV_0, 0, _KV_T - 1), 0)),
        ],
        out_specs=[
            pl.BlockSpec((tm, IN_TN), col(0, _Q_T)),
            pl.BlockSpec((n_slab, tm // S5_L, S5_L, 128), lambda i, j: (jnp.clip(j - _U_0, 0, _U_T - 1), i, 0, 0)),
            pl.BlockSpec((tm, IN_TN), col(_MG_0, _MG_T)),
            pl.BlockSpec((tm, IN_TN), lambda i, j: (i, 0)),
            kv_spec,
        ],
        out_shape=[
            jax.ShapeDtypeStruct((rows, _Q_T * IN_TN), BF16),
            jax.ShapeDtypeStruct((_U_T * n_slab, rows // S5_L, S5_L, 128), F32),
            jax.ShapeDtypeStruct((rows, _MG_T * IN_TN), BF16),
            jax.ShapeDtypeStruct((rows, IN_TN), F32),
            kv_shape,
        ],
        scratch_shapes=[pltpu.VMEM((tm, d), BF16)],
        compiler_params=pltpu.CompilerParams(
            dimension_semantics=("parallel", "arbitrary"), vmem_limit_bytes=VMEM_LIMIT),
        name="in_proj",
    )(x, g_mix.reshape(1, d), w_cat, w_kv_t)


def _pack_w_in(w_in, ssm_width):
    cuts = np.cumsum([Q_COLS, 6 * KV_COLS, 3 * N_HEADS, ssm_width]).tolist()
    wq, wkv, wgt, wu, wmg = jnp.split(w_in, cuts, axis=1)
    assert wq.shape[1] == _Q_T * IN_TN and wkv.shape[1] == _KV_T * IN_TN
    assert wu.shape[1] == _U_T * IN_TN and wmg.shape[1] == _MG_T * IN_TN and IN_TN == N_KV * 128
    d = w_in.shape[0]
    wgt = wgt.reshape(d, 3, N_KV, GROUP).transpose(0, 2, 1, 3).reshape(d, N_KV, 3 * GROUP)
    wgt = jnp.pad(wgt, ((0, 0), (0, 0), (0, 128 - 3 * GROUP))).reshape(d, IN_TN)
    w_cat = jnp.concatenate([wq, wu, wmg, wgt], axis=1).astype(BF16)
    return w_cat, wkv.T.astype(BF16)


def _pool_kernel(*refs, npg, n_tables):
    refs = refs[n_tables:]
    prev_ref = refs[0]
    page_refs = refs[1:npg + 1]
    mk_ref, mv_ref, mprev_ref, out_ref = refs[npg + 1:]
    x = jnp.concatenate([page_refs[j][0] for j in range(npg)], axis=1).astype(BF16)
    xp = prev_ref[0].astype(BF16)
    has_prev = pl.program_id(1) > 0
    for half, m_ref in ((0, mk_ref), (1, mv_ref)):
        rows = slice(half * KV_COLS, (half + 1) * KV_COLS)
        head = _dot(xp[rows], mprev_ref[half])
        out_ref[0, rows, :] = _dot(x[rows], m_ref[...]) + jnp.where(has_prev, head, 0.0)


def _pool_matrices(pool, npg):
    nblk = npg * CMP_PER_PAGE
    tok = np.arange(npg * PAGE)[:, None]
    n = np.arange(nblk)[None, :]
    idx = tok - CMP_STRIDE * (n - 1)
    ok = (idx >= 0) & (idx < CMP_LEN)
    idx = np.clip(idx, 0, CMP_LEN - 1)
    ptok = np.arange(PAGE)[:, None]
    pidx = ptok - (PAGE - CMP_STRIDE)
    pok = (pidx >= 0) & (n == 0)
    pidx = np.broadcast_to(np.clip(pidx, 0, CMP_LEN - 1), pok.shape)
    ms = [jnp.where(ok, pool[s][idx], 0.0).astype(BF16) for s in range(2)]
    mprev = jnp.stack([jnp.where(pok, pool[s][pidx], 0.0) for s in range(2)]).astype(BF16)
    return ms[0], ms[1], mprev


def _pool_pages(pages, page_index, tables, nseq, npages, mats):
    npg = PAGES_PER_STEP
    nstep = npages // npg
    nblk = npg * CMP_PER_PAGE
    mk, mv, mprev = mats

    def spec(off):
        return pl.BlockSpec((1, 2 * KV_COLS, PAGE),
                            lambda b, s, *t: page_index(b, jnp.maximum(s * npg + off, 0), *t))

    const = lambda shape: pl.BlockSpec(shape, lambda b, s, *t: (0,) * len(shape))
    return pl.pallas_call(
        functools.partial(_pool_kernel, npg=npg, n_tables=len(tables)),
        grid_spec=pltpu.PrefetchScalarGridSpec(
            num_scalar_prefetch=len(tables),
            grid=(nseq, nstep),
            in_specs=[spec(j) for j in range(-1, npg)] + [const(mk.shape), const(mv.shape), const(mprev.shape)],
            out_specs=pl.BlockSpec((1, 2 * KV_COLS, nblk), lambda b, s, *t: (b, 0, s)),
        ),
        out_shape=jax.ShapeDtypeStruct((nseq, 2 * KV_COLS, npages * CMP_PER_PAGE), F32),
        compiler_params=pltpu.CompilerParams(
            dimension_semantics=("parallel", "arbitrary"), vmem_limit_bytes=VMEM_LIMIT),
        name="cmp_pool",
    )(*tables, *([pages] * (npg + 1)), mk, mv, mprev)


def _cmp_finish_kernel(pooled_ref, poolw_ref, pe_ref, phi_ref, kc_ref, vc_ref):
    bias = jnp.sum(poolw_ref[...] * pe_ref[...], axis=1, keepdims=True)
    x = _gelu(pooled_ref[0] + bias)
    kc_ref[0] = _dot3(phi_ref[0], x[:KV_COLS]).astype(BF16)
    vc_ref[0] = _dot3(phi_ref[1], x[KV_COLS:]).astype(BF16)


def _cmp_finish(pooled, pool, pe, phi):
    nseq, _, n = pooled.shape
    poolw = jnp.concatenate([jnp.broadcast_to(pool[s][None, :], (KV_COLS, CMP_LEN)) for s in range(2)], axis=0)
    pe_t = jnp.concatenate([jnp.tile(pe[s].T, (N_KV, 1)) for s in range(2)], axis=0)
    phi_bd = jnp.stack([jnp.kron(jnp.eye(N_KV, dtype=F32), phi[s].T) for s in range(2)])
    spec = pl.BlockSpec((1, KV_COLS, n), lambda b: (b, 0, 0))
    return pl.pallas_call(
        _cmp_finish_kernel,
        grid=(nseq,),
        in_specs=[
            pl.BlockSpec((1, 2 * KV_COLS, n), lambda b: (b, 0, 0)),
            pl.BlockSpec((2 * KV_COLS, CMP_LEN), lambda b: (0, 0)),
            pl.BlockSpec((2 * KV_COLS, CMP_LEN), lambda b: (0, 0)),
            pl.BlockSpec((2, KV_COLS, KV_COLS), lambda b: (0, 0, 0)),
        ],
        out_specs=[spec, spec],
        out_shape=[jax.ShapeDtypeStruct((nseq, KV_COLS, n), BF16)] * 2,
        compiler_params=pltpu.CompilerParams(dimension_semantics=("parallel",)),
        name="cmp_finish",
    )(pooled, poolw, pe_t, phi_bd)


ATT_TQ = 128
ATT_TK = 512
NBLK_PAD = 128


def _attn_prompt_kernel(slopes_ref, q_ref, kc_ref, vc_ref, ks_ref, vs_ref, kw_ref, vw_ref, gt_ref,
                        ovt_ref, e_ref, o_ref, kvb_ref, m_ref, l_ref, acc_ref, *, tq, tk, n_sel, n_top):
    g = pl.program_id(1)
    qi = pl.program_id(2)
    q0 = qi * tq

    @pl.when(qi == 0)
    def _():
        for slot, ref in enumerate((ks_ref, vs_ref, kw_ref, vw_ref)):
            kvb_ref[slot] = ref[0].astype(BF16)

    qf = q_ref[...]
    q = jnp.concatenate([qf[:, n * HEAD_DIM:(n + 1) * HEAD_DIM] for n in range(GROUP)], axis=0)
    tcol = q0 + _iota((tq, 1), 0)
    qend = (q0 + tq).astype(F32)
    slope = [slopes_ref[GROUP * g + n] for n in range(GROUP)]
    rows = [slice(n * tq, (n + 1) * tq) for n in range(GROUP)]

    ncmp = kc_ref.shape[2]
    nrow = _iota((1, ncmp), 1)
    cend = CMP_STRIDE * nrow + (CMP_STRIDE - 1)
    valid = (nrow >= 1) & (cend <= tcol)
    crel = cend.astype(F32) - qend
    sc = _dot(q, kc_ref[0])
    ps = []
    for n in range(GROUP):
        s = jnp.where(valid, sc[rows[n]] + slope[n] * crel, NEG)
        ps.append(jnp.where(valid, _softmax_rows(s), 0.0))
    o_c = _dot_nt(jnp.concatenate(ps, axis=0).astype(BF16), vc_ref[0])

    psum = (ps[0] + ps[1]) + (ps[2] + ps[3])
    imp = _dot_nt_hl(ovt_ref[...], psum)[0:n_sel]
    blk = _iota((n_sel, 1), 0)
    cur = _div_pow2(q0 + _iota((1, tq), 1), SEL_BLOCK)
    forced = (blk == 0) | (blk == cur) | (blk == cur - 1)
    imp = jnp.where(forced, FORCE_SCORE, imp)
    imp = jnp.where(blk <= cur, imp, -1.0)
    cnt = jnp.zeros((n_sel, tq), F32)
    for j in range(n_sel):
        vj = imp[j:j + 1, :]
        beats = (vj > imp) | ((vj == imp) & (blk > j))
        cnt = cnt + jnp.where(beats, 1.0, 0.0)
    sel_t = jnp.where(cnt < n_top, 1.0, 0.0)
    sel_t = jnp.concatenate([sel_t, jnp.zeros((NBLK_PAD - n_sel, tq), F32)], axis=0)
    sel = sel_t.T.astype(BF16)

    m_ref[...] = jnp.full_like(m_ref, NEG)
    l_ref[...] = jnp.zeros_like(l_ref)
    acc_ref[...] = jnp.zeros_like(acc_ref)

    def body(kt, carry):
        k0 = pl.multiple_of(kt * tk, tk)
        s = _dot(q, kvb_ref[0, :, pl.ds(k0, tk)])
        v_t = kvb_ref[1, :, pl.ds(k0, tk)]
        kpos = k0 + _iota((1, tk), 1)
        chosen = _dot(sel, e_ref[:, pl.ds(k0, tk)])
        allowed = (chosen > 0.5) & (kpos <= tcol)
        bm = jnp.where(allowed, 0.0, NEG)
        krel = kpos.astype(F32) - qend
        for n in range(GROUP):
            sn = s[rows[n]] + slope[n] * krel + bm
            m_old = m_ref[rows[n]]
            m_new = jnp.maximum(m_old, jnp.max(sn, axis=-1, keepdims=True))
            alpha = jnp.exp(m_old - m_new)
            p = jnp.exp(sn - m_new)
            l_ref[rows[n]] = alpha * l_ref[rows[n]] + jnp.sum(p, axis=-1, keepdims=True)
            acc_ref[rows[n]] = alpha * acc_ref[rows[n]] + _dot_nt(p.astype(BF16), v_t)
            m_ref[rows[n]] = m_new
        return carry

    lax.fori_loop(0, (q0 + tq + tk - 1) // tk, body, 0)
    o_s = acc_ref[...] / l_ref[...]

    span = WINDOW + tq
    w0 = pl.multiple_of(jnp.maximum(q0 - WINDOW, 0), tq)
    sw = _dot(q, kvb_ref[2, :, pl.ds(w0, span)])
    kposw = w0 + _iota((1, span), 1)
    dist = tcol - kposw
    bmw = jnp.where((dist >= 0) & (dist < WINDOW), 0.0, NEG)
    krelw = kposw.astype(F32) - qend
    pw = [_softmax_rows(sw[rows[n]] + slope[n] * krelw + bmw) for n in range(GROUP)]
    o_w = _dot_nt(jnp.concatenate(pw, axis=0).astype(BF16), kvb_ref[3, :, pl.ds(w0, span)])

    gt = gt_ref[...]
    outs = []
    for n in range(GROUP):
        outs.append(gt[:, n:n + 1] * o_c[rows[n]] + gt[:, GROUP + n:GROUP + n + 1] * o_s[rows[n]]
                    + gt[:, 2 * GROUP + n:2 * GROUP + n + 1] * o_w[rows[n]])
    o_ref[...] = jnp.concatenate(outs, axis=1).astype(BF16)


def _overlap_t(n_blk_rows, n_cmp):
    s = np.arange(n_blk_rows)[:, None]
    n = np.arange(n_cmp)[None, :]
    r = SEL_BLOCK // CMP_STRIDE
    return jnp.asarray(((n >= r * s) & (n <= r * s + r)).astype(np.float32), dtype=BF16)


def _block_expand(n_blk_rows, n_keys):
    s = np.arange(n_blk_rows)[:, None]
    k = np.arange(n_keys)[None, :]
    return jnp.asarray((k // SEL_BLOCK == s).astype(np.float32), dtype=BF16)


def _attn_prompt(slopes, q, kc_t, vc_t, kv_t, gates, nbatch, seq):
    ncmp = kc_t.shape[2]
    tq, tk = ATT_TQ, ATT_TK
    nq = seq // tq
    n_sel = seq // SEL_BLOCK
    ovt = _overlap_t(NBLK_PAD, ncmp)
    emat = _block_expand(NBLK_PAD, seq)
    per_slot = KV_COLS // HEAD_DIM

    def kv_spec(slot):
        return pl.BlockSpec((1, HEAD_DIM, seq), lambda b, g, i: (b, slot * per_slot + g, 0))

    cmp_spec = pl.BlockSpec((1, HEAD_DIM, ncmp), lambda b, g, i: (b, g, 0))
    q_spec = pl.BlockSpec((tq, GROUP * HEAD_DIM), lambda b, g, i: (b * nq + i, g))
    return pl.pallas_call(
        functools.partial(_attn_prompt_kernel, tq=tq, tk=tk, n_sel=n_sel, n_top=min(SEL_TOP, n_sel)),
        grid=(nbatch, N_KV, nq),
        in_specs=[
            pl.BlockSpec(memory_space=pltpu.SMEM),
            q_spec, cmp_spec, cmp_spec, kv_spec(2), kv_spec(3), kv_spec(4), kv_spec(5),
            pl.BlockSpec((tq, 128), lambda b, g, i: (b * nq + i, g)),
            pl.BlockSpec((NBLK_PAD, ncmp), lambda b, g, i: (0, 0)),
            pl.BlockSpec((NBLK_PAD, seq), lambda b, g, i: (0, 0)),
        ],
        out_specs=q_spec,
        out_shape=jax.ShapeDtypeStruct(q.shape, BF16),
        scratch_shapes=[
            pltpu.VMEM((4, HEAD_DIM, seq), BF16),
            pltpu.VMEM((GROUP * tq, 1), F32),
            pltpu.VMEM((GROUP * tq, 1), F32),
            pltpu.VMEM((GROUP * tq, HEAD_DIM), F32),
        ],
        compiler_params=pltpu.CompilerParams(
            dimension_semantics=("parallel", "parallel", "arbitrary"), vmem_limit_bytes=VMEM_LIMIT),
        name="attn_prompt",
    )(slopes, q, kc_t, vc_t, kv_t, kv_t, kv_t, kv_t, gates, ovt, emat)


def _diag_blocks(o, maskbd):
    o = o * maskbd
    return (o[:, 0:64] + o[:, 64:128]) + (o[:, 128:192] + o[:, 192:256])


def _attn_dec1_kernel(qbd_ref, kc_ref, vc_ref, win_ref, kvnew_ref, gates_ref, slope_ref, ovt_ref,
                      maskbd_ref, opart_ref, sel_ref, *, past, tdec, n_blk, n_top):
    qb = qbd_ref[0]
    nrows = qb.shape[0]
    tcol = past + _mod_pow2(_iota((nrows, 1), 0), tdec)
    pref = float(past + tdec)
    slope = slope_ref[...]
    maskbd = maskbd_ref[...]

    ncmp = kc_ref.shape[2]
    nrow = _iota((1, ncmp), 1)
    cend = CMP_STRIDE * nrow + (CMP_STRIDE - 1)
    valid = (nrow >= 1) & (cend <= tcol)
    s = _dot(qb, kc_ref[0])
    s = jnp.where(valid, s + slope * (cend.astype(F32) - pref), NEG)
    p = jnp.where(valid, _softmax_rows(s), 0.0)
    o_c = _diag_blocks(_dot_nt(p.astype(BF16), vc_ref[0]), maskbd)

    per_g = GROUP * tdec
    psum = jnp.concatenate(
        [sum(p[gi * per_g + n * tdec: gi * per_g + (n + 1) * tdec] for n in range(GROUP)) for gi in range(N_KV)],
        axis=0)
    imp = _dot_nt_hl(ovt_ref[...], psum)
    nb_pad = imp.shape[0]
    blk = _iota((nb_pad, 1), 0)
    cur = _div_pow2(past + _mod_pow2(_iota((1, N_KV * tdec), 1), tdec), SEL_BLOCK)
    forced = (blk == 0) | (blk == cur) | (blk == cur - 1)
    imp = jnp.where(forced, FORCE_SCORE, imp)
    imp = jnp.where(blk <= cur, imp, -1.0)
    imp = jnp.where(blk < n_blk, imp, -2.0)
    sel = jnp.zeros(imp.shape, F32)
    for _ in range(n_top):
        mx = jnp.max(imp, axis=0, keepdims=True)
        first = jnp.min(jnp.where(imp == mx, blk, nb_pad), axis=0, keepdims=True)
        pick = blk == first
        sel = jnp.where(pick, 1.0, sel)
        imp = jnp.where(pick, -jnp.inf, imp)
    sel_ref[0] = sel

    nbuf = win_ref.shape[2]
    knew = kvnew_ref[0]
    zpad = jnp.zeros((PAGE - tdec, KV_COLS), F32)
    kn = jnp.concatenate([knew[:, 4 * KV_COLS:5 * KV_COLS], zpad], axis=0).astype(BF16)
    vn = jnp.concatenate([knew[:, 5 * KV_COLS:], zpad], axis=0).astype(BF16)
    sw = jnp.concatenate([_dot(qb, win_ref[0, :KV_COLS].astype(BF16)), _dot_nt(qb, kn)], axis=1)
    idx = _iota((1, nbuf + PAGE), 1)
    kposw = past - nbuf + idx
    dist = tcol - kposw
    validw = (dist >= 0) & (dist < WINDOW) & (kposw >= 0) & (idx < nbuf + tdec)
    sw = jnp.where(validw, sw + slope * (kposw.astype(F32) - pref), NEG)
    pw = _softmax_rows(sw).astype(BF16)
    o_w = _dot_nt(pw[:, :nbuf], win_ref[0, KV_COLS:].astype(BF16)) + _dot(pw[:, nbuf:], vn)
    o_w = _diag_blocks(o_w, maskbd)

    gates = gates_ref[0]
    opart_ref[0] = gates[:, 0:1] * o_c + gates[:, 2:3] * o_w


def _attn_dec1(qbd, kc_t, vc_t, win_t, kvnew, gates_rows, slopecol, maskbd, past, tdec, n_blk):
    nseq, nrows, _ = qbd.shape
    ncmp = kc_t.shape[2]
    nb_pad = -(-n_blk // 8) * 8
    ovt = _overlap_t(nb_pad, ncmp)
    nbuf = win_t.shape[2]
    return pl.pallas_call(
        functools.partial(_attn_dec1_kernel, past=past, tdec=tdec, n_blk=n_blk, n_top=min(SEL_TOP, n_blk)),
        grid=(nseq,),
        in_specs=[
            pl.BlockSpec((1, nrows, KV_COLS), lambda b: (b, 0, 0)),
            pl.BlockSpec((1, KV_COLS, ncmp), lambda b: (b, 0, 0)),
            pl.BlockSpec((1, KV_COLS, ncmp), lambda b: (b, 0, 0)),
            pl.BlockSpec((1, 2 * KV_COLS, nbuf), lambda b: (b, 0, 0)),
            pl.BlockSpec((1, tdec, 6 * KV_COLS), lambda b: (b, 0, 0)),
            pl.BlockSpec((1, nrows, 4), lambda b: (b, 0, 0)),
            pl.BlockSpec((nrows, 1), lambda b: (0, 0)),
            pl.BlockSpec((nb_pad, ncmp), lambda b: (0, 0)),
            pl.BlockSpec((nrows, KV_COLS), lambda b: (0, 0)),
        ],
        out_specs=[
            pl.BlockSpec((1, nrows, HEAD_DIM), lambda b: (b, 0, 0)),
            pl.BlockSpec((1, nb_pad, N_KV * tdec), lambda b: (b, 0, 0)),
        ],
        out_shape=[
            jax.ShapeDtypeStruct((nseq, nrows, HEAD_DIM), F32),
            jax.ShapeDtypeStruct((nseq, nb_pad, N_KV * tdec), F32),
        ],
        compiler_params=pltpu.CompilerParams(dimension_semantics=("parallel",), vmem_limit_bytes=VMEM_LIMIT),
        name="attn_dec_select",
    )(qbd, kc_t, vc_t, win_t, kvnew, gates_rows, slopecol, ovt, maskbd)


def _attn_dec2_kernel(pt_ref, *refs, npg, past, tdec):
    page_refs = refs[:npg]
    (qbd_ref, selrow_ref, sellast_ref, e_ref, slope_ref, knew_ref, opart_ref, gates_ref, maskbd_ref,
     o_ref, m_ref, l_ref, acc_ref) = refs[npg:]
    step = pl.program_id(1)
    nkeys = npg * PAGE
    qb = qbd_ref[0]
    nrows = qb.shape[0]
    pref = float(past + tdec)
    slope = slope_ref[...]

    @pl.when(step == 0)
    def _():
        m_ref[...] = jnp.full_like(m_ref, NEG)
        l_ref[...] = jnp.zeros_like(l_ref)
        acc_ref[...] = jnp.zeros_like(acc_ref)

    def update(s, pv_fn):
        m_old = m_ref[...]
        m_new = jnp.maximum(m_old, jnp.max(s, axis=-1, keepdims=True))
        alpha = jnp.exp(m_old - m_new)
        p = jnp.exp(s - m_new)
        l_ref[...] = alpha * l_ref[...] + jnp.sum(p, axis=-1, keepdims=True)
        acc_ref[...] = alpha * acc_ref[...] + pv_fn(p.astype(BF16))
        m_ref[...] = m_new

    s = jnp.concatenate([_dot(qb, page_refs[j][0, :KV_COLS].astype(BF16)) for j in range(npg)], axis=1)
    kpos = step * nkeys + _iota((1, nkeys), 1)
    chosen = _dot(selrow_ref[0, 0], e_ref[...])
    s = s + slope * (kpos.astype(F32) - pref) + jnp.where(chosen > 0.5, 0.0, NEG)

    def pv_pages(p):
        acc = _dot_nt(p[:, 0:PAGE], page_refs[0][0, KV_COLS:].astype(BF16))
        for j in range(1, npg):
            acc = acc + _dot_nt(p[:, PAGE * j:PAGE * (j + 1)], page_refs[j][0, KV_COLS:].astype(BF16))
        return acc

    update(s, pv_pages)

    @pl.when(step == pl.num_programs(1) - 1)
    def _():
        tcol = past + _mod_pow2(_iota((nrows, 1), 0), tdec)
        kn = knew_ref[0]
        kposn = past + _iota((1, PAGE), 1)
        sn = _dot_nt(qb, kn[:, :KV_COLS].astype(BF16))
        ok = (sellast_ref[0, 0][:, 0:1].astype(F32) > 0.5) & (kposn <= tcol)
        sn = sn + slope * (kposn.astype(F32) - pref) + jnp.where(ok, 0.0, NEG)
        update(sn, lambda p: _dot(p, kn[:, KV_COLS:].astype(BF16)))
        o_s = _diag_blocks(acc_ref[...] / l_ref[...], maskbd_ref[...])
        o_ref[0] = opart_ref[0] + gates_ref[0][:, 1:2] * o_s


def _attn_dec2(cache_t, page_table, qbd, selrows, slopecol, knew, opart, gates_rows, maskbd, past, tdec):
    nseq, npages = page_table.shape
    npg = PAGES_PER_STEP
    nstep = npages // npg
    nrows = qbd.shape[1]
    blk_per_step = npg * PAGE // SEL_BLOCK
    emat = _block_expand(blk_per_step, npg * PAGE)
    page_specs = [
        pl.BlockSpec((1, 2 * KV_COLS, PAGE), lambda b, s, pt, j=j: (pt[b, s * npg + j], 1, 0))
        for j in range(npg)
    ]
    per_seq = lambda shape: pl.BlockSpec((1,) + shape, lambda b, s, pt: (b,) + (0,) * len(shape))
    const = lambda shape: pl.BlockSpec(shape, lambda b, s, pt: (0,) * len(shape))
    return pl.pallas_call(
        functools.partial(_attn_dec2_kernel, npg=npg, past=past, tdec=tdec),
        grid_spec=pltpu.PrefetchScalarGridSpec(
            num_scalar_prefetch=1,
            grid=(nseq, nstep),
            in_specs=page_specs + [
                per_seq((nrows, KV_COLS)),
                pl.BlockSpec((1, 1, nrows, blk_per_step), lambda b, s, pt: (b, s, 0, 0)),
                pl.BlockSpec((1, 1, nrows, blk_per_step), lambda b, s, pt: (b, nstep, 0, 0)),
                const((blk_per_step, npg * PAGE)),
                const((nrows, 1)),
                per_seq((PAGE, 2 * KV_COLS)),
                per_seq((nrows, HEAD_DIM)),
                per_seq((nrows, 4)),
                const((nrows, KV_COLS)),
            ],
            out_specs=per_seq((nrows, HEAD_DIM)),
            scratch_shapes=[
                pltpu.VMEM((nrows, 1), F32),
                pltpu.VMEM((nrows, 1), F32),
                pltpu.VMEM((nrows, KV_COLS), F32),
            ],
        ),
        out_shape=jax.ShapeDtypeStruct((nseq, nrows, HEAD_DIM), F32),
        compiler_params=pltpu.CompilerParams(
            dimension_semantics=("parallel", "arbitrary"), vmem_limit_bytes=VMEM_LIMIT),
        name="attn_dec_selected",
    )(page_table, *([cache_t] * npg), qbd, selrows, selrows, emat, slopecol, knew, opart, gates_rows, maskbd)


def _s5_weights(lam_re, lam_im, log_dt, b_re, b_im, c_re, c_im, d_skip):
    hp = lax.Precision.HIGHEST
    L = S5_L
    ng, p = lam_re.shape
    c = SSM_GROUP
    ns, gs = ng // S5_SLAB, S5_SLAB
    dt = jnp.exp(log_dt)[:, None]
    lr, li = lam_re, lam_im
    mag = jnp.exp(lr * dt)
    ar = mag * jnp.cos(li * dt)
    ai = mag * jnp.sin(li * dt)
    den = lr * lr + li * li
    fr = ((ar - 1.0) * lr + ai * li) / den
    fi = (ai * lr - (ar - 1.0) * li) / den
    bbr = fr[..., None] * b_re - fi[..., None] * b_im
    bbi = fr[..., None] * b_im + fi[..., None] * b_re
    j = jnp.arange(L + 1, dtype=F32)[:, None, None]
    pmag = jnp.exp(j * (lr * dt))
    pr = pmag * jnp.cos(j * (li * dt))
    pi = pmag * jnp.sin(j * (li * dt))
    abr = pr[..., None] * bbr - pi[..., None] * bbi
    abi = pr[..., None] * bbi + pi[..., None] * bbr
    kd = (jnp.einsum('gcp,jgpk->jgck', c_re, abr, precision=hp)
          - jnp.einsum('gcp,jgpk->jgck', c_im, abi, precision=hp))
    kd = kd.at[0].add(d_skip.reshape(ng, c)[:, :, None] * jnp.eye(c, dtype=F32))
    eye = jnp.eye(gs, dtype=F32)
    tau = np.arange(L)
    delta = tau[None, :] - tau[:, None]
    wt = jnp.where((delta >= 0)[:, :, None, None, None], kd[np.clip(delta, 0, L)], 0.0)
    wt = jnp.einsum('stjgck,gh->jsgkthc', wt.reshape(L, L, ns, gs, c, c), eye).reshape(ns, L * gs * c, L * gs * c)
    ab = jnp.stack([abr[L - 1 - tau], abi[L - 1 - tau]])
    wp = jnp.einsum('rsjgpk,gh->jsgkrhp', ab.reshape(2, L, ns, gs, p, c), eye).reshape(ns, L * gs * c, 2 * gs * p)
    cr = c_re[None] * pr[1:, :, None, :] - c_im[None] * pi[1:, :, None, :]
    ci = -(c_re[None] * pi[1:, :, None, :] + c_im[None] * pr[1:, :, None, :])
    co = jnp.stack([cr, ci])
    wo = jnp.einsum('rtjgcp,gh->jrgpthc', co.reshape(2, L, ns, gs, c, p), eye).reshape(ns, 2 * gs * p, L * gs * c)
    al = jnp.stack([pr[L].reshape(-1), pi[L].reshape(-1)])
    return wt.astype(BF16), wp.astype(BF16), wo.astype(BF16), al


def _chunk_lanes(u_ref):
    return jnp.concatenate([u_ref[0, :, t, :] for t in range(S5_L)], axis=1)


def _s5_local_kernel(u_ref, wp_ref, pre_ref, pim_ref):
    r = _dot(_chunk_lanes(u_ref).astype(BF16), wp_ref[0])
    half = r.shape[1] // 2
    pre_ref[...] = r[:, :half]
    pim_ref[...] = r[:, half:]


def _s5_scan_kernel(pre_ref, pim_ref, al_ref, h0r_ref, h0i_ref, hsr_ref, hsi_ref, hfr_ref, hfi_ref):
    nchunk = pre_ref.shape[1]
    ar = al_ref[0:1, :]
    ai = al_ref[1:2, :]

    def body(k, carry):
        cr, ci = carry
        hsr_ref[:, pl.ds(k, 1), :] = cr[:, None, :]
        hsi_ref[:, pl.ds(k, 1), :] = ci[:, None, :]
        xr = pre_ref[:, pl.ds(k, 1), :][:, 0, :]
        xi = pim_ref[:, pl.ds(k, 1), :][:, 0, :]
        return ar * cr - ai * ci + xr, ar * ci + ai * cr + xi

    cr, ci = lax.fori_loop(0, nchunk, body, (h0r_ref[...], h0i_ref[...]))
    hfr_ref[...] = cr
    hfi_ref[...] = ci


def _s5_out_kernel(u_ref, wt_ref, wo_ref, hsr_ref, hsi_ref, y_ref):
    hs = jnp.concatenate([hsr_ref[...], hsi_ref[...]], axis=1).astype(BF16)
    y = _dot(_chunk_lanes(u_ref).astype(BF16), wt_ref[0]) + _dot(hs, wo_ref[0])
    for t in range(S5_L):
        y_ref[0, :, t, :] = y[:, 128 * t:128 * (t + 1)]


def _s5(u3, h0r, h0i, weights, nbatch):
    wt, wp, wo, al = weights
    ns, nb = u3.shape[0], u3.shape[1]
    nchunk = nb // nbatch
    gp = al.shape[1]
    lb = gp // ns
    d = S5_L * 128
    cp = pltpu.CompilerParams(dimension_semantics=("parallel",), vmem_limit_bytes=VMEM_LIMIT)
    u_spec = pl.BlockSpec((1, nb, S5_L, 128), lambda j: (j, 0, 0, 0))
    st_spec = pl.BlockSpec((nb, lb), lambda j: (0, j))
    pre, pim = pl.pallas_call(
        _s5_local_kernel,
        grid=(ns,),
        in_specs=[u_spec, pl.BlockSpec((1, d, 2 * lb), lambda j: (j, 0, 0))],
        out_specs=[st_spec] * 2,
        out_shape=[jax.ShapeDtypeStruct((nb, gp), F32)] * 2,
        compiler_params=cp,
        name="s5_local",
    )(u3, wp)

    seq3 = pl.BlockSpec((nbatch, nchunk, lb), lambda j: (0, 0, j))
    row = lambda r: pl.BlockSpec((r, lb), lambda j: (0, j))
    hsr, hsi, hfr, hfi = pl.pallas_call(
        _s5_scan_kernel,
        grid=(ns,),
        in_specs=[seq3, seq3, row(2), row(nbatch), row(nbatch)],
        out_specs=[seq3, seq3, row(nbatch), row(nbatch)],
        out_shape=[jax.ShapeDtypeStruct((nbatch, nchunk, gp), F32)] * 2 + [jax.ShapeDtypeStruct((nbatch, gp), F32)] * 2,
        compiler_params=cp,
        name="s5_scan",
    )(pre.reshape(nbatch, nchunk, gp), pim.reshape(nbatch, nchunk, gp), al, h0r, h0i)

    y3 = pl.pallas_call(
        _s5_out_kernel,
        grid=(ns,),
        in_specs=[u_spec,
                  pl.BlockSpec((1, d, d), lambda j: (j, 0, 0)),
                  pl.BlockSpec((1, 2 * lb, d), lambda j: (j, 0, 0)),
                  st_spec, st_spec],
        out_specs=u_spec,
        out_shape=jax.ShapeDtypeStruct(u3.shape, F32),
        compiler_params=cp,
        name="s5_out",
    )(u3, wt, wo, hsr.reshape(nb, gp), hsi.reshape(nb, gp))
    return y3, hfr, hfi


def _merge_kernel(o_ref, y_ref, wa_ref, wg1_ref, wg2_ref, sga_ref, sgb_ref, m_ref, gy_ref):
    @pl.when(pl.program_id(1) == 0)
    def _():
        y = jnp.concatenate([y_ref[s] for s in range(y_ref.shape[0])], axis=1)
        gy_ref[...] = _gelu(y).astype(BF16)

    ya = _dot(o_ref[...], wa_ref[...])
    gy = gy_ref[...]
    yb = _dot(gy, wg1_ref[...]) * jax.nn.sigmoid(_dot(gy, wg2_ref[...]))
    m = sga_ref[...].astype(F32) * ya + sgb_ref[...].astype(F32) * yb
    m_ref[...] = m.astype(BF16)


def _merge(o, y3, w_attn_out, w_glu, sg, tm, tn=512):
    rows, d = o.shape[0], w_attn_out.shape[1]
    nj = d // tn
    kq, ks = w_attn_out.shape[0], w_glu.shape[0]
    ns = y3.shape[0]
    return pl.pallas_call(
        _merge_kernel,
        grid=(rows // tm, nj),
        in_specs=[
            pl.BlockSpec((tm, kq), lambda i, j: (i, 0)),
            pl.BlockSpec((ns, tm, 128), lambda i, j: (0, i, 0)),
            pl.BlockSpec((kq, tn), lambda i, j: (0, j)),
            pl.BlockSpec((ks, tn), lambda i, j: (0, j)),
            pl.BlockSpec((ks, tn), lambda i, j: (0, j + nj)),
            pl.BlockSpec((tm, tn), lambda i, j: (i, j)),
            pl.BlockSpec((tm, tn), lambda i, j: (i, j + nj)),
        ],
        out_specs=pl.BlockSpec((tm, tn), lambda i, j: (i, j)),
        out_shape=jax.ShapeDtypeStruct((rows, d), BF16),
        scratch_shapes=[pltpu.VMEM((tm, ks), BF16)],
        compiler_params=pltpu.CompilerParams(
            dimension_semantics=("parallel", "arbitrary"), vmem_limit_bytes=VMEM_LIMIT),
        name="merge",
    )(o, y3, w_attn_out, w_glu, w_glu, sg, sg)


def _outproj_kernel(m_ref, x_ref, w_ref, g_ref, x1_ref, h_ref):
    x1 = x_ref[...] + _dot(m_ref[...], w_ref[...])
    x1_ref[...] = x1
    r = lax.rsqrt(jnp.mean(x1 * x1, axis=-1, keepdims=True) + EPS)
    h_ref[...] = (x1 * r * g_ref[...]).astype(BF16)


def _outproj(m, x, w_out, g_ffn, tm):
    rows, d = x.shape
    row = pl.BlockSpec((tm, d), lambda i: (i, 0))
    return pl.pallas_call(
        _outproj_kernel,
        grid=(rows // tm,),
        in_specs=[row, row, pl.BlockSpec((d, d), lambda i: (0, 0)), pl.BlockSpec((1, d), lambda i: (0, 0))],
        out_specs=[row, row],
        out_shape=[jax.ShapeDtypeStruct((rows, d), F32), jax.ShapeDtypeStruct((rows, d), BF16)],
        compiler_params=pltpu.CompilerParams(dimension_semantics=("parallel",), vmem_limit_bytes=VMEM_LIMIT),
        name="out_proj",
    )(m, x, w_out, g_ffn.reshape(1, d))


def _ffn_kernel(h_ref, x1_ref, wg_ref, wu_ref, wd_ref, gf_ref, y_ref, acc_ref):
    f = pl.program_id(1)

    @pl.when(f == 0)
    def _():
        acc_ref[...] = jnp.zeros_like(acc_ref)

    h = h_ref[...]
    a = _dot(h, wg_ref[...])
    a = (a * jax.nn.sigmoid(a)) * _dot(h, wu_ref[...])
    acc_ref[...] += _dot(a.astype(BF16), wd_ref[...])

    @pl.when(f == pl.num_programs(1) - 1)
    def _():
        y = x1_ref[...] + acc_ref[...]
        r = lax.rsqrt(jnp.mean(y * y, axis=-1, keepdims=True) + EPS)
        y_ref[...] = y * r * gf_ref[...]


def _ffn(h, x1, w_gate, w_up, w_down, g_final, tm, tf=512):
    rows, d = x1.shape
    dff = w_gate.shape[1]
    row = pl.BlockSpec((tm, d), lambda i, f: (i, 0))
    return pl.pallas_call(
        _ffn_kernel,
        grid=(rows // tm, dff // tf),
        in_specs=[row, row,
                  pl.BlockSpec((d, tf), lambda i, f: (0, f)),
                  pl.BlockSpec((d, tf), lambda i, f: (0, f)),
                  pl.BlockSpec((tf, d), lambda i, f: (f, 0)),
                  pl.BlockSpec((1, d), lambda i, f: (0, 0))],
        out_specs=row,
        out_shape=jax.ShapeDtypeStruct((rows, d), F32),
        scratch_shapes=[pltpu.VMEM((tm, d), F32)],
        compiler_params=pltpu.CompilerParams(
            dimension_semantics=("parallel", "arbitrary"), vmem_limit_bytes=VMEM_LIMIT),
        name="ffn",
    )(h, x1, w_gate, w_up, w_down, g_final.reshape(1, d))


def _alibi_slopes():
    return jnp.exp2(-8.0 * jnp.arange(1, N_HEADS + 1, dtype=F32) / N_HEADS)


def _layer_weights(p, l):
    w_cat, w_kv_t = _pack_w_in(p['w_in'][l], p['ssm_d'].shape[-1])
    s5 = _s5_weights(*(p[n][l] for n in ('ssm_lam_re', 'ssm_lam_im', 'ssm_log_dt', 'ssm_b_re', 'ssm_b_im',
                                          'ssm_c_re', 'ssm_c_im', 'ssm_d')))
    cast = lambda n: p[n][l].astype(BF16)
    return dict(
        g_mix=p['g_mix'][l], w_cat=w_cat, w_kv_t=w_kv_t,
        pool=p['w_cmp_pool'][l], pe=p['w_cmp_pe'][l], phi=p['w_cmp_phi'][l],
        pool_mats=_pool_matrices(p['w_cmp_pool'][l], PAGES_PER_STEP),
        w_attn_out=cast('w_attn_out'), w_glu=cast('w_glu'), w_out=cast('w_out'), g_ffn=p['g_ffn'][l],
        w_gate=cast('w_gate'), w_up=cast('w_up'), w_down=cast('w_down'), s5=s5)


def _tail(x, o, y3, sg, lw, g_final, tm):
    rows = x.shape[0]
    m = _merge(o, y3.reshape(y3.shape[0], rows, 128), lw['w_attn_out'], lw['w_glu'], sg, tm)
    x1, h = _outproj(m, x, lw['w_out'], lw['g_ffn'], min(tm, 256))
    return _ffn(h, x1, lw['w_gate'], lw['w_up'], lw['w_down'], g_final, tm)


def _feature_major_rows(kv_t, t0):
    b, f, t = kv_t.shape
    return kv_t[:, :, t0:].reshape(b, f // KV_COLS, N_KV, HEAD_DIM, t - t0).transpose(0, 4, 1, 2, 3)


def _layer_prompt(x, lw, slopes, g_out):
    b, t, d = x.shape
    rows = b * t
    assert t % (PAGES_PER_STEP * PAGE) == 0
    q, u3, sg, gt, kv_t = _in_proj(x.reshape(rows, d), lw['g_mix'], lw['w_cat'], lw['w_kv_t'], 512, seq_len=t)

    pooled = _pool_pages(kv_t, lambda bi, pg: (bi, 0, pg), (), b, t // PAGE, lw['pool_mats'])
    kc_t, vc_t = _cmp_finish(pooled, lw['pool'], lw['pe'], lw['phi'])
    o = _attn_prompt(slopes, q, kc_t, vc_t, kv_t, gt, b, t)

    gp = lw['s5'][3].shape[1]
    h0 = jnp.zeros((b, gp), F32)
    y3, hr, hi = _s5(u3, h0, h0, lw['s5'], b)

    y = _tail(x.reshape(rows, d), o, y3, sg, lw, g_out, 512).reshape(b, t, d)
    n_win = min(WINDOW, t)
    ng = gp // SSM_STATE
    return (y, _feature_major_rows(kv_t[:, :4 * KV_COLS], 0), _feature_major_rows(kv_t[:, 4 * KV_COLS:], t - n_win),
            hr.reshape(b, ng, SSM_STATE), hi.reshape(b, ng, SSM_STATE))


def _layer_sample(x, cache, page_table, win_buf, h_re, h_im, lw, slopes, g_out):
    b, t, d = x.shape
    rows = b * t
    npages = page_table.shape[1]
    past = npages * cache.shape[1]
    n_buf = win_buf.shape[1]
    assert cache.shape[1] == PAGE and rows % S5_L == 0 and t == S5_L
    assert past % CMP_STRIDE == 0 and t < CMP_STRIDE and past % SEL_BLOCK == 0 and t <= SEL_BLOCK
    q, u3, sg, gt, kv = _in_proj(x.reshape(rows, d), lw['g_mix'], lw['w_cat'], lw['w_kv_t'], rows)

    cache_t = cache.transpose(0, 2, 3, 4, 1).reshape(cache.shape[0], 4 * KV_COLS, PAGE)
    win_t = win_buf.transpose(0, 2, 3, 4, 1).reshape(b, 2 * KV_COLS, n_buf)
    pooled = _pool_pages(cache_t, lambda bi, pg, pt: (pt[bi, pg], 0, 0), (page_table,), b, npages, lw['pool_mats'])
    kc_t, vc_t = _cmp_finish(pooled, lw['pool'], lw['pe'], lw['phi'])

    nrows = N_HEADS * t
    eye = jnp.eye(N_KV, dtype=BF16)
    q5 = q.reshape(b, t, N_KV, GROUP, HEAD_DIM).transpose(0, 2, 3, 1, 4)
    qbd = (q5[:, :, :, :, None, :] * eye[None, :, None, None, :, None]).reshape(b, nrows, KV_COLS)
    maskbd = jnp.repeat(jnp.repeat(jnp.eye(N_KV, dtype=F32), GROUP * t, axis=0), HEAD_DIM, axis=1)
    slopecol = jnp.repeat(slopes, t).reshape(nrows, 1)
    g3 = gt.reshape(b, t, N_KV, 128)[..., :3 * GROUP].reshape(b, t, N_KV, 3, GROUP)
    g3 = g3.transpose(0, 2, 4, 1, 3).reshape(b, nrows, 3)
    gates_rows = jnp.pad(g3, ((0, 0), (0, 0), (0, 1)))
    kv3 = kv.reshape(b, t, 6 * KV_COLS)
    n_blk = -(-(past + t) // SEL_BLOCK)

    opart, sel_t = _attn_dec1(qbd, kc_t, vc_t, win_t, kv3, gates_rows, slopecol, maskbd, past, t, n_blk)
    bps = PAGES_PER_STEP * PAGE // SEL_BLOCK
    nstep = npages // PAGES_PER_STEP
    nb_all = (nstep + 1) * bps
    sel = jnp.pad(sel_t[:, :n_blk], ((0, 0), (0, nb_all - n_blk), (0, 0)))
    sel = sel.reshape(b, nstep + 1, bps, N_KV, 1, t).transpose(0, 1, 3, 4, 5, 2)
    selrows = jnp.broadcast_to(sel, (b, nstep + 1, N_KV, GROUP, t, bps)).reshape(b, nstep + 1, nrows, bps)
    knew = jnp.pad(kv3[:, :, 2 * KV_COLS:4 * KV_COLS], ((0, 0), (0, PAGE - t), (0, 0)))
    o_rows = _attn_dec2(cache_t, page_table, qbd, selrows.astype(BF16), slopecol, knew, opart, gates_rows,
                        maskbd, past, t)
    o = (o_rows.reshape(b, N_KV, GROUP, t, HEAD_DIM).transpose(0, 3, 1, 2, 4).reshape(rows, Q_COLS).astype(BF16))

    gp = h_re.shape[1] * h_re.shape[2]
    y3, hr, hi = _s5(u3, h_re.reshape(b, gp), h_im.reshape(b, gp), lw['s5'], b)

    y = _tail(x.reshape(rows, d), o, y3, sg, lw, g_out, rows).reshape(b, t, d)
    kv6 = kv.reshape(b, t, 6, N_KV, HEAD_DIM)
    kvw = jnp.concatenate([win_buf, kv6[:, :, 4:]], axis=1)
    n_keep = min(WINDOW, n_buf + t)
    return y, kv6[:, :, :4], kvw[:, n_buf + t - n_keep:], hr.reshape(h_re.shape), hi.reshape(h_im.shape)


def kernel(x_prompt, x_sample, cache_kv, state_win, state_ssm_re, state_ssm_im, page_table, g_mix, w_in, w_cmp_pe, w_cmp_pool, w_cmp_phi, w_attn_out, ssm_lam_re, ssm_lam_im, ssm_log_dt, ssm_b_re, ssm_b_im, ssm_c_re, ssm_c_im, ssm_d, w_glu, w_out, g_ffn, w_gate, w_up, w_down, g_final):
    depth = g_mix.shape[0]
    assert depth == 1, "final norm is fused into the last layer's FFN; one layer supported"
    params = dict(g_mix=g_mix, w_in=w_in, w_cmp_pe=w_cmp_pe, w_cmp_pool=w_cmp_pool, w_cmp_phi=w_cmp_phi,
                  w_attn_out=w_attn_out, ssm_lam_re=ssm_lam_re, ssm_lam_im=ssm_lam_im, ssm_log_dt=ssm_log_dt,
                  ssm_b_re=ssm_b_re, ssm_b_im=ssm_b_im, ssm_c_re=ssm_c_re, ssm_c_im=ssm_c_im, ssm_d=ssm_d,
                  w_glu=w_glu, w_out=w_out, g_ffn=g_ffn, w_gate=w_gate, w_up=w_up, w_down=w_down)
    slopes = _alibi_slopes()
    outs = [[] for _ in range(8)]
    xp, xs = x_prompt, x_sample
    for l in range(depth):
        lw = _layer_weights(params, l)
        xp, kvp, winp, hrp, hip = _layer_prompt(xp, lw, slopes, g_final)
        xs, kvs, wins, hrs, his = _layer_sample(xs, cache_kv[l], page_table, state_win[l], state_ssm_re[l],
                                                state_ssm_im[l], lw, slopes, g_final)
        for lst, v in zip(outs, (kvp, winp, hrp, hip, kvs, wins, hrs, his)):
            lst.append(v)
    st = [jnp.stack(v) for v in outs]
    return (xp, xs, st[0], st[1], st[2], st[3], st[4], st[5], st[6], st[7])
```

```python
import functools
import math

import numpy as np
import jax
import jax.numpy as jnp
from jax import lax
from jax.experimental import pallas as pl
from jax.experimental.pallas import tpu as pltpu

F32 = jnp.float32
BF16 = jnp.bfloat16

N_HEADS = 16
HEAD_DIM = 64
N_KV = 4
GROUP = N_HEADS // N_KV
CMP_LEN = 32
CMP_STRIDE = 16
SEL_BLOCK = 64
SEL_TOP = 16
WINDOW = 512
SSM_GROUP = 16
SSM_STATE = 64
EPS = 1e-6
NEG = -1e30
FORCE_SCORE = 1e4

KV_COLS = N_KV * HEAD_DIM
Q_COLS = N_HEADS * HEAD_DIM
PAGE = 128
CMP_PER_PAGE = PAGE // CMP_STRIDE
PAGES_PER_STEP = 16
S5_L = 8
S5_SLAB = 128 // SSM_GROUP
VMEM_LIMIT = 56 * 1024 * 1024


def _dot(a, b):
    return jnp.dot(a, b, preferred_element_type=F32)


def _dot_nt(a, b):
    return lax.dot_general(a, b, (((1,), (1,)), ((), ())), preferred_element_type=F32)


def _split(a):
    hi = a.astype(BF16)
    lo = (a - hi.astype(F32)).astype(BF16)
    return hi, lo


def _dot3(a, b):
    ah, al = _split(a)
    bh, bl = _split(b)
    return _dot(ah, bh) + _dot(ah, bl) + _dot(al, bh)


def _dot_nt_hl(w_bf16, a):
    ah, al = _split(a)
    return _dot_nt(w_bf16, ah) + _dot_nt(w_bf16, al)


def _gelu(x):
    return 0.5 * x * (1.0 + jnp.tanh(math.sqrt(2.0 / math.pi) * (x + 0.044715 * (x * x * x))))


def _iota(shape, dim):
    return lax.broadcasted_iota(jnp.int32, shape, dim)


def _log2(n):
    assert n > 0 and n & (n - 1) == 0, n
    return n.bit_length() - 1


def _div_pow2(x, n):
    return jnp.right_shift(x, _log2(n))


def _mod_pow2(x, n):
    return jnp.bitwise_and(x, (1 << _log2(n)) - 1)


def _softmax_rows(s):
    m = jnp.max(s, axis=-1, keepdims=True)
    e = jnp.exp(s - m)
    return e / jnp.sum(e, axis=-1, keepdims=True)


IN_TN = 512
_Q_T, _U_T, _MG_T, _KV_T = 2, 2, 8, 3
_U_0 = _Q_T
_MG_0 = _U_0 + _U_T
_GT_0 = _MG_0 + _MG_T
_KV_0 = _GT_0 + 1
IN_TILES = _KV_0 + _KV_T


def _inproj_kernel(x_ref, g_ref, wq_ref, wu_ref, wmg_ref, wgt_ref, wkv_ref, q_ref, u_ref, sg_ref, gt_ref, kv_ref,
                   xn_ref, *, kv_feature_major):
    j = pl.program_id(1)

    @pl.when(j == 0)
    def _():
        x = x_ref[...]
        r = lax.rsqrt(jnp.mean(x * x, axis=-1, keepdims=True) + EPS)
        xn_ref[...] = (x * r * g_ref[...]).astype(BF16)

    @pl.when(j < _U_0)
    def _():
        q_ref[...] = (_dot(xn_ref[...], wq_ref[...]) * (HEAD_DIM ** -0.5)).astype(BF16)

    @pl.when((j >= _U_0) & (j < _MG_0))
    def _():
        z = _dot(xn_ref[...], wu_ref[...])
        for s in range(IN_TN // 128):
            u_ref[s] = z[:, 128 * s:128 * (s + 1)].reshape(u_ref.shape[1:])

    @pl.when((j >= _MG_0) & (j < _GT_0))
    def _():
        sg_ref[...] = jax.nn.sigmoid(_dot(xn_ref[...], wmg_ref[...])).astype(BF16)

    @pl.when(j == _GT_0)
    def _():
        gt_ref[...] = jax.nn.sigmoid(_dot(xn_ref[...], wgt_ref[...]))

    @pl.when(j >= _KV_0)
    def _():
        if kv_feature_major:
            kv_ref[0] = _dot_nt(wkv_ref[...], xn_ref[...])
        else:
            kv_ref[...] = _dot_nt(xn_ref[...], wkv_ref[...])


def _in_proj(x, g_mix, w, tm, seq_len=None):
    rows, d = x.shape
    feature_major = seq_len is not None

    def col(lo, n):
        return lambda i, j: (i, jnp.clip(j - lo, 0, n - 1))

    if feature_major:
        tiles_per_seq = seq_len // tm
        kv_spec = pl.BlockSpec((1, IN_TN, tm),
                               lambda i, j: (i // tiles_per_seq, jnp.clip(j - _KV_0, 0, _KV_T - 1), i % tiles_per_seq))
        kv_shape = jax.ShapeDtypeStruct((rows // seq_len, _KV_T * IN_TN, seq_len), F32)
    else:
        kv_spec = pl.BlockSpec((tm, IN_TN), col(_KV_0, _KV_T))
        kv_shape = jax.ShapeDtypeStruct((rows, _KV_T * IN_TN), F32)
    n_slab = IN_TN // 128
    return pl.pallas_call(
        functools.partial(_inproj_kernel, kv_feature_major=feature_major),
        grid=(rows // tm, IN_TILES),
        in_specs=[
            pl.BlockSpec((tm, d), lambda i, j: (i, 0)),
            pl.BlockSpec((1, d), lambda i, j: (0, 0)),
            pl.BlockSpec((d, IN_TN), lambda i, j: (0, jnp.clip(j, 0, _Q_T - 1))),
            pl.BlockSpec((d, IN_TN), lambda i, j: (0, jnp.clip(j - _U_0, 0, _U_T - 1))),
            pl.BlockSpec((d, IN_TN), lambda i, j: (0, jnp.clip(j - _MG_0, 0, _MG_T - 1))),
            pl.BlockSpec((d, IN_TN), lambda i, j: (0, 0)),
            pl.BlockSpec((IN_TN, d), lambda i, j: (jnp.clip(j - _KV_0, 0, _KV_T - 1), 0)),
        ],
        out_specs=[
            pl.BlockSpec((tm, IN_TN), col(0, _Q_T)),
            pl.BlockSpec((n_slab, tm // S5_L, S5_L, 128), lambda i, j: (jnp.clip(j - _U_0, 0, _U_T - 1), i, 0, 0)),
            pl.BlockSpec((tm, IN_TN), col(_MG_0, _MG_T)),
            pl.BlockSpec((tm, IN_TN), lambda i, j: (i, 0)),
            kv_spec,
        ],
        out_shape=[
            jax.ShapeDtypeStruct((rows, _Q_T * IN_TN), BF16),
            jax.ShapeDtypeStruct((_U_T * n_slab, rows // S5_L, S5_L, 128), F32),
            jax.ShapeDtypeStruct((rows, _MG_T * IN_TN), BF16),
            jax.ShapeDtypeStruct((rows, IN_TN), F32),
            kv_shape,
        ],
        scratch_shapes=[pltpu.VMEM((tm, d), BF16)],
        compiler_params=pltpu.CompilerParams(
            dimension_semantics=("parallel", "arbitrary"), vmem_limit_bytes=VMEM_LIMIT),
        name="in_proj",
    )(x, g_mix.reshape(1, d), w['q'], w['u'], w['mg'], w['gt'], w['kv_t'])


def _pack_w_in(w_in, ssm_width):
    cuts = np.cumsum([Q_COLS, 6 * KV_COLS, 3 * N_HEADS, ssm_width]).tolist()
    wq, wkv, wgt, wu, wmg = jnp.split(w_in, cuts, axis=1)
    assert wq.shape[1] == _Q_T * IN_TN and wkv.shape[1] == _KV_T * IN_TN
    assert wu.shape[1] == _U_T * IN_TN and wmg.shape[1] == _MG_T * IN_TN and IN_TN == N_KV * 128
    d = w_in.shape[0]
    wgt = wgt.reshape(d, 3, N_KV, GROUP).transpose(0, 2, 1, 3).reshape(d, N_KV, 3 * GROUP)
    wgt = jnp.pad(wgt, ((0, 0), (0, 0), (0, 128 - 3 * GROUP))).reshape(d, IN_TN)
    return dict(q=wq.astype(BF16), u=wu.astype(BF16), mg=wmg.astype(BF16), gt=wgt.astype(BF16),
                kv_t=wkv.T.astype(BF16))


def _pool_kernel(*refs, npg, n_tables):
    refs = refs[n_tables:]
    prev_ref = refs[0]
    page_refs = refs[1:npg + 1]
    mk_ref, mv_ref, mprev_ref, out_ref = refs[npg + 1:]
    x = jnp.concatenate([page_refs[j][0] for j in range(npg)], axis=1).astype(BF16)
    xp = prev_ref[0].astype(BF16)
    has_prev = pl.program_id(1) > 0
    for half, m_ref in ((0, mk_ref), (1, mv_ref)):
        rows = slice(half * KV_COLS, (half + 1) * KV_COLS)
        head = _dot(xp[rows], mprev_ref[half])
        out_ref[0, rows, :] = _dot(x[rows], m_ref[...]) + jnp.where(has_prev, head, 0.0)


def _pool_matrices(pool, npg):
    nblk = npg * CMP_PER_PAGE
    cols = []
    for r in range(CMP_PER_PAGE):
        start = CMP_STRIDE * (r - 1)
        lo, hi = max(start, 0), min(start + CMP_LEN, PAGE)
        cols.append(jnp.pad(pool[:, lo - start:hi - start], ((0, 0), (lo, PAGE - hi))))
    body = jnp.stack(cols, axis=-1)
    carry = jnp.pad(pool[:, :CMP_STRIDE], ((0, 0), (PAGE - CMP_STRIDE, 0)))
    first = jnp.asarray(np.arange(CMP_PER_PAGE) == 0, F32)
    same = jnp.eye(npg, dtype=F32)[None, :, None, :, None]
    nxt = jnp.eye(npg, k=1, dtype=F32)[None, :, None, :, None]
    m = same * body[:, None, :, None, :] + nxt * (carry[:, None, :, None, None] * first)
    m = m.reshape(2, npg * PAGE, nblk).astype(BF16)
    mprev = (carry[:, :, None] * jnp.asarray(np.arange(nblk) == 0, F32)).astype(BF16)
    return m[0], m[1], mprev


def _pool_pages(pages, page_index, tables, nseq, npages, mats):
    npg = PAGES_PER_STEP
    nstep = npages // npg
    nblk = npg * CMP_PER_PAGE
    mk, mv, mprev = mats

    def spec(off):
        return pl.BlockSpec((1, 2 * KV_COLS, PAGE),
                            lambda b, s, *t: page_index(b, jnp.maximum(s * npg + off, 0), *t))

    const = lambda shape: pl.BlockSpec(shape, lambda b, s, *t: (0,) * len(shape))
    return pl.pallas_call(
        functools.partial(_pool_kernel, npg=npg, n_tables=len(tables)),
        grid_spec=pltpu.PrefetchScalarGridSpec(
            num_scalar_prefetch=len(tables),
            grid=(nseq, nstep),
            in_specs=[spec(j) for j in range(-1, npg)] + [const(mk.shape), const(mv.shape), const(mprev.shape)],
            out_specs=pl.BlockSpec((1, 2 * KV_COLS, nblk), lambda b, s, *t: (b, 0, s)),
        ),
        out_shape=jax.ShapeDtypeStruct((nseq, 2 * KV_COLS, npages * CMP_PER_PAGE), F32),
        compiler_params=pltpu.CompilerParams(
            dimension_semantics=("parallel", "arbitrary"), vmem_limit_bytes=VMEM_LIMIT),
        name="cmp_pool",
    )(*tables, *([pages] * (npg + 1)), mk, mv, mprev)


def _cmp_finish_kernel(pooled_ref, poolw_ref, pe_ref, phi_ref, kc_ref, vc_ref):
    bias = jnp.sum(poolw_ref[...] * pe_ref[...], axis=1, keepdims=True)
    x = _gelu(pooled_ref[0] + bias)
    kc_ref[0] = _dot3(phi_ref[0], x[:KV_COLS]).astype(BF16)
    vc_ref[0] = _dot3(phi_ref[1], x[KV_COLS:]).astype(BF16)


def _cmp_finish(pooled, pool, pe, phi):
    nseq, _, n = pooled.shape
    poolw = jnp.concatenate([jnp.broadcast_to(pool[s][None, :], (KV_COLS, CMP_LEN)) for s in range(2)], axis=0)
    pe_t = jnp.concatenate([jnp.tile(pe[s].T, (N_KV, 1)) for s in range(2)], axis=0)
    phi_bd = jnp.stack([jnp.kron(jnp.eye(N_KV, dtype=F32), phi[s].T) for s in range(2)])
    spec = pl.BlockSpec((1, KV_COLS, n), lambda b: (b, 0, 0))
    return pl.pallas_call(
        _cmp_finish_kernel,
        grid=(nseq,),
        in_specs=[
            pl.BlockSpec((1, 2 * KV_COLS, n), lambda b: (b, 0, 0)),
            pl.BlockSpec((2 * KV_COLS, CMP_LEN), lambda b: (0, 0)),
            pl.BlockSpec((2 * KV_COLS, CMP_LEN), lambda b: (0, 0)),
            pl.BlockSpec((2, KV_COLS, KV_COLS), lambda b: (0, 0, 0)),
        ],
        out_specs=[spec, spec],
        out_shape=[jax.ShapeDtypeStruct((nseq, KV_COLS, n), BF16)] * 2,
        compiler_params=pltpu.CompilerParams(dimension_semantics=("parallel",)),
        name="cmp_finish",
    )(pooled, poolw, pe_t, phi_bd)


ATT_TQ = 128
ATT_TK = 512
NBLK_PAD = 128


def _attn_prompt_kernel(slopes_ref, q_ref, kc_ref, vc_ref, ks_ref, vs_ref, kw_ref, vw_ref, gt_ref,
                        ovt_ref, e_ref, o_ref, kvb_ref, m_ref, l_ref, acc_ref, *, tq, tk, n_sel, n_top):
    g = pl.program_id(1)
    qi = pl.program_id(2)
    q0 = qi * tq

    @pl.when(qi == 0)
    def _():
        for slot, ref in enumerate((ks_ref, vs_ref, kw_ref, vw_ref)):
            kvb_ref[slot] = ref[0].astype(BF16)

    qf = q_ref[...]
    q = jnp.concatenate([qf[:, n * HEAD_DIM:(n + 1) * HEAD_DIM] for n in range(GROUP)], axis=0)
    tcol = q0 + _iota((tq, 1), 0)
    qend = (q0 + tq).astype(F32)
    slope = [slopes_ref[GROUP * g + n] for n in range(GROUP)]
    rows = [slice(n * tq, (n + 1) * tq) for n in range(GROUP)]

    ncmp = kc_ref.shape[2]
    nrow = _iota((1, ncmp), 1)
    cend = CMP_STRIDE * nrow + (CMP_STRIDE - 1)
    valid = (nrow >= 1) & (cend <= tcol)
    crel = cend.astype(F32) - qend
    sc = _dot(q, kc_ref[0])
    ps = []
    for n in range(GROUP):
        s = jnp.where(valid, sc[rows[n]] + slope[n] * crel, NEG)
        ps.append(jnp.where(valid, _softmax_rows(s), 0.0))
    o_c = _dot_nt(jnp.concatenate(ps, axis=0).astype(BF16), vc_ref[0])

    psum = (ps[0] + ps[1]) + (ps[2] + ps[3])
    imp = _dot_nt_hl(ovt_ref[...], psum)[0:n_sel]
    blk = _iota((n_sel, 1), 0)
    cur = _div_pow2(q0 + _iota((1, tq), 1), SEL_BLOCK)
    forced = (blk == 0) | (blk == cur) | (blk == cur - 1)
    imp = jnp.where(forced, FORCE_SCORE, imp)
    imp = jnp.where(blk <= cur, imp, -1.0)
    cnt = jnp.zeros((n_sel, tq), F32)
    for j in range(n_sel):
        vj = imp[j:j + 1, :]
        beats = (vj > imp) | ((vj == imp) & (blk > j))
        cnt = cnt + jnp.where(beats, 1.0, 0.0)
    sel_t = jnp.where(cnt < n_top, 1.0, 0.0)
    sel_t = jnp.concatenate([sel_t, jnp.zeros((NBLK_PAD - n_sel, tq), F32)], axis=0)
    sel = sel_t.T.astype(BF16)

    m_ref[...] = jnp.full_like(m_ref, NEG)
    l_ref[...] = jnp.zeros_like(l_ref)
    acc_ref[...] = jnp.zeros_like(acc_ref)

    def body(kt, carry):
        k0 = pl.multiple_of(kt * tk, tk)
        s = _dot(q, kvb_ref[0, :, pl.ds(k0, tk)])
        v_t = kvb_ref[1, :, pl.ds(k0, tk)]
        kpos = k0 + _iota((1, tk), 1)
        chosen = _dot(sel, e_ref[:, pl.ds(k0, tk)])
        allowed = (chosen > 0.5) & (kpos <= tcol)
        bm = jnp.where(allowed, 0.0, NEG)
        krel = kpos.astype(F32) - qend
        for n in range(GROUP):
            sn = s[rows[n]] + slope[n] * krel + bm
            m_old = m_ref[rows[n]]
            m_new = jnp.maximum(m_old, jnp.max(sn, axis=-1, keepdims=True))
            alpha = jnp.exp(m_old - m_new)
            p = jnp.exp(sn - m_new)
            l_ref[rows[n]] = alpha * l_ref[rows[n]] + jnp.sum(p, axis=-1, keepdims=True)
            acc_ref[rows[n]] = alpha * acc_ref[rows[n]] + _dot_nt(p.astype(BF16), v_t)
            m_ref[rows[n]] = m_new
        return carry

    lax.fori_loop(0, (q0 + tq + tk - 1) // tk, body, 0)
    o_s = acc_ref[...] / l_ref[...]

    span = WINDOW + tq
    w0 = pl.multiple_of(jnp.maximum(q0 - WINDOW, 0), tq)
    sw = _dot(q, kvb_ref[2, :, pl.ds(w0, span)])
    kposw = w0 + _iota((1, span), 1)
    dist = tcol - kposw
    bmw = jnp.where((dist >= 0) & (dist < WINDOW), 0.0, NEG)
    krelw = kposw.astype(F32) - qend
    pw = [_softmax_rows(sw[rows[n]] + slope[n] * krelw + bmw) for n in range(GROUP)]
    o_w = _dot_nt(jnp.concatenate(pw, axis=0).astype(BF16), kvb_ref[3, :, pl.ds(w0, span)])

    gt = gt_ref[...]
    outs = []
    for n in range(GROUP):
        outs.append(gt[:, n:n + 1] * o_c[rows[n]] + gt[:, GROUP + n:GROUP + n + 1] * o_s[rows[n]]
                    + gt[:, 2 * GROUP + n:2 * GROUP + n + 1] * o_w[rows[n]])
    o_ref[...] = jnp.concatenate(outs, axis=1).astype(BF16)


def _overlap_t(n_blk_rows, n_cmp):
    s = np.arange(n_blk_rows)[:, None]
    n = np.arange(n_cmp)[None, :]
    r = SEL_BLOCK // CMP_STRIDE
    return jnp.asarray(((n >= r * s) & (n <= r * s + r)).astype(np.float32), dtype=BF16)


def _block_expand(n_blk_rows, n_keys):
    s = np.arange(n_blk_rows)[:, None]
    k = np.arange(n_keys)[None, :]
    return jnp.asarray((k // SEL_BLOCK == s).astype(np.float32), dtype=BF16)


def _attn_prompt(slopes, q, kc_t, vc_t, kv_t, gates, nbatch, seq):
    ncmp = kc_t.shape[2]
    tq, tk = ATT_TQ, ATT_TK
    nq = seq // tq
    n_sel = seq // SEL_BLOCK
    ovt = _overlap_t(NBLK_PAD, ncmp)
    emat = _block_expand(NBLK_PAD, seq)
    per_slot = KV_COLS // HEAD_DIM

    def kv_spec(slot):
        return pl.BlockSpec((1, HEAD_DIM, seq), lambda b, g, i: (b, slot * per_slot + g, 0))

    cmp_spec = pl.BlockSpec((1, HEAD_DIM, ncmp), lambda b, g, i: (b, g, 0))
    q_spec = pl.BlockSpec((tq, GROUP * HEAD_DIM), lambda b, g, i: (b * nq + i, g))
    return pl.pallas_call(
        functools.partial(_attn_prompt_kernel, tq=tq, tk=tk, n_sel=n_sel, n_top=min(SEL_TOP, n_sel)),
        grid=(nbatch, N_KV, nq),
        in_specs=[
            pl.BlockSpec(memory_space=pltpu.SMEM),
            q_spec, cmp_spec, cmp_spec, kv_spec(2), kv_spec(3), kv_spec(4), kv_spec(5),
            pl.BlockSpec((tq, 128), lambda b, g, i: (b * nq + i, g)),
            pl.BlockSpec((NBLK_PAD, ncmp), lambda b, g, i: (0, 0)),
            pl.BlockSpec((NBLK_PAD, seq), lambda b, g, i: (0, 0)),
        ],
        out_specs=q_spec,
        out_shape=jax.ShapeDtypeStruct(q.shape, BF16),
        scratch_shapes=[
            pltpu.VMEM((4, HEAD_DIM, seq), BF16),
            pltpu.VMEM((GROUP * tq, 1), F32),
            pltpu.VMEM((GROUP * tq, 1), F32),
            pltpu.VMEM((GROUP * tq, HEAD_DIM), F32),
        ],
        compiler_params=pltpu.CompilerParams(
            dimension_semantics=("parallel", "parallel", "arbitrary"), vmem_limit_bytes=VMEM_LIMIT),
        name="attn_prompt",
    )(slopes, q, kc_t, vc_t, kv_t, kv_t, kv_t, kv_t, gates, ovt, emat)


def _diag_blocks(o, maskbd):
    o = o * maskbd
    return (o[:, 0:64] + o[:, 64:128]) + (o[:, 128:192] + o[:, 192:256])


def _attn_dec1_kernel(qbd_ref, kc_ref, vc_ref, win_ref, kvnew_ref, gates_ref, slope_ref, ovt_ref,
                      maskbd_ref, opart_ref, sel_ref, *, past, tdec, n_blk, n_top):
    qb = qbd_ref[0]
    nrows = qb.shape[0]
    tcol = past + _mod_pow2(_iota((nrows, 1), 0), tdec)
    pref = float(past + tdec)
    slope = slope_ref[...]
    maskbd = maskbd_ref[...]

    ncmp = kc_ref.shape[2]
    nrow = _iota((1, ncmp), 1)
    cend = CMP_STRIDE * nrow + (CMP_STRIDE - 1)
    valid = (nrow >= 1) & (cend <= tcol)
    s = _dot(qb, kc_ref[0])
    s = jnp.where(valid, s + slope * (cend.astype(F32) - pref), NEG)
    p = jnp.where(valid, _softmax_rows(s), 0.0)
    o_c = _diag_blocks(_dot_nt(p.astype(BF16), vc_ref[0]), maskbd)

    per_g = GROUP * tdec
    psum = jnp.concatenate(
        [sum(p[gi * per_g + n * tdec: gi * per_g + (n + 1) * tdec] for n in range(GROUP)) for gi in range(N_KV)],
        axis=0)
    imp = _dot_nt_hl(ovt_ref[...], psum)
    nb_pad = imp.shape[0]
    blk = _iota((nb_pad, 1), 0)
    cur = _div_pow2(past + _mod_pow2(_iota((1, N_KV * tdec), 1), tdec), SEL_BLOCK)
    forced = (blk == 0) | (blk == cur) | (blk == cur - 1)
    imp = jnp.where(forced, FORCE_SCORE, imp)
    imp = jnp.where(blk <= cur, imp, -1.0)
    imp = jnp.where(blk < n_blk, imp, -2.0)
    sel = jnp.zeros(imp.shape, F32)
    for _ in range(n_top):
        mx = jnp.max(imp, axis=0, keepdims=True)
        first = jnp.min(jnp.where(imp == mx, blk, nb_pad), axis=0, keepdims=True)
        pick = blk == first
        sel = jnp.where(pick, 1.0, sel)
        imp = jnp.where(pick, -jnp.inf, imp)
    sel_ref[0] = sel

    nbuf = win_ref.shape[2]
    knew = kvnew_ref[0]
    zpad = jnp.zeros((PAGE - tdec, KV_COLS), F32)
    kn = jnp.concatenate([knew[:, 4 * KV_COLS:5 * KV_COLS], zpad], axis=0).astype(BF16)
    vn = jnp.concatenate([knew[:, 5 * KV_COLS:], zpad], axis=0).astype(BF16)
    sw = jnp.concatenate([_dot(qb, win_ref[0, :KV_COLS].astype(BF16)), _dot_nt(qb, kn)], axis=1)
    idx = _iota((1, nbuf + PAGE), 1)
    kposw = past - nbuf + idx
    dist = tcol - kposw
    validw = (dist >= 0) & (dist < WINDOW) & (kposw >= 0) & (idx < nbuf + tdec)
    sw = jnp.where(validw, sw + slope * (kposw.astype(F32) - pref), NEG)
    pw = _softmax_rows(sw).astype(BF16)
    o_w = _dot_nt(pw[:, :nbuf], win_ref[0, KV_COLS:].astype(BF16)) + _dot(pw[:, nbuf:], vn)
    o_w = _diag_blocks(o_w, maskbd)

    gates = gates_ref[0]
    opart_ref[0] = gates[:, 0:1] * o_c + gates[:, 2:3] * o_w


def _attn_dec1(qbd, kc_t, vc_t, win_t, kvnew, gates_rows, slopecol, maskbd, past, tdec, n_blk):
    nseq, nrows, _ = qbd.shape
    ncmp = kc_t.shape[2]
    nb_pad = -(-n_blk // 8) * 8
    ovt = _overlap_t(nb_pad, ncmp)
    nbuf = win_t.shape[2]
    return pl.pallas_call(
        functools.partial(_attn_dec1_kernel, past=past, tdec=tdec, n_blk=n_blk, n_top=min(SEL_TOP, n_blk)),
        grid=(nseq,),
        in_specs=[
            pl.BlockSpec((1, nrows, KV_COLS), lambda b: (b, 0, 0)),
            pl.BlockSpec((1, KV_COLS, ncmp), lambda b: (b, 0, 0)),
            pl.BlockSpec((1, KV_COLS, ncmp), lambda b: (b, 0, 0)),
            pl.BlockSpec((1, 2 * KV_COLS, nbuf), lambda b: (b, 0, 0)),
            pl.BlockSpec((1, tdec, 6 * KV_COLS), lambda b: (b, 0, 0)),
            pl.BlockSpec((1, nrows, 4), lambda b: (b, 0, 0)),
            pl.BlockSpec((nrows, 1), lambda b: (0, 0)),
            pl.BlockSpec((nb_pad, ncmp), lambda b: (0, 0)),
            pl.BlockSpec((nrows, KV_COLS), lambda b: (0, 0)),
        ],
        out_specs=[
            pl.BlockSpec((1, nrows, HEAD_DIM), lambda b: (b, 0, 0)),
            pl.BlockSpec((1, nb_pad, N_KV * tdec), lambda b: (b, 0, 0)),
        ],
        out_shape=[
            jax.ShapeDtypeStruct((nseq, nrows, HEAD_DIM), F32),
            jax.ShapeDtypeStruct((nseq, nb_pad, N_KV * tdec), F32),
        ],
        compiler_params=pltpu.CompilerParams(dimension_semantics=("parallel",), vmem_limit_bytes=VMEM_LIMIT),
        name="attn_dec_select",
    )(qbd, kc_t, vc_t, win_t, kvnew, gates_rows, slopecol, ovt, maskbd)


def _attn_dec2_kernel(pt_ref, *refs, npg, past, tdec):
    page_refs = refs[:npg]
    (qbd_ref, selrow_ref, sellast_ref, e_ref, slope_ref, knew_ref, opart_ref, gates_ref, maskbd_ref,
     o_ref, m_ref, l_ref, acc_ref) = refs[npg:]
    step = pl.program_id(1)
    nkeys = npg * PAGE
    qb = qbd_ref[0]
    nrows = qb.shape[0]
    pref = float(past + tdec)
    slope = slope_ref[...]

    @pl.when(step == 0)
    def _():
        m_ref[...] = jnp.full_like(m_ref, NEG)
        l_ref[...] = jnp.zeros_like(l_ref)
        acc_ref[...] = jnp.zeros_like(acc_ref)

    def update(s, pv_fn):
        m_old = m_ref[...]
        m_new = jnp.maximum(m_old, jnp.max(s, axis=-1, keepdims=True))
        alpha = jnp.exp(m_old - m_new)
        p = jnp.exp(s - m_new)
        l_ref[...] = alpha * l_ref[...] + jnp.sum(p, axis=-1, keepdims=True)
        acc_ref[...] = alpha * acc_ref[...] + pv_fn(p.astype(BF16))
        m_ref[...] = m_new

    s = jnp.concatenate([_dot(qb, page_refs[j][0, :KV_COLS].astype(BF16)) for j in range(npg)], axis=1)
    kpos = step * nkeys + _iota((1, nkeys), 1)
    chosen = _dot(selrow_ref[0, 0], e_ref[...])
    s = s + slope * (kpos.astype(F32) - pref) + jnp.where(chosen > 0.5, 0.0, NEG)

    def pv_pages(p):
        acc = _dot_nt(p[:, 0:PAGE], page_refs[0][0, KV_COLS:].astype(BF16))
        for j in range(1, npg):
            acc = acc + _dot_nt(p[:, PAGE * j:PAGE * (j + 1)], page_refs[j][0, KV_COLS:].astype(BF16))
        return acc

    update(s, pv_pages)

    @pl.when(step == pl.num_programs(1) - 1)
    def _():
        tcol = past + _mod_pow2(_iota((nrows, 1), 0), tdec)
        kn = knew_ref[0]
        kposn = past + _iota((1, PAGE), 1)
        sn = _dot_nt(qb, kn[:, :KV_COLS].astype(BF16))
        ok = (sellast_ref[0, 0][:, 0:1].astype(F32) > 0.5) & (kposn <= tcol)
        sn = sn + slope * (kposn.astype(F32) - pref) + jnp.where(ok, 0.0, NEG)
        update(sn, lambda p: _dot(p, kn[:, KV_COLS:].astype(BF16)))
        o_s = _diag_blocks(acc_ref[...] / l_ref[...], maskbd_ref[...])
        o_ref[0] = opart_ref[0] + gates_ref[0][:, 1:2] * o_s


def _attn_dec2(cache_t, page_table, qbd, selrows, slopecol, knew, opart, gates_rows, maskbd, past, tdec):
    nseq, npages = page_table.shape
    npg = PAGES_PER_STEP
    nstep = npages // npg
    nrows = qbd.shape[1]
    blk_per_step = npg * PAGE // SEL_BLOCK
    emat = _block_expand(blk_per_step, npg * PAGE)
    page_specs = [
        pl.BlockSpec((1, 2 * KV_COLS, PAGE), lambda b, s, pt, j=j: (pt[b, s * npg + j], 1, 0))
        for j in range(npg)
    ]
    per_seq = lambda shape: pl.BlockSpec((1,) + shape, lambda b, s, pt: (b,) + (0,) * len(shape))
    const = lambda shape: pl.BlockSpec(shape, lambda b, s, pt: (0,) * len(shape))
    return pl.pallas_call(
        functools.partial(_attn_dec2_kernel, npg=npg, past=past, tdec=tdec),
        grid_spec=pltpu.PrefetchScalarGridSpec(
            num_scalar_prefetch=1,
            grid=(nseq, nstep),
            in_specs=page_specs + [
                per_seq((nrows, KV_COLS)),
                pl.BlockSpec((1, 1, nrows, blk_per_step), lambda b, s, pt: (b, s, 0, 0)),
                pl.BlockSpec((1, 1, nrows, blk_per_step), lambda b, s, pt: (b, nstep, 0, 0)),
                const((blk_per_step, npg * PAGE)),
                const((nrows, 1)),
                per_seq((PAGE, 2 * KV_COLS)),
                per_seq((nrows, HEAD_DIM)),
                per_seq((nrows, 4)),
                const((nrows, KV_COLS)),
            ],
            out_specs=per_seq((nrows, HEAD_DIM)),
            scratch_shapes=[
                pltpu.VMEM((nrows, 1), F32),
                pltpu.VMEM((nrows, 1), F32),
                pltpu.VMEM((nrows, KV_COLS), F32),
            ],
        ),
        out_shape=jax.ShapeDtypeStruct((nseq, nrows, HEAD_DIM), F32),
        compiler_params=pltpu.CompilerParams(
            dimension_semantics=("parallel", "arbitrary"), vmem_limit_bytes=VMEM_LIMIT),
        name="attn_dec_selected",
    )(page_table, *([cache_t] * npg), qbd, selrows, selrows, emat, slopecol, knew, opart, gates_rows, maskbd)


def _s5_weights(lam_re, lam_im, log_dt, b_re, b_im, c_re, c_im, d_skip):
    hp = lax.Precision.HIGHEST
    L = S5_L
    ng, p = lam_re.shape
    c = SSM_GROUP
    ns, gs = ng // S5_SLAB, S5_SLAB
    dt = jnp.exp(log_dt)[:, None]
    lr, li = lam_re, lam_im
    mag = jnp.exp(lr * dt)
    ar = mag * jnp.cos(li * dt)
    ai = mag * jnp.sin(li * dt)
    den = lr * lr + li * li
    fr = ((ar - 1.0) * lr + ai * li) / den
    fi = (ai * lr - (ar - 1.0) * li) / den
    bbr = fr[..., None] * b_re - fi[..., None] * b_im
    bbi = fr[..., None] * b_im + fi[..., None] * b_re
    j = jnp.arange(L + 1, dtype=F32)[:, None, None]
    pmag = jnp.exp(j * (lr * dt))
    pr = pmag * jnp.cos(j * (li * dt))
    pi = pmag * jnp.sin(j * (li * dt))
    abr = pr[..., None] * bbr - pi[..., None] * bbi
    abi = pr[..., None] * bbi + pi[..., None] * bbr
    kd = (jnp.einsum('gcp,jgpk->jgck', c_re, abr, precision=hp)
          - jnp.einsum('gcp,jgpk->jgck', c_im, abi, precision=hp))
    kd = kd.at[0].add(d_skip.reshape(ng, c)[:, :, None] * jnp.eye(c, dtype=F32))
    kc = kd[:L].reshape(L, ns, gs, c, c).transpose(1, 0, 2, 4, 3).reshape(ns, L, gs * c, c)
    ab = jnp.stack([abr[:L], abi[:L]]).reshape(2, L, ns, gs, p, c)
    abc = ab.transpose(2, 1, 3, 5, 0, 4).reshape(ns, L, gs * c, 2 * p)
    cr = c_re[None] * pr[1:, :, None, :] - c_im[None] * pi[1:, :, None, :]
    ci = -(c_re[None] * pi[1:, :, None, :] + c_im[None] * pr[1:, :, None, :])
    co = jnp.stack([cr, ci]).reshape(2, L, ns, gs, c, p)
    coc = co.transpose(2, 1, 0, 3, 5, 4).reshape(ns, L, 2 * gs * p, c)
    al = jnp.stack([pr[L].reshape(-1), pi[L].reshape(-1)])
    return kc.astype(BF16), abc.astype(BF16), coc.astype(BF16), al


def _s5_expanders():
    gs, c, p = S5_SLAB, SSM_GROUP, SSM_STATE
    lane = np.arange(gs * c)
    rep_c = (np.arange(c)[:, None] == lane[None, :] % c)
    same_g = (lane[:, None] // c == lane[None, :] // c)
    st = np.arange(2 * gs * p)
    rp = np.arange(2 * p)
    rep_p = (rp[:, None] // p == st[None, :] // (gs * p)) & (rp[:, None] % p == st[None, :] % p)
    g_in = (lane[:, None] // c == (st[None, :] // p) % gs)
    g_out = ((st[:, None] // p) % gs == lane[None, :] // c)
    as_bf = lambda a: jnp.asarray(a.astype(np.float32), dtype=BF16)
    return as_bf(rep_c), as_bf(same_g), as_bf(rep_p), as_bf(g_in), as_bf(g_out)


def _chunk_lanes(u_ref):
    return jnp.concatenate([u_ref[0, :, t, :] for t in range(S5_L)], axis=1)


def _s5_local_kernel(u_ref, abc_ref, rep_p_ref, g_in_ref, pre_ref, pim_ref, wp_ref):
    for s in range(S5_L):
        blk = _dot(abc_ref[0, S5_L - 1 - s], rep_p_ref[...]).astype(BF16) * g_in_ref[...]
        wp_ref[128 * s:128 * (s + 1), :] = blk
    r = _dot(_chunk_lanes(u_ref).astype(BF16), wp_ref[...])
    half = r.shape[1] // 2
    pre_ref[...] = r[:, :half]
    pim_ref[...] = r[:, half:]


def _s5_scan_kernel(pre_ref, pim_ref, al_ref, h0r_ref, h0i_ref, hsr_ref, hsi_ref, hfr_ref, hfi_ref):
    nchunk = pre_ref.shape[1]
    ar = al_ref[0:1, :]
    ai = al_ref[1:2, :]

    def body(k, carry):
        cr, ci = carry
        hsr_ref[:, pl.ds(k, 1), :] = cr[:, None, :]
        hsi_ref[:, pl.ds(k, 1), :] = ci[:, None, :]
        xr = pre_ref[:, pl.ds(k, 1), :][:, 0, :]
        xi = pim_ref[:, pl.ds(k, 1), :][:, 0, :]
        return ar * cr - ai * ci + xr, ar * ci + ai * cr + xi

    cr, ci = lax.fori_loop(0, nchunk, body, (h0r_ref[...], h0i_ref[...]))
    hfr_ref[...] = cr
    hfi_ref[...] = ci


def _s5_out_kernel(u_ref, kc_ref, coc_ref, rep_c_ref, same_g_ref, g_out_ref, hsr_ref, hsi_ref, y_ref,
                   wt_ref, wo_ref):
    rep_c = rep_c_ref[...]
    lag = [_dot(kc_ref[0, dl], rep_c).astype(BF16) * same_g_ref[...] for dl in range(S5_L)]
    zero = jnp.zeros((128, 128), BF16)
    for s in range(S5_L):
        for t in range(S5_L):
            wt_ref[128 * s:128 * (s + 1), 128 * t:128 * (t + 1)] = lag[t - s] if t >= s else zero
    for t in range(S5_L):
        wo_ref[:, 128 * t:128 * (t + 1)] = _dot(coc_ref[0, t], rep_c).astype(BF16) * g_out_ref[...]
    hs = jnp.concatenate([hsr_ref[...], hsi_ref[...]], axis=1).astype(BF16)
    y = _dot(_chunk_lanes(u_ref).astype(BF16), wt_ref[...]) + _dot(hs, wo_ref[...])
    for t in range(S5_L):
        y_ref[0, :, t, :] = y[:, 128 * t:128 * (t + 1)]


def _s5(u3, h0r, h0i, weights, nbatch):
    kc, abc, coc, al = weights
    rep_c, same_g, rep_p, g_in, g_out = _s5_expanders()
    ns, nb = u3.shape[0], u3.shape[1]
    nchunk = nb // nbatch
    gp = al.shape[1]
    lb = gp // ns
    d = S5_L * 128
    cp = pltpu.CompilerParams(dimension_semantics=("parallel",), vmem_limit_bytes=VMEM_LIMIT)
    u_spec = pl.BlockSpec((1, nb, S5_L, 128), lambda j: (j, 0, 0, 0))
    st_spec = pl.BlockSpec((nb, lb), lambda j: (0, j))
    slab = lambda a: pl.BlockSpec((1,) + a.shape[1:], lambda j: (j,) + (0,) * (a.ndim - 1))
    const = lambda a: pl.BlockSpec(a.shape, lambda j: (0,) * a.ndim)
    pre, pim = pl.pallas_call(
        _s5_local_kernel,
        grid=(ns,),
        in_specs=[u_spec, slab(abc), const(rep_p), const(g_in)],
        out_specs=[st_spec] * 2,
        out_shape=[jax.ShapeDtypeStruct((nb, gp), F32)] * 2,
        scratch_shapes=[pltpu.VMEM((d, 2 * lb), BF16)],
        compiler_params=cp,
        name="s5_local",
    )(u3, abc, rep_p, g_in)

    seq3 = pl.BlockSpec((nbatch, nchunk, lb), lambda j: (0, 0, j))
    row = lambda r: pl.BlockSpec((r, lb), lambda j: (0, j))
    hsr, hsi, hfr, hfi = pl.pallas_call(
        _s5_scan_kernel,
        grid=(ns,),
        in_specs=[seq3, seq3, row(2), row(nbatch), row(nbatch)],
        out_specs=[seq3, seq3, row(nbatch), row(nbatch)],
        out_shape=[jax.ShapeDtypeStruct((nbatch, nchunk, gp), F32)] * 2 + [jax.ShapeDtypeStruct((nbatch, gp), F32)] * 2,
        compiler_params=cp,
        name="s5_scan",
    )(pre.reshape(nbatch, nchunk, gp), pim.reshape(nbatch, nchunk, gp), al, h0r, h0i)

    y3 = pl.pallas_call(
        _s5_out_kernel,
        grid=(ns,),
        in_specs=[u_spec, slab(kc), slab(coc), const(rep_c), const(same_g), const(g_out), st_spec, st_spec],
        out_specs=u_spec,
        out_shape=jax.ShapeDtypeStruct(u3.shape, F32),
        scratch_shapes=[pltpu.VMEM((d, d), BF16), pltpu.VMEM((2 * lb, d), BF16)],
        compiler_params=cp,
        name="s5_out",
    )(u3, kc, coc, rep_c, same_g, g_out, hsr.reshape(nb, gp), hsi.reshape(nb, gp))
    return y3, hfr, hfi


def _merge_kernel(o_ref, y_ref, wa_ref, wg1_ref, wg2_ref, sga_ref, sgb_ref, m_ref, gy_ref):
    @pl.when(pl.program_id(1) == 0)
    def _():
        y = jnp.concatenate([y_ref[s] for s in range(y_ref.shape[0])], axis=1)
        gy_ref[...] = _gelu(y).astype(BF16)

    ya = _dot(o_ref[...], wa_ref[...])
    gy = gy_ref[...]
    yb = _dot(gy, wg1_ref[...]) * jax.nn.sigmoid(_dot(gy, wg2_ref[...]))
    m = sga_ref[...].astype(F32) * ya + sgb_ref[...].astype(F32) * yb
    m_ref[...] = m.astype(BF16)


def _merge(o, y3, w_attn_out, w_glu, sg, tm, tn=512):
    rows, d = o.shape[0], w_attn_out.shape[1]
    nj = d // tn
    kq, ks = w_attn_out.shape[0], w_glu.shape[0]
    ns = y3.shape[0]
    return pl.pallas_call(
        _merge_kernel,
        grid=(rows // tm, nj),
        in_specs=[
            pl.BlockSpec((tm, kq), lambda i, j: (i, 0)),
            pl.BlockSpec((ns, tm, 128), lambda i, j: (0, i, 0)),
            pl.BlockSpec((kq, tn), lambda i, j: (0, j)),
            pl.BlockSpec((ks, tn), lambda i, j: (0, j)),
            pl.BlockSpec((ks, tn), lambda i, j: (0, j + nj)),
            pl.BlockSpec((tm, tn), lambda i, j: (i, j)),
            pl.BlockSpec((tm, tn), lambda i, j: (i, j + nj)),
        ],
        out_specs=pl.BlockSpec((tm, tn), lambda i, j: (i, j)),
        out_shape=jax.ShapeDtypeStruct((rows, d), BF16),
        scratch_shapes=[pltpu.VMEM((tm, ks), BF16)],
        compiler_params=pltpu.CompilerParams(
            dimension_semantics=("parallel", "arbitrary"), vmem_limit_bytes=VMEM_LIMIT),
        name="merge",
    )(o, y3, w_attn_out, w_glu, w_glu, sg, sg)


def _outproj_kernel(m_ref, x_ref, w_ref, g_ref, x1_ref, h_ref):
    x1 = x_ref[...] + _dot(m_ref[...], w_ref[...])
    x1_ref[...] = x1
    r = lax.rsqrt(jnp.mean(x1 * x1, axis=-1, keepdims=True) + EPS)
    h_ref[...] = (x1 * r * g_ref[...]).astype(BF16)


def _outproj(m, x, w_out, g_ffn, tm):
    rows, d = x.shape
    row = pl.BlockSpec((tm, d), lambda i: (i, 0))
    return pl.pallas_call(
        _outproj_kernel,
        grid=(rows // tm,),
        in_specs=[row, row, pl.BlockSpec((d, d), lambda i: (0, 0)), pl.BlockSpec((1, d), lambda i: (0, 0))],
        out_specs=[row, row],
        out_shape=[jax.ShapeDtypeStruct((rows, d), F32), jax.ShapeDtypeStruct((rows, d), BF16)],
        compiler_params=pltpu.CompilerParams(dimension_semantics=("parallel",), vmem_limit_bytes=VMEM_LIMIT),
        name="out_proj",
    )(m, x, w_out, g_ffn.reshape(1, d))


def _ffn_kernel(h_ref, x1_ref, wg_ref, wu_ref, wd_ref, gf_ref, y_ref, acc_ref):
    f = pl.program_id(1)

    @pl.when(f == 0)
    def _():
        acc_ref[...] = jnp.zeros_like(acc_ref)

    h = h_ref[...]
    a = _dot(h, wg_ref[...])
    a = (a * jax.nn.sigmoid(a)) * _dot(h, wu_ref[...])
    acc_ref[...] += _dot(a.astype(BF16), wd_ref[...])

    @pl.when(f == pl.num_programs(1) - 1)
    def _():
        y = x1_ref[...] + acc_ref[...]
        r = lax.rsqrt(jnp.mean(y * y, axis=-1, keepdims=True) + EPS)
        y_ref[...] = y * r * gf_ref[...]


def _ffn(h, x1, w_gate, w_up, w_down, g_final, tm, tf=512):
    rows, d = x1.shape
    dff = w_gate.shape[1]
    row = pl.BlockSpec((tm, d), lambda i, f: (i, 0))
    return pl.pallas_call(
        _ffn_kernel,
        grid=(rows // tm, dff // tf),
        in_specs=[row, row,
                  pl.BlockSpec((d, tf), lambda i, f: (0, f)),
                  pl.BlockSpec((d, tf), lambda i, f: (0, f)),
                  pl.BlockSpec((tf, d), lambda i, f: (f, 0)),
                  pl.BlockSpec((1, d), lambda i, f: (0, 0))],
        out_specs=row,
        out_shape=jax.ShapeDtypeStruct((rows, d), F32),
        scratch_shapes=[pltpu.VMEM((tm, d), F32)],
        compiler_params=pltpu.CompilerParams(
            dimension_semantics=("parallel", "arbitrary"), vmem_limit_bytes=VMEM_LIMIT),
        name="ffn",
    )(h, x1, w_gate, w_up, w_down, g_final.reshape(1, d))


def _alibi_slopes():
    return jnp.exp2(-8.0 * jnp.arange(1, N_HEADS + 1, dtype=F32) / N_HEADS)


def _layer_weights(p, l):
    w_in = _pack_w_in(p['w_in'][l], p['ssm_d'].shape[-1])
    s5 = _s5_weights(*(p[n][l] for n in ('ssm_lam_re', 'ssm_lam_im', 'ssm_log_dt', 'ssm_b_re', 'ssm_b_im',
                                          'ssm_c_re', 'ssm_c_im', 'ssm_d')))
    cast = lambda n: p[n][l].astype(BF16)
    return dict(
        g_mix=p['g_mix'][l], w_in=w_in,
        pool=p['w_cmp_pool'][l], pe=p['w_cmp_pe'][l], phi=p['w_cmp_phi'][l],
        pool_mats=_pool_matrices(p['w_cmp_pool'][l], PAGES_PER_STEP),
        w_attn_out=cast('w_attn_out'), w_glu=cast('w_glu'), w_out=cast('w_out'), g_ffn=p['g_ffn'][l],
        w_gate=cast('w_gate'), w_up=cast('w_up'), w_down=cast('w_down'), s5=s5)


def _tail(x, o, y3, sg, lw, g_final, tm):
    rows = x.shape[0]
    m = _merge(o, y3.reshape(y3.shape[0], rows, 128), lw['w_attn_out'], lw['w_glu'], sg, tm)
    x1, h = _outproj(m, x, lw['w_out'], lw['g_ffn'], min(tm, 256))
    return _ffn(h, x1, lw['w_gate'], lw['w_up'], lw['w_down'], g_final, tm)


def _feature_major_rows(kv_t, t0):
    b, f, t = kv_t.shape
    return kv_t[:, :, t0:].reshape(b, f // KV_COLS, N_KV, HEAD_DIM, t - t0).transpose(0, 4, 1, 2, 3)


def _layer_prompt(x, lw, slopes, g_out):
    b, t, d = x.shape
    rows = b * t
    assert t % (PAGES_PER_STEP * PAGE) == 0
    q, u3, sg, gt, kv_t = _in_proj(x.reshape(rows, d), lw['g_mix'], lw['w_in'], 512, seq_len=t)

    pooled = _pool_pages(kv_t, lambda bi, pg: (bi, 0, pg), (), b, t // PAGE, lw['pool_mats'])
    kc_t, vc_t = _cmp_finish(pooled, lw['pool'], lw['pe'], lw['phi'])
    o = _attn_prompt(slopes, q, kc_t, vc_t, kv_t, gt, b, t)

    gp = lw['s5'][3].shape[1]
    h0 = jnp.zeros((b, gp), F32)
    y3, hr, hi = _s5(u3, h0, h0, lw['s5'], b)

    y = _tail(x.reshape(rows, d), o, y3, sg, lw, g_out, 512).reshape(b, t, d)
    n_win = min(WINDOW, t)
    ng = gp // SSM_STATE
    return (y, _feature_major_rows(kv_t[:, :4 * KV_COLS], 0), _feature_major_rows(kv_t[:, 4 * KV_COLS:], t - n_win),
            hr.reshape(b, ng, SSM_STATE), hi.reshape(b, ng, SSM_STATE))


def _layer_sample(x, cache, page_table, win_buf, h_re, h_im, lw, slopes, g_out):
    b, t, d = x.shape
    rows = b * t
    npages = page_table.shape[1]
    past = npages * cache.shape[1]
    n_buf = win_buf.shape[1]
    assert cache.shape[1] == PAGE and rows % S5_L == 0 and t == S5_L
    assert past % CMP_STRIDE == 0 and t < CMP_STRIDE and past % SEL_BLOCK == 0 and t <= SEL_BLOCK
    q, u3, sg, gt, kv = _in_proj(x.reshape(rows, d), lw['g_mix'], lw['w_in'], rows)

    cache_t = cache.transpose(0, 2, 3, 4, 1).reshape(cache.shape[0], 4 * KV_COLS, PAGE)
    win_t = win_buf.transpose(0, 2, 3, 4, 1).reshape(b, 2 * KV_COLS, n_buf)
    pooled = _pool_pages(cache_t, lambda bi, pg, pt: (pt[bi, pg], 0, 0), (page_table,), b, npages, lw['pool_mats'])
    kc_t, vc_t = _cmp_finish(pooled, lw['pool'], lw['pe'], lw['phi'])

    nrows = N_HEADS * t
    eye = jnp.eye(N_KV, dtype=BF16)
    q5 = q.reshape(b, t, N_KV, GROUP, HEAD_DIM).transpose(0, 2, 3, 1, 4)
    qbd = (q5[:, :, :, :, None, :] * eye[None, :, None, None, :, None]).reshape(b, nrows, KV_COLS)
    maskbd = jnp.repeat(jnp.repeat(jnp.eye(N_KV, dtype=F32), GROUP * t, axis=0), HEAD_DIM, axis=1)
    slopecol = jnp.repeat(slopes, t).reshape(nrows, 1)
    g3 = gt.reshape(b, t, N_KV, 128)[..., :3 * GROUP].reshape(b, t, N_KV, 3, GROUP)
    g3 = g3.transpose(0, 2, 4, 1, 3).reshape(b, nrows, 3)
    gates_rows = jnp.pad(g3, ((0, 0), (0, 0), (0, 1)))
    kv3 = kv.reshape(b, t, 6 * KV_COLS)
    n_blk = -(-(past + t) // SEL_BLOCK)

    opart, sel_t = _attn_dec1(qbd, kc_t, vc_t, win_t, kv3, gates_rows, slopecol, maskbd, past, t, n_blk)
    bps = PAGES_PER_STEP * PAGE // SEL_BLOCK
    nstep = npages // PAGES_PER_STEP
    nb_all = (nstep + 1) * bps
    sel = jnp.pad(sel_t[:, :n_blk], ((0, 0), (0, nb_all - n_blk), (0, 0)))
    sel = sel.reshape(b, nstep + 1, bps, N_KV, 1, t).transpose(0, 1, 3, 4, 5, 2)
    selrows = jnp.broadcast_to(sel, (b, nstep + 1, N_KV, GROUP, t, bps)).reshape(b, nstep + 1, nrows, bps)
    knew = jnp.pad(kv3[:, :, 2 * KV_COLS:4 * KV_COLS], ((0, 0), (0, PAGE - t), (0, 0)))
    o_rows = _attn_dec2(cache_t, page_table, qbd, selrows.astype(BF16), slopecol, knew, opart, gates_rows,
                        maskbd, past, t)
    o = (o_rows.reshape(b, N_KV, GROUP, t, HEAD_DIM).transpose(0, 3, 1, 2, 4).reshape(rows, Q_COLS).astype(BF16))

    gp = h_re.shape[1] * h_re.shape[2]
    y3, hr, hi = _s5(u3, h_re.reshape(b, gp), h_im.reshape(b, gp), lw['s5'], b)

    y = _tail(x.reshape(rows, d), o, y3, sg, lw, g_out, rows).reshape(b, t, d)
    kv6 = kv.reshape(b, t, 6, N_KV, HEAD_DIM)
    kvw = jnp.concatenate([win_buf, kv6[:, :, 4:]], axis=1)
    n_keep = min(WINDOW, n_buf + t)
    return y, kv6[:, :, :4], kvw[:, n_buf + t - n_keep:], hr.reshape(h_re.shape), hi.reshape(h_im.shape)


def kernel(x_prompt, x_sample, cache_kv, state_win, state_ssm_re, state_ssm_im, page_table, g_mix, w_in, w_cmp_pe, w_cmp_pool, w_cmp_phi, w_attn_out, ssm_lam_re, ssm_lam_im, ssm_log_dt, ssm_b_re, ssm_b_im, ssm_c_re, ssm_c_im, ssm_d, w_glu, w_out, g_ffn, w_gate, w_up, w_down, g_final):
    depth = g_mix.shape[0]
    assert depth == 1, "final norm is fused into the last layer's FFN; one layer supported"
    params = dict(g_mix=g_mix, w_in=w_in, w_cmp_pe=w_cmp_pe, w_cmp_pool=w_cmp_pool, w_cmp_phi=w_cmp_phi,
                  w_attn_out=w_attn_out, ssm_lam_re=ssm_lam_re, ssm_lam_im=ssm_lam_im, ssm_log_dt=ssm_log_dt,
                  ssm_b_re=ssm_b_re, ssm_b_im=ssm_b_im, ssm_c_re=ssm_c_re, ssm_c_im=ssm_c_im, ssm_d=ssm_d,
                  w_glu=w_glu, w_out=w_out, g_ffn=g_ffn, w_gate=w_gate, w_up=w_up, w_down=w_down)
    slopes = _alibi_slopes()
    outs = [[] for _ in range(8)]
    xp, xs = x_prompt, x_sample
    for l in range(depth):
        lw = _layer_weights(params, l)
        xp, kvp, winp, hrp, hip = _layer_prompt(xp, lw, slopes, g_final)
        xs, kvs, wins, hrs, his = _layer_sample(xs, cache_kv[l], page_table, state_win[l], state_ssm_re[l],
                                                state_ssm_im[l], lw, slopes, g_final)
        for lst, v in zip(outs, (kvp, winp, hrp, hip, kvs, wins, hrs, his)):
            lst.append(v)
    st = [jnp.stack(v) for v in outs]
    return (xp, xs, st[0], st[1], st[2], st[3], st[4], st[5], st[6], st[7])
```

```python
import functools
import math

import numpy as np
import jax
import jax.numpy as jnp
from jax import lax
from jax.experimental import pallas as pl
from jax.experimental.pallas import tpu as pltpu

F32 = jnp.float32
BF16 = jnp.bfloat16

N_HEADS = 16
HEAD_DIM = 64
N_KV = 4
GROUP = N_HEADS // N_KV
CMP_LEN = 32
CMP_STRIDE = 16
SEL_BLOCK = 64
SEL_TOP = 16
WINDOW = 512
SSM_GROUP = 16
SSM_STATE = 64
EPS = 1e-6
NEG = -1e30
FORCE_SCORE = 1e4

KV_COLS = N_KV * HEAD_DIM
Q_COLS = N_HEADS * HEAD_DIM
PAGE = 128
CMP_PER_PAGE = PAGE // CMP_STRIDE
PAGES_PER_STEP = 16
S5_L = 8
S5_SLAB = 128 // SSM_GROUP
VMEM_LIMIT = 56 * 1024 * 1024
LOG2E = 1.0 / math.log(2.0)
Q_SCALE = HEAD_DIM ** -0.5 * LOG2E


def _dot(a, b):
    return jnp.dot(a, b, preferred_element_type=F32)


def _dot_nt(a, b):
    return lax.dot_general(a, b, (((1,), (1,)), ((), ())), preferred_element_type=F32)


def _split(a):
    hi = a.astype(BF16)
    lo = (a - hi.astype(F32)).astype(BF16)
    return hi, lo


def _dot3(a, b):
    ah, al = _split(a)
    bh, bl = _split(b)
    return _dot(ah, bh) + _dot(ah, bl) + _dot(al, bh)


def _dot_nt_hl(w_bf16, a):
    ah, al = _split(a)
    return _dot_nt(w_bf16, ah) + _dot_nt(w_bf16, al)


def _gelu(x):
    return 0.5 * x * (1.0 + jnp.tanh(math.sqrt(2.0 / math.pi) * (x + 0.044715 * (x * x * x))))


def _iota(shape, dim):
    return lax.broadcasted_iota(jnp.int32, shape, dim)


def _log2(n):
    assert n > 0 and n & (n - 1) == 0, n
    return n.bit_length() - 1


def _div_pow2(x, n):
    return jnp.right_shift(x, _log2(n))


def _mod_pow2(x, n):
    return jnp.bitwise_and(x, (1 << _log2(n)) - 1)


def _softmax2_rows(s):
    m = jnp.max(s, axis=-1, keepdims=True)
    e = jnp.exp2(s - m)
    return e / jnp.sum(e, axis=-1, keepdims=True)


IN_TN = 512
_Q_T, _U_T, _MG_T, _KV_T = 2, 2, 8, 3
_U_0 = _Q_T
_MG_0 = _U_0 + _U_T
_GT_0 = _MG_0 + _MG_T
_KV_0 = _GT_0 + 1
IN_TILES = _KV_0 + _KV_T


def _inproj_kernel(x_ref, g_ref, wq_ref, wu_ref, wmg_ref, wgt_ref, wkv_ref, q_ref, u_ref, sg_ref, gt_ref, kv_ref,
                   xn_ref, *, kv_feature_major):
    j = pl.program_id(1)

    @pl.when(j == 0)
    def _():
        x = x_ref[...]
        r = lax.rsqrt(jnp.mean(x * x, axis=-1, keepdims=True) + EPS)
        xn_ref[...] = (x * r * g_ref[...]).astype(BF16)

    @pl.when(j < _U_0)
    def _():
        q_ref[...] = (_dot(xn_ref[...], wq_ref[...]) * Q_SCALE).astype(BF16)

    @pl.when((j >= _U_0) & (j < _MG_0))
    def _():
        z = _dot(xn_ref[...], wu_ref[...])
        for s in range(IN_TN // 128):
            u_ref[s] = z[:, 128 * s:128 * (s + 1)].reshape(u_ref.shape[1:])

    @pl.when((j >= _MG_0) & (j < _GT_0))
    def _():
        sg_ref[...] = jax.nn.sigmoid(_dot(xn_ref[...], wmg_ref[...])).astype(BF16)

    @pl.when(j == _GT_0)
    def _():
        gt_ref[...] = jax.nn.sigmoid(_dot(xn_ref[...], wgt_ref[...]))

    @pl.when(j >= _KV_0)
    def _():
        if kv_feature_major:
            kv_ref[0] = _dot_nt(wkv_ref[...], xn_ref[...])
        else:
            kv_ref[...] = _dot_nt(xn_ref[...], wkv_ref[...])


def _in_proj(x, g_mix, w, tm, seq_len=None):
    rows, d = x.shape
    feature_major = seq_len is not None

    def col(lo, n):
        return lambda i, j: (i, jnp.clip(j - lo, 0, n - 1))

    if feature_major:
        tiles_per_seq = seq_len // tm
        kv_spec = pl.BlockSpec((1, IN_TN, tm),
                               lambda i, j: (i // tiles_per_seq, jnp.clip(j - _KV_0, 0, _KV_T - 1), i % tiles_per_seq))
        kv_shape = jax.ShapeDtypeStruct((rows // seq_len, _KV_T * IN_TN, seq_len), F32)
    else:
        kv_spec = pl.BlockSpec((tm, IN_TN), col(_KV_0, _KV_T))
        kv_shape = jax.ShapeDtypeStruct((rows, _KV_T * IN_TN), F32)
    n_slab = IN_TN // 128
    return pl.pallas_call(
        functools.partial(_inproj_kernel, kv_feature_major=feature_major),
        grid=(rows // tm, IN_TILES),
        in_specs=[
            pl.BlockSpec((tm, d), lambda i, j: (i, 0)),
            pl.BlockSpec((1, d), lambda i, j: (0, 0)),
            pl.BlockSpec((d, IN_TN), lambda i, j: (0, jnp.clip(j, 0, _Q_T - 1))),
            pl.BlockSpec((d, IN_TN), lambda i, j: (0, jnp.clip(j - _U_0, 0, _U_T - 1))),
            pl.BlockSpec((d, IN_TN), lambda i, j: (0, jnp.clip(j - _MG_0, 0, _MG_T - 1))),
            pl.BlockSpec((d, IN_TN), lambda i, j: (0, 0)),
            pl.BlockSpec((IN_TN, d), lambda i, j: (jnp.clip(j - _KV_0, 0, _KV_T - 1), 0)),
        ],
        out_specs=[
            pl.BlockSpec((tm, IN_TN), col(0, _Q_T)),
            pl.BlockSpec((n_slab, tm // S5_L, S5_L, 128), lambda i, j: (jnp.clip(j - _U_0, 0, _U_T - 1), i, 0, 0)),
            pl.BlockSpec((tm, IN_TN), col(_MG_0, _MG_T)),
            pl.BlockSpec((tm, IN_TN), lambda i, j: (i, 0)),
            kv_spec,
        ],
        out_shape=[
            jax.ShapeDtypeStruct((rows, _Q_T * IN_TN), BF16),
            jax.ShapeDtypeStruct((_U_T * n_slab, rows // S5_L, S5_L, 128), F32),
            jax.ShapeDtypeStruct((rows, _MG_T * IN_TN), BF16),
            jax.ShapeDtypeStruct((rows, IN_TN), F32),
            kv_shape,
        ],
        scratch_shapes=[pltpu.VMEM((tm, d), BF16)],
        compiler_params=pltpu.CompilerParams(
            dimension_semantics=("parallel", "arbitrary"), vmem_limit_bytes=VMEM_LIMIT),
        name="in_proj",
    )(x, g_mix.reshape(1, d), w['q'], w['u'], w['mg'], w['gt'], w['kv_t'])


def _pack_w_in(w_in, ssm_width):
    cuts = np.cumsum([Q_COLS, 6 * KV_COLS, 3 * N_HEADS, ssm_width]).tolist()
    wq, wkv, wgt, wu, wmg = jnp.split(w_in, cuts, axis=1)
    assert wq.shape[1] == _Q_T * IN_TN and wkv.shape[1] == _KV_T * IN_TN
    assert wu.shape[1] == _U_T * IN_TN and wmg.shape[1] == _MG_T * IN_TN and IN_TN == N_KV * 128
    d = w_in.shape[0]
    wgt = wgt.reshape(d, 3, N_KV, GROUP).transpose(0, 2, 1, 3).reshape(d, N_KV, 3 * GROUP)
    wgt = jnp.pad(wgt, ((0, 0), (0, 0), (0, 128 - 3 * GROUP))).reshape(d, IN_TN)
    return dict(q=wq.astype(BF16), u=wu.astype(BF16), mg=wmg.astype(BF16), gt=wgt.astype(BF16),
                kv_t=wkv.T.astype(BF16))


def _pool_kernel(*refs, npg, n_tables):
    refs = refs[n_tables:]
    prev_ref = refs[0]
    page_refs = refs[1:npg + 1]
    mk_ref, mv_ref, mprev_ref, out_ref = refs[npg + 1:]
    x = jnp.concatenate([page_refs[j][0] for j in range(npg)], axis=1).astype(BF16)
    xp = prev_ref[0].astype(BF16)
    has_prev = pl.program_id(1) > 0
    for half, m_ref in ((0, mk_ref), (1, mv_ref)):
        rows = slice(half * KV_COLS, (half + 1) * KV_COLS)
        head = _dot(xp[rows], mprev_ref[half])
        out_ref[0, rows, :] = _dot(x[rows], m_ref[...]) + jnp.where(has_prev, head, 0.0)


def _pool_matrices(pool, npg):
    nblk = npg * CMP_PER_PAGE
    cols = []
    for r in range(CMP_PER_PAGE):
        start = CMP_STRIDE * (r - 1)
        lo, hi = max(start, 0), min(start + CMP_LEN, PAGE)
        cols.append(jnp.pad(pool[:, lo - start:hi - start], ((0, 0), (lo, PAGE - hi))))
    body = jnp.stack(cols, axis=-1)
    carry = jnp.pad(pool[:, :CMP_STRIDE], ((0, 0), (PAGE - CMP_STRIDE, 0)))
    first = jnp.asarray(np.arange(CMP_PER_PAGE) == 0, F32)
    same = jnp.eye(npg, dtype=F32)[None, :, None, :, None]
    nxt = jnp.eye(npg, k=1, dtype=F32)[None, :, None, :, None]
    m = same * body[:, None, :, None, :] + nxt * (carry[:, None, :, None, None] * first)
    m = m.reshape(2, npg * PAGE, nblk).astype(BF16)
    mprev = (carry[:, :, None] * jnp.asarray(np.arange(nblk) == 0, F32)).astype(BF16)
    return m[0], m[1], mprev


def _pool_pages(pages, page_index, tables, nseq, npages, mats):
    npg = PAGES_PER_STEP
    nstep = npages // npg
    nblk = npg * CMP_PER_PAGE
    mk, mv, mprev = mats

    def spec(off):
        return pl.BlockSpec((1, 2 * KV_COLS, PAGE),
                            lambda b, s, *t: page_index(b, jnp.maximum(s * npg + off, 0), *t))

    const = lambda shape: pl.BlockSpec(shape, lambda b, s, *t: (0,) * len(shape))
    return pl.pallas_call(
        functools.partial(_pool_kernel, npg=npg, n_tables=len(tables)),
        grid_spec=pltpu.PrefetchScalarGridSpec(
            num_scalar_prefetch=len(tables),
            grid=(nseq, nstep),
            in_specs=[spec(j) for j in range(-1, npg)] + [const(mk.shape), const(mv.shape), const(mprev.shape)],
            out_specs=pl.BlockSpec((1, 2 * KV_COLS, nblk), lambda b, s, *t: (b, 0, s)),
        ),
        out_shape=jax.ShapeDtypeStruct((nseq, 2 * KV_COLS, npages * CMP_PER_PAGE), F32),
        compiler_params=pltpu.CompilerParams(
            dimension_semantics=("parallel", "arbitrary"), vmem_limit_bytes=VMEM_LIMIT),
        name="cmp_pool",
    )(*tables, *([pages] * (npg + 1)), mk, mv, mprev)


def _cmp_finish_kernel(pooled_ref, poolw_ref, pe_ref, phi_ref, kc_ref, vc_ref, kcr_ref):
    bias = jnp.sum(poolw_ref[...] * pe_ref[...], axis=1, keepdims=True)
    x = _gelu(pooled_ref[0] + bias)
    kc = _dot3(phi_ref[0], x[:KV_COLS])
    kc_ref[0] = kc.astype(BF16)
    kcr_ref[0] = kc.T.astype(BF16)
    vc_ref[0] = _dot3(phi_ref[1], x[KV_COLS:]).astype(BF16)


def _cmp_finish(pooled, pool, pe, phi):
    nseq, _, n = pooled.shape
    poolw = jnp.concatenate([jnp.broadcast_to(pool[s][None, :], (KV_COLS, CMP_LEN)) for s in range(2)], axis=0)
    pe_t = jnp.concatenate([jnp.tile(pe[s].T, (N_KV, 1)) for s in range(2)], axis=0)
    phi_bd = jnp.stack([jnp.kron(jnp.eye(N_KV, dtype=F32), phi[s].T) for s in range(2)])
    spec = pl.BlockSpec((1, KV_COLS, n), lambda b: (b, 0, 0))
    return pl.pallas_call(
        _cmp_finish_kernel,
        grid=(nseq,),
        in_specs=[
            pl.BlockSpec((1, 2 * KV_COLS, n), lambda b: (b, 0, 0)),
            pl.BlockSpec((2 * KV_COLS, CMP_LEN), lambda b: (0, 0)),
            pl.BlockSpec((2 * KV_COLS, CMP_LEN), lambda b: (0, 0)),
            pl.BlockSpec((2, KV_COLS, KV_COLS), lambda b: (0, 0, 0)),
        ],
        out_specs=[spec, spec, pl.BlockSpec((1, n, KV_COLS), lambda b: (b, 0, 0))],
        out_shape=[jax.ShapeDtypeStruct((nseq, KV_COLS, n), BF16)] * 2
        + [jax.ShapeDtypeStruct((nseq, n, KV_COLS), BF16)],
        compiler_params=pltpu.CompilerParams(dimension_semantics=("parallel",)),
        name="cmp_finish",
    )(pooled, poolw, pe_t, phi_bd)


ATT_TQ = 256
ATT_TK = 512
NBLK_PAD = 128
POS_ROWS = 16
AUG = HEAD_DIM + POS_ROWS


def _attn_prompt_kernel(q_ref, kca_ref, vc_ref, ks_ref, vs_ref, kw_ref, vw_ref, gt_ref, ovt_ref, ktab_ref,
                        stab_ref, o_ref, ksa_ref, vsa_ref, kwa_ref, vwa_ref, qat_ref, m_ref, acc_ref,
                        *, tq, tk, n_sel, n_top):
    qi = pl.program_id(2)
    q0 = qi * tq
    seq = ksa_ref.shape[0]

    @pl.when(qi == 0)
    def _():
        zero = jnp.zeros((128 - HEAD_DIM, seq), F32)
        ones = jnp.where(_iota((POS_ROWS, seq), 0) == 0, 1.0, 0.0).astype(BF16)
        for src, dst, lo in ((ks_ref, ksa_ref, NBLK_PAD), (kw_ref, kwa_ref, 0)):
            k_rows = jnp.concatenate([src[0], zero], axis=0).T[:, :HEAD_DIM].astype(BF16)
            dst[:, lo:] = jnp.concatenate([k_rows, ktab_ref[:, NBLK_PAD:]], axis=1)
        ksa_ref[:, 0:NBLK_PAD] = ktab_ref[:, 0:NBLK_PAD]
        for src, dst in ((vs_ref, vsa_ref), (vw_ref, vwa_ref)):
            dst[0:HEAD_DIM] = src[0].astype(BF16)
            dst[HEAD_DIM:] = ones

    q_t = q_ref[...].astype(F32).T
    q_t = jnp.concatenate([q_t[n * HEAD_DIM:(n + 1) * HEAD_DIM] for n in range(GROUP)], axis=1).astype(BF16)
    stab = stab_ref[0]
    sp_t = jnp.concatenate([jnp.broadcast_to(stab[:, n:n + 1], (POS_ROWS, tq)) for n in range(GROUP)], axis=1)
    qat_ref[NBLK_PAD:NBLK_PAD + HEAD_DIM] = q_t
    qat_ref[NBLK_PAD + HEAD_DIM:] = sp_t.astype(BF16)
    trow = q0 + _iota((1, tq), 1)
    lanes = [slice(n * tq, (n + 1) * tq) for n in range(GROUP)]

    ncmp = kca_ref.shape[2]
    nidx = _iota((ncmp, 1), 0)
    heads = lambda a: jnp.concatenate([a] * GROUP, axis=1)
    valid = (nidx >= 1) & (CMP_STRIDE * nidx + (CMP_STRIDE - 1) <= trow)
    valid = heads(jnp.where(valid, 1.0, 0.0)) > 0.5
    s = jnp.where(valid, _dot(kca_ref[0, 0], qat_ref[NBLK_PAD:]), NEG)
    e = jnp.exp2(s - jnp.max(s, axis=0, keepdims=True))
    pc = jnp.where(valid, e / jnp.sum(e, axis=0, keepdims=True), 0.0)
    o_c = _dot(vc_ref[0], pc.astype(BF16))

    ph, plo = _split((pc[:, lanes[0]] + pc[:, lanes[1]]) + (pc[:, lanes[2]] + pc[:, lanes[3]]))
    imp = (_dot(ovt_ref[...], ph) + _dot(ovt_ref[...], plo))[0:n_sel]
    blk = _iota((n_sel, 1), 0)
    cur = _div_pow2(q0 + _iota((1, tq), 1), SEL_BLOCK)
    forced = (blk == 0) | (blk == cur) | (blk == cur - 1)
    imp = jnp.where(forced, FORCE_SCORE, imp)
    imp = jnp.where(blk <= cur, imp, -1.0)
    cnt = jnp.zeros((n_sel, tq), F32)
    for j in range(n_sel):
        vj = imp[j:j + 1, :]
        beats = (vj > imp) | ((vj == imp) & (blk > j))
        cnt = cnt + jnp.where(beats, 1.0, 0.0)
    mask_t = jnp.where((cnt < n_top) & (blk <= cur), 0.0, NEG)
    mask_t = jnp.concatenate([mask_t, jnp.full((NBLK_PAD - n_sel, tq), NEG, F32)], axis=0).astype(BF16)
    qat_ref[0:NBLK_PAD] = heads(mask_t)

    m_ref[...] = jnp.full_like(m_ref, NEG)
    acc_ref[...] = jnp.zeros_like(acc_ref)

    def key_tile(kt, diagonal):
        k0 = pl.multiple_of(kt * tk, tk)
        k_a = ksa_ref[pl.ds(k0, tk), :]
        v_a = vsa_ref[:, pl.ds(k0, tk)]
        s = _dot(k_a, qat_ref[...])
        if diagonal:
            s = s + heads(jnp.where(k0 + _iota((tk, 1), 0) <= trow, 0.0, NEG))
        m_old = m_ref[...]
        m_new = jnp.maximum(m_old, jnp.max(s, axis=0, keepdims=True))
        p = jnp.exp2(s - m_new).astype(BF16)
        acc_ref[...] = jnp.exp2(m_old - m_new) * acc_ref[...] + _dot(v_a, p)
        m_ref[...] = m_new

    n_tiles = (q0 + tq + tk - 1) // tk

    def body(kt, carry):
        key_tile(kt, False)
        return carry

    lax.fori_loop(0, n_tiles - 1, body, 0)
    key_tile(n_tiles - 1, True)
    acc = acc_ref[...]
    o_s = acc[:HEAD_DIM] / acc[HEAD_DIM:HEAD_DIM + 1]

    span = WINDOW + tq
    w0 = pl.multiple_of(jnp.maximum(q0 - WINDOW, 0), tq)
    kw_a = kwa_ref[pl.ds(w0, span), :]
    vw_a = vwa_ref[:, pl.ds(w0, span)]
    dist = trow - (w0 + _iota((span, 1), 0))
    s = _dot(kw_a, qat_ref[NBLK_PAD:]) + heads(jnp.where((dist >= 0) & (dist < WINDOW), 0.0, NEG))
    ow = _dot(vw_a, jnp.exp2(s - jnp.max(s, axis=0, keepdims=True)).astype(BF16))
    o_w = ow[:HEAD_DIM] / ow[HEAD_DIM:HEAD_DIM + 1]

    gt_t = gt_ref[...].T
    outs = []
    for n in range(GROUP):
        outs.append(gt_t[n:n + 1] * o_c[:, lanes[n]] + gt_t[GROUP + n:GROUP + n + 1] * o_s[:, lanes[n]]
                    + gt_t[2 * GROUP + n:2 * GROUP + n + 1] * o_w[:, lanes[n]])
    o_ref[...] = jnp.concatenate(outs, axis=0).T.astype(BF16)


def _overlap_t(n_blk_rows, n_cmp):
    s = np.arange(n_blk_rows)[:, None]
    n = np.arange(n_cmp)[None, :]
    r = SEL_BLOCK // CMP_STRIDE
    return jnp.asarray(((n >= r * s) & (n <= r * s + r)).astype(np.float32), dtype=BF16)


def _block_expand(n_blk_rows, n_keys):
    s = np.arange(n_blk_rows)[:, None]
    k = np.arange(n_keys)[None, :]
    return jnp.asarray((k // SEL_BLOCK == s).astype(np.float32), dtype=BF16)


def _attn_prompt(slopes, q, kc_rows, vc_t, kv_t, gates, nbatch, seq):
    ncmp = kc_rows.shape[1]
    tq, tk = ATT_TQ, ATT_TK
    assert tk % tq == 0 and seq % tk == 0 and WINDOW % tq == 0 and WINDOW + tq <= seq
    nq = seq // tq
    n_sel = seq // SEL_BLOCK
    ovt = _overlap_t(NBLK_PAD, ncmp)
    ktab = jnp.concatenate([_block_expand(NBLK_PAD, seq), _position_rows(seq)], axis=0).T
    cmp_end = CMP_STRIDE * np.arange(ncmp) + (CMP_STRIDE - 1)
    kca = jnp.concatenate([kc_rows.reshape(nbatch, ncmp, N_KV, HEAD_DIM).transpose(0, 2, 1, 3),
                           jnp.broadcast_to(_position_rows(cmp_end).T, (nbatch, N_KV, ncmp, POS_ROWS))], axis=-1)
    stab = _slope_table(slopes).astype(F32).transpose(0, 2, 1)
    per_slot = KV_COLS // HEAD_DIM

    def kv_spec(slot):
        return pl.BlockSpec((1, HEAD_DIM, seq), lambda b, g, i: (b, slot * per_slot + g, 0))

    q_spec = pl.BlockSpec((tq, GROUP * HEAD_DIM), lambda b, g, i: (b * nq + i, g))
    return pl.pallas_call(
        functools.partial(_attn_prompt_kernel, tq=tq, tk=tk, n_sel=n_sel, n_top=min(SEL_TOP, n_sel)),
        grid=(nbatch, N_KV, nq),
        in_specs=[
            q_spec,
            pl.BlockSpec((1, 1, ncmp, AUG), lambda b, g, i: (b, g, 0, 0)),
            pl.BlockSpec((1, HEAD_DIM, ncmp), lambda b, g, i: (b, g, 0)),
            kv_spec(2), kv_spec(3), kv_spec(4), kv_spec(5),
            pl.BlockSpec((tq, 128), lambda b, g, i: (b * nq + i, g)),
            pl.BlockSpec((NBLK_PAD, ncmp), lambda b, g, i: (0, 0)),
            pl.BlockSpec((seq, NBLK_PAD + POS_ROWS), lambda b, g, i: (0, 0)),
            pl.BlockSpec((1, POS_ROWS, GROUP), lambda b, g, i: (g, 0, 0)),
        ],
        out_specs=q_spec,
        out_shape=jax.ShapeDtypeStruct(q.shape, BF16),
        scratch_shapes=[
            pltpu.VMEM((seq, NBLK_PAD + AUG), BF16),
            pltpu.VMEM((AUG, seq), BF16),
            pltpu.VMEM((seq, AUG), BF16),
            pltpu.VMEM((AUG, seq), BF16),
            pltpu.VMEM((NBLK_PAD + AUG, GROUP * tq), BF16),
            pltpu.VMEM((1, GROUP * tq), F32),
            pltpu.VMEM((AUG, GROUP * tq), F32),
        ],
        compiler_params=pltpu.CompilerParams(
            dimension_semantics=("parallel", "parallel", "arbitrary"), vmem_limit_bytes=VMEM_LIMIT),
        name="attn_prompt",
    )(q, kca, vc_t, kv_t, kv_t, kv_t, kv_t, gates, ovt, ktab, stab)


def _diag_blocks(o, maskbd):
    o = o * maskbd
    return (o[:, 0:64] + o[:, 64:128]) + (o[:, 128:192] + o[:, 192:256])


def _attn_dec1_kernel(qbd_ref, kc_ref, vc_ref, win_ref, kvnew_ref, gates_ref, slope_ref, ovt_ref,
                      maskbd_ref, opart_ref, sel_ref, *, past, tdec, n_blk, n_top):
    qb = qbd_ref[0]
    nrows = qb.shape[0]
    tcol = past + _mod_pow2(_iota((nrows, 1), 0), tdec)
    pref = float(past + tdec)
    slope = slope_ref[...]
    maskbd = maskbd_ref[...]

    ncmp = kc_ref.shape[2]
    nrow = _iota((1, ncmp), 1)
    cend = CMP_STRIDE * nrow + (CMP_STRIDE - 1)
    valid = (nrow >= 1) & (cend <= tcol)
    s = _dot(qb, kc_ref[0])
    s = jnp.where(valid, s + slope * (cend.astype(F32) - pref), NEG)
    p = jnp.where(valid, _softmax2_rows(s), 0.0)
    o_c = _diag_blocks(_dot_nt(p.astype(BF16), vc_ref[0]), maskbd)

    per_g = GROUP * tdec
    psum = jnp.concatenate(
        [sum(p[gi * per_g + n * tdec: gi * per_g + (n + 1) * tdec] for n in range(GROUP)) for gi in range(N_KV)],
        axis=0)
    imp = _dot_nt_hl(ovt_ref[...], psum)
    nb_pad = imp.shape[0]
    blk = _iota((nb_pad, 1), 0)
    cur = _div_pow2(past + _mod_pow2(_iota((1, N_KV * tdec), 1), tdec), SEL_BLOCK)
    forced = (blk == 0) | (blk == cur) | (blk == cur - 1)
    imp = jnp.where(forced, FORCE_SCORE, imp)
    imp = jnp.where(blk <= cur, imp, -1.0)
    imp = jnp.where(blk < n_blk, imp, -2.0)
    sel = jnp.zeros(imp.shape, F32)
    for _ in range(n_top):
        mx = jnp.max(imp, axis=0, keepdims=True)
        first = jnp.min(jnp.where(imp == mx, blk, nb_pad), axis=0, keepdims=True)
        pick = blk == first
        sel = jnp.where(pick, 1.0, sel)
        imp = jnp.where(pick, -jnp.inf, imp)
    sel_ref[0] = sel

    nbuf = win_ref.shape[2]
    knew = kvnew_ref[0]
    zpad = jnp.zeros((PAGE - tdec, KV_COLS), F32)
    kn = jnp.concatenate([knew[:, 4 * KV_COLS:5 * KV_COLS], zpad], axis=0).astype(BF16)
    vn = jnp.concatenate([knew[:, 5 * KV_COLS:], zpad], axis=0).astype(BF16)
    sw = jnp.concatenate([_dot(qb, win_ref[0, :KV_COLS].astype(BF16)), _dot_nt(qb, kn)], axis=1)
    idx = _iota((1, nbuf + PAGE), 1)
    kposw = past - nbuf + idx
    dist = tcol - kposw
    validw = (dist >= 0) & (dist < WINDOW) & (kposw >= 0) & (idx < nbuf + tdec)
    sw = jnp.where(validw, sw + slope * (kposw.astype(F32) - pref), NEG)
    pw = _softmax2_rows(sw).astype(BF16)
    o_w = _dot_nt(pw[:, :nbuf], win_ref[0, KV_COLS:].astype(BF16)) + _dot(pw[:, nbuf:], vn)
    o_w = _diag_blocks(o_w, maskbd)

    gates = gates_ref[0]
    opart_ref[0] = gates[:, 0:1] * o_c + gates[:, 2:3] * o_w


def _attn_dec1(qbd, kc_t, vc_t, win_t, kvnew, gates_rows, slopecol, maskbd, past, tdec, n_blk):
    nseq, nrows, _ = qbd.shape
    ncmp = kc_t.shape[2]
    nb_pad = -(-n_blk // 8) * 8
    ovt = _overlap_t(nb_pad, ncmp)
    nbuf = win_t.shape[2]
    return pl.pallas_call(
        functools.partial(_attn_dec1_kernel, past=past, tdec=tdec, n_blk=n_blk, n_top=min(SEL_TOP, n_blk)),
        grid=(nseq,),
        in_specs=[
            pl.BlockSpec((1, nrows, KV_COLS), lambda b: (b, 0, 0)),
            pl.BlockSpec((1, KV_COLS, ncmp), lambda b: (b, 0, 0)),
            pl.BlockSpec((1, KV_COLS, ncmp), lambda b: (b, 0, 0)),
            pl.BlockSpec((1, 2 * KV_COLS, nbuf), lambda b: (b, 0, 0)),
            pl.BlockSpec((1, tdec, 6 * KV_COLS), lambda b: (b, 0, 0)),
            pl.BlockSpec((1, nrows, 4), lambda b: (b, 0, 0)),
            pl.BlockSpec((nrows, 1), lambda b: (0, 0)),
            pl.BlockSpec((nb_pad, ncmp), lambda b: (0, 0)),
            pl.BlockSpec((nrows, KV_COLS), lambda b: (0, 0)),
        ],
        out_specs=[
            pl.BlockSpec((1, nrows, HEAD_DIM), lambda b: (b, 0, 0)),
            pl.BlockSpec((1, nb_pad, N_KV * tdec), lambda b: (b, 0, 0)),
        ],
        out_shape=[
            jax.ShapeDtypeStruct((nseq, nrows, HEAD_DIM), F32),
            jax.ShapeDtypeStruct((nseq, nb_pad, N_KV * tdec), F32),
        ],
        compiler_params=pltpu.CompilerParams(dimension_semantics=("parallel",), vmem_limit_bytes=VMEM_LIMIT),
        name="attn_dec_select",
    )(qbd, kc_t, vc_t, win_t, kvnew, gates_rows, slopecol, ovt, maskbd)


def _attn_dec2_kernel(pt_ref, *refs, npg, past, tdec):
    page_refs = refs[:npg]
    (qbd_ref, selrow_ref, sellast_ref, e_ref, slope_ref, knew_ref, opart_ref, gates_ref, maskbd_ref,
     o_ref, m_ref, l_ref, acc_ref) = refs[npg:]
    step = pl.program_id(1)
    nkeys = npg * PAGE
    qb = qbd_ref[0]
    nrows = qb.shape[0]
    pref = float(past + tdec)
    slope = slope_ref[...]

    @pl.when(step == 0)
    def _():
        m_ref[...] = jnp.full_like(m_ref, NEG)
        l_ref[...] = jnp.zeros_like(l_ref)
        acc_ref[...] = jnp.zeros_like(acc_ref)

    def update(s, pv_fn):
        m_old = m_ref[...]
        m_new = jnp.maximum(m_old, jnp.max(s, axis=-1, keepdims=True))
        alpha = jnp.exp2(m_old - m_new)
        p = jnp.exp2(s - m_new)
        l_ref[...] = alpha * l_ref[...] + jnp.sum(p, axis=-1, keepdims=True)
        acc_ref[...] = alpha * acc_ref[...] + pv_fn(p.astype(BF16))
        m_ref[...] = m_new

    s = jnp.concatenate([_dot(qb, page_refs[j][0, :KV_COLS].astype(BF16)) for j in range(npg)], axis=1)
    kpos = step * nkeys + _iota((1, nkeys), 1)
    chosen = _dot(selrow_ref[0, 0], e_ref[...])
    s = s + slope * (kpos.astype(F32) - pref) + jnp.where(chosen > 0.5, 0.0, NEG)

    def pv_pages(p):
        acc = _dot_nt(p[:, 0:PAGE], page_refs[0][0, KV_COLS:].astype(BF16))
        for j in range(1, npg):
            acc = acc + _dot_nt(p[:, PAGE * j:PAGE * (j + 1)], page_refs[j][0, KV_COLS:].astype(BF16))
        return acc

    update(s, pv_pages)

    @pl.when(step == pl.num_programs(1) - 1)
    def _():
        tcol = past + _mod_pow2(_iota((nrows, 1), 0), tdec)
        kn = knew_ref[0]
        kposn = past + _iota((1, PAGE), 1)
        sn = _dot_nt(qb, kn[:, :KV_COLS].astype(BF16))
        ok = (sellast_ref[0, 0][:, 0:1].astype(F32) > 0.5) & (kposn <= tcol)
        sn = sn + slope * (kposn.astype(F32) - pref) + jnp.where(ok, 0.0, NEG)
        update(sn, lambda p: _dot(p, kn[:, KV_COLS:].astype(BF16)))
        o_s = _diag_blocks(acc_ref[...] / l_ref[...], maskbd_ref[...])
        o_ref[0] = opart_ref[0] + gates_ref[0][:, 1:2] * o_s


def _attn_dec2(cache_t, page_table, qbd, selrows, slopecol, knew, opart, gates_rows, maskbd, past, tdec):
    nseq, npages = page_table.shape
    npg = PAGES_PER_STEP
    nstep = npages // npg
    nrows = qbd.shape[1]
    blk_per_step = npg * PAGE // SEL_BLOCK
    emat = _block_expand(blk_per_step, npg * PAGE)
    page_specs = [
        pl.BlockSpec((1, 2 * KV_COLS, PAGE), lambda b, s, pt, j=j: (pt[b, s * npg + j], 1, 0))
        for j in range(npg)
    ]
    per_seq = lambda shape: pl.BlockSpec((1,) + shape, lambda b, s, pt: (b,) + (0,) * len(shape))
    const = lambda shape: pl.BlockSpec(shape, lambda b, s, pt: (0,) * len(shape))
    return pl.pallas_call(
        functools.partial(_attn_dec2_kernel, npg=npg, past=past, tdec=tdec),
        grid_spec=pltpu.PrefetchScalarGridSpec(
            num_scalar_prefetch=1,
            grid=(nseq, nstep),
            in_specs=page_specs + [
                per_seq((nrows, KV_COLS)),
                pl.BlockSpec((1, 1, nrows, blk_per_step), lambda b, s, pt: (b, s, 0, 0)),
                pl.BlockSpec((1, 1, nrows, blk_per_step), lambda b, s, pt: (b, nstep, 0, 0)),
                const((blk_per_step, npg * PAGE)),
                const((nrows, 1)),
                per_seq((PAGE, 2 * KV_COLS)),
                per_seq((nrows, HEAD_DIM)),
                per_seq((nrows, 4)),
                const((nrows, KV_COLS)),
            ],
            out_specs=per_seq((nrows, HEAD_DIM)),
            scratch_shapes=[
                pltpu.VMEM((nrows, 1), F32),
                pltpu.VMEM((nrows, 1), F32),
                pltpu.VMEM((nrows, KV_COLS), F32),
            ],
        ),
        out_shape=jax.ShapeDtypeStruct((nseq, nrows, HEAD_DIM), F32),
        compiler_params=pltpu.CompilerParams(
            dimension_semantics=("parallel", "arbitrary"), vmem_limit_bytes=VMEM_LIMIT),
        name="attn_dec_selected",
    )(page_table, *([cache_t] * npg), qbd, selrows, selrows, emat, slopecol, knew, opart, gates_rows, maskbd)


def _s5_weights(lam_re, lam_im, log_dt, b_re, b_im, c_re, c_im, d_skip):
    hp = lax.Precision.HIGHEST
    L = S5_L
    ng, p = lam_re.shape
    c = SSM_GROUP
    ns, gs = ng // S5_SLAB, S5_SLAB
    dt = jnp.exp(log_dt)[:, None]
    lr, li = lam_re, lam_im
    mag = jnp.exp(lr * dt)
    ar = mag * jnp.cos(li * dt)
    ai = mag * jnp.sin(li * dt)
    den = lr * lr + li * li
    fr = ((ar - 1.0) * lr + ai * li) / den
    fi = (ai * lr - (ar - 1.0) * li) / den
    bbr = fr[..., None] * b_re - fi[..., None] * b_im
    bbi = fr[..., None] * b_im + fi[..., None] * b_re
    j = jnp.arange(L + 1, dtype=F32)[:, None, None]
    pmag = jnp.exp(j * (lr * dt))
    pr = pmag * jnp.cos(j * (li * dt))
    pi = pmag * jnp.sin(j * (li * dt))
    abr = pr[..., None] * bbr - pi[..., None] * bbi
    abi = pr[..., None] * bbi + pi[..., None] * bbr
    kd = (jnp.einsum('gcp,jgpk->jgck', c_re, abr, precision=hp)
          - jnp.einsum('gcp,jgpk->jgck', c_im, abi, precision=hp))
    kd = kd.at[0].add(d_skip.reshape(ng, c)[:, :, None] * jnp.eye(c, dtype=F32))
    kc = kd[:L].reshape(L, ns, gs, c, c).transpose(1, 0, 2, 4, 3).reshape(ns, L, gs * c, c)
    ab = jnp.stack([abr[:L], abi[:L]]).reshape(2, L, ns, gs, p, c)
    abc = ab.transpose(2, 1, 3, 5, 0, 4).reshape(ns, L, gs * c, 2 * p)
    cr = c_re[None] * pr[1:, :, None, :] - c_im[None] * pi[1:, :, None, :]
    ci = -(c_re[None] * pi[1:, :, None, :] + c_im[None] * pr[1:, :, None, :])
    co = jnp.stack([cr, ci]).reshape(2, L, ns, gs, c, p)
    coc = co.transpose(2, 1, 0, 3, 5, 4).reshape(ns, L, 2 * gs * p, c)
    al = jnp.stack([pr[L].reshape(-1), pi[L].reshape(-1)])
    return kc.astype(BF16), abc.astype(BF16), coc.astype(BF16), al


def _s5_expanders():
    gs, c, p = S5_SLAB, SSM_GROUP, SSM_STATE
    lane = np.arange(gs * c)
    rep_c = (np.arange(c)[:, None] == lane[None, :] % c)
    same_g = (lane[:, None] // c == lane[None, :] // c)
    st = np.arange(2 * gs * p)
    rp = np.arange(2 * p)
    rep_p = (rp[:, None] // p == st[None, :] // (gs * p)) & (rp[:, None] % p == st[None, :] % p)
    g_in = (lane[:, None] // c == (st[None, :] // p) % gs)
    g_out = ((st[:, None] // p) % gs == lane[None, :] // c)
    as_bf = lambda a: jnp.asarray(a.astype(np.float32), dtype=BF16)
    return as_bf(rep_c), as_bf(same_g), as_bf(rep_p), as_bf(g_in), as_bf(g_out)


def _chunk_lanes(u_ref):
    return jnp.concatenate([u_ref[0, :, t, :] for t in range(S5_L)], axis=1)


def _s5_local_kernel(u_ref, abc_ref, rep_p_ref, g_in_ref, pre_ref, pim_ref, wp_ref):
    for s in range(S5_L):
        blk = _dot(abc_ref[0, S5_L - 1 - s], rep_p_ref[...]).astype(BF16) * g_in_ref[...]
        wp_ref[128 * s:128 * (s + 1), :] = blk
    r = _dot(_chunk_lanes(u_ref).astype(BF16), wp_ref[...])
    half = r.shape[1] // 2
    pre_ref[...] = r[:, :half]
    pim_ref[...] = r[:, half:]


def _s5_scan_kernel(pre_ref, pim_ref, al_ref, h0r_ref, h0i_ref, hsr_ref, hsi_ref, hfr_ref, hfi_ref):
    nchunk = pre_ref.shape[1]
    ar = al_ref[0:1, :]
    ai = al_ref[1:2, :]

    def body(k, carry):
        cr, ci = carry
        hsr_ref[:, pl.ds(k, 1), :] = cr[:, None, :]
        hsi_ref[:, pl.ds(k, 1), :] = ci[:, None, :]
        xr = pre_ref[:, pl.ds(k, 1), :][:, 0, :]
        xi = pim_ref[:, pl.ds(k, 1), :][:, 0, :]
        return ar * cr - ai * ci + xr, ar * ci + ai * cr + xi

    cr, ci = lax.fori_loop(0, nchunk, body, (h0r_ref[...], h0i_ref[...]))
    hfr_ref[...] = cr
    hfi_ref[...] = ci


def _s5_out_kernel(u_ref, kc_ref, coc_ref, rep_c_ref, same_g_ref, g_out_ref, hsr_ref, hsi_ref, y_ref,
                   wt_ref, wo_ref):
    rep_c = rep_c_ref[...]
    lag = [_dot(kc_ref[0, dl], rep_c).astype(BF16) * same_g_ref[...] for dl in range(S5_L)]
    zero = jnp.zeros((128, 128), BF16)
    for s in range(S5_L):
        for t in range(S5_L):
            wt_ref[128 * s:128 * (s + 1), 128 * t:128 * (t + 1)] = lag[t - s] if t >= s else zero
    for t in range(S5_L):
        wo_ref[:, 128 * t:128 * (t + 1)] = _dot(coc_ref[0, t], rep_c).astype(BF16) * g_out_ref[...]
    hs = jnp.concatenate([hsr_ref[...], hsi_ref[...]], axis=1).astype(BF16)
    y = _dot(_chunk_lanes(u_ref).astype(BF16), wt_ref[...]) + _dot(hs, wo_ref[...])
    for t in range(S5_L):
        y_ref[0, :, t, :] = y[:, 128 * t:128 * (t + 1)]


def _s5(u3, h0r, h0i, weights, nbatch):
    kc, abc, coc, al = weights
    rep_c, same_g, rep_p, g_in, g_out = _s5_expanders()
    ns, nb = u3.shape[0], u3.shape[1]
    nchunk = nb // nbatch
    gp = al.shape[1]
    lb = gp // ns
    d = S5_L * 128
    cp = pltpu.CompilerParams(dimension_semantics=("parallel",), vmem_limit_bytes=VMEM_LIMIT)
    u_spec = pl.BlockSpec((1, nb, S5_L, 128), lambda j: (j, 0, 0, 0))
    st_spec = pl.BlockSpec((nb, lb), lambda j: (0, j))
    slab = lambda a: pl.BlockSpec((1,) + a.shape[1:], lambda j: (j,) + (0,) * (a.ndim - 1))
    const = lambda a: pl.BlockSpec(a.shape, lambda j: (0,) * a.ndim)
    pre, pim = pl.pallas_call(
        _s5_local_kernel,
        grid=(ns,),
        in_specs=[u_spec, slab(abc), const(rep_p), const(g_in)],
        out_specs=[st_spec] * 2,
        out_shape=[jax.ShapeDtypeStruct((nb, gp), F32)] * 2,
        scratch_shapes=[pltpu.VMEM((d, 2 * lb), BF16)],
        compiler_params=cp,
        name="s5_local",
    )(u3, abc, rep_p, g_in)

    seq3 = pl.BlockSpec((nbatch, nchunk, lb), lambda j: (0, 0, j))
    row = lambda r: pl.BlockSpec((r, lb), lambda j: (0, j))
    hsr, hsi, hfr, hfi = pl.pallas_call(
        _s5_scan_kernel,
        grid=(ns,),
        in_specs=[seq3, seq3, row(2), row(nbatch), row(nbatch)],
        out_specs=[seq3, seq3, row(nbatch), row(nbatch)],
        out_shape=[jax.ShapeDtypeStruct((nbatch, nchunk, gp), F32)] * 2 + [jax.ShapeDtypeStruct((nbatch, gp), F32)] * 2,
        compiler_params=cp,
        name="s5_scan",
    )(pre.reshape(nbatch, nchunk, gp), pim.reshape(nbatch, nchunk, gp), al, h0r, h0i)

    y3 = pl.pallas_call(
        _s5_out_kernel,
        grid=(ns,),
        in_specs=[u_spec, slab(kc), slab(coc), const(rep_c), const(same_g), const(g_out), st_spec, st_spec],
        out_specs=u_spec,
        out_shape=jax.ShapeDtypeStruct(u3.shape, F32),
        scratch_shapes=[pltpu.VMEM((d, d), BF16), pltpu.VMEM((2 * lb, d), BF16)],
        compiler_params=cp,
        name="s5_out",
    )(u3, kc, coc, rep_c, same_g, g_out, hsr.reshape(nb, gp), hsi.reshape(nb, gp))
    return y3, hfr, hfi


def _merge_kernel(o_ref, y_ref, wa_ref, wg1_ref, wg2_ref, sga_ref, sgb_ref, m_ref, gy_ref):
    @pl.when(pl.program_id(1) == 0)
    def _():
        y = jnp.concatenate([y_ref[s] for s in range(y_ref.shape[0])], axis=1)
        gy_ref[...] = _gelu(y).astype(BF16)

    ya = _dot(o_ref[...], wa_ref[...])
    gy = gy_ref[...]
    yb = _dot(gy, wg1_ref[...]) * jax.nn.sigmoid(_dot(gy, wg2_ref[...]))
    m = sga_ref[...].astype(F32) * ya + sgb_ref[...].astype(F32) * yb
    m_ref[...] = m.astype(BF16)


def _merge(o, y3, w_attn_out, w_glu, sg, tm, tn=512):
    rows, d = o.shape[0], w_attn_out.shape[1]
    nj = d // tn
    kq, ks = w_attn_out.shape[0], w_glu.shape[0]
    ns = y3.shape[0]
    return pl.pallas_call(
        _merge_kernel,
        grid=(rows // tm, nj),
        in_specs=[
            pl.BlockSpec((tm, kq), lambda i, j: (i, 0)),
            pl.BlockSpec((ns, tm, 128), lambda i, j: (0, i, 0)),
            pl.BlockSpec((kq, tn), lambda i, j: (0, j)),
            pl.BlockSpec((ks, tn), lambda i, j: (0, j)),
            pl.BlockSpec((ks, tn), lambda i, j: (0, j + nj)),
            pl.BlockSpec((tm, tn), lambda i, j: (i, j)),
            pl.BlockSpec((tm, tn), lambda i, j: (i, j + nj)),
        ],
        out_specs=pl.BlockSpec((tm, tn), lambda i, j: (i, j)),
        out_shape=jax.ShapeDtypeStruct((rows, d), BF16),
        scratch_shapes=[pltpu.VMEM((tm, ks), BF16)],
        compiler_params=pltpu.CompilerParams(
            dimension_semantics=("parallel", "arbitrary"), vmem_limit_bytes=VMEM_LIMIT),
        name="merge",
    )(o, y3, w_attn_out, w_glu, w_glu, sg, sg)


def _outproj_kernel(m_ref, x_ref, w_ref, g_ref, x1_ref, h_ref):
    x1 = x_ref[...] + _dot(m_ref[...], w_ref[...])
    x1_ref[...] = x1
    r = lax.rsqrt(jnp.mean(x1 * x1, axis=-1, keepdims=True) + EPS)
    h_ref[...] = (x1 * r * g_ref[...]).astype(BF16)


def _outproj(m, x, w_out, g_ffn, tm):
    rows, d = x.shape
    row = pl.BlockSpec((tm, d), lambda i: (i, 0))
    return pl.pallas_call(
        _outproj_kernel,
        grid=(rows // tm,),
        in_specs=[row, row, pl.BlockSpec((d, d), lambda i: (0, 0)), pl.BlockSpec((1, d), lambda i: (0, 0))],
        out_specs=[row, row],
        out_shape=[jax.ShapeDtypeStruct((rows, d), F32), jax.ShapeDtypeStruct((rows, d), BF16)],
        compiler_params=pltpu.CompilerParams(dimension_semantics=("parallel",), vmem_limit_bytes=VMEM_LIMIT),
        name="out_proj",
    )(m, x, w_out, g_ffn.reshape(1, d))


def _ffn_kernel(h_ref, x1_ref, wg_ref, wu_ref, wd_ref, gf_ref, y_ref, acc_ref):
    f = pl.program_id(1)

    @pl.when(f == 0)
    def _():
        acc_ref[...] = jnp.zeros_like(acc_ref)

    h = h_ref[...]
    a = _dot(h, wg_ref[...])
    a = (a * jax.nn.sigmoid(a)) * _dot(h, wu_ref[...])
    acc_ref[...] += _dot(a.astype(BF16), wd_ref[...])

    @pl.when(f == pl.num_programs(1) - 1)
    def _():
        y = x1_ref[...] + acc_ref[...]
        r = lax.rsqrt(jnp.mean(y * y, axis=-1, keepdims=True) + EPS)
        y_ref[...] = y * r * gf_ref[...]


def _ffn(h, x1, w_gate, w_up, w_down, g_final, tm, tf=512):
    rows, d = x1.shape
    dff = w_gate.shape[1]
    row = pl.BlockSpec((tm, d), lambda i, f: (i, 0))
    return pl.pallas_call(
        _ffn_kernel,
        grid=(rows // tm, dff // tf),
        in_specs=[row, row,
                  pl.BlockSpec((d, tf), lambda i, f: (0, f)),
                  pl.BlockSpec((d, tf), lambda i, f: (0, f)),
                  pl.BlockSpec((tf, d), lambda i, f: (f, 0)),
                  pl.BlockSpec((1, d), lambda i, f: (0, 0))],
        out_specs=row,
        out_shape=jax.ShapeDtypeStruct((rows, d), F32),
        scratch_shapes=[pltpu.VMEM((tm, d), F32)],
        compiler_params=pltpu.CompilerParams(
            dimension_semantics=("parallel", "arbitrary"), vmem_limit_bytes=VMEM_LIMIT),
        name="ffn",
    )(h, x1, w_gate, w_up, w_down, g_final.reshape(1, d))


def _alibi_slopes():
    return jnp.exp2(-8.0 * jnp.arange(1, N_HEADS + 1, dtype=F32) / N_HEADS) * LOG2E


def _slope_table(slopes):
    s1 = slopes.astype(BF16)
    r1 = slopes - s1.astype(F32)
    s2 = r1.astype(BF16)
    s3 = (r1 - s2.astype(F32)).astype(BF16)
    tab = jnp.stack([s1, s1, s2, s2, s3, s3] + [jnp.zeros_like(s1)] * 10, axis=-1)
    return tab.reshape(N_KV, GROUP, 16)


def _position_rows(pos):
    pos = np.arange(pos) if np.isscalar(pos) else np.asarray(pos)
    hi, lo = (pos // 64) * 64, pos % 64
    assert pos.max() < 64 * 256
    rows = np.stack([hi, lo, hi, lo, hi, lo] + [np.zeros_like(pos)] * 10).astype(np.float32)
    return jnp.asarray(rows, dtype=BF16)


def _layer_weights(p, l):
    w_in = _pack_w_in(p['w_in'][l], p['ssm_d'].shape[-1])
    s5 = _s5_weights(*(p[n][l] for n in ('ssm_lam_re', 'ssm_lam_im', 'ssm_log_dt', 'ssm_b_re', 'ssm_b_im',
                                          'ssm_c_re', 'ssm_c_im', 'ssm_d')))
    cast = lambda n: p[n][l].astype(BF16)
    return dict(
        g_mix=p['g_mix'][l], w_in=w_in,
        pool=p['w_cmp_pool'][l], pe=p['w_cmp_pe'][l], phi=p['w_cmp_phi'][l],
        pool_mats=_pool_matrices(p['w_cmp_pool'][l], PAGES_PER_STEP),
        w_attn_out=cast('w_attn_out'), w_glu=cast('w_glu'), w_out=cast('w_out'), g_ffn=p['g_ffn'][l],
        w_gate=cast('w_gate'), w_up=cast('w_up'), w_down=cast('w_down'), s5=s5)


def _tail(x, o, y3, sg, lw, g_final, tm):
    rows = x.shape[0]
    m = _merge(o, y3.reshape(y3.shape[0], rows, 128), lw['w_attn_out'], lw['w_glu'], sg, tm)
    x1, h = _outproj(m, x, lw['w_out'], lw['g_ffn'], min(tm, 256))
    return _ffn(h, x1, lw['w_gate'], lw['w_up'], lw['w_down'], g_final, tm)


def _feature_major_rows(kv_t, t0):
    b, f, t = kv_t.shape
    return kv_t[:, :, t0:].reshape(b, f // KV_COLS, N_KV, HEAD_DIM, t - t0).transpose(0, 4, 1, 2, 3)


def _layer_prompt(x, lw, slopes, g_out):
    b, t, d = x.shape
    rows = b * t
    assert t % (PAGES_PER_STEP * PAGE) == 0
    q, u3, sg, gt, kv_t = _in_proj(x.reshape(rows, d), lw['g_mix'], lw['w_in'], 512, seq_len=t)

    pooled = _pool_pages(kv_t, lambda bi, pg: (bi, 0, pg), (), b, t // PAGE, lw['pool_mats'])
    _, vc_t, kc_rows = _cmp_finish(pooled, lw['pool'], lw['pe'], lw['phi'])
    o = _attn_prompt(slopes, q, kc_rows, vc_t, kv_t, gt, b, t)

    gp = lw['s5'][3].shape[1]
    h0 = jnp.zeros((b, gp), F32)
    y3, hr, hi = _s5(u3, h0, h0, lw['s5'], b)

    y = _tail(x.reshape(rows, d), o, y3, sg, lw, g_out, 512).reshape(b, t, d)
    n_win = min(WINDOW, t)
    ng = gp // SSM_STATE
    return (y, _feature_major_rows(kv_t[:, :4 * KV_COLS], 0), _feature_major_rows(kv_t[:, 4 * KV_COLS:], t - n_win),
            hr.reshape(b, ng, SSM_STATE), hi.reshape(b, ng, SSM_STATE))


def _layer_sample(x, cache, page_table, win_buf, h_re, h_im, lw, slopes, g_out):
    b, t, d = x.shape
    rows = b * t
    npages = page_table.shape[1]
    past = npages * cache.shape[1]
    n_buf = win_buf.shape[1]
    assert cache.shape[1] == PAGE and rows % S5_L == 0 and t == S5_L
    assert past % CMP_STRIDE == 0 and t < CMP_STRIDE and past % SEL_BLOCK == 0 and t <= SEL_BLOCK
    q, u3, sg, gt, kv = _in_proj(x.reshape(rows, d), lw['g_mix'], lw['w_in'], rows)

    cache_t = cache.transpose(0, 2, 3, 4, 1).reshape(cache.shape[0], 4 * KV_COLS, PAGE)
    win_t = win_buf.transpose(0, 2, 3, 4, 1).reshape(b, 2 * KV_COLS, n_buf)
    pooled = _pool_pages(cache_t, lambda bi, pg, pt: (pt[bi, pg], 0, 0), (page_table,), b, npages, lw['pool_mats'])
    kc_t, vc_t, _ = _cmp_finish(pooled, lw['pool'], lw['pe'], lw['phi'])

    nrows = N_HEADS * t
    eye = jnp.eye(N_KV, dtype=BF16)
    q5 = q.reshape(b, t, N_KV, GROUP, HEAD_DIM).transpose(0, 2, 3, 1, 4)
    qbd = (q5[:, :, :, :, None, :] * eye[None, :, None, None, :, None]).reshape(b, nrows, KV_COLS)
    maskbd = jnp.repeat(jnp.repeat(jnp.eye(N_KV, dtype=F32), GROUP * t, axis=0), HEAD_DIM, axis=1)
    slopecol = jnp.repeat(slopes, t).reshape(nrows, 1)
    g3 = gt.reshape(b, t, N_KV, 128)[..., :3 * GROUP].reshape(b, t, N_KV, 3, GROUP)
    g3 = g3.transpose(0, 2, 4, 1, 3).reshape(b, nrows, 3)
    gates_rows = jnp.pad(g3, ((0, 0), (0, 0), (0, 1)))
    kv3 = kv.reshape(b, t, 6 * KV_COLS)
    n_blk = -(-(past + t) // SEL_BLOCK)

    opart, sel_t = _attn_dec1(qbd, kc_t, vc_t, win_t, kv3, gates_rows, slopecol, maskbd, past, t, n_blk)
    bps = PAGES_PER_STEP * PAGE // SEL_BLOCK
    nstep = npages // PAGES_PER_STEP
    nb_all = (nstep + 1) * bps
    sel = jnp.pad(sel_t[:, :n_blk], ((0, 0), (0, nb_all - n_blk), (0, 0)))
    sel = sel.reshape(b, nstep + 1, bps, N_KV, 1, t).transpose(0, 1, 3, 4, 5, 2)
    selrows = jnp.broadcast_to(sel, (b, nstep + 1, N_KV, GROUP, t, bps)).reshape(b, nstep + 1, nrows, bps)
    knew = jnp.pad(kv3[:, :, 2 * KV_COLS:4 * KV_COLS], ((0, 0), (0, PAGE - t), (0, 0)))
    o_rows = _attn_dec2(cache_t, page_table, qbd, selrows.astype(BF16), slopecol, knew, opart, gates_rows,
                        maskbd, past, t)
    o = (o_rows.reshape(b, N_KV, GROUP, t, HEAD_DIM).transpose(0, 3, 1, 2, 4).reshape(rows, Q_COLS).astype(BF16))

    gp = h_re.shape[1] * h_re.shape[2]
    y3, hr, hi = _s5(u3, h_re.reshape(b, gp), h_im.reshape(b, gp), lw['s5'], b)

    y = _tail(x.reshape(rows, d), o, y3, sg, lw, g_out, rows).reshape(b, t, d)
    kv6 = kv.reshape(b, t, 6, N_KV, HEAD_DIM)
    kvw = jnp.concatenate([win_buf, kv6[:, :, 4:]], axis=1)
    n_keep = min(WINDOW, n_buf + t)
    return y, kv6[:, :, :4], kvw[:, n_buf + t - n_keep:], hr.reshape(h_re.shape), hi.reshape(h_im.shape)


def kernel(x_prompt, x_sample, cache_kv, state_win, state_ssm_re, state_ssm_im, page_table, g_mix, w_in, w_cmp_pe, w_cmp_pool, w_cmp_phi, w_attn_out, ssm_lam_re, ssm_lam_im, ssm_log_dt, ssm_b_re, ssm_b_im, ssm_c_re, ssm_c_im, ssm_d, w_glu, w_out, g_ffn, w_gate, w_up, w_down, g_final):
    depth = g_mix.shape[0]
    assert depth == 1, "final norm is fused into the last layer's FFN; one layer supported"
    params = dict(g_mix=g_mix, w_in=w_in, w_cmp_pe=w_cmp_pe, w_cmp_pool=w_cmp_pool, w_cmp_phi=w_cmp_phi,
                  w_attn_out=w_attn_out, ssm_lam_re=ssm_lam_re, ssm_lam_im=ssm_lam_im, ssm_log_dt=ssm_log_dt,
                  ssm_b_re=ssm_b_re, ssm_b_im=ssm_b_im, ssm_c_re=ssm_c_re, ssm_c_im=ssm_c_im, ssm_d=ssm_d,
                  w_glu=w_glu, w_out=w_out, g_ffn=g_ffn, w_gate=w_gate, w_up=w_up, w_down=w_down)
    slopes = _alibi_slopes()
    outs = [[] for _ in range(8)]
    xp, xs = x_prompt, x_sample
    for l in range(depth):
        lw = _layer_weights(params, l)
        xp, kvp, winp, hrp, hip = _layer_prompt(xp, lw, slopes, g_final)
        xs, kvs, wins, hrs, his = _layer_sample(xs, cache_kv[l], page_table, state_win[l], state_ssm_re[l],
                                                state_ssm_im[l], lw, slopes, g_final)
        for lst, v in zip(outs, (kvp, winp, hrp, hip, kvs, wins, hrs, his)):
            lst.append(v)
    st = [jnp.stack(v) for v in outs]
    return (xp, xs, st[0], st[1], st[2], st[3], st[4], st[5], st[6], st[7])
```

```python
import functools
import math

import numpy as np
import jax
import jax.numpy as jnp
from jax import lax
from jax.experimental import pallas as pl
from jax.experimental.pallas import tpu as pltpu

F32 = jnp.float32
BF16 = jnp.bfloat16

N_HEADS = 16
HEAD_DIM = 64
N_KV = 4
GROUP = N_HEADS // N_KV
CMP_LEN = 32
CMP_STRIDE = 16
SEL_BLOCK = 64
SEL_TOP = 16
WINDOW = 512
SSM_GROUP = 16
SSM_STATE = 64
EPS = 1e-6
NEG = -1e30
FORCE_SCORE = 1e4

KV_COLS = N_KV * HEAD_DIM
Q_COLS = N_HEADS * HEAD_DIM
PAGE = 128
CMP_PER_PAGE = PAGE // CMP_STRIDE
PROMPT_PAGES_PER_STEP = 16
DECODE_PAGES_PER_STEP = 32
S5_L = 8
S5_SLAB = 128 // SSM_GROUP
VMEM_LIMIT = 56 * 1024 * 1024
LOG2E = 1.0 / math.log(2.0)
Q_SCALE = HEAD_DIM ** -0.5 * LOG2E


def _dot(a, b):
    return jnp.dot(a, b, preferred_element_type=F32)


def _dot_nt(a, b):
    return lax.dot_general(a, b, (((1,), (1,)), ((), ())), preferred_element_type=F32)


def _split(a):
    hi = a.astype(BF16)
    lo = (a - hi.astype(F32)).astype(BF16)
    return hi, lo


def _dot3(a, b):
    ah, al = _split(a)
    bh, bl = _split(b)
    return _dot(ah, bh) + _dot(ah, bl) + _dot(al, bh)


def _dot_nt_hl(w_bf16, a):
    ah, al = _split(a)
    return _dot_nt(w_bf16, ah) + _dot_nt(w_bf16, al)


def _gelu(x):
    return 0.5 * x * (1.0 + jnp.tanh(math.sqrt(2.0 / math.pi) * (x + 0.044715 * (x * x * x))))


def _iota(shape, dim):
    return lax.broadcasted_iota(jnp.int32, shape, dim)


def _log2(n):
    assert n > 0 and n & (n - 1) == 0, n
    return n.bit_length() - 1


def _div_pow2(x, n):
    return jnp.right_shift(x, _log2(n))


def _mod_pow2(x, n):
    return jnp.bitwise_and(x, (1 << _log2(n)) - 1)


def _softmax2_rows(s):
    m = jnp.max(s, axis=-1, keepdims=True)
    e = jnp.exp2(s - m)
    return e / jnp.sum(e, axis=-1, keepdims=True)


IN_TN = 512
_Q_T, _U_T, _MG_T, _KV_T = 2, 2, 8, 3
_KVC_T = 2
_U_0 = _Q_T
_MG_0 = _U_0 + _U_T
_GT_0 = _MG_0 + _MG_T
_KV_0 = _GT_0 + 1
IN_TILES = _KV_0 + _KV_T


def _inproj_kernel(x_ref, g_ref, wq_ref, wu_ref, wmg_ref, wgt_ref, wkv_ref, q_ref, u_ref, sg_ref, gt_ref, kvc_ref,
                   kvw_ref, xn_ref, *, kv_feature_major):
    j = pl.program_id(1)

    @pl.when(j == 0)
    def _():
        x = x_ref[...]
        r = lax.rsqrt(jnp.mean(x * x, axis=-1, keepdims=True) + EPS)
        xn_ref[...] = (x * r * g_ref[...]).astype(BF16)

    @pl.when(j < _U_0)
    def _():
        q_ref[...] = (_dot(xn_ref[...], wq_ref[...]) * Q_SCALE).astype(BF16)

    @pl.when((j >= _U_0) & (j < _MG_0))
    def _():
        z = _dot(xn_ref[...], wu_ref[...])
        for s in range(IN_TN // 128):
            u_ref[s] = z[:, 128 * s:128 * (s + 1)].reshape(u_ref.shape[1:])

    @pl.when((j >= _MG_0) & (j < _GT_0))
    def _():
        sg_ref[...] = jax.nn.sigmoid(_dot(xn_ref[...], wmg_ref[...])).astype(BF16)

    @pl.when(j == _GT_0)
    def _():
        gt_ref[...] = jax.nn.sigmoid(_dot(xn_ref[...], wgt_ref[...]))

    @pl.when(j >= _KV_0)
    def _():
        if kv_feature_major:
            z = _dot_nt(wkv_ref[...], xn_ref[...])
        else:
            z = _dot(xn_ref[...], wkv_ref[...])

        @pl.when(j < _KV_0 + _KVC_T)
        def _():
            kvc_ref[...] = z.reshape(kvc_ref.shape)

        @pl.when(j >= _KV_0 + _KVC_T)
        def _():
            kvw_ref[...] = z.reshape(kvw_ref.shape)


def _in_proj(x, g_mix, w, tm, seq_len=None):
    rows, d = x.shape
    feature_major = seq_len is not None

    def col(lo, n):
        return lambda i, j: (i, jnp.clip(j - lo, 0, n - 1))

    if feature_major:
        tiles_per_seq = seq_len // tm
        kvc_spec = pl.BlockSpec((1, IN_TN, tm), lambda i, j: (i // tiles_per_seq, jnp.clip(j - _KV_0, 0, _KVC_T - 1),
                                                              i % tiles_per_seq))
        kvw_spec = pl.BlockSpec((1, IN_TN, tm), lambda i, j: (i // tiles_per_seq, 0, i % tiles_per_seq))
        kvc_shape = jax.ShapeDtypeStruct((rows // seq_len, _KVC_T * IN_TN, seq_len), F32)
        kvw_shape = jax.ShapeDtypeStruct((rows // seq_len, IN_TN, seq_len), F32)
        wkv, wkv_spec = w['kv_t'], pl.BlockSpec((IN_TN, d), lambda i, j: (jnp.clip(j - _KV_0, 0, _KV_T - 1), 0))
    else:
        kvc_spec = pl.BlockSpec((tm, IN_TN), col(_KV_0, _KVC_T))
        kvw_spec = pl.BlockSpec((tm, IN_TN), lambda i, j: (i, 0))
        kvc_shape = jax.ShapeDtypeStruct((rows, _KVC_T * IN_TN), F32)
        kvw_shape = jax.ShapeDtypeStruct((rows, IN_TN), F32)
        wkv, wkv_spec = w['kv'], pl.BlockSpec((d, IN_TN), lambda i, j: (0, jnp.clip(j - _KV_0, 0, _KV_T - 1)))
    n_slab = IN_TN // 128
    return pl.pallas_call(
        functools.partial(_inproj_kernel, kv_feature_major=feature_major),
        grid=(rows // tm, IN_TILES),
        in_specs=[
            pl.BlockSpec((tm, d), lambda i, j: (i, 0), pipeline_mode=pl.Buffered(1)),
            pl.BlockSpec((1, d), lambda i, j: (0, 0)),
            pl.BlockSpec((d, IN_TN), lambda i, j: (0, jnp.clip(j, 0, _Q_T - 1))),
            pl.BlockSpec((d, IN_TN), lambda i, j: (0, jnp.clip(j - _U_0, 0, _U_T - 1))),
            pl.BlockSpec((d, IN_TN), lambda i, j: (0, jnp.clip(j - _MG_0, 0, _MG_T - 1))),
            pl.BlockSpec((d, IN_TN), lambda i, j: (0, 0), pipeline_mode=pl.Buffered(1)),
            wkv_spec,
        ],
        out_specs=[
            pl.BlockSpec((tm, IN_TN), col(0, _Q_T)),
            pl.BlockSpec((n_slab, tm // S5_L, S5_L, 128), lambda i, j: (jnp.clip(j - _U_0, 0, _U_T - 1), i, 0, 0)),
            pl.BlockSpec((tm, IN_TN), col(_MG_0, _MG_T)),
            pl.BlockSpec((tm, IN_TN), lambda i, j: (i, 0)),
            kvc_spec, kvw_spec,
        ],
        out_shape=[
            jax.ShapeDtypeStruct((rows, _Q_T * IN_TN), BF16),
            jax.ShapeDtypeStruct((_U_T * n_slab, rows // S5_L, S5_L, 128), F32),
            jax.ShapeDtypeStruct((rows, _MG_T * IN_TN), BF16),
            jax.ShapeDtypeStruct((rows, IN_TN), F32),
            kvc_shape, kvw_shape,
        ],
        scratch_shapes=[pltpu.VMEM((tm, d), BF16)],
        compiler_params=pltpu.CompilerParams(
            dimension_semantics=("parallel", "arbitrary"), vmem_limit_bytes=VMEM_LIMIT),
        name="in_proj",
    )(x, g_mix.reshape(1, d), w['q'], w['u'], w['mg'], w['gt'], wkv)


def _pack_w_in(w_in, ssm_width):
    cuts = np.cumsum([Q_COLS, 6 * KV_COLS, 3 * N_HEADS, ssm_width]).tolist()
    wq, wkv, wgt, wu, wmg = jnp.split(w_in, cuts, axis=1)
    assert wq.shape[1] == _Q_T * IN_TN and wkv.shape[1] == _KV_T * IN_TN
    assert wu.shape[1] == _U_T * IN_TN and wmg.shape[1] == _MG_T * IN_TN and IN_TN == N_KV * 128
    d = w_in.shape[0]
    wgt = wgt.reshape(d, 3, N_KV, GROUP).transpose(0, 2, 1, 3).reshape(d, N_KV, 3 * GROUP)
    wgt = jnp.pad(wgt, ((0, 0), (0, 0), (0, 128 - 3 * GROUP))).reshape(d, IN_TN)
    return dict(q=wq.astype(BF16), u=wu.astype(BF16), mg=wmg.astype(BF16), gt=wgt.astype(BF16),
                kv=wkv.astype(BF16), kv_t=wkv.T.astype(BF16))


def _pool_kernel(*refs, npg, n_tables):
    refs = refs[n_tables:]
    prev_ref = refs[0]
    page_refs = refs[1:npg + 1]
    mk_ref, mv_ref, mprev_ref, out_ref = refs[npg + 1:]
    pages = [prev_ref[0].astype(BF16)] + [page_refs[j][0].astype(BF16) for j in range(npg)]
    mpg = mk_ref.shape[0] // PAGE
    nblk = mk_ref.shape[1]
    for seg in range(npg // mpg):
        x = jnp.concatenate(pages[1 + seg * mpg:1 + (seg + 1) * mpg], axis=1)
        for half, m_ref in ((0, mk_ref), (1, mv_ref)):
            rows = slice(half * KV_COLS, (half + 1) * KV_COLS)
            head = _dot(pages[seg * mpg][rows], mprev_ref[half])
            if seg == 0:
                head = jnp.where(pl.program_id(1) > 0, head, 0.0)
            out_ref[0, rows, seg * nblk:(seg + 1) * nblk] = _dot(x[rows], m_ref[...]) + head


def _pool_matrices(pool, npg):
    nblk = npg * CMP_PER_PAGE
    cols = []
    for r in range(CMP_PER_PAGE):
        start = CMP_STRIDE * (r - 1)
        lo, hi = max(start, 0), min(start + CMP_LEN, PAGE)
        cols.append(jnp.pad(pool[:, lo - start:hi - start], ((0, 0), (lo, PAGE - hi))))
    body = jnp.stack(cols, axis=-1)
    carry = jnp.pad(pool[:, :CMP_STRIDE], ((0, 0), (PAGE - CMP_STRIDE, 0)))
    first = jnp.asarray(np.arange(CMP_PER_PAGE) == 0, F32)
    same = jnp.eye(npg, dtype=F32)[None, :, None, :, None]
    nxt = jnp.eye(npg, k=1, dtype=F32)[None, :, None, :, None]
    m = same * body[:, None, :, None, :] + nxt * (carry[:, None, :, None, None] * first)
    m = m.reshape(2, npg * PAGE, nblk).astype(BF16)
    mprev = (carry[:, :, None] * jnp.asarray(np.arange(nblk) == 0, F32)).astype(BF16)
    return m[0], m[1], mprev


def _pool_pages(pages, page_index, tables, nseq, npages, mats, npg):
    assert npg % (mats[0].shape[0] // PAGE) == 0
    nstep = npages // npg
    nblk = npg * CMP_PER_PAGE
    mk, mv, mprev = mats

    def spec(off):
        return pl.BlockSpec((1, 2 * KV_COLS, PAGE),
                            lambda b, s, *t: page_index(b, jnp.maximum(s * npg + off, 0), *t))

    const = lambda shape: pl.BlockSpec(shape, lambda b, s, *t: (0,) * len(shape))
    return pl.pallas_call(
        functools.partial(_pool_kernel, npg=npg, n_tables=len(tables)),
        grid_spec=pltpu.PrefetchScalarGridSpec(
            num_scalar_prefetch=len(tables),
            grid=(nseq, nstep),
            in_specs=[spec(j) for j in range(-1, npg)] + [const(mk.shape), const(mv.shape), const(mprev.shape)],
            out_specs=pl.BlockSpec((1, 2 * KV_COLS, nblk), lambda b, s, *t: (b, 0, s)),
        ),
        out_shape=jax.ShapeDtypeStruct((nseq, 2 * KV_COLS, npages * CMP_PER_PAGE), F32),
        compiler_params=pltpu.CompilerParams(
            dimension_semantics=("parallel", "arbitrary"), vmem_limit_bytes=VMEM_LIMIT),
        name="cmp_pool",
    )(*tables, *([pages] * (npg + 1)), mk, mv, mprev)


def _cmp_finish_kernel(pooled_ref, poolw_ref, pe_ref, phi_ref, kc_ref, vc_ref, kcr_ref):
    bias = jnp.sum(poolw_ref[...] * pe_ref[...], axis=1, keepdims=True)
    x = _gelu(pooled_ref[0] + bias)
    kc = _dot3(phi_ref[0], x[:KV_COLS])
    kc_ref[0] = kc.astype(BF16)
    kcr_ref[0] = kc.T.astype(BF16)
    vc_ref[0] = _dot3(phi_ref[1], x[KV_COLS:]).astype(BF16)


def _cmp_finish(pooled, pool, pe, phi):
    nseq, _, n = pooled.shape
    poolw = jnp.concatenate([jnp.broadcast_to(pool[s][None, :], (KV_COLS, CMP_LEN)) for s in range(2)], axis=0)
    pe_t = jnp.concatenate([jnp.tile(pe[s].T, (N_KV, 1)) for s in range(2)], axis=0)
    phi_bd = jnp.stack([jnp.kron(jnp.eye(N_KV, dtype=F32), phi[s].T) for s in range(2)])
    spec = pl.BlockSpec((1, KV_COLS, n), lambda b: (b, 0, 0))
    return pl.pallas_call(
        _cmp_finish_kernel,
        grid=(nseq,),
        in_specs=[
            pl.BlockSpec((1, 2 * KV_COLS, n), lambda b: (b, 0, 0)),
            pl.BlockSpec((2 * KV_COLS, CMP_LEN), lambda b: (0, 0)),
            pl.BlockSpec((2 * KV_COLS, CMP_LEN), lambda b: (0, 0)),
            pl.BlockSpec((2, KV_COLS, KV_COLS), lambda b: (0, 0, 0)),
        ],
        out_specs=[spec, spec, pl.BlockSpec((1, n, KV_COLS), lambda b: (b, 0, 0))],
        out_shape=[jax.ShapeDtypeStruct((nseq, KV_COLS, n), BF16)] * 2
        + [jax.ShapeDtypeStruct((nseq, n, KV_COLS), BF16)],
        compiler_params=pltpu.CompilerParams(dimension_semantics=("parallel",)),
        name="cmp_finish",
    )(pooled, poolw, pe_t, phi_bd)


ATT_TQ = 256
ATT_TK = 256
NBLK_PAD = 128
POS_ROWS = 16
AUG = HEAD_DIM + POS_ROWS


def _attn_prompt_kernel(q_ref, kca_ref, vc_ref, ks_ref, vs_ref, kw_ref, vw_ref, gt_ref, ovt_ref, ktab_ref,
                        stab_ref, o_ref, ksa_ref, vsa_ref, kwa_ref, vwa_ref, qat_ref, m_ref, acc_ref,
                        *, tq, tk, n_sel, n_top):
    qi = pl.program_id(2)
    q0 = qi * tq
    seq = ksa_ref.shape[0]

    @pl.when(qi == 0)
    def _():
        zero = jnp.zeros((128 - HEAD_DIM, seq), F32)
        ones = jnp.where(_iota((POS_ROWS, seq), 0) == 0, 1.0, 0.0).astype(BF16)
        for src, dst, lo in ((ks_ref, ksa_ref, NBLK_PAD), (kw_ref, kwa_ref, 0)):
            k_rows = jnp.concatenate([src[0], zero], axis=0).T[:, :HEAD_DIM].astype(BF16)
            dst[:, lo:] = jnp.concatenate([k_rows, ktab_ref[:, NBLK_PAD:]], axis=1)
        ksa_ref[:, 0:NBLK_PAD] = ktab_ref[:, 0:NBLK_PAD]
        for src, dst in ((vs_ref, vsa_ref), (vw_ref, vwa_ref)):
            dst[0:HEAD_DIM] = src[0].astype(BF16)
            dst[HEAD_DIM:] = ones

    q_t = q_ref[...].astype(F32).T
    q_t = jnp.concatenate([q_t[n * HEAD_DIM:(n + 1) * HEAD_DIM] for n in range(GROUP)], axis=1).astype(BF16)
    stab = stab_ref[0]
    sp_t = jnp.concatenate([jnp.broadcast_to(stab[:, n:n + 1], (POS_ROWS, tq)) for n in range(GROUP)], axis=1)
    qat_ref[NBLK_PAD:NBLK_PAD + HEAD_DIM] = q_t
    qat_ref[NBLK_PAD + HEAD_DIM:] = sp_t.astype(BF16)
    trow = q0 + _iota((1, tq), 1)
    lanes = [slice(n * tq, (n + 1) * tq) for n in range(GROUP)]

    ncmp = kca_ref.shape[2]
    nidx = _iota((ncmp, 1), 0)
    heads = lambda a: jnp.concatenate([a] * GROUP, axis=1)
    valid = (nidx >= 1) & (CMP_STRIDE * nidx + (CMP_STRIDE - 1) <= trow)
    valid = heads(jnp.where(valid, 1.0, 0.0)) > 0.5
    s = jnp.where(valid, _dot(kca_ref[0, 0], qat_ref[NBLK_PAD:]), NEG)
    e = jnp.exp2(s - jnp.max(s, axis=0, keepdims=True))
    pc = jnp.where(valid, e / jnp.sum(e, axis=0, keepdims=True), 0.0)
    o_c = _dot(vc_ref[0], pc.astype(BF16))

    ph, plo = _split((pc[:, lanes[0]] + pc[:, lanes[1]]) + (pc[:, lanes[2]] + pc[:, lanes[3]]))
    imp = (_dot(ovt_ref[...], ph) + _dot(ovt_ref[...], plo))[0:n_sel]
    blk = _iota((n_sel, 1), 0)
    cur = _div_pow2(q0 + _iota((1, tq), 1), SEL_BLOCK)
    forced = (blk == 0) | (blk == cur) | (blk == cur - 1)
    imp = jnp.where(forced, FORCE_SCORE, imp)
    imp = jnp.where(blk <= cur, imp, -1.0)
    cnt = jnp.zeros((n_sel, tq), F32)
    for j in range(n_sel):
        vj = imp[j:j + 1, :]
        beats = (vj > imp) | ((vj == imp) & (blk > j))
        cnt = cnt + jnp.where(beats, 1.0, 0.0)
    mask_t = jnp.where((cnt < n_top) & (blk <= cur), 0.0, NEG)
    mask_t = jnp.concatenate([mask_t, jnp.full((NBLK_PAD - n_sel, tq), NEG, F32)], axis=0).astype(BF16)
    qat_ref[0:NBLK_PAD] = heads(mask_t)

    m_ref[...] = jnp.full_like(m_ref, NEG)
    acc_ref[...] = jnp.zeros_like(acc_ref)

    def scores(kt):
        return _dot(ksa_ref[pl.ds(pl.multiple_of(kt * tk, tk), tk), :], qat_ref[...])

    def update(kt, s):
        m_old = m_ref[...]
        m_new = jnp.maximum(m_old, jnp.max(s, axis=0, keepdims=True))
        p = jnp.exp2(s - m_new).astype(BF16)
        pv = _dot(vsa_ref[:, pl.ds(pl.multiple_of(kt * tk, tk), tk)], p)
        acc_ref[...] = jnp.exp2(m_old - m_new) * acc_ref[...] + pv
        m_ref[...] = m_new

    def pair(j, carry):
        s0, s1 = scores(2 * j), scores(2 * j + 1)
        update(2 * j, s0)
        update(2 * j + 1, s1)
        return carry

    lax.fori_loop(0, qi // 2, pair, 0)

    @pl.when(qi % 2 == 1)
    def _():
        update(qi - 1, scores(qi - 1))

    span = WINDOW + tq
    w0 = pl.multiple_of(jnp.maximum(q0 - WINDOW, 0), tq)
    s_diag = scores(qi) + heads(jnp.where(q0 + _iota((tk, 1), 0) <= trow, 0.0, NEG))
    dist = trow - (w0 + _iota((span, 1), 0))
    s_win = (_dot(kwa_ref[pl.ds(w0, span), :], qat_ref[NBLK_PAD:])
             + heads(jnp.where((dist >= 0) & (dist < WINDOW), 0.0, NEG)))
    update(qi, s_diag)
    acc = acc_ref[...]
    o_s = acc[:HEAD_DIM] / acc[HEAD_DIM:HEAD_DIM + 1]
    ow = _dot(vwa_ref[:, pl.ds(w0, span)], jnp.exp2(s_win - jnp.max(s_win, axis=0, keepdims=True)).astype(BF16))
    o_w = ow[:HEAD_DIM] / ow[HEAD_DIM:HEAD_DIM + 1]

    gt_t = gt_ref[...].T
    outs = []
    for n in range(GROUP):
        outs.append(gt_t[n:n + 1] * o_c[:, lanes[n]] + gt_t[GROUP + n:GROUP + n + 1] * o_s[:, lanes[n]]
                    + gt_t[2 * GROUP + n:2 * GROUP + n + 1] * o_w[:, lanes[n]])
    o_ref[...] = jnp.concatenate(outs, axis=0).T.astype(BF16)


def _overlap_t(n_blk_rows, n_cmp):
    s = np.arange(n_blk_rows)[:, None]
    n = np.arange(n_cmp)[None, :]
    r = SEL_BLOCK // CMP_STRIDE
    return jnp.asarray(((n >= r * s) & (n <= r * s + r)).astype(np.float32), dtype=BF16)


def _block_expand(n_blk_rows, n_keys):
    s = np.arange(n_blk_rows)[:, None]
    k = np.arange(n_keys)[None, :]
    return jnp.asarray((k // SEL_BLOCK == s).astype(np.float32), dtype=BF16)


def _attn_prompt(slopes, q, kc_rows, vc_t, kvc_t, kvw_t, gates, nbatch, seq):
    ncmp = kc_rows.shape[1]
    tq, tk = ATT_TQ, ATT_TK
    assert tk == tq and seq % tk == 0 and WINDOW % tq == 0 and WINDOW + tq <= seq
    nq = seq // tq
    n_sel = seq // SEL_BLOCK
    ovt = _overlap_t(NBLK_PAD, ncmp)
    ktab = jnp.concatenate([_block_expand(NBLK_PAD, seq), _position_rows(seq)], axis=0).T
    cmp_end = CMP_STRIDE * np.arange(ncmp) + (CMP_STRIDE - 1)
    kca = jnp.concatenate([kc_rows.reshape(nbatch, ncmp, N_KV, HEAD_DIM).transpose(0, 2, 1, 3),
                           jnp.broadcast_to(_position_rows(cmp_end).T, (nbatch, N_KV, ncmp, POS_ROWS))], axis=-1)
    stab = _slope_table(slopes).astype(F32).transpose(0, 2, 1)
    per_slot = KV_COLS // HEAD_DIM

    def kv_spec(slot):
        return pl.BlockSpec((1, HEAD_DIM, seq), lambda b, g, i: (b, slot * per_slot + g, 0))

    q_spec = pl.BlockSpec((tq, GROUP * HEAD_DIM), lambda b, g, i: (b * nq + i, g))
    return pl.pallas_call(
        functools.partial(_attn_prompt_kernel, tq=tq, tk=tk, n_sel=n_sel, n_top=min(SEL_TOP, n_sel)),
        grid=(nbatch, N_KV, nq),
        in_specs=[
            q_spec,
            pl.BlockSpec((1, 1, ncmp, AUG), lambda b, g, i: (b, g, 0, 0)),
            pl.BlockSpec((1, HEAD_DIM, ncmp), lambda b, g, i: (b, g, 0)),
            kv_spec(2), kv_spec(3), kv_spec(0), kv_spec(1),
            pl.BlockSpec((tq, 128), lambda b, g, i: (b * nq + i, g)),
            pl.BlockSpec((NBLK_PAD, ncmp), lambda b, g, i: (0, 0)),
            pl.BlockSpec((seq, NBLK_PAD + POS_ROWS), lambda b, g, i: (0, 0)),
            pl.BlockSpec((1, POS_ROWS, GROUP), lambda b, g, i: (g, 0, 0)),
        ],
        out_specs=q_spec,
        out_shape=jax.ShapeDtypeStruct(q.shape, BF16),
        scratch_shapes=[
            pltpu.VMEM((seq, NBLK_PAD + AUG), BF16),
            pltpu.VMEM((AUG, seq), BF16),
            pltpu.VMEM((seq, AUG), BF16),
            pltpu.VMEM((AUG, seq), BF16),
            pltpu.VMEM((NBLK_PAD + AUG, GROUP * tq), BF16),
            pltpu.VMEM((1, GROUP * tq), F32),
            pltpu.VMEM((AUG, GROUP * tq), F32),
        ],
        compiler_params=pltpu.CompilerParams(
            dimension_semantics=("parallel", "parallel", "arbitrary"), vmem_limit_bytes=VMEM_LIMIT),
        name="attn_prompt",
    )(q, kca, vc_t, kvc_t, kvc_t, kvw_t, kvw_t, gates, ovt, ktab, stab)


def _diag_blocks(o, maskbd):
    o = o * maskbd
    return (o[:, 0:64] + o[:, 64:128]) + (o[:, 128:192] + o[:, 192:256])


def _attn_dec1_kernel(qbd_ref, kc_ref, vc_ref, win_ref, kvnew_ref, gates_ref, slope_ref, ovt_ref,
                      maskbd_ref, opart_ref, sel_ref, *, past, tdec, n_blk, n_top):
    qb = qbd_ref[0]
    nrows = qb.shape[0]
    tcol = past + _mod_pow2(_iota((nrows, 1), 0), tdec)
    pref = float(past + tdec)
    slope = slope_ref[...]
    maskbd = maskbd_ref[...]

    ncmp = kc_ref.shape[2]
    nrow = _iota((1, ncmp), 1)
    cend = CMP_STRIDE * nrow + (CMP_STRIDE - 1)
    valid = (nrow >= 1) & (cend <= tcol)
    s = _dot(qb, kc_ref[0])
    s = jnp.where(valid, s + slope * (cend.astype(F32) - pref), NEG)
    p = jnp.where(valid, _softmax2_rows(s), 0.0)
    o_c = _diag_blocks(_dot_nt(p.astype(BF16), vc_ref[0]), maskbd)

    per_g = GROUP * tdec
    psum = jnp.concatenate(
        [sum(p[gi * per_g + n * tdec: gi * per_g + (n + 1) * tdec] for n in range(GROUP)) for gi in range(N_KV)],
        axis=0)
    imp = _dot_nt_hl(ovt_ref[...], psum)
    nb_pad = imp.shape[0]
    blk = _iota((nb_pad, 1), 0)
    cur = _div_pow2(past + _mod_pow2(_iota((1, N_KV * tdec), 1), tdec), SEL_BLOCK)
    forced = (blk == 0) | (blk == cur) | (blk == cur - 1)
    imp = jnp.where(forced, FORCE_SCORE, imp)
    imp = jnp.where(blk <= cur, imp, -1.0)
    imp = jnp.where(blk < n_blk, imp, -2.0)
    sel = jnp.zeros(imp.shape, F32)
    for _ in range(n_top):
        mx = jnp.max(imp, axis=0, keepdims=True)
        first = jnp.min(jnp.where(imp == mx, blk, nb_pad), axis=0, keepdims=True)
        pick = blk == first
        sel = jnp.where(pick, 1.0, sel)
        imp = jnp.where(pick, -jnp.inf, imp)
    sel_ref[0] = sel

    nbuf = win_ref.shape[2]
    knew = kvnew_ref[0]
    zpad = jnp.zeros((PAGE - tdec, KV_COLS), F32)
    kn = jnp.concatenate([knew[:, :KV_COLS], zpad], axis=0).astype(BF16)
    vn = jnp.concatenate([knew[:, KV_COLS:], zpad], axis=0).astype(BF16)
    sw = jnp.concatenate([_dot(qb, win_ref[0, :KV_COLS].astype(BF16)), _dot_nt(qb, kn)], axis=1)
    idx = _iota((1, nbuf + PAGE), 1)
    kposw = past - nbuf + idx
    dist = tcol - kposw
    validw = (dist >= 0) & (dist < WINDOW) & (kposw >= 0) & (idx < nbuf + tdec)
    sw = jnp.where(validw, sw + slope * (kposw.astype(F32) - pref), NEG)
    pw = _softmax2_rows(sw).astype(BF16)
    o_w = _dot_nt(pw[:, :nbuf], win_ref[0, KV_COLS:].astype(BF16)) + _dot(pw[:, nbuf:], vn)
    o_w = _diag_blocks(o_w, maskbd)

    gates = gates_ref[0]
    opart_ref[0] = gates[:, 0:1] * o_c + gates[:, 2:3] * o_w


def _attn_dec1(qbd, kc_t, vc_t, win_t, kvnew, gates_rows, slopecol, maskbd, past, tdec, n_blk):
    nseq, nrows, _ = qbd.shape
    ncmp = kc_t.shape[2]
    nb_pad = -(-n_blk // 8) * 8
    ovt = _overlap_t(nb_pad, ncmp)
    nbuf = win_t.shape[2]
    return pl.pallas_call(
        functools.partial(_attn_dec1_kernel, past=past, tdec=tdec, n_blk=n_blk, n_top=min(SEL_TOP, n_blk)),
        grid=(nseq,),
        in_specs=[
            pl.BlockSpec((1, nrows, KV_COLS), lambda b: (b, 0, 0)),
            pl.BlockSpec((1, KV_COLS, ncmp), lambda b: (b, 0, 0)),
            pl.BlockSpec((1, KV_COLS, ncmp), lambda b: (b, 0, 0)),
            pl.BlockSpec((1, 2 * KV_COLS, nbuf), lambda b: (b, 0, 0)),
            pl.BlockSpec((1, tdec, 2 * KV_COLS), lambda b: (b, 0, 0)),
            pl.BlockSpec((1, nrows, 4), lambda b: (b, 0, 0)),
            pl.BlockSpec((nrows, 1), lambda b: (0, 0)),
            pl.BlockSpec((nb_pad, ncmp), lambda b: (0, 0)),
            pl.BlockSpec((nrows, KV_COLS), lambda b: (0, 0)),
        ],
        out_specs=[
            pl.BlockSpec((1, nrows, HEAD_DIM), lambda b: (b, 0, 0)),
            pl.BlockSpec((1, nb_pad, N_KV * tdec), lambda b: (b, 0, 0)),
        ],
        out_shape=[
            jax.ShapeDtypeStruct((nseq, nrows, HEAD_DIM), F32),
            jax.ShapeDtypeStruct((nseq, nb_pad, N_KV * tdec), F32),
        ],
        compiler_params=pltpu.CompilerParams(dimension_semantics=("parallel",), vmem_limit_bytes=VMEM_LIMIT),
        name="attn_dec_select",
    )(qbd, kc_t, vc_t, win_t, kvnew, gates_rows, slopecol, ovt, maskbd)


def _attn_dec2_kernel(pt_ref, *refs, npg, past, tdec):
    page_refs = refs[:npg]
    (qbd_ref, selrow_ref, sellast_ref, e_ref, slope_ref, knew_ref, opart_ref, gates_ref, maskbd_ref,
     o_ref, m_ref, l_ref, acc_ref) = refs[npg:]
    step = pl.program_id(1)
    nkeys = npg * PAGE
    qb = qbd_ref[0]
    nrows = qb.shape[0]
    pref = float(past + tdec)
    slope = slope_ref[...]

    @pl.when(step == 0)
    def _():
        m_ref[...] = jnp.full_like(m_ref, NEG)
        l_ref[...] = jnp.zeros_like(l_ref)
        acc_ref[...] = jnp.zeros_like(acc_ref)

    def update(s, pv_fn):
        m_old = m_ref[...]
        m_new = jnp.maximum(m_old, jnp.max(s, axis=-1, keepdims=True))
        alpha = jnp.exp2(m_old - m_new)
        p = jnp.exp2(s - m_new)
        l_ref[...] = alpha * l_ref[...] + jnp.sum(p, axis=-1, keepdims=True)
        acc_ref[...] = alpha * acc_ref[...] + pv_fn(p.astype(BF16))
        m_ref[...] = m_new

    s = jnp.concatenate([_dot(qb, page_refs[j][0, :KV_COLS].astype(BF16)) for j in range(npg)], axis=1)
    kpos = step * nkeys + _iota((1, nkeys), 1)
    chosen = _dot(selrow_ref[0, 0], e_ref[...])
    s = s + slope * (kpos.astype(F32) - pref) + jnp.where(chosen > 0.5, 0.0, NEG)

    def pv_pages(p):
        acc = _dot_nt(p[:, 0:PAGE], page_refs[0][0, KV_COLS:].astype(BF16))
        for j in range(1, npg):
            acc = acc + _dot_nt(p[:, PAGE * j:PAGE * (j + 1)], page_refs[j][0, KV_COLS:].astype(BF16))
        return acc

    update(s, pv_pages)

    @pl.when(step == pl.num_programs(1) - 1)
    def _():
        tcol = past + _mod_pow2(_iota((nrows, 1), 0), tdec)
        kn = knew_ref[0]
        kposn = past + _iota((1, PAGE), 1)
        sn = _dot_nt(qb, kn[:, :KV_COLS].astype(BF16))
        ok = (sellast_ref[0, 0][:, 0:1].astype(F32) > 0.5) & (kposn <= tcol)
        sn = sn + slope * (kposn.astype(F32) - pref) + jnp.where(ok, 0.0, NEG)
        update(sn, lambda p: _dot(p, kn[:, KV_COLS:].astype(BF16)))
        o_s = _diag_blocks(acc_ref[...] / l_ref[...], maskbd_ref[...])
        o_ref[0] = opart_ref[0] + gates_ref[0][:, 1:2] * o_s


def _attn_dec2(cache_t, page_table, qbd, selrows, slopecol, knew, opart, gates_rows, maskbd, past, tdec, npg):
    nseq, npages = page_table.shape
    nstep = npages // npg
    nrows = qbd.shape[1]
    blk_per_step = npg * PAGE // SEL_BLOCK
    emat = _block_expand(blk_per_step, npg * PAGE)
    page_specs = [
        pl.BlockSpec((1, 2 * KV_COLS, PAGE), lambda b, s, pt, j=j: (pt[b, s * npg + j], 1, 0))
        for j in range(npg)
    ]
    per_seq = lambda shape: pl.BlockSpec((1,) + shape, lambda b, s, pt: (b,) + (0,) * len(shape))
    const = lambda shape: pl.BlockSpec(shape, lambda b, s, pt: (0,) * len(shape))
    return pl.pallas_call(
        functools.partial(_attn_dec2_kernel, npg=npg, past=past, tdec=tdec),
        grid_spec=pltpu.PrefetchScalarGridSpec(
            num_scalar_prefetch=1,
            grid=(nseq, nstep),
            in_specs=page_specs + [
                per_seq((nrows, KV_COLS)),
                pl.BlockSpec((1, 1, nrows, blk_per_step), lambda b, s, pt: (b, s, 0, 0)),
                pl.BlockSpec((1, 1, nrows, blk_per_step), lambda b, s, pt: (b, nstep, 0, 0)),
                const((blk_per_step, npg * PAGE)),
                const((nrows, 1)),
                per_seq((PAGE, 2 * KV_COLS)),
                per_seq((nrows, HEAD_DIM)),
                per_seq((nrows, 4)),
                const((nrows, KV_COLS)),
            ],
            out_specs=per_seq((nrows, HEAD_DIM)),
            scratch_shapes=[
                pltpu.VMEM((nrows, 1), F32),
                pltpu.VMEM((nrows, 1), F32),
                pltpu.VMEM((nrows, KV_COLS), F32),
            ],
        ),
        out_shape=jax.ShapeDtypeStruct((nseq, nrows, HEAD_DIM), F32),
        compiler_params=pltpu.CompilerParams(
            dimension_semantics=("parallel", "arbitrary"), vmem_limit_bytes=VMEM_LIMIT),
        name="attn_dec_selected",
    )(page_table, *([cache_t] * npg), qbd, selrows, selrows, emat, slopecol, knew, opart, gates_rows, maskbd)


def _s5_weights(lam_re, lam_im, log_dt, b_re, b_im, c_re, c_im, d_skip):
    hp = lax.Precision.HIGHEST
    L = S5_L
    ng, p = lam_re.shape
    c = SSM_GROUP
    ns, gs = ng // S5_SLAB, S5_SLAB
    dt = jnp.exp(log_dt)[:, None]
    lr, li = lam_re, lam_im
    mag = jnp.exp(lr * dt)
    ar = mag * jnp.cos(li * dt)
    ai = mag * jnp.sin(li * dt)
    den = lr * lr + li * li
    fr = ((ar - 1.0) * lr + ai * li) / den
    fi = (ai * lr - (ar - 1.0) * li) / den
    bbr = fr[..., None] * b_re - fi[..., None] * b_im
    bbi = fr[..., None] * b_im + fi[..., None] * b_re
    j = jnp.arange(L + 1, dtype=F32)[:, None, None]
    pmag = jnp.exp(j * (lr * dt))
    pr = pmag * jnp.cos(j * (li * dt))
    pi = pmag * jnp.sin(j * (li * dt))
    abr = pr[..., None] * bbr - pi[..., None] * bbi
    abi = pr[..., None] * bbi + pi[..., None] * bbr
    kd = (jnp.einsum('gcp,jgpk->jgck', c_re, abr, precision=hp)
          - jnp.einsum('gcp,jgpk->jgck', c_im, abi, precision=hp))
    kd = kd.at[0].add(d_skip.reshape(ng, c)[:, :, None] * jnp.eye(c, dtype=F32))
    kc = kd[:L].reshape(L, ns, gs, c, c).transpose(1, 0, 2, 4, 3).reshape(ns, L, gs * c, c)
    ab = jnp.stack([abr[:L], abi[:L]]).reshape(2, L, ns, gs, p, c)
    abc = ab.transpose(2, 1, 3, 5, 0, 4).reshape(ns, L, gs * c, 2 * p)
    cr = c_re[None] * pr[1:, :, None, :] - c_im[None] * pi[1:, :, None, :]
    ci = -(c_re[None] * pi[1:, :, None, :] + c_im[None] * pr[1:, :, None, :])
    co = jnp.stack([cr, ci]).reshape(2, L, ns, gs, c, p)
    coc = co.transpose(2, 1, 0, 3, 5, 4).reshape(ns, L, 2 * gs * p, c)
    al = jnp.stack([pr[L].reshape(-1), pi[L].reshape(-1)])
    return kc.astype(BF16), abc.astype(BF16), coc.astype(BF16), al


def _s5_expanders():
    gs, c, p = S5_SLAB, SSM_GROUP, SSM_STATE
    lane = np.arange(gs * c)
    rep_c = (np.arange(c)[:, None] == lane[None, :] % c)
    same_g = (lane[:, None] // c == lane[None, :] // c)
    st = np.arange(2 * gs * p)
    rp = np.arange(2 * p)
    rep_p = (rp[:, None] // p == st[None, :] // (gs * p)) & (rp[:, None] % p == st[None, :] % p)
    g_in = (lane[:, None] // c == (st[None, :] // p) % gs)
    g_out = ((st[:, None] // p) % gs == lane[None, :] // c)
    as_bf = lambda a: jnp.asarray(a.astype(np.float32), dtype=BF16)
    return as_bf(rep_c), as_bf(same_g), as_bf(rep_p), as_bf(g_in), as_bf(g_out)


def _chunk_lanes(u_ref):
    return jnp.concatenate([u_ref[0, :, t, :] for t in range(S5_L)], axis=1)


def _s5_local_kernel(u_ref, abc_ref, rep_p_ref, g_in_ref, pre_ref, pim_ref, wp_ref):
    for s in range(S5_L):
        blk = _dot(abc_ref[0, S5_L - 1 - s], rep_p_ref[...]).astype(BF16) * g_in_ref[...]
        wp_ref[128 * s:128 * (s + 1), :] = blk
    r = _dot(_chunk_lanes(u_ref).astype(BF16), wp_ref[...])
    half = r.shape[1] // 2
    pre_ref[...] = r[:, :half]
    pim_ref[...] = r[:, half:]


def _s5_scan_kernel(pre_ref, pim_ref, al_ref, h0r_ref, h0i_ref, hsr_ref, hsi_ref, hfr_ref, hfi_ref):
    nchunk = pre_ref.shape[1]
    ar = al_ref[0:1, :]
    ai = al_ref[1:2, :]

    def body(k, carry):
        cr, ci = carry
        hsr_ref[:, pl.ds(k, 1), :] = cr[:, None, :]
        hsi_ref[:, pl.ds(k, 1), :] = ci[:, None, :]
        xr = pre_ref[:, pl.ds(k, 1), :][:, 0, :]
        xi = pim_ref[:, pl.ds(k, 1), :][:, 0, :]
        return ar * cr - ai * ci + xr, ar * ci + ai * cr + xi

    cr, ci = lax.fori_loop(0, nchunk, body, (h0r_ref[...], h0i_ref[...]))
    hfr_ref[...] = cr
    hfi_ref[...] = ci


def _s5_out_kernel(u_ref, kc_ref, coc_ref, rep_c_ref, same_g_ref, g_out_ref, hsr_ref, hsi_ref, y_ref,
                   wt_ref, wo_ref):
    rep_c = rep_c_ref[...]
    lag = [_dot(kc_ref[0, dl], rep_c).astype(BF16) * same_g_ref[...] for dl in range(S5_L)]
    zero = jnp.zeros((128, 128), BF16)
    for s in range(S5_L):
        for t in range(S5_L):
            wt_ref[128 * s:128 * (s + 1), 128 * t:128 * (t + 1)] = lag[t - s] if t >= s else zero
    for t in range(S5_L):
        wo_ref[:, 128 * t:128 * (t + 1)] = _dot(coc_ref[0, t], rep_c).astype(BF16) * g_out_ref[...]
    hs = jnp.concatenate([hsr_ref[...], hsi_ref[...]], axis=1).astype(BF16)
    y = _dot(_chunk_lanes(u_ref).astype(BF16), wt_ref[...]) + _dot(hs, wo_ref[...])
    for t in range(S5_L):
        y_ref[0, :, t, :] = y[:, 128 * t:128 * (t + 1)]


def _s5(u3, h0r, h0i, weights, nbatch):
    kc, abc, coc, al = weights
    rep_c, same_g, rep_p, g_in, g_out = _s5_expanders()
    ns, nb = u3.shape[0], u3.shape[1]
    nchunk = nb // nbatch
    gp = al.shape[1]
    lb = gp // ns
    d = S5_L * 128
    cp = pltpu.CompilerParams(dimension_semantics=("parallel",), vmem_limit_bytes=VMEM_LIMIT)
    u_spec = pl.BlockSpec((1, nb, S5_L, 128), lambda j: (j, 0, 0, 0))
    st_spec = pl.BlockSpec((nb, lb), lambda j: (0, j))
    slab = lambda a: pl.BlockSpec((1,) + a.shape[1:], lambda j: (j,) + (0,) * (a.ndim - 1))
    const = lambda a: pl.BlockSpec(a.shape, lambda j: (0,) * a.ndim)
    pre, pim = pl.pallas_call(
        _s5_local_kernel,
        grid=(ns,),
        in_specs=[u_spec, slab(abc), const(rep_p), const(g_in)],
        out_specs=[st_spec] * 2,
        out_shape=[jax.ShapeDtypeStruct((nb, gp), F32)] * 2,
        scratch_shapes=[pltpu.VMEM((d, 2 * lb), BF16)],
        compiler_params=cp,
        name="s5_local",
    )(u3, abc, rep_p, g_in)

    seq3 = pl.BlockSpec((nbatch, nchunk, lb), lambda j: (0, 0, j))
    row = lambda r: pl.BlockSpec((r, lb), lambda j: (0, j))
    hsr, hsi, hfr, hfi = pl.pallas_call(
        _s5_scan_kernel,
        grid=(ns,),
        in_specs=[seq3, seq3, row(2), row(nbatch), row(nbatch)],
        out_specs=[seq3, seq3, row(nbatch), row(nbatch)],
        out_shape=[jax.ShapeDtypeStruct((nbatch, nchunk, gp), F32)] * 2 + [jax.ShapeDtypeStruct((nbatch, gp), F32)] * 2,
        compiler_params=cp,
        name="s5_scan",
    )(pre.reshape(nbatch, nchunk, gp), pim.reshape(nbatch, nchunk, gp), al, h0r, h0i)

    y3 = pl.pallas_call(
        _s5_out_kernel,
        grid=(ns,),
        in_specs=[u_spec, slab(kc), slab(coc), const(rep_c), const(same_g), const(g_out), st_spec, st_spec],
        out_specs=u_spec,
        out_shape=jax.ShapeDtypeStruct(u3.shape, F32),
        scratch_shapes=[pltpu.VMEM((d, d), BF16), pltpu.VMEM((2 * lb, d), BF16)],
        compiler_params=cp,
        name="s5_out",
    )(u3, kc, coc, rep_c, same_g, g_out, hsr.reshape(nb, gp), hsi.reshape(nb, gp))
    return y3, hfr, hfi


def _merge_kernel(o_ref, y_ref, wa_ref, wg1_ref, wg2_ref, sga_ref, sgb_ref, m_ref, gy_ref):
    @pl.when(pl.program_id(1) == 0)
    def _():
        y = jnp.concatenate([y_ref[s] for s in range(y_ref.shape[0])], axis=1)
        gy_ref[...] = _gelu(y).astype(BF16)

    ya = _dot(o_ref[...], wa_ref[...])
    gy = gy_ref[...]
    yb = _dot(gy, wg1_ref[...]) * jax.nn.sigmoid(_dot(gy, wg2_ref[...]))
    m = sga_ref[...].astype(F32) * ya + sgb_ref[...].astype(F32) * yb
    m_ref[...] = m.astype(BF16)


def _merge(o, y3, w_attn_out, w_glu, sg, tm, tn=1024):
    rows, d = o.shape[0], w_attn_out.shape[1]
    nj = d // tn
    kq, ks = w_attn_out.shape[0], w_glu.shape[0]
    ns = y3.shape[0]
    return pl.pallas_call(
        _merge_kernel,
        grid=(rows // tm, nj),
        in_specs=[
            pl.BlockSpec((tm, kq), lambda i, j: (i, 0)),
            pl.BlockSpec((ns, tm, 128), lambda i, j: (0, i, 0)),
            pl.BlockSpec((kq, tn), lambda i, j: (0, j)),
            pl.BlockSpec((ks, tn), lambda i, j: (0, j)),
            pl.BlockSpec((ks, tn), lambda i, j: (0, j + nj)),
            pl.BlockSpec((tm, tn), lambda i, j: (i, j)),
            pl.BlockSpec((tm, tn), lambda i, j: (i, j + nj)),
        ],
        out_specs=pl.BlockSpec((tm, tn), lambda i, j: (i, j)),
        out_shape=jax.ShapeDtypeStruct((rows, d), BF16),
        scratch_shapes=[pltpu.VMEM((tm, ks), BF16)],
        compiler_params=pltpu.CompilerParams(
            dimension_semantics=("parallel", "arbitrary"), vmem_limit_bytes=VMEM_LIMIT),
        name="merge",
    )(o, y3, w_attn_out, w_glu, w_glu, sg, sg)


def _outproj_kernel(m_ref, x_ref, w_ref, g_ref, x1_ref, h_ref):
    x1 = x_ref[...] + _dot(m_ref[...], w_ref[...])
    x1_ref[...] = x1
    r = lax.rsqrt(jnp.mean(x1 * x1, axis=-1, keepdims=True) + EPS)
    h_ref[...] = (x1 * r * g_ref[...]).astype(BF16)


def _outproj(m, x, w_out, g_ffn, tm):
    rows, d = x.shape
    row = pl.BlockSpec((tm, d), lambda i: (i, 0))
    return pl.pallas_call(
        _outproj_kernel,
        grid=(rows // tm,),
        in_specs=[row, row, pl.BlockSpec((d, d), lambda i: (0, 0), pipeline_mode=pl.Buffered(1)),
                  pl.BlockSpec((1, d), lambda i: (0, 0))],
        out_specs=[row, row],
        out_shape=[jax.ShapeDtypeStruct((rows, d), F32), jax.ShapeDtypeStruct((rows, d), BF16)],
        compiler_params=pltpu.CompilerParams(dimension_semantics=("parallel",), vmem_limit_bytes=VMEM_LIMIT),
        name="out_proj",
    )(m, x, w_out, g_ffn.reshape(1, d))


def _ffn_kernel(h_ref, x1_ref, wg_ref, wu_ref, wd_ref, gf_ref, y_ref, acc_ref):
    f = pl.program_id(1)

    @pl.when(f == 0)
    def _():
        acc_ref[...] = jnp.zeros_like(acc_ref)

    h = h_ref[...]
    a = _dot(h, wg_ref[...])
    a = (a * jax.nn.sigmoid(a)) * _dot(h, wu_ref[...])
    acc_ref[...] += _dot(a.astype(BF16), wd_ref[...])

    @pl.when(f == pl.num_programs(1) - 1)
    def _():
        y = x1_ref[...] + acc_ref[...]
        r = lax.rsqrt(jnp.mean(y * y, axis=-1, keepdims=True) + EPS)
        y_ref[...] = y * r * gf_ref[...]


def _ffn(h, x1, w_gate, w_up, w_down, g_final, tm, tf=512):
    rows, d = x1.shape
    dff = w_gate.shape[1]
    row = pl.BlockSpec((tm, d), lambda i, f: (i, 0))
    return pl.pallas_call(
        _ffn_kernel,
        grid=(rows // tm, dff // tf),
        in_specs=[row, row,
                  pl.BlockSpec((d, tf), lambda i, f: (0, f)),
                  pl.BlockSpec((d, tf), lambda i, f: (0, f)),
                  pl.BlockSpec((tf, d), lambda i, f: (f, 0)),
                  pl.BlockSpec((1, d), lambda i, f: (0, 0))],
        out_specs=row,
        out_shape=jax.ShapeDtypeStruct((rows, d), F32),
        scratch_shapes=[pltpu.VMEM((tm, d), F32)],
        compiler_params=pltpu.CompilerParams(
            dimension_semantics=("parallel", "arbitrary"), vmem_limit_bytes=VMEM_LIMIT),
        name="ffn",
    )(h, x1, w_gate, w_up, w_down, g_final.reshape(1, d))


def _alibi_slopes():
    return jnp.exp2(-8.0 * jnp.arange(1, N_HEADS + 1, dtype=F32) / N_HEADS) * LOG2E


def _slope_table(slopes):
    s1 = slopes.astype(BF16)
    r1 = slopes - s1.astype(F32)
    s2 = r1.astype(BF16)
    s3 = (r1 - s2.astype(F32)).astype(BF16)
    tab = jnp.stack([s1, s1, s2, s2, s3, s3] + [jnp.zeros_like(s1)] * 10, axis=-1)
    return tab.reshape(N_KV, GROUP, 16)


def _position_rows(pos):
    pos = np.arange(pos) if np.isscalar(pos) else np.asarray(pos)
    hi, lo = (pos // 64) * 64, pos % 64
    assert pos.max() < 64 * 256
    rows = np.stack([hi, lo, hi, lo, hi, lo] + [np.zeros_like(pos)] * 10).astype(np.float32)
    return jnp.asarray(rows, dtype=BF16)


def _layer_weights(p, l):
    w_in = _pack_w_in(p['w_in'][l], p['ssm_d'].shape[-1])
    s5 = _s5_weights(*(p[n][l] for n in ('ssm_lam_re', 'ssm_lam_im', 'ssm_log_dt', 'ssm_b_re', 'ssm_b_im',
                                          'ssm_c_re', 'ssm_c_im', 'ssm_d')))
    cast = lambda n: p[n][l].astype(BF16)
    return dict(
        g_mix=p['g_mix'][l], w_in=w_in,
        pool=p['w_cmp_pool'][l], pe=p['w_cmp_pe'][l], phi=p['w_cmp_phi'][l],
        pool_mats=_pool_matrices(p['w_cmp_pool'][l], PROMPT_PAGES_PER_STEP),
        w_attn_out=cast('w_attn_out'), w_glu=cast('w_glu'), w_out=cast('w_out'), g_ffn=p['g_ffn'][l],
        w_gate=cast('w_gate'), w_up=cast('w_up'), w_down=cast('w_down'), s5=s5)


def _tail(x, o, y3, sg, lw, g_final, tm):
    rows = x.shape[0]
    m = _merge(o, y3.reshape(y3.shape[0], rows, 128), lw['w_attn_out'], lw['w_glu'], sg, tm)
    x1, h = _outproj(m, x, lw['w_out'], lw['g_ffn'], tm)
    return _ffn(h, x1, lw['w_gate'], lw['w_up'], lw['w_down'], g_final, tm)


def _feature_major_rows(kv_t, t0):
    b, f, t = kv_t.shape
    return kv_t[:, :, t0:].reshape(b, f // KV_COLS, N_KV, HEAD_DIM, t - t0).transpose(0, 4, 1, 2, 3)


def _layer_prompt(x, lw, slopes, g_out):
    b, t, d = x.shape
    rows = b * t
    assert t % (PROMPT_PAGES_PER_STEP * PAGE) == 0
    q, u3, sg, gt, kvc_t, kvw_t = _in_proj(x.reshape(rows, d), lw['g_mix'], lw['w_in'], 1024, seq_len=t)

    pooled = _pool_pages(kvc_t, lambda bi, pg: (bi, 0, pg), (), b, t // PAGE, lw['pool_mats'],
                         PROMPT_PAGES_PER_STEP)
    _, vc_t, kc_rows = _cmp_finish(pooled, lw['pool'], lw['pe'], lw['phi'])
    o = _attn_prompt(slopes, q, kc_rows, vc_t, kvc_t, kvw_t, gt, b, t)

    gp = lw['s5'][3].shape[1]
    h0 = jnp.zeros((b, gp), F32)
    y3, hr, hi = _s5(u3, h0, h0, lw['s5'], b)

    y = _tail(x.reshape(rows, d), o, y3, sg, lw, g_out, 512).reshape(b, t, d)
    n_win = min(WINDOW, t)
    ng = gp // SSM_STATE
    return (y, _feature_major_rows(kvc_t, 0), _feature_major_rows(kvw_t, t - n_win),
            hr.reshape(b, ng, SSM_STATE), hi.reshape(b, ng, SSM_STATE))


def _layer_sample(x, cache, page_table, win_buf, h_re, h_im, lw, slopes, g_out):
    b, t, d = x.shape
    rows = b * t
    npages = page_table.shape[1]
    past = npages * cache.shape[1]
    n_buf = win_buf.shape[1]
    assert cache.shape[1] == PAGE and rows % S5_L == 0 and t == S5_L
    assert past % CMP_STRIDE == 0 and t < CMP_STRIDE and past % SEL_BLOCK == 0 and t <= SEL_BLOCK
    npg = DECODE_PAGES_PER_STEP
    assert npages % npg == 0
    q, u3, sg, gt, kvc, kvw = _in_proj(x.reshape(rows, d), lw['g_mix'], lw['w_in'], rows)

    cache_t = cache.transpose(0, 2, 3, 4, 1).reshape(cache.shape[0], 4 * KV_COLS, PAGE)
    win_t = win_buf.transpose(0, 2, 3, 4, 1).reshape(b, 2 * KV_COLS, n_buf)
    pooled = _pool_pages(cache_t, lambda bi, pg, pt: (pt[bi, pg], 0, 0), (page_table,), b, npages,
                         lw['pool_mats'], npg)
    kc_t, vc_t, _ = _cmp_finish(pooled, lw['pool'], lw['pe'], lw['phi'])

    nrows = N_HEADS * t
    eye = jnp.eye(N_KV, dtype=BF16)
    q5 = q.reshape(b, t, N_KV, GROUP, HEAD_DIM).transpose(0, 2, 3, 1, 4)
    qbd = (q5[:, :, :, :, None, :] * eye[None, :, None, None, :, None]).reshape(b, nrows, KV_COLS)
    maskbd = jnp.repeat(jnp.repeat(jnp.eye(N_KV, dtype=F32), GROUP * t, axis=0), HEAD_DIM, axis=1)
    slopecol = jnp.repeat(slopes, t).reshape(nrows, 1)
    g3 = gt.reshape(b, t, N_KV, 128)[..., :3 * GROUP].reshape(b, t, N_KV, 3, GROUP)
    g3 = g3.transpose(0, 2, 4, 1, 3).reshape(b, nrows, 3)
    gates_rows = jnp.pad(g3, ((0, 0), (0, 0), (0, 1)))
    n_blk = -(-(past + t) // SEL_BLOCK)

    opart, sel_t = _attn_dec1(qbd, kc_t, vc_t, win_t, kvw.reshape(b, t, 2 * KV_COLS), gates_rows, slopecol, maskbd,
                              past, t, n_blk)
    bps = npg * PAGE // SEL_BLOCK
    nstep = npages // npg
    nb_all = (nstep + 1) * bps
    sel = jnp.pad(sel_t[:, :n_blk], ((0, 0), (0, nb_all - n_blk), (0, 0)))
    sel = sel.reshape(b, nstep + 1, bps, N_KV, 1, t).transpose(0, 1, 3, 4, 5, 2)
    selrows = jnp.broadcast_to(sel, (b, nstep + 1, N_KV, GROUP, t, bps)).reshape(b, nstep + 1, nrows, bps)
    knew = jnp.pad(kvc.reshape(b, t, 4 * KV_COLS)[:, :, 2 * KV_COLS:], ((0, 0), (0, PAGE - t), (0, 0)))
    o_rows = _attn_dec2(cache_t, page_table, qbd, selrows.astype(BF16), slopecol, knew, opart, gates_rows,
                        maskbd, past, t, npg)
    o = (o_rows.reshape(b, N_KV, GROUP, t, HEAD_DIM).transpose(0, 3, 1, 2, 4).reshape(rows, Q_COLS).astype(BF16))

    gp = h_re.shape[1] * h_re.shape[2]
    y3, hr, hi = _s5(u3, h_re.reshape(b, gp), h_im.reshape(b, gp), lw['s5'], b)

    y = _tail(x.reshape(rows, d), o, y3, sg, lw, g_out, rows).reshape(b, t, d)
    win_all = jnp.concatenate([win_buf, kvw.reshape(b, t, 2, N_KV, HEAD_DIM)], axis=1)
    n_keep = min(WINDOW, n_buf + t)
    return (y, kvc.reshape(b, t, 4, N_KV, HEAD_DIM), win_all[:, n_buf + t - n_keep:], hr.reshape(h_re.shape),
            hi.reshape(h_im.shape))


def kernel(x_prompt, x_sample, cache_kv, state_win, state_ssm_re, state_ssm_im, page_table, g_mix, w_in, w_cmp_pe, w_cmp_pool, w_cmp_phi, w_attn_out, ssm_lam_re, ssm_lam_im, ssm_log_dt, ssm_b_re, ssm_b_im, ssm_c_re, ssm_c_im, ssm_d, w_glu, w_out, g_ffn, w_gate, w_up, w_down, g_final):
    depth = g_mix.shape[0]
    assert depth == 1, "final norm is fused into the last layer's FFN; one layer supported"
    params = dict(g_mix=g_mix, w_in=w_in, w_cmp_pe=w_cmp_pe, w_cmp_pool=w_cmp_pool, w_cmp_phi=w_cmp_phi,
                  w_attn_out=w_attn_out, ssm_lam_re=ssm_lam_re, ssm_lam_im=ssm_lam_im, ssm_log_dt=ssm_log_dt,
                  ssm_b_re=ssm_b_re, ssm_b_im=ssm_b_im, ssm_c_re=ssm_c_re, ssm_c_im=ssm_c_im, ssm_d=ssm_d,
                  w_glu=w_glu, w_out=w_out, g_ffn=g_ffn, w_gate=w_gate, w_up=w_up, w_down=w_down)
    slopes = _alibi_slopes()
    outs = [[] for _ in range(8)]
    xp, xs = x_prompt, x_sample
    for l in range(depth):
        lw = _layer_weights(params, l)
        xp, kvp, winp, hrp, hip = _layer_prompt(xp, lw, slopes, g_final)
        xs, kvs, wins, hrs, his = _layer_sample(xs, cache_kv[l], page_table, state_win[l], state_ssm_re[l],
                                                state_ssm_im[l], lw, slopes, g_final)
        for lst, v in zip(outs, (kvp, winp, hrp, hip, kvs, wins, hrs, his)):
            lst.append(v)
    st = [jnp.stack(v) for v in outs]
    return (xp, xs, st[0], st[1], st[2], st[3], st[4], st[5], st[6], st[7])
```

```python
import functools
import math

import numpy as np
import jax
import jax.numpy as jnp
from jax import lax
from jax.experimental import pallas as pl
from jax.experimental.pallas import tpu as pltpu

F32 = jnp.float32
BF16 = jnp.bfloat16

N_HEADS = 16
HEAD_DIM = 64
N_KV = 4
GROUP = N_HEADS // N_KV
CMP_LEN = 32
CMP_STRIDE = 16
SEL_BLOCK = 64
SEL_TOP = 16
WINDOW = 512
SSM_GROUP = 16
SSM_STATE = 64
EPS = 1e-6
NEG = -1e30
FORCE_SCORE = 1e4

KV_COLS = N_KV * HEAD_DIM
Q_COLS = N_HEADS * HEAD_DIM
PAGE = 128
CMP_PER_PAGE = PAGE // CMP_STRIDE
PROMPT_PAGES_PER_STEP = 16
DECODE_PAGES_PER_STEP = 32
S5_L = 8
S5_SLAB = 128 // SSM_GROUP
VMEM_LIMIT = 56 * 1024 * 1024
LOG2E = 1.0 / math.log(2.0)
Q_SCALE = HEAD_DIM ** -0.5 * LOG2E


def _dot(a, b):
    return jnp.dot(a, b, preferred_element_type=F32)


def _dot_nt(a, b):
    return lax.dot_general(a, b, (((1,), (1,)), ((), ())), preferred_element_type=F32)


def _split(a):
    hi = a.astype(BF16)
    lo = (a - hi.astype(F32)).astype(BF16)
    return hi, lo


def _dot3(a, b):
    ah, al = _split(a)
    bh, bl = _split(b)
    return _dot(ah, bh) + _dot(ah, bl) + _dot(al, bh)


def _dot_nt_hl(w_bf16, a):
    ah, al = _split(a)
    return _dot_nt(w_bf16, ah) + _dot_nt(w_bf16, al)


def _gelu(x):
    return 0.5 * x * (1.0 + jnp.tanh(math.sqrt(2.0 / math.pi) * (x + 0.044715 * (x * x * x))))


def _iota(shape, dim):
    return lax.broadcasted_iota(jnp.int32, shape, dim)


def _log2(n):
    assert n > 0 and n & (n - 1) == 0, n
    return n.bit_length() - 1


def _div_pow2(x, n):
    return jnp.right_shift(x, _log2(n))


def _mod_pow2(x, n):
    return jnp.bitwise_and(x, (1 << _log2(n)) - 1)


def _softmax2_rows(s):
    m = jnp.max(s, axis=-1, keepdims=True)
    e = jnp.exp2(s - m)
    return e / jnp.sum(e, axis=-1, keepdims=True)


IN_TN = 512
_Q_T, _U_T, _MG_T, _KV_T = 2, 2, 8, 3
_KVC_T = 2
_U_0 = _Q_T
_MG_0 = _U_0 + _U_T
_GT_0 = _MG_0 + _MG_T
_KV_0 = _GT_0 + 1
IN_TILES = _KV_0 + _KV_T


def _inproj_kernel(x_ref, g_ref, wq_ref, wu_ref, wmg_ref, wgt_ref, wkv_ref, q_ref, u_ref, sg_ref, gt_ref, kvc_ref,
                   kvw_ref, xn_ref, *, kv_feature_major):
    j = pl.program_id(1)

    @pl.when(j == 0)
    def _():
        x = x_ref[...]
        r = lax.rsqrt(jnp.mean(x * x, axis=-1, keepdims=True) + EPS)
        xn_ref[...] = (x * r * g_ref[...]).astype(BF16)

    @pl.when(j < _U_0)
    def _():
        q_ref[...] = (_dot(xn_ref[...], wq_ref[...]) * Q_SCALE).astype(BF16)

    @pl.when((j >= _U_0) & (j < _MG_0))
    def _():
        z = _dot(xn_ref[...], wu_ref[...])
        for s in range(IN_TN // 128):
            u_ref[s] = z[:, 128 * s:128 * (s + 1)].reshape(u_ref.shape[1:])

    @pl.when((j >= _MG_0) & (j < _GT_0))
    def _():
        sg_ref[...] = jax.nn.sigmoid(_dot(xn_ref[...], wmg_ref[...])).astype(BF16)

    @pl.when(j == _GT_0)
    def _():
        gt_ref[...] = jax.nn.sigmoid(_dot(xn_ref[...], wgt_ref[...]))

    @pl.when(j >= _KV_0)
    def _():
        if kv_feature_major:
            z = _dot_nt(wkv_ref[...], xn_ref[...])
        else:
            z = _dot(xn_ref[...], wkv_ref[...])

        @pl.when(j < _KV_0 + _KVC_T)
        def _():
            kvc_ref[...] = z.reshape(kvc_ref.shape)

        @pl.when(j >= _KV_0 + _KVC_T)
        def _():
            kvw_ref[...] = z.reshape(kvw_ref.shape)


def _in_proj(x, g_mix, w, tm, seq_len=None):
    rows, d = x.shape
    feature_major = seq_len is not None

    def col(lo, n):
        return lambda i, j: (i, jnp.clip(j - lo, 0, n - 1))

    if feature_major:
        tiles_per_seq = seq_len // tm
        kvc_spec = pl.BlockSpec((1, IN_TN, tm), lambda i, j: (i // tiles_per_seq, jnp.clip(j - _KV_0, 0, _KVC_T - 1),
                                                              i % tiles_per_seq))
        kvw_spec = pl.BlockSpec((1, IN_TN, tm), lambda i, j: (i // tiles_per_seq, 0, i % tiles_per_seq))
        kvc_shape = jax.ShapeDtypeStruct((rows // seq_len, _KVC_T * IN_TN, seq_len), F32)
        kvw_shape = jax.ShapeDtypeStruct((rows // seq_len, IN_TN, seq_len), F32)
        wkv, wkv_spec = w['kv_t'], pl.BlockSpec((IN_TN, d), lambda i, j: (jnp.clip(j - _KV_0, 0, _KV_T - 1), 0))
    else:
        kvc_spec = pl.BlockSpec((tm, IN_TN), col(_KV_0, _KVC_T))
        kvw_spec = pl.BlockSpec((tm, IN_TN), lambda i, j: (i, 0))
        kvc_shape = jax.ShapeDtypeStruct((rows, _KVC_T * IN_TN), F32)
        kvw_shape = jax.ShapeDtypeStruct((rows, IN_TN), F32)
        wkv, wkv_spec = w['kv'], pl.BlockSpec((d, IN_TN), lambda i, j: (0, jnp.clip(j - _KV_0, 0, _KV_T - 1)))
    n_slab = IN_TN // 128
    return pl.pallas_call(
        functools.partial(_inproj_kernel, kv_feature_major=feature_major),
        grid=(rows // tm, IN_TILES),
        in_specs=[
            pl.BlockSpec((tm, d), lambda i, j: (i, 0), pipeline_mode=pl.Buffered(1)),
            pl.BlockSpec((1, d), lambda i, j: (0, 0)),
            pl.BlockSpec((d, IN_TN), lambda i, j: (0, jnp.clip(j, 0, _Q_T - 1))),
            pl.BlockSpec((d, IN_TN), lambda i, j: (0, jnp.clip(j - _U_0, 0, _U_T - 1))),
            pl.BlockSpec((d, IN_TN), lambda i, j: (0, jnp.clip(j - _MG_0, 0, _MG_T - 1))),
            pl.BlockSpec((d, IN_TN), lambda i, j: (0, 0), pipeline_mode=pl.Buffered(1)),
            wkv_spec,
        ],
        out_specs=[
            pl.BlockSpec((tm, IN_TN), col(0, _Q_T)),
            pl.BlockSpec((n_slab, tm // S5_L, S5_L, 128), lambda i, j: (jnp.clip(j - _U_0, 0, _U_T - 1), i, 0, 0)),
            pl.BlockSpec((tm, IN_TN), col(_MG_0, _MG_T)),
            pl.BlockSpec((tm, IN_TN), lambda i, j: (i, 0)),
            kvc_spec, kvw_spec,
        ],
        out_shape=[
            jax.ShapeDtypeStruct((rows, _Q_T * IN_TN), BF16),
            jax.ShapeDtypeStruct((_U_T * n_slab, rows // S5_L, S5_L, 128), F32),
            jax.ShapeDtypeStruct((rows, _MG_T * IN_TN), BF16),
            jax.ShapeDtypeStruct((rows, IN_TN), F32),
            kvc_shape, kvw_shape,
        ],
        scratch_shapes=[pltpu.VMEM((tm, d), BF16)],
        compiler_params=pltpu.CompilerParams(
            dimension_semantics=("parallel", "arbitrary"), vmem_limit_bytes=VMEM_LIMIT),
        name="in_proj",
    )(x, g_mix.reshape(1, d), w['q'], w['u'], w['mg'], w['gt'], wkv)


def _pack_w_in(w_in, ssm_width):
    cuts = np.cumsum([Q_COLS, 6 * KV_COLS, 3 * N_HEADS, ssm_width]).tolist()
    wq, wkv, wgt, wu, wmg = jnp.split(w_in, cuts, axis=1)
    assert wq.shape[1] == _Q_T * IN_TN and wkv.shape[1] == _KV_T * IN_TN
    assert wu.shape[1] == _U_T * IN_TN and wmg.shape[1] == _MG_T * IN_TN and IN_TN == N_KV * 128
    d = w_in.shape[0]
    wgt = wgt.reshape(d, 3, N_KV, GROUP).transpose(0, 2, 1, 3).reshape(d, N_KV, 3 * GROUP)
    wgt = jnp.pad(wgt, ((0, 0), (0, 0), (0, 128 - 3 * GROUP))).reshape(d, IN_TN)
    return dict(q=wq.astype(BF16), u=wu.astype(BF16), mg=wmg.astype(BF16), gt=wgt.astype(BF16),
                kv=wkv.astype(BF16), kv_t=wkv.T.astype(BF16))


def _pool_kernel(*refs, npg, n_tables):
    refs = refs[n_tables:]
    prev_ref = refs[0]
    page_refs = refs[1:npg + 1]
    mk_ref, mv_ref, mprev_ref, out_ref = refs[npg + 1:]
    pages = [prev_ref[0].astype(BF16)] + [page_refs[j][0].astype(BF16) for j in range(npg)]
    mpg = mk_ref.shape[0] // PAGE
    nblk = mk_ref.shape[1]
    for seg in range(npg // mpg):
        x = jnp.concatenate(pages[1 + seg * mpg:1 + (seg + 1) * mpg], axis=1)
        for half, m_ref in ((0, mk_ref), (1, mv_ref)):
            rows = slice(half * KV_COLS, (half + 1) * KV_COLS)
            head = _dot(pages[seg * mpg][rows], mprev_ref[half])
            if seg == 0:
                head = jnp.where(pl.program_id(1) > 0, head, 0.0)
            out_ref[0, rows, seg * nblk:(seg + 1) * nblk] = _dot(x[rows], m_ref[...]) + head


def _pool_matrices(pool, npg):
    nblk = npg * CMP_PER_PAGE
    cols = []
    for r in range(CMP_PER_PAGE):
        start = CMP_STRIDE * (r - 1)
        lo, hi = max(start, 0), min(start + CMP_LEN, PAGE)
        cols.append(jnp.pad(pool[:, lo - start:hi - start], ((0, 0), (lo, PAGE - hi))))
    body = jnp.stack(cols, axis=-1)
    carry = jnp.pad(pool[:, :CMP_STRIDE], ((0, 0), (PAGE - CMP_STRIDE, 0)))
    first = jnp.asarray(np.arange(CMP_PER_PAGE) == 0, F32)
    same = jnp.eye(npg, dtype=F32)[None, :, None, :, None]
    nxt = jnp.eye(npg, k=1, dtype=F32)[None, :, None, :, None]
    m = same * body[:, None, :, None, :] + nxt * (carry[:, None, :, None, None] * first)
    m = m.reshape(2, npg * PAGE, nblk).astype(BF16)
    mprev = (carry[:, :, None] * jnp.asarray(np.arange(nblk) == 0, F32)).astype(BF16)
    return m[0], m[1], mprev


def _pool_pages(pages, page_index, tables, nseq, npages, mats, npg):
    assert npg % (mats[0].shape[0] // PAGE) == 0
    nstep = npages // npg
    nblk = npg * CMP_PER_PAGE
    mk, mv, mprev = mats

    def spec(off):
        return pl.BlockSpec((1, 2 * KV_COLS, PAGE),
                            lambda b, s, *t: page_index(b, jnp.maximum(s * npg + off, 0), *t))

    const = lambda shape: pl.BlockSpec(shape, lambda b, s, *t: (0,) * len(shape))
    return pl.pallas_call(
        functools.partial(_pool_kernel, npg=npg, n_tables=len(tables)),
        grid_spec=pltpu.PrefetchScalarGridSpec(
            num_scalar_prefetch=len(tables),
            grid=(nseq, nstep),
            in_specs=[spec(j) for j in range(-1, npg)] + [const(mk.shape), const(mv.shape), const(mprev.shape)],
            out_specs=pl.BlockSpec((1, 2 * KV_COLS, nblk), lambda b, s, *t: (b, 0, s)),
        ),
        out_shape=jax.ShapeDtypeStruct((nseq, 2 * KV_COLS, npages * CMP_PER_PAGE), F32),
        compiler_params=pltpu.CompilerParams(
            dimension_semantics=("parallel", "arbitrary"), vmem_limit_bytes=VMEM_LIMIT),
        name="cmp_pool",
    )(*tables, *([pages] * (npg + 1)), mk, mv, mprev)


def _cmp_finish_kernel(pooled_ref, poolw_ref, pe_ref, phi_ref, kc_ref, vc_ref, kcr_ref):
    bias = jnp.sum(poolw_ref[...] * pe_ref[...], axis=1, keepdims=True)
    x = _gelu(pooled_ref[0] + bias)
    kc = _dot3(phi_ref[0], x[:KV_COLS])
    kc_ref[0] = kc.astype(BF16)
    kcr_ref[0] = kc.T.astype(BF16)
    vc_ref[0] = _dot3(phi_ref[1], x[KV_COLS:]).astype(BF16)


def _cmp_finish(pooled, pool, pe, phi):
    nseq, _, n = pooled.shape
    poolw = jnp.concatenate([jnp.broadcast_to(pool[s][None, :], (KV_COLS, CMP_LEN)) for s in range(2)], axis=0)
    pe_t = jnp.concatenate([jnp.tile(pe[s].T, (N_KV, 1)) for s in range(2)], axis=0)
    phi_bd = jnp.stack([jnp.kron(jnp.eye(N_KV, dtype=F32), phi[s].T) for s in range(2)])
    spec = pl.BlockSpec((1, KV_COLS, n), lambda b: (b, 0, 0))
    return pl.pallas_call(
        _cmp_finish_kernel,
        grid=(nseq,),
        in_specs=[
            pl.BlockSpec((1, 2 * KV_COLS, n), lambda b: (b, 0, 0)),
            pl.BlockSpec((2 * KV_COLS, CMP_LEN), lambda b: (0, 0)),
            pl.BlockSpec((2 * KV_COLS, CMP_LEN), lambda b: (0, 0)),
            pl.BlockSpec((2, KV_COLS, KV_COLS), lambda b: (0, 0, 0)),
        ],
        out_specs=[spec, spec, pl.BlockSpec((1, n, KV_COLS), lambda b: (b, 0, 0))],
        out_shape=[jax.ShapeDtypeStruct((nseq, KV_COLS, n), BF16)] * 2
        + [jax.ShapeDtypeStruct((nseq, n, KV_COLS), BF16)],
        compiler_params=pltpu.CompilerParams(dimension_semantics=("parallel",)),
        name="cmp_finish",
    )(pooled, poolw, pe_t, phi_bd)


ATT_TQ = 256
ATT_TK = 256
NBLK_PAD = 128
POS_ROWS = 16
AUG = HEAD_DIM + POS_ROWS


def _attn_prompt_kernel(q_ref, kca_ref, vc_ref, ks_ref, vs_ref, kw_ref, vw_ref, gt_ref, ovt_ref, ktab_ref,
                        stab_ref, o_ref, ksa_ref, vsa_ref, kwa_ref, vwa_ref, qat_ref, m_ref, acc_ref,
                        *, tq, tk, n_sel, n_top):
    qi = pl.program_id(2)
    q0 = qi * tq
    seq = ksa_ref.shape[0]

    @pl.when(qi == 0)
    def _():
        zero = jnp.zeros((128 - HEAD_DIM, seq), F32)
        ones = jnp.where(_iota((POS_ROWS, seq), 0) == 0, 1.0, 0.0).astype(BF16)
        for src, dst, lo in ((ks_ref, ksa_ref, NBLK_PAD), (kw_ref, kwa_ref, 0)):
            k_rows = jnp.concatenate([src[0], zero], axis=0).T[:, :HEAD_DIM].astype(BF16)
            dst[:, lo:] = jnp.concatenate([k_rows, ktab_ref[:, NBLK_PAD:]], axis=1)
        ksa_ref[:, 0:NBLK_PAD] = ktab_ref[:, 0:NBLK_PAD]
        for src, dst in ((vs_ref, vsa_ref), (vw_ref, vwa_ref)):
            dst[0:HEAD_DIM] = src[0].astype(BF16)
            dst[HEAD_DIM:] = ones

    q_t = q_ref[...].astype(F32).T
    q_t = jnp.concatenate([q_t[n * HEAD_DIM:(n + 1) * HEAD_DIM] for n in range(GROUP)], axis=1).astype(BF16)
    stab = stab_ref[0]
    sp_t = jnp.concatenate([jnp.broadcast_to(stab[:, n:n + 1], (POS_ROWS, tq)) for n in range(GROUP)], axis=1)
    qat_ref[NBLK_PAD:NBLK_PAD + HEAD_DIM] = q_t
    qat_ref[NBLK_PAD + HEAD_DIM:] = sp_t.astype(BF16)
    trow = q0 + _iota((1, tq), 1)
    lanes = [slice(n * tq, (n + 1) * tq) for n in range(GROUP)]

    ncmp = kca_ref.shape[2]
    nidx = _iota((ncmp, 1), 0)
    heads = lambda a: jnp.concatenate([a] * GROUP, axis=1)
    valid = (nidx >= 1) & (CMP_STRIDE * nidx + (CMP_STRIDE - 1) <= trow)
    valid = heads(jnp.where(valid, 1.0, 0.0)) > 0.5
    s = jnp.where(valid, _dot(kca_ref[0, 0], qat_ref[NBLK_PAD:]), NEG)
    e = jnp.exp2(s - jnp.max(s, axis=0, keepdims=True))
    pc = jnp.where(valid, e / jnp.sum(e, axis=0, keepdims=True), 0.0)
    o_c = _dot(vc_ref[0], pc.astype(BF16))

    ph, plo = _split((pc[:, lanes[0]] + pc[:, lanes[1]]) + (pc[:, lanes[2]] + pc[:, lanes[3]]))
    imp = (_dot(ovt_ref[...], ph) + _dot(ovt_ref[...], plo))[0:n_sel]
    blk = _iota((n_sel, 1), 0)
    cur = _div_pow2(q0 + _iota((1, tq), 1), SEL_BLOCK)
    forced = (blk == 0) | (blk == cur) | (blk == cur - 1)
    imp = jnp.where(forced, FORCE_SCORE, imp)
    imp = jnp.where(blk <= cur, imp, -1.0)
    cnt = jnp.zeros((n_sel, tq), F32)
    for j in range(n_sel):
        vj = imp[j:j + 1, :]
        beats = (vj > imp) | ((vj == imp) & (blk > j))
        cnt = cnt + jnp.where(beats, 1.0, 0.0)
    mask_t = jnp.where((cnt < n_top) & (blk <= cur), 0.0, NEG)
    mask_t = jnp.concatenate([mask_t, jnp.full((NBLK_PAD - n_sel, tq), NEG, F32)], axis=0).astype(BF16)
    qat_ref[0:NBLK_PAD] = heads(mask_t)

    m_ref[...] = jnp.full_like(m_ref, NEG)
    acc_ref[...] = jnp.zeros_like(acc_ref)

    def scores(kt):
        return _dot(ksa_ref[pl.ds(pl.multiple_of(kt * tk, tk), tk), :], qat_ref[...])

    def update(kt, s):
        m_old = m_ref[...]
        m_new = jnp.maximum(m_old, jnp.max(s, axis=0, keepdims=True))
        p = jnp.exp2(s - m_new).astype(BF16)
        pv = _dot(vsa_ref[:, pl.ds(pl.multiple_of(kt * tk, tk), tk)], p)
        acc_ref[...] = jnp.exp2(m_old - m_new) * acc_ref[...] + pv
        m_ref[...] = m_new

    def pair(j, carry):
        s0, s1 = scores(2 * j), scores(2 * j + 1)
        update(2 * j, s0)
        update(2 * j + 1, s1)
        return carry

    lax.fori_loop(0, qi // 2, pair, 0)

    @pl.when(qi % 2 == 1)
    def _():
        update(qi - 1, scores(qi - 1))

    span = WINDOW + tq
    w0 = pl.multiple_of(jnp.maximum(q0 - WINDOW, 0), tq)
    s_diag = scores(qi) + heads(jnp.where(q0 + _iota((tk, 1), 0) <= trow, 0.0, NEG))
    dist = trow - (w0 + _iota((span, 1), 0))
    s_win = (_dot(kwa_ref[pl.ds(w0, span), :], qat_ref[NBLK_PAD:])
             + heads(jnp.where((dist >= 0) & (dist < WINDOW), 0.0, NEG)))
    update(qi, s_diag)
    acc = acc_ref[...]
    o_s = acc[:HEAD_DIM] / acc[HEAD_DIM:HEAD_DIM + 1]
    ow = _dot(vwa_ref[:, pl.ds(w0, span)], jnp.exp2(s_win - jnp.max(s_win, axis=0, keepdims=True)).astype(BF16))
    o_w = ow[:HEAD_DIM] / ow[HEAD_DIM:HEAD_DIM + 1]

    gt_t = gt_ref[...].T
    outs = []
    for n in range(GROUP):
        outs.append(gt_t[n:n + 1] * o_c[:, lanes[n]] + gt_t[GROUP + n:GROUP + n + 1] * o_s[:, lanes[n]]
                    + gt_t[2 * GROUP + n:2 * GROUP + n + 1] * o_w[:, lanes[n]])
    o_ref[...] = jnp.concatenate(outs, axis=0).T.astype(BF16)


def _overlap_t(n_blk_rows, n_cmp):
    s = np.arange(n_blk_rows)[:, None]
    n = np.arange(n_cmp)[None, :]
    r = SEL_BLOCK // CMP_STRIDE
    return jnp.asarray(((n >= r * s) & (n <= r * s + r)).astype(np.float32), dtype=BF16)


def _block_expand(n_blk_rows, n_keys):
    s = np.arange(n_blk_rows)[:, None]
    k = np.arange(n_keys)[None, :]
    return jnp.asarray((k // SEL_BLOCK == s).astype(np.float32), dtype=BF16)


def _attn_prompt(slopes, q, kc_rows, vc_t, kvc_t, kvw_t, gates, nbatch, seq):
    ncmp = kc_rows.shape[1]
    tq, tk = ATT_TQ, ATT_TK
    assert tk == tq and seq % tk == 0 and WINDOW % tq == 0 and WINDOW + tq <= seq
    nq = seq // tq
    n_sel = seq // SEL_BLOCK
    ovt = _overlap_t(NBLK_PAD, ncmp)
    ktab = jnp.concatenate([_block_expand(NBLK_PAD, seq), _position_rows(seq)], axis=0).T
    cmp_end = CMP_STRIDE * np.arange(ncmp) + (CMP_STRIDE - 1)
    kca = jnp.concatenate([kc_rows.reshape(nbatch, ncmp, N_KV, HEAD_DIM).transpose(0, 2, 1, 3),
                           jnp.broadcast_to(_position_rows(cmp_end).T, (nbatch, N_KV, ncmp, POS_ROWS))], axis=-1)
    stab = _slope_table(slopes).astype(F32).transpose(0, 2, 1)
    per_slot = KV_COLS // HEAD_DIM

    def kv_spec(slot):
        return pl.BlockSpec((1, HEAD_DIM, seq), lambda b, g, i: (b, slot * per_slot + g, 0))

    q_spec = pl.BlockSpec((tq, GROUP * HEAD_DIM), lambda b, g, i: (b * nq + i, g))
    return pl.pallas_call(
        functools.partial(_attn_prompt_kernel, tq=tq, tk=tk, n_sel=n_sel, n_top=min(SEL_TOP, n_sel)),
        grid=(nbatch, N_KV, nq),
        in_specs=[
            q_spec,
            pl.BlockSpec((1, 1, ncmp, AUG), lambda b, g, i: (b, g, 0, 0)),
            pl.BlockSpec((1, HEAD_DIM, ncmp), lambda b, g, i: (b, g, 0)),
            kv_spec(2), kv_spec(3), kv_spec(0), kv_spec(1),
            pl.BlockSpec((tq, 128), lambda b, g, i: (b * nq + i, g)),
            pl.BlockSpec((NBLK_PAD, ncmp), lambda b, g, i: (0, 0)),
            pl.BlockSpec((seq, NBLK_PAD + POS_ROWS), lambda b, g, i: (0, 0)),
            pl.BlockSpec((1, POS_ROWS, GROUP), lambda b, g, i: (g, 0, 0)),
        ],
        out_specs=q_spec,
        out_shape=jax.ShapeDtypeStruct(q.shape, BF16),
        scratch_shapes=[
            pltpu.VMEM((seq, NBLK_PAD + AUG), BF16),
            pltpu.VMEM((AUG, seq), BF16),
            pltpu.VMEM((seq, AUG), BF16),
            pltpu.VMEM((AUG, seq), BF16),
            pltpu.VMEM((NBLK_PAD + AUG, GROUP * tq), BF16),
            pltpu.VMEM((1, GROUP * tq), F32),
            pltpu.VMEM((AUG, GROUP * tq), F32),
        ],
        compiler_params=pltpu.CompilerParams(
            dimension_semantics=("parallel", "parallel", "arbitrary"), vmem_limit_bytes=VMEM_LIMIT),
        name="attn_prompt",
    )(q, kca, vc_t, kvc_t, kvc_t, kvw_t, kvw_t, gates, ovt, ktab, stab)


def _diag_blocks(o, maskbd):
    o = o * maskbd
    return (o[:, 0:64] + o[:, 64:128]) + (o[:, 128:192] + o[:, 192:256])


def _attn_dec1_kernel(qbd_ref, kc_ref, vc_ref, win_ref, kvnew_ref, gates_ref, slope_ref, ovt_ref,
                      maskbd_ref, opart_ref, sel_ref, *, past, tdec, n_blk, n_top):
    qb = qbd_ref[0]
    nrows = qb.shape[0]
    tcol = past + _mod_pow2(_iota((nrows, 1), 0), tdec)
    pref = float(past + tdec)
    slope = slope_ref[...]
    maskbd = maskbd_ref[...]

    ncmp = kc_ref.shape[2]
    nrow = _iota((1, ncmp), 1)
    cend = CMP_STRIDE * nrow + (CMP_STRIDE - 1)
    valid = (nrow >= 1) & (cend <= tcol)
    s = _dot(qb, kc_ref[0])
    s = jnp.where(valid, s + slope * (cend.astype(F32) - pref), NEG)
    p = jnp.where(valid, _softmax2_rows(s), 0.0)
    o_c = _diag_blocks(_dot_nt(p.astype(BF16), vc_ref[0]), maskbd)

    per_g = GROUP * tdec
    psum = jnp.concatenate(
        [sum(p[gi * per_g + n * tdec: gi * per_g + (n + 1) * tdec] for n in range(GROUP)) for gi in range(N_KV)],
        axis=0)
    imp = _dot_nt_hl(ovt_ref[...], psum)
    nb_pad = imp.shape[0]
    blk = _iota((nb_pad, 1), 0)
    cur = _div_pow2(past + _mod_pow2(_iota((1, N_KV * tdec), 1), tdec), SEL_BLOCK)
    forced = (blk == 0) | (blk == cur) | (blk == cur - 1)
    imp = jnp.where(forced, FORCE_SCORE, imp)
    imp = jnp.where(blk <= cur, imp, -1.0)
    imp = jnp.where(blk < n_blk, imp, -2.0)
    sel = jnp.zeros(imp.shape, F32)
    for _ in range(n_top):
        mx = jnp.max(imp, axis=0, keepdims=True)
        first = jnp.min(jnp.where(imp == mx, blk, nb_pad), axis=0, keepdims=True)
        pick = blk == first
        sel = jnp.where(pick, 1.0, sel)
        imp = jnp.where(pick, -jnp.inf, imp)
    sel_ref[0] = sel

    nbuf = win_ref.shape[2]
    knew = kvnew_ref[0]
    zpad = jnp.zeros((PAGE - tdec, KV_COLS), F32)
    kn = jnp.concatenate([knew[:, :KV_COLS], zpad], axis=0).astype(BF16)
    vn = jnp.concatenate([knew[:, KV_COLS:], zpad], axis=0).astype(BF16)
    sw = jnp.concatenate([_dot(qb, win_ref[0, :KV_COLS].astype(BF16)), _dot_nt(qb, kn)], axis=1)
    idx = _iota((1, nbuf + PAGE), 1)
    kposw = past - nbuf + idx
    dist = tcol - kposw
    validw = (dist >= 0) & (dist < WINDOW) & (kposw >= 0) & (idx < nbuf + tdec)
    sw = jnp.where(validw, sw + slope * (kposw.astype(F32) - pref), NEG)
    pw = _softmax2_rows(sw).astype(BF16)
    o_w = _dot_nt(pw[:, :nbuf], win_ref[0, KV_COLS:].astype(BF16)) + _dot(pw[:, nbuf:], vn)
    o_w = _diag_blocks(o_w, maskbd)

    gates = gates_ref[0]
    opart_ref[0] = gates[:, 0:1] * o_c + gates[:, 2:3] * o_w


def _attn_dec1(qbd, kc_t, vc_t, win_t, kvnew, gates_rows, slopecol, maskbd, past, tdec, n_blk):
    nseq, nrows, _ = qbd.shape
    ncmp = kc_t.shape[2]
    nb_pad = -(-n_blk // 8) * 8
    ovt = _overlap_t(nb_pad, ncmp)
    nbuf = win_t.shape[2]
    return pl.pallas_call(
        functools.partial(_attn_dec1_kernel, past=past, tdec=tdec, n_blk=n_blk, n_top=min(SEL_TOP, n_blk)),
        grid=(nseq,),
        in_specs=[
            pl.BlockSpec((1, nrows, KV_COLS), lambda b: (b, 0, 0)),
            pl.BlockSpec((1, KV_COLS, ncmp), lambda b: (b, 0, 0)),
            pl.BlockSpec((1, KV_COLS, ncmp), lambda b: (b, 0, 0)),
            pl.BlockSpec((1, 2 * KV_COLS, nbuf), lambda b: (b, 0, 0)),
            pl.BlockSpec((1, tdec, 2 * KV_COLS), lambda b: (b, 0, 0)),
            pl.BlockSpec((1, nrows, 4), lambda b: (b, 0, 0)),
            pl.BlockSpec((nrows, 1), lambda b: (0, 0)),
            pl.BlockSpec((nb_pad, ncmp), lambda b: (0, 0)),
            pl.BlockSpec((nrows, KV_COLS), lambda b: (0, 0)),
        ],
        out_specs=[
            pl.BlockSpec((1, nrows, HEAD_DIM), lambda b: (b, 0, 0)),
            pl.BlockSpec((1, nb_pad, N_KV * tdec), lambda b: (b, 0, 0)),
        ],
        out_shape=[
            jax.ShapeDtypeStruct((nseq, nrows, HEAD_DIM), F32),
            jax.ShapeDtypeStruct((nseq, nb_pad, N_KV * tdec), F32),
        ],
        compiler_params=pltpu.CompilerParams(dimension_semantics=("parallel",), vmem_limit_bytes=VMEM_LIMIT),
        name="attn_dec_select",
    )(qbd, kc_t, vc_t, win_t, kvnew, gates_rows, slopecol, ovt, maskbd)


def _attn_dec2_kernel(pt_ref, *refs, npg, past, tdec):
    page_refs = refs[:npg]
    (qbd_ref, selrow_ref, sellast_ref, e_ref, slope_ref, knew_ref, opart_ref, gates_ref, maskbd_ref,
     o_ref, m_ref, l_ref, acc_ref) = refs[npg:]
    step = pl.program_id(1)
    nkeys = npg * PAGE
    qb = qbd_ref[0]
    nrows = qb.shape[0]
    pref = float(past + tdec)
    slope = slope_ref[...]

    @pl.when(step == 0)
    def _():
        m_ref[...] = jnp.full_like(m_ref, NEG)
        l_ref[...] = jnp.zeros_like(l_ref)
        acc_ref[...] = jnp.zeros_like(acc_ref)

    def update(s, pv_fn):
        m_old = m_ref[...]
        m_new = jnp.maximum(m_old, jnp.max(s, axis=-1, keepdims=True))
        alpha = jnp.exp2(m_old - m_new)
        p = jnp.exp2(s - m_new)
        l_ref[...] = alpha * l_ref[...] + jnp.sum(p, axis=-1, keepdims=True)
        acc_ref[...] = alpha * acc_ref[...] + pv_fn(p.astype(BF16))
        m_ref[...] = m_new

    s = jnp.concatenate([_dot(qb, page_refs[j][0, :KV_COLS].astype(BF16)) for j in range(npg)], axis=1)
    kpos = step * nkeys + _iota((1, nkeys), 1)
    chosen = _dot(selrow_ref[0, 0], e_ref[...])
    s = s + slope * (kpos.astype(F32) - pref) + jnp.where(chosen > 0.5, 0.0, NEG)

    def pv_pages(p):
        acc = _dot_nt(p[:, 0:PAGE], page_refs[0][0, KV_COLS:].astype(BF16))
        for j in range(1, npg):
            acc = acc + _dot_nt(p[:, PAGE * j:PAGE * (j + 1)], page_refs[j][0, KV_COLS:].astype(BF16))
        return acc

    update(s, pv_pages)

    @pl.when(step == pl.num_programs(1) - 1)
    def _():
        tcol = past + _mod_pow2(_iota((nrows, 1), 0), tdec)
        kn = knew_ref[0]
        kposn = past + _iota((1, PAGE), 1)
        sn = _dot_nt(qb, kn[:, :KV_COLS].astype(BF16))
        ok = (sellast_ref[0, 0][:, 0:1].astype(F32) > 0.5) & (kposn <= tcol)
        sn = sn + slope * (kposn.astype(F32) - pref) + jnp.where(ok, 0.0, NEG)
        update(sn, lambda p: _dot(p, kn[:, KV_COLS:].astype(BF16)))
        o_s = _diag_blocks(acc_ref[...] / l_ref[...], maskbd_ref[...])
        o_ref[0] = opart_ref[0] + gates_ref[0][:, 1:2] * o_s


def _attn_dec2(cache_t, page_table, qbd, selrows, slopecol, knew, opart, gates_rows, maskbd, past, tdec, npg):
    nseq, npages = page_table.shape
    nstep = npages // npg
    nrows = qbd.shape[1]
    blk_per_step = npg * PAGE // SEL_BLOCK
    emat = _block_expand(blk_per_step, npg * PAGE)
    page_specs = [
        pl.BlockSpec((1, 2 * KV_COLS, PAGE), lambda b, s, pt, j=j: (pt[b, s * npg + j], 1, 0))
        for j in range(npg)
    ]
    per_seq = lambda shape: pl.BlockSpec((1,) + shape, lambda b, s, pt: (b,) + (0,) * len(shape))
    const = lambda shape: pl.BlockSpec(shape, lambda b, s, pt: (0,) * len(shape))
    return pl.pallas_call(
        functools.partial(_attn_dec2_kernel, npg=npg, past=past, tdec=tdec),
        grid_spec=pltpu.PrefetchScalarGridSpec(
            num_scalar_prefetch=1,
            grid=(nseq, nstep),
            in_specs=page_specs + [
                per_seq((nrows, KV_COLS)),
                pl.BlockSpec((1, 1, nrows, blk_per_step), lambda b, s, pt: (b, s, 0, 0)),
                pl.BlockSpec((1, 1, nrows, blk_per_step), lambda b, s, pt: (b, nstep, 0, 0)),
                const((blk_per_step, npg * PAGE)),
                const((nrows, 1)),
                per_seq((PAGE, 2 * KV_COLS)),
                per_seq((nrows, HEAD_DIM)),
                per_seq((nrows, 4)),
                const((nrows, KV_COLS)),
            ],
            out_specs=per_seq((nrows, HEAD_DIM)),
            scratch_shapes=[
                pltpu.VMEM((nrows, 1), F32),
                pltpu.VMEM((nrows, 1), F32),
                pltpu.VMEM((nrows, KV_COLS), F32),
            ],
        ),
        out_shape=jax.ShapeDtypeStruct((nseq, nrows, HEAD_DIM), F32),
        compiler_params=pltpu.CompilerParams(
            dimension_semantics=("parallel", "arbitrary"), vmem_limit_bytes=VMEM_LIMIT),
        name="attn_dec_selected",
    )(page_table, *([cache_t] * npg), qbd, selrows, selrows, emat, slopecol, knew, opart, gates_rows, maskbd)


def _s5_weights(lam_re, lam_im, log_dt, b_re, b_im, c_re, c_im, d_skip):
    hp = lax.Precision.HIGHEST
    L = S5_L
    ng, p = lam_re.shape
    c = SSM_GROUP
    ns, gs = ng // S5_SLAB, S5_SLAB
    dt = jnp.exp(log_dt)[:, None]
    lr, li = lam_re, lam_im
    mag = jnp.exp(lr * dt)
    ar = mag * jnp.cos(li * dt)
    ai = mag * jnp.sin(li * dt)
    den = lr * lr + li * li
    fr = ((ar - 1.0) * lr + ai * li) / den
    fi = (ai * lr - (ar - 1.0) * li) / den
    bbr = fr[..., None] * b_re - fi[..., None] * b_im
    bbi = fr[..., None] * b_im + fi[..., None] * b_re
    j = jnp.arange(L + 1, dtype=F32)[:, None, None]
    pmag = jnp.exp(j * (lr * dt))
    pr = pmag * jnp.cos(j * (li * dt))
    pi = pmag * jnp.sin(j * (li * dt))
    abr = pr[..., None] * bbr - pi[..., None] * bbi
    abi = pr[..., None] * bbi + pi[..., None] * bbr
    kd = (jnp.einsum('gcp,jgpk->jgck', c_re, abr, precision=hp)
          - jnp.einsum('gcp,jgpk->jgck', c_im, abi, precision=hp))
    kd = kd.at[0].add(d_skip.reshape(ng, c)[:, :, None] * jnp.eye(c, dtype=F32))
    kc = kd[:L].reshape(L, ns, gs, c, c).transpose(1, 0, 2, 4, 3).reshape(ns, L, gs * c, c)
    ab = jnp.stack([abr[:L], abi[:L]]).reshape(2, L, ns, gs, p, c)
    abc = ab.transpose(2, 1, 3, 5, 0, 4).reshape(ns, L, gs * c, 2 * p)
    cr = c_re[None] * pr[1:, :, None, :] - c_im[None] * pi[1:, :, None, :]
    ci = -(c_re[None] * pi[1:, :, None, :] + c_im[None] * pr[1:, :, None, :])
    co = jnp.stack([cr, ci]).reshape(2, L, ns, gs, c, p)
    coc = co.transpose(2, 1, 0, 3, 5, 4).reshape(ns, L, 2 * gs * p, c)
    al = jnp.stack([pr[L].reshape(-1), pi[L].reshape(-1)])
    return kc.astype(BF16), abc.astype(BF16), coc.astype(BF16), al


def _s5_expanders():
    gs, c, p = S5_SLAB, SSM_GROUP, SSM_STATE
    lane = np.arange(gs * c)
    rep_c = (np.arange(c)[:, None] == lane[None, :] % c)
    same_g = (lane[:, None] // c == lane[None, :] // c)
    st = np.arange(2 * gs * p)
    rp = np.arange(2 * p)
    rep_p = (rp[:, None] // p == st[None, :] // (gs * p)) & (rp[:, None] % p == st[None, :] % p)
    g_in = (lane[:, None] // c == (st[None, :] // p) % gs)
    g_out = ((st[:, None] // p) % gs == lane[None, :] // c)
    as_bf = lambda a: jnp.asarray(a.astype(np.float32), dtype=BF16)
    return as_bf(rep_c), as_bf(same_g), as_bf(rep_p), as_bf(g_in), as_bf(g_out)


def _chunk_lanes(u_ref):
    return jnp.concatenate([u_ref[0, :, t, :] for t in range(S5_L)], axis=1)


def _s5_local_kernel(u_ref, abc_ref, rep_p_ref, g_in_ref, pre_ref, pim_ref, wp_ref):
    for s in range(S5_L):
        blk = _dot(abc_ref[0, S5_L - 1 - s], rep_p_ref[...]).astype(BF16) * g_in_ref[...]
        wp_ref[128 * s:128 * (s + 1), :] = blk
    r = _dot(_chunk_lanes(u_ref).astype(BF16), wp_ref[...])
    half = r.shape[1] // 2
    pre_ref[...] = r[:, :half]
    pim_ref[...] = r[:, half:]


def _s5_scan_kernel(pre_ref, pim_ref, al_ref, h0r_ref, h0i_ref, hsr_ref, hsi_ref, hfr_ref, hfi_ref):
    nchunk = pre_ref.shape[1]
    ar = al_ref[0:1, :]
    ai = al_ref[1:2, :]

    def body(k, carry):
        cr, ci = carry
        hsr_ref[:, pl.ds(k, 1), :] = cr[:, None, :]
        hsi_ref[:, pl.ds(k, 1), :] = ci[:, None, :]
        xr = pre_ref[:, pl.ds(k, 1), :][:, 0, :]
        xi = pim_ref[:, pl.ds(k, 1), :][:, 0, :]
        return ar * cr - ai * ci + xr, ar * ci + ai * cr + xi

    cr, ci = lax.fori_loop(0, nchunk, body, (h0r_ref[...], h0i_ref[...]), unroll=min(nchunk, 4))
    hfr_ref[...] = cr
    hfi_ref[...] = ci


def _s5_out_kernel(u_ref, kc_ref, coc_ref, rep_c_ref, same_g_ref, g_out_ref, hsr_ref, hsi_ref, y_ref,
                   wt_ref, wo_ref):
    rep_c = rep_c_ref[...]
    lag = [_dot(kc_ref[0, dl], rep_c).astype(BF16) * same_g_ref[...] for dl in range(S5_L)]
    zero = jnp.zeros((128, 128), BF16)
    for s in range(S5_L):
        for t in range(S5_L):
            wt_ref[128 * s:128 * (s + 1), 128 * t:128 * (t + 1)] = lag[t - s] if t >= s else zero
    for t in range(S5_L):
        wo_ref[:, 128 * t:128 * (t + 1)] = _dot(coc_ref[0, t], rep_c).astype(BF16) * g_out_ref[...]
    hs = jnp.concatenate([hsr_ref[...], hsi_ref[...]], axis=1).astype(BF16)
    y = _dot(_chunk_lanes(u_ref).astype(BF16), wt_ref[...]) + _dot(hs, wo_ref[...])
    for t in range(S5_L):
        y_ref[0, :, t, :] = y[:, 128 * t:128 * (t + 1)]


def _s5(u3, h0r, h0i, weights, nbatch):
    kc, abc, coc, al = weights
    rep_c, same_g, rep_p, g_in, g_out = _s5_expanders()
    ns, nb = u3.shape[0], u3.shape[1]
    nchunk = nb // nbatch
    gp = al.shape[1]
    lb = gp // ns
    d = S5_L * 128
    cp = pltpu.CompilerParams(dimension_semantics=("parallel",), vmem_limit_bytes=VMEM_LIMIT)
    u_spec = pl.BlockSpec((1, nb, S5_L, 128), lambda j: (j, 0, 0, 0))
    st_spec = pl.BlockSpec((nb, lb), lambda j: (0, j))
    slab = lambda a: pl.BlockSpec((1,) + a.shape[1:], lambda j: (j,) + (0,) * (a.ndim - 1))
    const = lambda a: pl.BlockSpec(a.shape, lambda j: (0,) * a.ndim)
    pre, pim = pl.pallas_call(
        _s5_local_kernel,
        grid=(ns,),
        in_specs=[u_spec, slab(abc), const(rep_p), const(g_in)],
        out_specs=[st_spec] * 2,
        out_shape=[jax.ShapeDtypeStruct((nb, gp), F32)] * 2,
        scratch_shapes=[pltpu.VMEM((d, 2 * lb), BF16)],
        compiler_params=cp,
        name="s5_local",
    )(u3, abc, rep_p, g_in)

    ls = 2 * lb
    seq3 = pl.BlockSpec((nbatch, nchunk, ls), lambda j: (0, 0, j))
    row = lambda r: pl.BlockSpec((r, ls), lambda j: (0, j))
    hsr, hsi, hfr, hfi = pl.pallas_call(
        _s5_scan_kernel,
        grid=(gp // ls,),
        in_specs=[seq3, seq3, row(2), row(nbatch), row(nbatch)],
        out_specs=[seq3, seq3, row(nbatch), row(nbatch)],
        out_shape=[jax.ShapeDtypeStruct((nbatch, nchunk, gp), F32)] * 2 + [jax.ShapeDtypeStruct((nbatch, gp), F32)] * 2,
        compiler_params=cp,
        name="s5_scan",
    )(pre.reshape(nbatch, nchunk, gp), pim.reshape(nbatch, nchunk, gp), al, h0r, h0i)

    y3 = pl.pallas_call(
        _s5_out_kernel,
        grid=(ns,),
        in_specs=[u_spec, slab(kc), slab(coc), const(rep_c), const(same_g), const(g_out), st_spec, st_spec],
        out_specs=u_spec,
        out_shape=jax.ShapeDtypeStruct(u3.shape, F32),
        scratch_shapes=[pltpu.VMEM((d, d), BF16), pltpu.VMEM((2 * lb, d), BF16)],
        compiler_params=cp,
        name="s5_out",
    )(u3, kc, coc, rep_c, same_g, g_out, hsr.reshape(nb, gp), hsi.reshape(nb, gp))
    return y3, hfr, hfi


def _merge_kernel(o_ref, y_ref, wa_ref, wg1_ref, wg2_ref, sga_ref, sgb_ref, m_ref, gy_ref):
    @pl.when(pl.program_id(1) == 0)
    def _():
        y = jnp.concatenate([y_ref[s] for s in range(y_ref.shape[0])], axis=1)
        gy_ref[...] = _gelu(y).astype(BF16)

    ya = _dot(o_ref[...], wa_ref[...])
    gy = gy_ref[...]
    yb = _dot(gy, wg1_ref[...]) * jax.nn.sigmoid(_dot(gy, wg2_ref[...]))
    m = sga_ref[...].astype(F32) * ya + sgb_ref[...].astype(F32) * yb
    m_ref[...] = m.astype(BF16)


def _merge(o, y3, w_attn_out, w_glu, sg, tm, tn=1024):
    rows, d = o.shape[0], w_attn_out.shape[1]
    nj = d // tn
    kq, ks = w_attn_out.shape[0], w_glu.shape[0]
    ns = y3.shape[0]
    return pl.pallas_call(
        _merge_kernel,
        grid=(rows // tm, nj),
        in_specs=[
            pl.BlockSpec((tm, kq), lambda i, j: (i, 0)),
            pl.BlockSpec((ns, tm, 128), lambda i, j: (0, i, 0)),
            pl.BlockSpec((kq, tn), lambda i, j: (0, j)),
            pl.BlockSpec((ks, tn), lambda i, j: (0, j)),
            pl.BlockSpec((ks, tn), lambda i, j: (0, j + nj)),
            pl.BlockSpec((tm, tn), lambda i, j: (i, j)),
            pl.BlockSpec((tm, tn), lambda i, j: (i, j + nj)),
        ],
        out_specs=pl.BlockSpec((tm, tn), lambda i, j: (i, j)),
        out_shape=jax.ShapeDtypeStruct((rows, d), BF16),
        scratch_shapes=[pltpu.VMEM((tm, ks), BF16)],
        compiler_params=pltpu.CompilerParams(
            dimension_semantics=("parallel", "arbitrary"), vmem_limit_bytes=VMEM_LIMIT),
        name="merge",
    )(o, y3, w_attn_out, w_glu, w_glu, sg, sg)


def _outproj_kernel(m_ref, x_ref, w_ref, g_ref, x1_ref, h_ref):
    x1 = x_ref[...] + _dot(m_ref[...], w_ref[...])
    x1_ref[...] = x1
    r = lax.rsqrt(jnp.mean(x1 * x1, axis=-1, keepdims=True) + EPS)
    h_ref[...] = (x1 * r * g_ref[...]).astype(BF16)


def _outproj(m, x, w_out, g_ffn, tm):
    rows, d = x.shape
    row = pl.BlockSpec((tm, d), lambda i: (i, 0))
    return pl.pallas_call(
        _outproj_kernel,
        grid=(rows // tm,),
        in_specs=[row, row, pl.BlockSpec((d, d), lambda i: (0, 0), pipeline_mode=pl.Buffered(1)),
                  pl.BlockSpec((1, d), lambda i: (0, 0))],
        out_specs=[row, row],
        out_shape=[jax.ShapeDtypeStruct((rows, d), F32), jax.ShapeDtypeStruct((rows, d), BF16)],
        compiler_params=pltpu.CompilerParams(dimension_semantics=("parallel",), vmem_limit_bytes=VMEM_LIMIT),
        name="out_proj",
    )(m, x, w_out, g_ffn.reshape(1, d))


def _ffn_kernel(h_ref, x1_ref, wg_ref, wu_ref, wd_ref, gf_ref, y_ref, *rest):
    acc_ref = rest[-1]
    f = pl.program_id(1)

    @pl.when(f == 0)
    def _():
        acc_ref[...] = jnp.zeros_like(acc_ref)

    wg, wu, wd = wg_ref[...].astype(BF16), wu_ref[...].astype(BF16), wd_ref[...].astype(BF16)
    for copy_ref, w in zip(rest[:-1], (wg, wu, wd)):
        copy_ref[...] = w
    h = h_ref[...]
    a = _dot(h, wg)
    a = (a * jax.nn.sigmoid(a)) * _dot(h, wu)
    acc_ref[...] += _dot(a.astype(BF16), wd)

    @pl.when(f == pl.num_programs(1) - 1)
    def _():
        y = x1_ref[...] + acc_ref[...]
        r = lax.rsqrt(jnp.mean(y * y, axis=-1, keepdims=True) + EPS)
        y_ref[...] = y * r * gf_ref[...]


def _ffn(h, x1, w_gate, w_up, w_down, g_final, tm, tf=512):
    rows, d = x1.shape
    dff = w_gate.shape[1]
    emit = w_gate.dtype != BF16
    assert not emit or rows == tm
    row = pl.BlockSpec((tm, d), lambda i, f: (i, 0))
    w_specs = [pl.BlockSpec((d, tf), lambda i, f: (0, f)),
               pl.BlockSpec((d, tf), lambda i, f: (0, f)),
               pl.BlockSpec((tf, d), lambda i, f: (f, 0))]
    y_shape = jax.ShapeDtypeStruct((rows, d), F32)
    copies = [jax.ShapeDtypeStruct(w.shape, BF16) for w in (w_gate, w_up, w_down)] if emit else []
    out = pl.pallas_call(
        _ffn_kernel,
        grid=(rows // tm, dff // tf),
        in_specs=[row, row] + w_specs + [pl.BlockSpec((1, d), lambda i, f: (0, 0))],
        out_specs=[row] + (w_specs if emit else []),
        out_shape=[y_shape] + copies,
        scratch_shapes=[pltpu.VMEM((tm, d), F32)],
        compiler_params=pltpu.CompilerParams(
            dimension_semantics=("parallel", "arbitrary"), vmem_limit_bytes=VMEM_LIMIT),
        name="ffn",
    )(h, x1, w_gate, w_up, w_down, g_final.reshape(1, d))
    return tuple(out) if emit else out[0]


def _alibi_slopes():
    return jnp.exp2(-8.0 * jnp.arange(1, N_HEADS + 1, dtype=F32) / N_HEADS) * LOG2E


def _slope_table(slopes):
    s1 = slopes.astype(BF16)
    r1 = slopes - s1.astype(F32)
    s2 = r1.astype(BF16)
    s3 = (r1 - s2.astype(F32)).astype(BF16)
    tab = jnp.stack([s1, s1, s2, s2, s3, s3] + [jnp.zeros_like(s1)] * 10, axis=-1)
    return tab.reshape(N_KV, GROUP, 16)


def _position_rows(pos):
    pos = np.arange(pos) if np.isscalar(pos) else np.asarray(pos)
    hi, lo = (pos // 64) * 64, pos % 64
    assert pos.max() < 64 * 256
    rows = np.stack([hi, lo, hi, lo, hi, lo] + [np.zeros_like(pos)] * 10).astype(np.float32)
    return jnp.asarray(rows, dtype=BF16)


def _layer_weights(p, l):
    w_in = _pack_w_in(p['w_in'][l], p['ssm_d'].shape[-1])
    s5 = _s5_weights(*(p[n][l] for n in ('ssm_lam_re', 'ssm_lam_im', 'ssm_log_dt', 'ssm_b_re', 'ssm_b_im',
                                          'ssm_c_re', 'ssm_c_im', 'ssm_d')))
    cast = lambda n: p[n][l].astype(BF16)
    return dict(
        g_mix=p['g_mix'][l], w_in=w_in,
        pool=p['w_cmp_pool'][l], pe=p['w_cmp_pe'][l], phi=p['w_cmp_phi'][l],
        pool_mats=_pool_matrices(p['w_cmp_pool'][l], PROMPT_PAGES_PER_STEP),
        w_attn_out=cast('w_attn_out'), w_glu=cast('w_glu'), w_out=cast('w_out'), g_ffn=p['g_ffn'][l],
        ffn_w=tuple(p[n][l] for n in ('w_gate', 'w_up', 'w_down')), s5=s5)


def _tail(x, o, y3, sg, lw, g_final, tm):
    rows = x.shape[0]
    m = _merge(o, y3.reshape(y3.shape[0], rows, 128), lw['w_attn_out'], lw['w_glu'], sg, tm)
    x1, h = _outproj(m, x, lw['w_out'], lw['g_ffn'], tm)
    return _ffn(h, x1, *lw['ffn_w'], g_final, tm)


def _feature_major_rows(kv_t, t0):
    b, f, t = kv_t.shape
    return kv_t[:, :, t0:].reshape(b, f // KV_COLS, N_KV, HEAD_DIM, t - t0).transpose(0, 4, 1, 2, 3)


def _layer_prompt(x, lw, slopes, g_out):
    b, t, d = x.shape
    rows = b * t
    assert t % (PROMPT_PAGES_PER_STEP * PAGE) == 0
    q, u3, sg, gt, kvc_t, kvw_t = _in_proj(x.reshape(rows, d), lw['g_mix'], lw['w_in'], 1024, seq_len=t)

    pooled = _pool_pages(kvc_t, lambda bi, pg: (bi, 0, pg), (), b, t // PAGE, lw['pool_mats'],
                         PROMPT_PAGES_PER_STEP)
    _, vc_t, kc_rows = _cmp_finish(pooled, lw['pool'], lw['pe'], lw['phi'])
    o = _attn_prompt(slopes, q, kc_rows, vc_t, kvc_t, kvw_t, gt, b, t)

    gp = lw['s5'][3].shape[1]
    h0 = jnp.zeros((b, gp), F32)
    y3, hr, hi = _s5(u3, h0, h0, lw['s5'], b)

    y = _tail(x.reshape(rows, d), o, y3, sg, lw, g_out, 512).reshape(b, t, d)
    n_win = min(WINDOW, t)
    ng = gp // SSM_STATE
    return (y, _feature_major_rows(kvc_t, 0), _feature_major_rows(kvw_t, t - n_win),
            hr.reshape(b, ng, SSM_STATE), hi.reshape(b, ng, SSM_STATE))


def _layer_sample(x, cache, page_table, win_buf, h_re, h_im, lw, slopes, g_out):
    b, t, d = x.shape
    rows = b * t
    npages = page_table.shape[1]
    past = npages * cache.shape[1]
    n_buf = win_buf.shape[1]
    assert cache.shape[1] == PAGE and rows % S5_L == 0 and t == S5_L
    assert past % CMP_STRIDE == 0 and t < CMP_STRIDE and past % SEL_BLOCK == 0 and t <= SEL_BLOCK
    npg = DECODE_PAGES_PER_STEP
    assert npages % npg == 0
    q, u3, sg, gt, kvc, kvw = _in_proj(x.reshape(rows, d), lw['g_mix'], lw['w_in'], rows)

    cache_t = cache.transpose(0, 2, 3, 4, 1).reshape(cache.shape[0], 4 * KV_COLS, PAGE)
    win_t = win_buf.transpose(0, 2, 3, 4, 1).reshape(b, 2 * KV_COLS, n_buf)
    pooled = _pool_pages(cache_t, lambda bi, pg, pt: (pt[bi, pg], 0, 0), (page_table,), b, npages,
                         lw['pool_mats'], npg)
    kc_t, vc_t, _ = _cmp_finish(pooled, lw['pool'], lw['pe'], lw['phi'])

    nrows = N_HEADS * t
    eye = jnp.eye(N_KV, dtype=BF16)
    q5 = q.reshape(b, t, N_KV, GROUP, HEAD_DIM).transpose(0, 2, 3, 1, 4)
    qbd = (q5[:, :, :, :, None, :] * eye[None, :, None, None, :, None]).reshape(b, nrows, KV_COLS)
    maskbd = jnp.repeat(jnp.repeat(jnp.eye(N_KV, dtype=F32), GROUP * t, axis=0), HEAD_DIM, axis=1)
    slopecol = jnp.repeat(slopes, t).reshape(nrows, 1)
    g3 = gt.reshape(b, t, N_KV, 128)[..., :3 * GROUP].reshape(b, t, N_KV, 3, GROUP)
    g3 = g3.transpose(0, 2, 4, 1, 3).reshape(b, nrows, 3)
    gates_rows = jnp.pad(g3, ((0, 0), (0, 0), (0, 1)))
    n_blk = -(-(past + t) // SEL_BLOCK)

    opart, sel_t = _attn_dec1(qbd, kc_t, vc_t, win_t, kvw.reshape(b, t, 2 * KV_COLS), gates_rows, slopecol, maskbd,
                              past, t, n_blk)
    bps = npg * PAGE // SEL_BLOCK
    nstep = npages // npg
    nb_all = (nstep + 1) * bps
    sel = jnp.pad(sel_t[:, :n_blk], ((0, 0), (0, nb_all - n_blk), (0, 0)))
    sel = sel.reshape(b, nstep + 1, bps, N_KV, 1, t).transpose(0, 1, 3, 4, 5, 2)
    selrows = jnp.broadcast_to(sel, (b, nstep + 1, N_KV, GROUP, t, bps)).reshape(b, nstep + 1, nrows, bps)
    knew = jnp.pad(kvc.reshape(b, t, 4 * KV_COLS)[:, :, 2 * KV_COLS:], ((0, 0), (0, PAGE - t), (0, 0)))
    o_rows = _attn_dec2(cache_t, page_table, qbd, selrows.astype(BF16), slopecol, knew, opart, gates_rows,
                        maskbd, past, t, npg)
    o = (o_rows.reshape(b, N_KV, GROUP, t, HEAD_DIM).transpose(0, 3, 1, 2, 4).reshape(rows, Q_COLS).astype(BF16))

    gp = h_re.shape[1] * h_re.shape[2]
    y3, hr, hi = _s5(u3, h_re.reshape(b, gp), h_im.reshape(b, gp), lw['s5'], b)

    y, *ffn_w_bf16 = _tail(x.reshape(rows, d), o, y3, sg, lw, g_out, rows)
    y = y.reshape(b, t, d)
    win_all = jnp.concatenate([win_buf, kvw.reshape(b, t, 2, N_KV, HEAD_DIM)], axis=1)
    n_keep = min(WINDOW, n_buf + t)
    return (y, kvc.reshape(b, t, 4, N_KV, HEAD_DIM), win_all[:, n_buf + t - n_keep:], hr.reshape(h_re.shape),
            hi.reshape(h_im.shape), tuple(ffn_w_bf16))


def kernel(x_prompt, x_sample, cache_kv, state_win, state_ssm_re, state_ssm_im, page_table, g_mix, w_in, w_cmp_pe, w_cmp_pool, w_cmp_phi, w_attn_out, ssm_lam_re, ssm_lam_im, ssm_log_dt, ssm_b_re, ssm_b_im, ssm_c_re, ssm_c_im, ssm_d, w_glu, w_out, g_ffn, w_gate, w_up, w_down, g_final):
    depth = g_mix.shape[0]
    assert depth == 1, "final norm is fused into the last layer's FFN; one layer supported"
    params = dict(g_mix=g_mix, w_in=w_in, w_cmp_pe=w_cmp_pe, w_cmp_pool=w_cmp_pool, w_cmp_phi=w_cmp_phi,
                  w_attn_out=w_attn_out, ssm_lam_re=ssm_lam_re, ssm_lam_im=ssm_lam_im, ssm_log_dt=ssm_log_dt,
                  ssm_b_re=ssm_b_re, ssm_b_im=ssm_b_im, ssm_c_re=ssm_c_re, ssm_c_im=ssm_c_im, ssm_d=ssm_d,
                  w_glu=w_glu, w_out=w_out, g_ffn=g_ffn, w_gate=w_gate, w_up=w_up, w_down=w_down)
    slopes = _alibi_slopes()
    outs = [[] for _ in range(8)]
    xp, xs = x_prompt, x_sample
    for l in range(depth):
        lw = _layer_weights(params, l)
        xs, kvs, wins, hrs, his, ffn_w = _layer_sample(xs, cache_kv[l], page_table, state_win[l], state_ssm_re[l],
                                                       state_ssm_im[l], lw, slopes, g_final)
        xp, kvp, winp, hrp, hip = _layer_prompt(xp, dict(lw, ffn_w=ffn_w), slopes, g_final)
        for lst, v in zip(outs, (kvp, winp, hrp, hip, kvs, wins, hrs, his)):
            lst.append(v)
    st = [jnp.stack(v) for v in outs]
    return (xp, xs, st[0], st[1], st[2], st[3], st[4], st[5], st[6], st[7])
```

```python
import functools
import math

import numpy as np
import jax
import jax.numpy as jnp
from jax import lax
from jax.experimental import pallas as pl
from jax.experimental.pallas import tpu as pltpu

F32 = jnp.float32
BF16 = jnp.bfloat16

N_HEADS = 16
HEAD_DIM = 64
N_KV = 4
GROUP = N_HEADS // N_KV
CMP_LEN = 32
CMP_STRIDE = 16
SEL_BLOCK = 64
SEL_TOP = 16
WINDOW = 512
SSM_GROUP = 16
SSM_STATE = 64
EPS = 1e-6
NEG = -1e30
FORCE_SCORE = 1e4

KV_COLS = N_KV * HEAD_DIM
Q_COLS = N_HEADS * HEAD_DIM
PAGE = 128
CMP_PER_PAGE = PAGE // CMP_STRIDE
PROMPT_PAGES_PER_STEP = 16
DECODE_PAGES_PER_STEP = 32
S5_L = 8
S5_SLAB = 128 // SSM_GROUP
VMEM_LIMIT = 56 * 1024 * 1024
LOG2E = 1.0 / math.log(2.0)
Q_SCALE = HEAD_DIM ** -0.5 * LOG2E


def _dot(a, b):
    return jnp.dot(a, b, preferred_element_type=F32)


def _dot_nt(a, b):
    return lax.dot_general(a, b, (((1,), (1,)), ((), ())), preferred_element_type=F32)


def _split(a):
    hi = a.astype(BF16)
    lo = (a - hi.astype(F32)).astype(BF16)
    return hi, lo


def _dot3(a, b):
    ah, al = _split(a)
    bh, bl = _split(b)
    return _dot(ah, bh) + _dot(ah, bl) + _dot(al, bh)


def _dot_nt_hl(w_bf16, a):
    ah, al = _split(a)
    return _dot_nt(w_bf16, ah) + _dot_nt(w_bf16, al)


def _gelu(x):
    return 0.5 * x * (1.0 + jnp.tanh(math.sqrt(2.0 / math.pi) * (x + 0.044715 * (x * x * x))))


def _iota(shape, dim):
    return lax.broadcasted_iota(jnp.int32, shape, dim)


def _log2(n):
    assert n > 0 and n & (n - 1) == 0, n
    return n.bit_length() - 1


def _div_pow2(x, n):
    return jnp.right_shift(x, _log2(n))


def _mod_pow2(x, n):
    return jnp.bitwise_and(x, (1 << _log2(n)) - 1)


def _softmax2_rows(s):
    m = jnp.max(s, axis=-1, keepdims=True)
    e = jnp.exp2(s - m)
    return e / jnp.sum(e, axis=-1, keepdims=True)


IN_TN = 512
_Q_T, _U_T, _MG_T, _KV_T = 2, 2, 8, 3
_KVC_T = 2
_U_0 = _Q_T
_MG_0 = _U_0 + _U_T
_GT_0 = _MG_0 + _MG_T
_KV_0 = _GT_0 + 1
IN_TILES = _KV_0 + _KV_T


def _inproj_kernel(x_ref, g_ref, wq_ref, wu_ref, wmg_ref, wgt_ref, wkv_ref, q_ref, u_ref, sg_ref, gt_ref, kvc_ref,
                   kvw_ref, xn_ref, *, kv_feature_major):
    j = pl.program_id(1)

    @pl.when(j == 0)
    def _():
        x = x_ref[...]
        r = lax.rsqrt(jnp.mean(x * x, axis=-1, keepdims=True) + EPS)
        xn_ref[...] = (x * r * g_ref[...]).astype(BF16)

    @pl.when(j < _U_0)
    def _():
        q_ref[...] = (_dot(xn_ref[...], wq_ref[...]) * Q_SCALE).astype(BF16)

    @pl.when((j >= _U_0) & (j < _MG_0))
    def _():
        z = _dot(xn_ref[...], wu_ref[...])
        for s in range(IN_TN // 128):
            u_ref[s] = z[:, 128 * s:128 * (s + 1)].reshape(u_ref.shape[1:])

    @pl.when((j >= _MG_0) & (j < _GT_0))
    def _():
        sg_ref[...] = jax.nn.sigmoid(_dot(xn_ref[...], wmg_ref[...])).astype(BF16)

    @pl.when(j == _GT_0)
    def _():
        gt_ref[...] = jax.nn.sigmoid(_dot(xn_ref[...], wgt_ref[...]))

    @pl.when(j >= _KV_0)
    def _():
        if kv_feature_major:
            z = _dot_nt(wkv_ref[...], xn_ref[...])
        else:
            z = _dot(xn_ref[...], wkv_ref[...])

        @pl.when(j < _KV_0 + _KVC_T)
        def _():
            kvc_ref[...] = z.reshape(kvc_ref.shape)

        @pl.when(j >= _KV_0 + _KVC_T)
        def _():
            kvw_ref[...] = z.reshape(kvw_ref.shape)


def _in_proj(x, g_mix, w, tm, seq_len=None):
    rows, d = x.shape
    feature_major = seq_len is not None

    def col(lo, n):
        return lambda i, j: (i, jnp.clip(j - lo, 0, n - 1))

    if feature_major:
        tiles_per_seq = seq_len // tm
        kvc_spec = pl.BlockSpec((1, IN_TN, tm), lambda i, j: (i // tiles_per_seq, jnp.clip(j - _KV_0, 0, _KVC_T - 1),
                                                              i % tiles_per_seq))
        kvw_spec = pl.BlockSpec((1, IN_TN, tm), lambda i, j: (i // tiles_per_seq, 0, i % tiles_per_seq))
        kvc_shape = jax.ShapeDtypeStruct((rows // seq_len, _KVC_T * IN_TN, seq_len), F32)
        kvw_shape = jax.ShapeDtypeStruct((rows // seq_len, IN_TN, seq_len), F32)
        wkv, wkv_spec = w['kv_t'], pl.BlockSpec((IN_TN, d), lambda i, j: (jnp.clip(j - _KV_0, 0, _KV_T - 1), 0))
    else:
        kvc_spec = pl.BlockSpec((tm, IN_TN), col(_KV_0, _KVC_T))
        kvw_spec = pl.BlockSpec((tm, IN_TN), lambda i, j: (i, 0))
        kvc_shape = jax.ShapeDtypeStruct((rows, _KVC_T * IN_TN), F32)
        kvw_shape = jax.ShapeDtypeStruct((rows, IN_TN), F32)
        wkv, wkv_spec = w['kv'], pl.BlockSpec((d, IN_TN), lambda i, j: (0, jnp.clip(j - _KV_0, 0, _KV_T - 1)))
    n_slab = IN_TN // 128
    return pl.pallas_call(
        functools.partial(_inproj_kernel, kv_feature_major=feature_major),
        grid=(rows // tm, IN_TILES),
        in_specs=[
            pl.BlockSpec((tm, d), lambda i, j: (i, 0), pipeline_mode=pl.Buffered(1)),
            pl.BlockSpec((1, d), lambda i, j: (0, 0)),
            pl.BlockSpec((d, IN_TN), lambda i, j: (0, jnp.clip(j, 0, _Q_T - 1))),
            pl.BlockSpec((d, IN_TN), lambda i, j: (0, jnp.clip(j - _U_0, 0, _U_T - 1))),
            pl.BlockSpec((d, IN_TN), lambda i, j: (0, jnp.clip(j - _MG_0, 0, _MG_T - 1))),
            pl.BlockSpec((d, IN_TN), lambda i, j: (0, 0), pipeline_mode=pl.Buffered(1)),
            wkv_spec,
        ],
        out_specs=[
            pl.BlockSpec((tm, IN_TN), col(0, _Q_T)),
            pl.BlockSpec((n_slab, tm // S5_L, S5_L, 128), lambda i, j: (jnp.clip(j - _U_0, 0, _U_T - 1), i, 0, 0)),
            pl.BlockSpec((tm, IN_TN), col(_MG_0, _MG_T)),
            pl.BlockSpec((tm, IN_TN), lambda i, j: (i, 0)),
            kvc_spec, kvw_spec,
        ],
        out_shape=[
            jax.ShapeDtypeStruct((rows, _Q_T * IN_TN), BF16),
            jax.ShapeDtypeStruct((_U_T * n_slab, rows // S5_L, S5_L, 128), F32),
            jax.ShapeDtypeStruct((rows, _MG_T * IN_TN), BF16),
            jax.ShapeDtypeStruct((rows, IN_TN), F32),
            kvc_shape, kvw_shape,
        ],
        scratch_shapes=[pltpu.VMEM((tm, d), BF16)],
        compiler_params=pltpu.CompilerParams(
            dimension_semantics=("parallel", "arbitrary"), vmem_limit_bytes=VMEM_LIMIT),
        name="in_proj",
    )(x, g_mix.reshape(1, d), w['q'], w['u'], w['mg'], w['gt'], wkv)


def _pack_w_in(w_in, ssm_width):
    cuts = np.cumsum([Q_COLS, 6 * KV_COLS, 3 * N_HEADS, ssm_width]).tolist()
    wq, wkv, wgt, wu, wmg = jnp.split(w_in, cuts, axis=1)
    assert wq.shape[1] == _Q_T * IN_TN and wkv.shape[1] == _KV_T * IN_TN
    assert wu.shape[1] == _U_T * IN_TN and wmg.shape[1] == _MG_T * IN_TN and IN_TN == N_KV * 128
    d = w_in.shape[0]
    wgt = wgt.reshape(d, 3, N_KV, GROUP).transpose(0, 2, 1, 3).reshape(d, N_KV, 3 * GROUP)
    wgt = jnp.pad(wgt, ((0, 0), (0, 0), (0, 128 - 3 * GROUP))).reshape(d, IN_TN)
    return dict(q=wq.astype(BF16), u=wu.astype(BF16), mg=wmg.astype(BF16), gt=wgt.astype(BF16),
                kv=wkv.astype(BF16), kv_t=wkv.T.astype(BF16))


def _pool_kernel(*refs, npg, n_tables):
    refs = refs[n_tables:]
    prev_ref = refs[0]
    page_refs = refs[1:npg + 1]
    mk_ref, mv_ref, mprev_ref, out_ref = refs[npg + 1:]
    pages = [prev_ref[0].astype(BF16)] + [page_refs[j][0].astype(BF16) for j in range(npg)]
    mpg = mk_ref.shape[0] // PAGE
    nblk = mk_ref.shape[1]
    for seg in range(npg // mpg):
        x = jnp.concatenate(pages[1 + seg * mpg:1 + (seg + 1) * mpg], axis=1)
        for half, m_ref in ((0, mk_ref), (1, mv_ref)):
            rows = slice(half * KV_COLS, (half + 1) * KV_COLS)
            head = _dot(pages[seg * mpg][rows], mprev_ref[half])
            if seg == 0:
                head = jnp.where(pl.program_id(1) > 0, head, 0.0)
            out_ref[0, rows, seg * nblk:(seg + 1) * nblk] = _dot(x[rows], m_ref[...]) + head


def _pool_matrices(pool, npg):
    nblk = npg * CMP_PER_PAGE
    cols = []
    for r in range(CMP_PER_PAGE):
        start = CMP_STRIDE * (r - 1)
        lo, hi = max(start, 0), min(start + CMP_LEN, PAGE)
        cols.append(jnp.pad(pool[:, lo - start:hi - start], ((0, 0), (lo, PAGE - hi))))
    body = jnp.stack(cols, axis=-1)
    carry = jnp.pad(pool[:, :CMP_STRIDE], ((0, 0), (PAGE - CMP_STRIDE, 0)))
    first = jnp.asarray(np.arange(CMP_PER_PAGE) == 0, F32)
    same = jnp.eye(npg, dtype=F32)[None, :, None, :, None]
    nxt = jnp.eye(npg, k=1, dtype=F32)[None, :, None, :, None]
    m = same * body[:, None, :, None, :] + nxt * (carry[:, None, :, None, None] * first)
    m = m.reshape(2, npg * PAGE, nblk).astype(BF16)
    mprev = (carry[:, :, None] * jnp.asarray(np.arange(nblk) == 0, F32)).astype(BF16)
    return m[0], m[1], mprev


def _pool_pages(pages, page_index, tables, nseq, npages, mats, npg):
    assert npg % (mats[0].shape[0] // PAGE) == 0
    nstep = npages // npg
    nblk = npg * CMP_PER_PAGE
    mk, mv, mprev = mats

    def spec(off):
        return pl.BlockSpec((1, 2 * KV_COLS, PAGE),
                            lambda b, s, *t: page_index(b, jnp.maximum(s * npg + off, 0), *t))

    const = lambda shape: pl.BlockSpec(shape, lambda b, s, *t: (0,) * len(shape))
    return pl.pallas_call(
        functools.partial(_pool_kernel, npg=npg, n_tables=len(tables)),
        grid_spec=pltpu.PrefetchScalarGridSpec(
            num_scalar_prefetch=len(tables),
            grid=(nseq, nstep),
            in_specs=[spec(j) for j in range(-1, npg)] + [const(mk.shape), const(mv.shape), const(mprev.shape)],
            out_specs=pl.BlockSpec((1, 2 * KV_COLS, nblk), lambda b, s, *t: (b, 0, s)),
        ),
        out_shape=jax.ShapeDtypeStruct((nseq, 2 * KV_COLS, npages * CMP_PER_PAGE), F32),
        compiler_params=pltpu.CompilerParams(
            dimension_semantics=("parallel", "arbitrary"), vmem_limit_bytes=VMEM_LIMIT),
        name="cmp_pool",
    )(*tables, *([pages] * (npg + 1)), mk, mv, mprev)


def _cmp_finish_kernel(pooled_ref, poolw_ref, pe_ref, phi_ref, kc_ref, vc_ref, kcr_ref):
    bias = jnp.sum(poolw_ref[...] * pe_ref[...], axis=1, keepdims=True)
    x = _gelu(pooled_ref[0] + bias)
    kc = _dot3(phi_ref[0], x[:KV_COLS])
    kc_ref[0] = kc.astype(BF16)
    kcr_ref[0] = kc.T.astype(BF16)
    vc_ref[0] = _dot3(phi_ref[1], x[KV_COLS:]).astype(BF16)


def _cmp_finish(pooled, pool, pe, phi):
    nseq, _, n = pooled.shape
    poolw = jnp.concatenate([jnp.broadcast_to(pool[s][None, :], (KV_COLS, CMP_LEN)) for s in range(2)], axis=0)
    pe_t = jnp.concatenate([jnp.tile(pe[s].T, (N_KV, 1)) for s in range(2)], axis=0)
    phi_bd = jnp.stack([jnp.kron(jnp.eye(N_KV, dtype=F32), phi[s].T) for s in range(2)])
    spec = pl.BlockSpec((1, KV_COLS, n), lambda b: (b, 0, 0))
    return pl.pallas_call(
        _cmp_finish_kernel,
        grid=(nseq,),
        in_specs=[
            pl.BlockSpec((1, 2 * KV_COLS, n), lambda b: (b, 0, 0)),
            pl.BlockSpec((2 * KV_COLS, CMP_LEN), lambda b: (0, 0)),
            pl.BlockSpec((2 * KV_COLS, CMP_LEN), lambda b: (0, 0)),
            pl.BlockSpec((2, KV_COLS, KV_COLS), lambda b: (0, 0, 0)),
        ],
        out_specs=[spec, spec, pl.BlockSpec((1, n, KV_COLS), lambda b: (b, 0, 0))],
        out_shape=[jax.ShapeDtypeStruct((nseq, KV_COLS, n), BF16)] * 2
        + [jax.ShapeDtypeStruct((nseq, n, KV_COLS), BF16)],
        compiler_params=pltpu.CompilerParams(dimension_semantics=("parallel",)),
        name="cmp_finish",
    )(pooled, poolw, pe_t, phi_bd)


ATT_TQ = 256
ATT_TK = 256
NBLK_PAD = 128
POS_ROWS = 16
AUG = HEAD_DIM + POS_ROWS


def _attn_prompt_kernel(q_ref, kca_ref, vc_ref, ks_ref, vs_ref, kw_ref, vw_ref, gt_ref, ovt_ref, ktab_ref,
                        stab_ref, o_ref, ksa_ref, vsa_ref, kwa_ref, vwa_ref, qat_ref, m_ref, acc_ref,
                        *, tq, tk, n_sel, n_top):
    qi = pl.program_id(2)
    q0 = qi * tq
    seq = ksa_ref.shape[0]

    @pl.when(qi == 0)
    def _():
        zero = jnp.zeros((128 - HEAD_DIM, seq), F32)
        ones = jnp.where(_iota((POS_ROWS, seq), 0) == 0, 1.0, 0.0).astype(BF16)
        for src, dst, lo in ((ks_ref, ksa_ref, NBLK_PAD), (kw_ref, kwa_ref, 0)):
            k_rows = jnp.concatenate([src[0], zero], axis=0).T[:, :HEAD_DIM].astype(BF16)
            dst[:, lo:] = jnp.concatenate([k_rows, ktab_ref[:, NBLK_PAD:]], axis=1)
        ksa_ref[:, 0:NBLK_PAD] = ktab_ref[:, 0:NBLK_PAD]
        for src, dst in ((vs_ref, vsa_ref), (vw_ref, vwa_ref)):
            dst[0:HEAD_DIM] = src[0].astype(BF16)
            dst[HEAD_DIM:] = ones

    q_t = q_ref[...].astype(F32).T
    q_t = jnp.concatenate([q_t[n * HEAD_DIM:(n + 1) * HEAD_DIM] for n in range(GROUP)], axis=1).astype(BF16)
    stab = stab_ref[0]
    sp_t = jnp.concatenate([jnp.broadcast_to(stab[:, n:n + 1], (POS_ROWS, tq)) for n in range(GROUP)], axis=1)
    qat_ref[NBLK_PAD:NBLK_PAD + HEAD_DIM] = q_t
    qat_ref[NBLK_PAD + HEAD_DIM:] = sp_t.astype(BF16)
    trow = q0 + _iota((1, tq), 1)
    lanes = [slice(n * tq, (n + 1) * tq) for n in range(GROUP)]

    ncmp = kca_ref.shape[2]
    nidx = _iota((ncmp, 1), 0)
    heads = lambda a: jnp.concatenate([a] * GROUP, axis=1)
    valid = (nidx >= 1) & (CMP_STRIDE * nidx + (CMP_STRIDE - 1) <= trow)
    valid = heads(jnp.where(valid, 1.0, 0.0)) > 0.5
    s = jnp.where(valid, _dot(kca_ref[0, 0], qat_ref[NBLK_PAD:]), NEG)
    e = jnp.exp2(s - jnp.max(s, axis=0, keepdims=True))
    pc = jnp.where(valid, e / jnp.sum(e, axis=0, keepdims=True), 0.0)
    o_c = _dot(vc_ref[0], pc.astype(BF16))

    ph, plo = _split((pc[:, lanes[0]] + pc[:, lanes[1]]) + (pc[:, lanes[2]] + pc[:, lanes[3]]))
    imp = (_dot(ovt_ref[...], ph) + _dot(ovt_ref[...], plo))[0:n_sel]
    blk = _iota((n_sel, 1), 0)
    cur = _div_pow2(q0 + _iota((1, tq), 1), SEL_BLOCK)
    forced = (blk == 0) | (blk == cur) | (blk == cur - 1)
    imp = jnp.where(forced, FORCE_SCORE, imp)
    imp = jnp.where(blk <= cur, imp, -1.0)
    cnt = jnp.zeros((n_sel, tq), F32)
    for j in range(n_sel):
        vj = imp[j:j + 1, :]
        tie = jnp.where(blk > j, 1.0, 0.0)
        cnt = cnt + jnp.where(vj > imp, 1.0, jnp.where(vj == imp, tie, 0.0))
    mask_t = jnp.where((cnt < n_top) & (blk <= cur), 0.0, NEG)
    mask_t = jnp.concatenate([mask_t, jnp.full((NBLK_PAD - n_sel, tq), NEG, F32)], axis=0).astype(BF16)
    qat_ref[0:NBLK_PAD] = heads(mask_t)

    m_ref[...] = jnp.full_like(m_ref, NEG)
    acc_ref[...] = jnp.zeros_like(acc_ref)

    def scores(kt):
        return _dot(ksa_ref[pl.ds(pl.multiple_of(kt * tk, tk), tk), :], qat_ref[...])

    def update(kt, s):
        m_old = m_ref[...]
        m_new = jnp.maximum(m_old, jnp.max(s, axis=0, keepdims=True))
        p = jnp.exp2(s - m_new).astype(BF16)
        pv = _dot(vsa_ref[:, pl.ds(pl.multiple_of(kt * tk, tk), tk)], p)
        acc_ref[...] = jnp.exp2(m_old - m_new) * acc_ref[...] + pv
        m_ref[...] = m_new

    def pair(j, carry):
        s0, s1 = scores(2 * j), scores(2 * j + 1)
        update(2 * j, s0)
        update(2 * j + 1, s1)
        return carry

    lax.fori_loop(0, qi // 2, pair, 0)

    @pl.when(qi % 2 == 1)
    def _():
        update(qi - 1, scores(qi - 1))

    span = WINDOW + tq
    w0 = pl.multiple_of(jnp.maximum(q0 - WINDOW, 0), tq)
    s_diag = scores(qi) + heads(jnp.where(q0 + _iota((tk, 1), 0) <= trow, 0.0, NEG))
    dist = trow - (w0 + _iota((span, 1), 0))
    s_win = (_dot(kwa_ref[pl.ds(w0, span), :], qat_ref[NBLK_PAD:])
             + heads(jnp.where((dist >= 0) & (dist < WINDOW), 0.0, NEG)))
    update(qi, s_diag)
    acc = acc_ref[...]
    o_s = acc[:HEAD_DIM] / acc[HEAD_DIM:HEAD_DIM + 1]
    ow = _dot(vwa_ref[:, pl.ds(w0, span)], jnp.exp2(s_win - jnp.max(s_win, axis=0, keepdims=True)).astype(BF16))
    o_w = ow[:HEAD_DIM] / ow[HEAD_DIM:HEAD_DIM + 1]

    gt_t = gt_ref[...].T
    outs = []
    for n in range(GROUP):
        outs.append(gt_t[n:n + 1] * o_c[:, lanes[n]] + gt_t[GROUP + n:GROUP + n + 1] * o_s[:, lanes[n]]
                    + gt_t[2 * GROUP + n:2 * GROUP + n + 1] * o_w[:, lanes[n]])
    o_ref[...] = jnp.concatenate(outs, axis=0).T.astype(BF16)


def _overlap_t(n_blk_rows, n_cmp):
    s = np.arange(n_blk_rows)[:, None]
    n = np.arange(n_cmp)[None, :]
    r = SEL_BLOCK // CMP_STRIDE
    return jnp.asarray(((n >= r * s) & (n <= r * s + r)).astype(np.float32), dtype=BF16)


def _block_expand(n_blk_rows, n_keys):
    s = np.arange(n_blk_rows)[:, None]
    k = np.arange(n_keys)[None, :]
    return jnp.asarray((k // SEL_BLOCK == s).astype(np.float32), dtype=BF16)


def _attn_prompt(slopes, q, kc_rows, vc_t, kvc_t, kvw_t, gates, nbatch, seq):
    ncmp = kc_rows.shape[1]
    tq, tk = ATT_TQ, ATT_TK
    assert tk == tq and seq % tk == 0 and WINDOW % tq == 0 and WINDOW + tq <= seq
    nq = seq // tq
    n_sel = seq // SEL_BLOCK
    ovt = _overlap_t(NBLK_PAD, ncmp)
    ktab = jnp.concatenate([_block_expand(NBLK_PAD, seq), _position_rows(seq)], axis=0).T
    cmp_end = CMP_STRIDE * np.arange(ncmp) + (CMP_STRIDE - 1)
    kca = jnp.concatenate([kc_rows.reshape(nbatch, ncmp, N_KV, HEAD_DIM).transpose(0, 2, 1, 3),
                           jnp.broadcast_to(_position_rows(cmp_end).T, (nbatch, N_KV, ncmp, POS_ROWS))], axis=-1)
    stab = _slope_table(slopes).astype(F32).transpose(0, 2, 1)
    per_slot = KV_COLS // HEAD_DIM

    def kv_spec(slot):
        return pl.BlockSpec((1, HEAD_DIM, seq), lambda b, g, i: (b, slot * per_slot + g, 0))

    q_spec = pl.BlockSpec((tq, GROUP * HEAD_DIM), lambda b, g, i: (b * nq + i, g))
    return pl.pallas_call(
        functools.partial(_attn_prompt_kernel, tq=tq, tk=tk, n_sel=n_sel, n_top=min(SEL_TOP, n_sel)),
        grid=(nbatch, N_KV, nq),
        in_specs=[
            q_spec,
            pl.BlockSpec((1, 1, ncmp, AUG), lambda b, g, i: (b, g, 0, 0)),
            pl.BlockSpec((1, HEAD_DIM, ncmp), lambda b, g, i: (b, g, 0)),
            kv_spec(2), kv_spec(3), kv_spec(0), kv_spec(1),
            pl.BlockSpec((tq, 128), lambda b, g, i: (b * nq + i, g)),
            pl.BlockSpec((NBLK_PAD, ncmp), lambda b, g, i: (0, 0)),
            pl.BlockSpec((seq, NBLK_PAD + POS_ROWS), lambda b, g, i: (0, 0)),
            pl.BlockSpec((1, POS_ROWS, GROUP), lambda b, g, i: (g, 0, 0)),
        ],
        out_specs=q_spec,
        out_shape=jax.ShapeDtypeStruct(q.shape, BF16),
        scratch_shapes=[
            pltpu.VMEM((seq, NBLK_PAD + AUG), BF16),
            pltpu.VMEM((AUG, seq), BF16),
            pltpu.VMEM((seq, AUG), BF16),
            pltpu.VMEM((AUG, seq), BF16),
            pltpu.VMEM((NBLK_PAD + AUG, GROUP * tq), BF16),
            pltpu.VMEM((1, GROUP * tq), F32),
            pltpu.VMEM((AUG, GROUP * tq), F32),
        ],
        compiler_params=pltpu.CompilerParams(
            dimension_semantics=("parallel", "parallel", "arbitrary"), vmem_limit_bytes=VMEM_LIMIT),
        name="attn_prompt",
    )(q, kca, vc_t, kvc_t, kvc_t, kvw_t, kvw_t, gates, ovt, ktab, stab)


def _diag_blocks(o, maskbd):
    o = o * maskbd
    return (o[:, 0:64] + o[:, 64:128]) + (o[:, 128:192] + o[:, 192:256])


def _attn_dec1_kernel(qbd_ref, kc_ref, vc_ref, win_ref, kvnew_ref, gates_ref, slope_ref, ovt_ref,
                      maskbd_ref, opart_ref, sel_ref, *, past, tdec, n_blk, n_top):
    qb = qbd_ref[0]
    nrows = qb.shape[0]
    tcol = past + _mod_pow2(_iota((nrows, 1), 0), tdec)
    pref = float(past + tdec)
    slope = slope_ref[...]
    maskbd = maskbd_ref[...]

    ncmp = kc_ref.shape[2]
    nrow = _iota((1, ncmp), 1)
    cend = CMP_STRIDE * nrow + (CMP_STRIDE - 1)
    valid = (nrow >= 1) & (cend <= tcol)
    s = _dot(qb, kc_ref[0])
    s = jnp.where(valid, s + slope * (cend.astype(F32) - pref), NEG)
    p = jnp.where(valid, _softmax2_rows(s), 0.0)
    o_c = _diag_blocks(_dot_nt(p.astype(BF16), vc_ref[0]), maskbd)

    per_g = GROUP * tdec
    psum = jnp.concatenate(
        [sum(p[gi * per_g + n * tdec: gi * per_g + (n + 1) * tdec] for n in range(GROUP)) for gi in range(N_KV)],
        axis=0)
    imp = _dot_nt_hl(ovt_ref[...], psum)
    nb_pad = imp.shape[0]
    blk = _iota((nb_pad, 1), 0)
    cur = _div_pow2(past + _mod_pow2(_iota((1, N_KV * tdec), 1), tdec), SEL_BLOCK)
    forced = (blk == 0) | (blk == cur) | (blk == cur - 1)
    imp = jnp.where(forced, FORCE_SCORE, imp)
    imp = jnp.where(blk <= cur, imp, -1.0)
    imp = jnp.where(blk < n_blk, imp, -2.0)
    sel = jnp.zeros(imp.shape, F32)
    for _ in range(n_top):
        mx = jnp.max(imp, axis=0, keepdims=True)
        first = jnp.min(jnp.where(imp == mx, blk, nb_pad), axis=0, keepdims=True)
        pick = blk == first
        sel = jnp.where(pick, 1.0, sel)
        imp = jnp.where(pick, -jnp.inf, imp)
    sel_ref[0] = sel

    nbuf = win_ref.shape[2]
    knew = kvnew_ref[0]
    zpad = jnp.zeros((PAGE - tdec, KV_COLS), F32)
    kn = jnp.concatenate([knew[:, :KV_COLS], zpad], axis=0).astype(BF16)
    vn = jnp.concatenate([knew[:, KV_COLS:], zpad], axis=0).astype(BF16)
    sw = jnp.concatenate([_dot(qb, win_ref[0, :KV_COLS].astype(BF16)), _dot_nt(qb, kn)], axis=1)
    idx = _iota((1, nbuf + PAGE), 1)
    kposw = past - nbuf + idx
    dist = tcol - kposw
    validw = (dist >= 0) & (dist < WINDOW) & (kposw >= 0) & (idx < nbuf + tdec)
    sw = jnp.where(validw, sw + slope * (kposw.astype(F32) - pref), NEG)
    pw = _softmax2_rows(sw).astype(BF16)
    o_w = _dot_nt(pw[:, :nbuf], win_ref[0, KV_COLS:].astype(BF16)) + _dot(pw[:, nbuf:], vn)
    o_w = _diag_blocks(o_w, maskbd)

    gates = gates_ref[0]
    opart_ref[0] = gates[:, 0:1] * o_c + gates[:, 2:3] * o_w


def _attn_dec1(qbd, kc_t, vc_t, win_t, kvnew, gates_rows, slopecol, maskbd, past, tdec, n_blk):
    nseq, nrows, _ = qbd.shape
    ncmp = kc_t.shape[2]
    nb_pad = -(-n_blk // 8) * 8
    ovt = _overlap_t(nb_pad, ncmp)
    nbuf = win_t.shape[2]
    return pl.pallas_call(
        functools.partial(_attn_dec1_kernel, past=past, tdec=tdec, n_blk=n_blk, n_top=min(SEL_TOP, n_blk)),
        grid=(nseq,),
        in_specs=[
            pl.BlockSpec((1, nrows, KV_COLS), lambda b: (b, 0, 0)),
            pl.BlockSpec((1, KV_COLS, ncmp), lambda b: (b, 0, 0)),
            pl.BlockSpec((1, KV_COLS, ncmp), lambda b: (b, 0, 0)),
            pl.BlockSpec((1, 2 * KV_COLS, nbuf), lambda b: (b, 0, 0)),
            pl.BlockSpec((1, tdec, 2 * KV_COLS), lambda b: (b, 0, 0)),
            pl.BlockSpec((1, nrows, 4), lambda b: (b, 0, 0)),
            pl.BlockSpec((nrows, 1), lambda b: (0, 0)),
            pl.BlockSpec((nb_pad, ncmp), lambda b: (0, 0)),
            pl.BlockSpec((nrows, KV_COLS), lambda b: (0, 0)),
        ],
        out_specs=[
            pl.BlockSpec((1, nrows, HEAD_DIM), lambda b: (b, 0, 0)),
            pl.BlockSpec((1, nb_pad, N_KV * tdec), lambda b: (b, 0, 0)),
        ],
        out_shape=[
            jax.ShapeDtypeStruct((nseq, nrows, HEAD_DIM), F32),
            jax.ShapeDtypeStruct((nseq, nb_pad, N_KV * tdec), F32),
        ],
        compiler_params=pltpu.CompilerParams(dimension_semantics=("parallel",), vmem_limit_bytes=VMEM_LIMIT),
        name="attn_dec_select",
    )(qbd, kc_t, vc_t, win_t, kvnew, gates_rows, slopecol, ovt, maskbd)


def _attn_dec2_kernel(pt_ref, *refs, npg, past, tdec):
    page_refs = refs[:npg]
    (qbd_ref, selrow_ref, sellast_ref, e_ref, slope_ref, knew_ref, opart_ref, gates_ref, maskbd_ref,
     o_ref, m_ref, l_ref, acc_ref) = refs[npg:]
    step = pl.program_id(1)
    nkeys = npg * PAGE
    qb = qbd_ref[0]
    nrows = qb.shape[0]
    pref = float(past + tdec)
    slope = slope_ref[...]

    @pl.when(step == 0)
    def _():
        m_ref[...] = jnp.full_like(m_ref, NEG)
        l_ref[...] = jnp.zeros_like(l_ref)
        acc_ref[...] = jnp.zeros_like(acc_ref)

    def update(s, pv_fn):
        m_old = m_ref[...]
        m_new = jnp.maximum(m_old, jnp.max(s, axis=-1, keepdims=True))
        alpha = jnp.exp2(m_old - m_new)
        p = jnp.exp2(s - m_new)
        l_ref[...] = alpha * l_ref[...] + jnp.sum(p, axis=-1, keepdims=True)
        acc_ref[...] = alpha * acc_ref[...] + pv_fn(p.astype(BF16))
        m_ref[...] = m_new

    k_t = jnp.concatenate([page_refs[j][0, :KV_COLS].astype(BF16) for j in range(npg)], axis=1)
    v_t = jnp.concatenate([page_refs[j][0, KV_COLS:].astype(BF16) for j in range(npg)], axis=1)
    kpos = step * nkeys + _iota((1, nkeys), 1)
    chosen = _dot(selrow_ref[0, 0], e_ref[...])
    s = _dot(qb, k_t) + slope * (kpos.astype(F32) - pref) + jnp.where(chosen > 0.5, 0.0, NEG)
    update(s, lambda p: _dot_nt(p, v_t))

    @pl.when(step == pl.num_programs(1) - 1)
    def _():
        tcol = past + _mod_pow2(_iota((nrows, 1), 0), tdec)
        kn = knew_ref[0]
        kposn = past + _iota((1, PAGE), 1)
        sn = _dot_nt(qb, kn[:, :KV_COLS].astype(BF16))
        ok = (sellast_ref[0, 0][:, 0:1].astype(F32) > 0.5) & (kposn <= tcol)
        sn = sn + slope * (kposn.astype(F32) - pref) + jnp.where(ok, 0.0, NEG)
        update(sn, lambda p: _dot(p, kn[:, KV_COLS:].astype(BF16)))
        o_s = _diag_blocks(acc_ref[...] / l_ref[...], maskbd_ref[...])
        o_ref[0] = opart_ref[0] + gates_ref[0][:, 1:2] * o_s


def _attn_dec2(cache_t, page_table, qbd, selrows, slopecol, knew, opart, gates_rows, maskbd, past, tdec, npg):
    nseq, npages = page_table.shape
    nstep = npages // npg
    nrows = qbd.shape[1]
    blk_per_step = npg * PAGE // SEL_BLOCK
    emat = _block_expand(blk_per_step, npg * PAGE)
    page_specs = [
        pl.BlockSpec((1, 2 * KV_COLS, PAGE), lambda b, s, pt, j=j: (pt[b, s * npg + j], 1, 0))
        for j in range(npg)
    ]
    per_seq = lambda shape: pl.BlockSpec((1,) + shape, lambda b, s, pt: (b,) + (0,) * len(shape))
    const = lambda shape: pl.BlockSpec(shape, lambda b, s, pt: (0,) * len(shape))
    return pl.pallas_call(
        functools.partial(_attn_dec2_kernel, npg=npg, past=past, tdec=tdec),
        grid_spec=pltpu.PrefetchScalarGridSpec(
            num_scalar_prefetch=1,
            grid=(nseq, nstep),
            in_specs=page_specs + [
                per_seq((nrows, KV_COLS)),
                pl.BlockSpec((1, 1, nrows, blk_per_step), lambda b, s, pt: (b, s, 0, 0)),
                pl.BlockSpec((1, 1, nrows, blk_per_step), lambda b, s, pt: (b, nstep, 0, 0)),
                const((blk_per_step, npg * PAGE)),
                const((nrows, 1)),
                per_seq((PAGE, 2 * KV_COLS)),
                per_seq((nrows, HEAD_DIM)),
                per_seq((nrows, 4)),
                const((nrows, KV_COLS)),
            ],
            out_specs=per_seq((nrows, HEAD_DIM)),
            scratch_shapes=[
                pltpu.VMEM((nrows, 1), F32),
                pltpu.VMEM((nrows, 1), F32),
                pltpu.VMEM((nrows, KV_COLS), F32),
            ],
        ),
        out_shape=jax.ShapeDtypeStruct((nseq, nrows, HEAD_DIM), F32),
        compiler_params=pltpu.CompilerParams(
            dimension_semantics=("parallel", "arbitrary"), vmem_limit_bytes=VMEM_LIMIT),
        name="attn_dec_selected",
    )(page_table, *([cache_t] * npg), qbd, selrows, selrows, emat, slopecol, knew, opart, gates_rows, maskbd)


def _s5_weights(lam_re, lam_im, log_dt, b_re, b_im, c_re, c_im, d_skip):
    hp = lax.Precision.HIGHEST
    L = S5_L
    ng, p = lam_re.shape
    c = SSM_GROUP
    ns, gs = ng // S5_SLAB, S5_SLAB
    dt = jnp.exp(log_dt)[:, None]
    lr, li = lam_re, lam_im
    mag = jnp.exp(lr * dt)
    ar = mag * jnp.cos(li * dt)
    ai = mag * jnp.sin(li * dt)
    den = lr * lr + li * li
    fr = ((ar - 1.0) * lr + ai * li) / den
    fi = (ai * lr - (ar - 1.0) * li) / den
    bbr = fr[..., None] * b_re - fi[..., None] * b_im
    bbi = fr[..., None] * b_im + fi[..., None] * b_re
    j = jnp.arange(L + 1, dtype=F32)[:, None, None]
    pmag = jnp.exp(j * (lr * dt))
    pr = pmag * jnp.cos(j * (li * dt))
    pi = pmag * jnp.sin(j * (li * dt))
    abr = pr[..., None] * bbr - pi[..., None] * bbi
    abi = pr[..., None] * bbi + pi[..., None] * bbr
    kd = (jnp.einsum('gcp,jgpk->jgck', c_re, abr, precision=hp)
          - jnp.einsum('gcp,jgpk->jgck', c_im, abi, precision=hp))
    kd = kd.at[0].add(d_skip.reshape(ng, c)[:, :, None] * jnp.eye(c, dtype=F32))
    kc = kd[:L].reshape(L, ns, gs, c, c).transpose(1, 0, 2, 4, 3).reshape(ns, L, gs * c, c)
    ab = jnp.stack([abr[:L], abi[:L]]).reshape(2, L, ns, gs, p, c)
    abc = ab.transpose(2, 1, 3, 5, 0, 4).reshape(ns, L, gs * c, 2 * p)
    cr = c_re[None] * pr[1:, :, None, :] - c_im[None] * pi[1:, :, None, :]
    ci = -(c_re[None] * pi[1:, :, None, :] + c_im[None] * pr[1:, :, None, :])
    co = jnp.stack([cr, ci]).reshape(2, L, ns, gs, c, p)
    coc = co.transpose(2, 1, 0, 3, 5, 4).reshape(ns, L, 2 * gs * p, c)
    al = jnp.stack([pr[L].reshape(-1), pi[L].reshape(-1)])
    return kc.astype(BF16), abc.astype(BF16), coc.astype(BF16), al


def _s5_expanders():
    gs, c, p = S5_SLAB, SSM_GROUP, SSM_STATE
    lane = np.arange(gs * c)
    rep_c = (np.arange(c)[:, None] == lane[None, :] % c)
    same_g = (lane[:, None] // c == lane[None, :] // c)
    st = np.arange(2 * gs * p)
    rp = np.arange(2 * p)
    rep_p = (rp[:, None] // p == st[None, :] // (gs * p)) & (rp[:, None] % p == st[None, :] % p)
    g_in = (lane[:, None] // c == (st[None, :] // p) % gs)
    g_out = ((st[:, None] // p) % gs == lane[None, :] // c)
    as_bf = lambda a: jnp.asarray(a.astype(np.float32), dtype=BF16)
    return as_bf(rep_c), as_bf(same_g), as_bf(rep_p), as_bf(g_in), as_bf(g_out)


def _chunk_lanes(u_ref):
    return jnp.concatenate([u_ref[0, :, t, :] for t in range(S5_L)], axis=1)


def _s5_local_kernel(u_ref, abc_ref, rep_p_ref, g_in_ref, pre_ref, pim_ref, wp_ref):
    for s in range(S5_L):
        blk = _dot(abc_ref[0, S5_L - 1 - s], rep_p_ref[...]).astype(BF16) * g_in_ref[...]
        wp_ref[128 * s:128 * (s + 1), :] = blk
    r = _dot(_chunk_lanes(u_ref).astype(BF16), wp_ref[...])
    half = r.shape[1] // 2
    pre_ref[...] = r[:, :half]
    pim_ref[...] = r[:, half:]


def _s5_scan_kernel(pre_ref, pim_ref, al_ref, h0r_ref, h0i_ref, hsr_ref, hsi_ref, hfr_ref, hfi_ref):
    nchunk = pre_ref.shape[1]
    ar = al_ref[0:1, :]
    ai = al_ref[1:2, :]

    def body(k, carry):
        cr, ci = carry
        hsr_ref[:, pl.ds(k, 1), :] = cr[:, None, :]
        hsi_ref[:, pl.ds(k, 1), :] = ci[:, None, :]
        xr = pre_ref[:, pl.ds(k, 1), :][:, 0, :]
        xi = pim_ref[:, pl.ds(k, 1), :][:, 0, :]
        return ar * cr - ai * ci + xr, ar * ci + ai * cr + xi

    cr, ci = lax.fori_loop(0, nchunk, body, (h0r_ref[...], h0i_ref[...]), unroll=min(nchunk, 4))
    hfr_ref[...] = cr
    hfi_ref[...] = ci


def _s5_out_kernel(u_ref, kc_ref, coc_ref, rep_c_ref, same_g_ref, g_out_ref, hsr_ref, hsi_ref, y_ref,
                   wt_ref, wo_ref):
    rep_c = rep_c_ref[...]
    lag = [_dot(kc_ref[0, dl], rep_c).astype(BF16) * same_g_ref[...] for dl in range(S5_L)]
    zero = jnp.zeros((128, 128), BF16)
    for s in range(S5_L):
        for t in range(S5_L):
            wt_ref[128 * s:128 * (s + 1), 128 * t:128 * (t + 1)] = lag[t - s] if t >= s else zero
    for t in range(S5_L):
        wo_ref[:, 128 * t:128 * (t + 1)] = _dot(coc_ref[0, t], rep_c).astype(BF16) * g_out_ref[...]
    hs = jnp.concatenate([hsr_ref[...], hsi_ref[...]], axis=1).astype(BF16)
    y = _dot(_chunk_lanes(u_ref).astype(BF16), wt_ref[...]) + _dot(hs, wo_ref[...])
    for t in range(S5_L):
        y_ref[0, :, t, :] = y[:, 128 * t:128 * (t + 1)]


def _s5(u3, h0r, h0i, weights, nbatch):
    kc, abc, coc, al = weights
    rep_c, same_g, rep_p, g_in, g_out = _s5_expanders()
    ns, nb = u3.shape[0], u3.shape[1]
    nchunk = nb // nbatch
    gp = al.shape[1]
    lb = gp // ns
    d = S5_L * 128
    cp = pltpu.CompilerParams(dimension_semantics=("parallel",), vmem_limit_bytes=VMEM_LIMIT)
    u_spec = pl.BlockSpec((1, nb, S5_L, 128), lambda j: (j, 0, 0, 0))
    st_spec = pl.BlockSpec((nb, lb), lambda j: (0, j))
    slab = lambda a: pl.BlockSpec((1,) + a.shape[1:], lambda j: (j,) + (0,) * (a.ndim - 1))
    const = lambda a: pl.BlockSpec(a.shape, lambda j: (0,) * a.ndim)
    pre, pim = pl.pallas_call(
        _s5_local_kernel,
        grid=(ns,),
        in_specs=[u_spec, slab(abc), const(rep_p), const(g_in)],
        out_specs=[st_spec] * 2,
        out_shape=[jax.ShapeDtypeStruct((nb, gp), F32)] * 2,
        scratch_shapes=[pltpu.VMEM((d, 2 * lb), BF16)],
        compiler_params=cp,
        name="s5_local",
    )(u3, abc, rep_p, g_in)

    ls = 2 * lb
    seq3 = pl.BlockSpec((nbatch, nchunk, ls), lambda j: (0, 0, j))
    row = lambda r: pl.BlockSpec((r, ls), lambda j: (0, j))
    hsr, hsi, hfr, hfi = pl.pallas_call(
        _s5_scan_kernel,
        grid=(gp // ls,),
        in_specs=[seq3, seq3, row(2), row(nbatch), row(nbatch)],
        out_specs=[seq3, seq3, row(nbatch), row(nbatch)],
        out_shape=[jax.ShapeDtypeStruct((nbatch, nchunk, gp), F32)] * 2 + [jax.ShapeDtypeStruct((nbatch, gp), F32)] * 2,
        compiler_params=cp,
        name="s5_scan",
    )(pre.reshape(nbatch, nchunk, gp), pim.reshape(nbatch, nchunk, gp), al, h0r, h0i)

    y3 = pl.pallas_call(
        _s5_out_kernel,
        grid=(ns,),
        in_specs=[u_spec, slab(kc), slab(coc), const(rep_c), const(same_g), const(g_out), st_spec, st_spec],
        out_specs=u_spec,
        out_shape=jax.ShapeDtypeStruct(u3.shape, F32),
        scratch_shapes=[pltpu.VMEM((d, d), BF16), pltpu.VMEM((2 * lb, d), BF16)],
        compiler_params=cp,
        name="s5_out",
    )(u3, kc, coc, rep_c, same_g, g_out, hsr.reshape(nb, gp), hsi.reshape(nb, gp))
    return y3, hfr, hfi


def _merge_kernel(o_ref, y_ref, wa_ref, wg1_ref, wg2_ref, sga_ref, sgb_ref, m_ref, gy_ref):
    @pl.when(pl.program_id(1) == 0)
    def _():
        y = jnp.concatenate([y_ref[s] for s in range(y_ref.shape[0])], axis=1)
        gy_ref[...] = _gelu(y).astype(BF16)

    ya = _dot(o_ref[...], wa_ref[...])
    gy = gy_ref[...]
    yb = _dot(gy, wg1_ref[...]) * jax.nn.sigmoid(_dot(gy, wg2_ref[...]))
    m = sga_ref[...].astype(F32) * ya + sgb_ref[...].astype(F32) * yb
    m_ref[...] = m.astype(BF16)


def _merge(o, y3, w_attn_out, w_glu, sg, tm, tn=1024):
    rows, d = o.shape[0], w_attn_out.shape[1]
    nj = d // tn
    kq, ks = w_attn_out.shape[0], w_glu.shape[0]
    ns = y3.shape[0]
    return pl.pallas_call(
        _merge_kernel,
        grid=(rows // tm, nj),
        in_specs=[
            pl.BlockSpec((tm, kq), lambda i, j: (i, 0)),
            pl.BlockSpec((ns, tm, 128), lambda i, j: (0, i, 0)),
            pl.BlockSpec((kq, tn), lambda i, j: (0, j)),
            pl.BlockSpec((ks, tn), lambda i, j: (0, j)),
            pl.BlockSpec((ks, tn), lambda i, j: (0, j + nj)),
            pl.BlockSpec((tm, tn), lambda i, j: (i, j)),
            pl.BlockSpec((tm, tn), lambda i, j: (i, j + nj)),
        ],
        out_specs=pl.BlockSpec((tm, tn), lambda i, j: (i, j)),
        out_shape=jax.ShapeDtypeStruct((rows, d), BF16),
        scratch_shapes=[pltpu.VMEM((tm, ks), BF16)],
        compiler_params=pltpu.CompilerParams(
            dimension_semantics=("parallel", "arbitrary"), vmem_limit_bytes=VMEM_LIMIT),
        name="merge",
    )(o, y3, w_attn_out, w_glu, w_glu, sg, sg)


def _outproj_kernel(m_ref, x_ref, w_ref, g_ref, x1_ref, h_ref):
    x1 = x_ref[...] + _dot(m_ref[...], w_ref[...])
    x1_ref[...] = x1
    r = lax.rsqrt(jnp.mean(x1 * x1, axis=-1, keepdims=True) + EPS)
    h_ref[...] = (x1 * r * g_ref[...]).astype(BF16)


def _outproj(m, x, w_out, g_ffn, tm):
    rows, d = x.shape
    row = pl.BlockSpec((tm, d), lambda i: (i, 0))
    return pl.pallas_call(
        _outproj_kernel,
        grid=(rows // tm,),
        in_specs=[row, row, pl.BlockSpec((d, d), lambda i: (0, 0), pipeline_mode=pl.Buffered(1)),
                  pl.BlockSpec((1, d), lambda i: (0, 0))],
        out_specs=[row, row],
        out_shape=[jax.ShapeDtypeStruct((rows, d), F32), jax.ShapeDtypeStruct((rows, d), BF16)],
        compiler_params=pltpu.CompilerParams(dimension_semantics=("parallel",), vmem_limit_bytes=VMEM_LIMIT),
        name="out_proj",
    )(m, x, w_out, g_ffn.reshape(1, d))


def _ffn_kernel(h_ref, x1_ref, wg_ref, wu_ref, wd_ref, gf_ref, y_ref, *rest):
    acc_ref = rest[-1]
    f = pl.program_id(1)

    @pl.when(f == 0)
    def _():
        acc_ref[...] = jnp.zeros_like(acc_ref)

    wg, wu, wd = wg_ref[...].astype(BF16), wu_ref[...].astype(BF16), wd_ref[...].astype(BF16)
    for copy_ref, w in zip(rest[:-1], (wg, wu, wd)):
        copy_ref[...] = w
    h = h_ref[...]
    a = _dot(h, wg)
    a = (a * jax.nn.sigmoid(a)) * _dot(h, wu)
    acc_ref[...] += _dot(a.astype(BF16), wd)

    @pl.when(f == pl.num_programs(1) - 1)
    def _():
        y = x1_ref[...] + acc_ref[...]
        r = lax.rsqrt(jnp.mean(y * y, axis=-1, keepdims=True) + EPS)
        y_ref[...] = y * r * gf_ref[...]


def _ffn(h, x1, w_gate, w_up, w_down, g_final, tm, tf=512):
    rows, d = x1.shape
    dff = w_gate.shape[1]
    emit = w_gate.dtype != BF16
    assert not emit or rows == tm
    row = pl.BlockSpec((tm, d), lambda i, f: (i, 0))
    w_specs = [pl.BlockSpec((d, tf), lambda i, f: (0, f)),
               pl.BlockSpec((d, tf), lambda i, f: (0, f)),
               pl.BlockSpec((tf, d), lambda i, f: (f, 0))]
    y_shape = jax.ShapeDtypeStruct((rows, d), F32)
    copies = [jax.ShapeDtypeStruct(w.shape, BF16) for w in (w_gate, w_up, w_down)] if emit else []
    out = pl.pallas_call(
        _ffn_kernel,
        grid=(rows // tm, dff // tf),
        in_specs=[row, row] + w_specs + [pl.BlockSpec((1, d), lambda i, f: (0, 0))],
        out_specs=[row] + (w_specs if emit else []),
        out_shape=[y_shape] + copies,
        scratch_shapes=[pltpu.VMEM((tm, d), F32)],
        compiler_params=pltpu.CompilerParams(
            dimension_semantics=("parallel", "arbitrary"), vmem_limit_bytes=VMEM_LIMIT),
        name="ffn",
    )(h, x1, w_gate, w_up, w_down, g_final.reshape(1, d))
    return tuple(out) if emit else out[0]


def _alibi_slopes():
    return jnp.exp2(-8.0 * jnp.arange(1, N_HEADS + 1, dtype=F32) / N_HEADS) * LOG2E


def _slope_table(slopes):
    s1 = slopes.astype(BF16)
    r1 = slopes - s1.astype(F32)
    s2 = r1.astype(BF16)
    s3 = (r1 - s2.astype(F32)).astype(BF16)
    tab = jnp.stack([s1, s1, s2, s2, s3, s3] + [jnp.zeros_like(s1)] * 10, axis=-1)
    return tab.reshape(N_KV, GROUP, 16)


def _position_rows(pos):
    pos = np.arange(pos) if np.isscalar(pos) else np.asarray(pos)
    hi, lo = (pos // 64) * 64, pos % 64
    assert pos.max() < 64 * 256
    rows = np.stack([hi, lo, hi, lo, hi, lo] + [np.zeros_like(pos)] * 10).astype(np.float32)
    return jnp.asarray(rows, dtype=BF16)


def _layer_weights(p, l):
    w_in = _pack_w_in(p['w_in'][l], p['ssm_d'].shape[-1])
    s5 = _s5_weights(*(p[n][l] for n in ('ssm_lam_re', 'ssm_lam_im', 'ssm_log_dt', 'ssm_b_re', 'ssm_b_im',
                                          'ssm_c_re', 'ssm_c_im', 'ssm_d')))
    cast = lambda n: p[n][l].astype(BF16)
    return dict(
        g_mix=p['g_mix'][l], w_in=w_in,
        pool=p['w_cmp_pool'][l], pe=p['w_cmp_pe'][l], phi=p['w_cmp_phi'][l],
        pool_mats=_pool_matrices(p['w_cmp_pool'][l], PROMPT_PAGES_PER_STEP),
        w_attn_out=cast('w_attn_out'), w_glu=cast('w_glu'), w_out=cast('w_out'), g_ffn=p['g_ffn'][l],
        ffn_w=tuple(p[n][l] for n in ('w_gate', 'w_up', 'w_down')), s5=s5)


def _tail(x, o, y3, sg, lw, g_final, tm):
    rows = x.shape[0]
    m = _merge(o, y3.reshape(y3.shape[0], rows, 128), lw['w_attn_out'], lw['w_glu'], sg, tm)
    x1, h = _outproj(m, x, lw['w_out'], lw['g_ffn'], tm)
    return _ffn(h, x1, *lw['ffn_w'], g_final, tm)


def _feature_major_rows(kv_t, t0):
    b, f, t = kv_t.shape
    return kv_t[:, :, t0:].reshape(b, f // KV_COLS, N_KV, HEAD_DIM, t - t0).transpose(0, 4, 1, 2, 3)


def _layer_prompt(x, lw, slopes, g_out):
    b, t, d = x.shape
    rows = b * t
    assert t % (PROMPT_PAGES_PER_STEP * PAGE) == 0
    q, u3, sg, gt, kvc_t, kvw_t = _in_proj(x.reshape(rows, d), lw['g_mix'], lw['w_in'], 1024, seq_len=t)

    pooled = _pool_pages(kvc_t, lambda bi, pg: (bi, 0, pg), (), b, t // PAGE, lw['pool_mats'],
                         PROMPT_PAGES_PER_STEP)
    _, vc_t, kc_rows = _cmp_finish(pooled, lw['pool'], lw['pe'], lw['phi'])
    o = _attn_prompt(slopes, q, kc_rows, vc_t, kvc_t, kvw_t, gt, b, t)

    gp = lw['s5'][3].shape[1]
    h0 = jnp.zeros((b, gp), F32)
    y3, hr, hi = _s5(u3, h0, h0, lw['s5'], b)

    y = _tail(x.reshape(rows, d), o, y3, sg, lw, g_out, 512).reshape(b, t, d)
    n_win = min(WINDOW, t)
    ng = gp // SSM_STATE
    return (y, _feature_major_rows(kvc_t, 0), _feature_major_rows(kvw_t, t - n_win),
            hr.reshape(b, ng, SSM_STATE), hi.reshape(b, ng, SSM_STATE))


def _layer_sample(x, cache, page_table, win_buf, h_re, h_im, lw, slopes, g_out):
    b, t, d = x.shape
    rows = b * t
    npages = page_table.shape[1]
    past = npages * cache.shape[1]
    n_buf = win_buf.shape[1]
    assert cache.shape[1] == PAGE and rows % S5_L == 0 and t == S5_L
    assert past % CMP_STRIDE == 0 and t < CMP_STRIDE and past % SEL_BLOCK == 0 and t <= SEL_BLOCK
    npg = DECODE_PAGES_PER_STEP
    assert npages % npg == 0
    q, u3, sg, gt, kvc, kvw = _in_proj(x.reshape(rows, d), lw['g_mix'], lw['w_in'], rows)

    cache_t = cache.transpose(0, 2, 3, 4, 1).reshape(cache.shape[0], 4 * KV_COLS, PAGE)
    win_t = win_buf.transpose(0, 2, 3, 4, 1).reshape(b, 2 * KV_COLS, n_buf)
    pooled = _pool_pages(cache_t, lambda bi, pg, pt: (pt[bi, pg], 0, 0), (page_table,), b, npages,
                         lw['pool_mats'], npg)
    kc_t, vc_t, _ = _cmp_finish(pooled, lw['pool'], lw['pe'], lw['phi'])

    nrows = N_HEADS * t
    eye = jnp.eye(N_KV, dtype=BF16)
    q5 = q.reshape(b, t, N_KV, GROUP, HEAD_DIM).transpose(0, 2, 3, 1, 4)
    qbd = (q5[:, :, :, :, None, :] * eye[None, :, None, None, :, None]).reshape(b, nrows, KV_COLS)
    maskbd = jnp.repeat(jnp.repeat(jnp.eye(N_KV, dtype=F32), GROUP * t, axis=0), HEAD_DIM, axis=1)
    slopecol = jnp.repeat(slopes, t).reshape(nrows, 1)
    g3 = gt.reshape(b, t, N_KV, 128)[..., :3 * GROUP].reshape(b, t, N_KV, 3, GROUP)
    g3 = g3.transpose(0, 2, 4, 1, 3).reshape(b, nrows, 3)
    gates_rows = jnp.pad(g3, ((0, 0), (0, 0), (0, 1)))
    n_blk = -(-(past + t) // SEL_BLOCK)

    opart, sel_t = _attn_dec1(qbd, kc_t, vc_t, win_t, kvw.reshape(b, t, 2 * KV_COLS), gates_rows, slopecol, maskbd,
                              past, t, n_blk)
    bps = npg * PAGE // SEL_BLOCK
    nstep = npages // npg
    nb_all = (nstep + 1) * bps
    sel = jnp.pad(sel_t[:, :n_blk], ((0, 0), (0, nb_all - n_blk), (0, 0)))
    sel = sel.reshape(b, nstep + 1, bps, N_KV, 1, t).transpose(0, 1, 3, 4, 5, 2)
    selrows = jnp.broadcast_to(sel, (b, nstep + 1, N_KV, GROUP, t, bps)).reshape(b, nstep + 1, nrows, bps)
    knew = jnp.pad(kvc.reshape(b, t, 4 * KV_COLS)[:, :, 2 * KV_COLS:], ((0, 0), (0, PAGE - t), (0, 0)))
    o_rows = _attn_dec2(cache_t, page_table, qbd, selrows.astype(BF16), slopecol, knew, opart, gates_rows,
                        maskbd, past, t, npg)
    o = (o_rows.reshape(b, N_KV, GROUP, t, HEAD_DIM).transpose(0, 3, 1, 2, 4).reshape(rows, Q_COLS).astype(BF16))

    gp = h_re.shape[1] * h_re.shape[2]
    y3, hr, hi = _s5(u3, h_re.reshape(b, gp), h_im.reshape(b, gp), lw['s5'], b)

    y, *ffn_w_bf16 = _tail(x.reshape(rows, d), o, y3, sg, lw, g_out, rows)
    y = y.reshape(b, t, d)
    win_all = jnp.concatenate([win_buf, kvw.reshape(b, t, 2, N_KV, HEAD_DIM)], axis=1)
    n_keep = min(WINDOW, n_buf + t)
    return (y, kvc.reshape(b, t, 4, N_KV, HEAD_DIM), win_all[:, n_buf + t - n_keep:], hr.reshape(h_re.shape),
            hi.reshape(h_im.shape), tuple(ffn_w_bf16))


def kernel(x_prompt, x_sample, cache_kv, state_win, state_ssm_re, state_ssm_im, page_table, g_mix, w_in, w_cmp_pe, w_cmp_pool, w_cmp_phi, w_attn_out, ssm_lam_re, ssm_lam_im, ssm_log_dt, ssm_b_re, ssm_b_im, ssm_c_re, ssm_c_im, ssm_d, w_glu, w_out, g_ffn, w_gate, w_up, w_down, g_final):
    depth = g_mix.shape[0]
    assert depth == 1, "final norm is fused into the last layer's FFN; one layer supported"
    params = dict(g_mix=g_mix, w_in=w_in, w_cmp_pe=w_cmp_pe, w_cmp_pool=w_cmp_pool, w_cmp_phi=w_cmp_phi,
                  w_attn_out=w_attn_out, ssm_lam_re=ssm_lam_re, ssm_lam_im=ssm_lam_im, ssm_log_dt=ssm_log_dt,
                  ssm_b_re=ssm_b_re, ssm_b_im=ssm_b_im, ssm_c_re=ssm_c_re, ssm_c_im=ssm_c_im, ssm_d=ssm_d,
                  w_glu=w_glu, w_out=w_out, g_ffn=g_ffn, w_gate=w_gate, w_up=w_up, w_down=w_down)
    slopes = _alibi_slopes()
    outs = [[] for _ in range(8)]
    xp, xs = x_prompt, x_sample
    for l in range(depth):
        lw = _layer_weights(params, l)
        xs, kvs, wins, hrs, his, ffn_w = _layer_sample(xs, cache_kv[l], page_table, state_win[l], state_ssm_re[l],
                                                       state_ssm_im[l], lw, slopes, g_final)
        xp, kvp, winp, hrp, hip = _layer_prompt(xp, dict(lw, ffn_w=ffn_w), slopes, g_final)
        for lst, v in zip(outs, (kvp, winp, hrp, hip, kvs, wins, hrs, his)):
            lst.append(v)
    st = [jnp.stack(v) for v in outs]
    return (xp, xs, st[0], st[1], st[2], st[3], st[4], st[5], st[6], st[7])
```

```python
import functools
import math

import numpy as np
import jax
import jax.numpy as jnp
from jax import lax
from jax.experimental import pallas as pl
from jax.experimental.pallas import tpu as pltpu

F32 = jnp.float32
BF16 = jnp.bfloat16

N_HEADS = 16
HEAD_DIM = 64
N_KV = 4
GROUP = N_HEADS // N_KV
CMP_LEN = 32
CMP_STRIDE = 16
SEL_BLOCK = 64
SEL_TOP = 16
WINDOW = 512
SSM_GROUP = 16
SSM_STATE = 64
EPS = 1e-6
NEG = -1e30
FORCE_SCORE = 1e4

KV_COLS = N_KV * HEAD_DIM
Q_COLS = N_HEADS * HEAD_DIM
PAGE = 128
CMP_PER_PAGE = PAGE // CMP_STRIDE
PROMPT_PAGES_PER_STEP = 16
DECODE_PAGES_PER_STEP = 32
S5_L = 8
S5_SLAB = 128 // SSM_GROUP
VMEM_LIMIT = 56 * 1024 * 1024
LOG2E = 1.0 / math.log(2.0)
Q_SCALE = HEAD_DIM ** -0.5 * LOG2E


def _dot(a, b):
    return jnp.dot(a, b, preferred_element_type=F32)


def _dot_nt(a, b):
    return lax.dot_general(a, b, (((1,), (1,)), ((), ())), preferred_element_type=F32)


def _split(a):
    hi = a.astype(BF16)
    lo = (a - hi.astype(F32)).astype(BF16)
    return hi, lo


def _dot3(a, b):
    ah, al = _split(a)
    bh, bl = _split(b)
    return _dot(ah, bh) + _dot(ah, bl) + _dot(al, bh)


def _dot_nt_hl(w_bf16, a):
    ah, al = _split(a)
    return _dot_nt(w_bf16, ah) + _dot_nt(w_bf16, al)


def _gelu(x):
    return 0.5 * x * (1.0 + jnp.tanh(math.sqrt(2.0 / math.pi) * (x + 0.044715 * (x * x * x))))


def _iota(shape, dim):
    return lax.broadcasted_iota(jnp.int32, shape, dim)


def _log2(n):
    assert n > 0 and n & (n - 1) == 0, n
    return n.bit_length() - 1


def _div_pow2(x, n):
    return jnp.right_shift(x, _log2(n))


def _mod_pow2(x, n):
    return jnp.bitwise_and(x, (1 << _log2(n)) - 1)


def _softmax2_rows(s):
    m = jnp.max(s, axis=-1, keepdims=True)
    e = jnp.exp2(s - m)
    return e / jnp.sum(e, axis=-1, keepdims=True)


IN_TN = 512
_Q_T, _U_T, _MG_T, _KV_T = 2, 2, 8, 3
_KVC_T = 2
_U_0 = _Q_T
_MG_0 = _U_0 + _U_T
_GT_0 = _MG_0 + _MG_T
_KV_0 = _GT_0 + 1
IN_TILES = _KV_0 + _KV_T


def _inproj_kernel(x_ref, g_ref, wq_ref, wu_ref, wmg_ref, wgt_ref, wkv_ref, q_ref, u_ref, sg_ref, gt_ref, kvc_ref,
                   kvw_ref, xn_ref, *, kv_feature_major):
    j = pl.program_id(1)

    @pl.when(j == 0)
    def _():
        x = x_ref[...]
        r = lax.rsqrt(jnp.mean(x * x, axis=-1, keepdims=True) + EPS)
        xn_ref[...] = (x * r * g_ref[...]).astype(BF16)

    @pl.when(j < _U_0)
    def _():
        q_ref[...] = (_dot(xn_ref[...], wq_ref[...]) * Q_SCALE).astype(BF16)

    @pl.when((j >= _U_0) & (j < _MG_0))
    def _():
        z = _dot(xn_ref[...], wu_ref[...])
        for s in range(IN_TN // 128):
            u_ref[s] = z[:, 128 * s:128 * (s + 1)].reshape(u_ref.shape[1:])

    @pl.when((j >= _MG_0) & (j < _GT_0))
    def _():
        sg_ref[...] = jax.nn.sigmoid(_dot(xn_ref[...], wmg_ref[...])).astype(BF16)

    @pl.when(j == _GT_0)
    def _():
        gt_ref[...] = jax.nn.sigmoid(_dot(xn_ref[...], wgt_ref[...]))

    @pl.when(j >= _KV_0)
    def _():
        if kv_feature_major:
            z = _dot_nt(wkv_ref[...], xn_ref[...])
        else:
            z = _dot(xn_ref[...], wkv_ref[...])

        @pl.when(j < _KV_0 + _KVC_T)
        def _():
            kvc_ref[...] = z.reshape(kvc_ref.shape)

        @pl.when(j >= _KV_0 + _KVC_T)
        def _():
            kvw_ref[...] = z.reshape(kvw_ref.shape)


def _in_proj(x, g_mix, w, tm, seq_len=None):
    rows, d = x.shape
    feature_major = seq_len is not None

    def col(lo, n):
        return lambda i, j: (i, jnp.clip(j - lo, 0, n - 1))

    if feature_major:
        tiles_per_seq = seq_len // tm
        kvc_spec = pl.BlockSpec((1, IN_TN, tm), lambda i, j: (i // tiles_per_seq, jnp.clip(j - _KV_0, 0, _KVC_T - 1),
                                                              i % tiles_per_seq))
        kvw_spec = pl.BlockSpec((1, IN_TN, tm), lambda i, j: (i // tiles_per_seq, 0, i % tiles_per_seq))
        kvc_shape = jax.ShapeDtypeStruct((rows // seq_len, _KVC_T * IN_TN, seq_len), F32)
        kvw_shape = jax.ShapeDtypeStruct((rows // seq_len, IN_TN, seq_len), F32)
        wkv, wkv_spec = w['kv_t'], pl.BlockSpec((IN_TN, d), lambda i, j: (jnp.clip(j - _KV_0, 0, _KV_T - 1), 0))
    else:
        kvc_spec = pl.BlockSpec((tm, IN_TN), col(_KV_0, _KVC_T))
        kvw_spec = pl.BlockSpec((tm, IN_TN), lambda i, j: (i, 0))
        kvc_shape = jax.ShapeDtypeStruct((rows, _KVC_T * IN_TN), F32)
        kvw_shape = jax.ShapeDtypeStruct((rows, IN_TN), F32)
        wkv, wkv_spec = w['kv'], pl.BlockSpec((d, IN_TN), lambda i, j: (0, jnp.clip(j - _KV_0, 0, _KV_T - 1)))
    n_slab = IN_TN // 128
    return pl.pallas_call(
        functools.partial(_inproj_kernel, kv_feature_major=feature_major),
        grid=(rows // tm, IN_TILES),
        in_specs=[
            pl.BlockSpec((tm, d), lambda i, j: (i, 0), pipeline_mode=pl.Buffered(1)),
            pl.BlockSpec((1, d), lambda i, j: (0, 0)),
            pl.BlockSpec((d, IN_TN), lambda i, j: (0, jnp.clip(j, 0, _Q_T - 1))),
            pl.BlockSpec((d, IN_TN), lambda i, j: (0, jnp.clip(j - _U_0, 0, _U_T - 1))),
            pl.BlockSpec((d, IN_TN), lambda i, j: (0, jnp.clip(j - _MG_0, 0, _MG_T - 1))),
            pl.BlockSpec((d, IN_TN), lambda i, j: (0, 0), pipeline_mode=pl.Buffered(1)),
            wkv_spec,
        ],
        out_specs=[
            pl.BlockSpec((tm, IN_TN), col(0, _Q_T)),
            pl.BlockSpec((n_slab, tm // S5_L, S5_L, 128), lambda i, j: (jnp.clip(j - _U_0, 0, _U_T - 1), i, 0, 0)),
            pl.BlockSpec((tm, IN_TN), col(_MG_0, _MG_T)),
            pl.BlockSpec((tm, IN_TN), lambda i, j: (i, 0)),
            kvc_spec, kvw_spec,
        ],
        out_shape=[
            jax.ShapeDtypeStruct((rows, _Q_T * IN_TN), BF16),
            jax.ShapeDtypeStruct((_U_T * n_slab, rows // S5_L, S5_L, 128), F32),
            jax.ShapeDtypeStruct((rows, _MG_T * IN_TN), BF16),
            jax.ShapeDtypeStruct((rows, IN_TN), F32),
            kvc_shape, kvw_shape,
        ],
        scratch_shapes=[pltpu.VMEM((tm, d), BF16)],
        compiler_params=pltpu.CompilerParams(
            dimension_semantics=("parallel", "arbitrary"), vmem_limit_bytes=VMEM_LIMIT),
        name="in_proj",
    )(x, g_mix.reshape(1, d), w['q'], w['u'], w['mg'], w['gt'], wkv)


def _pack_w_in(w_in, ssm_width):
    cuts = np.cumsum([Q_COLS, 6 * KV_COLS, 3 * N_HEADS, ssm_width]).tolist()
    wq, wkv, wgt, wu, wmg = jnp.split(w_in, cuts, axis=1)
    assert wq.shape[1] == _Q_T * IN_TN and wkv.shape[1] == _KV_T * IN_TN
    assert wu.shape[1] == _U_T * IN_TN and wmg.shape[1] == _MG_T * IN_TN and IN_TN == N_KV * 128
    d = w_in.shape[0]
    wgt = wgt.reshape(d, 3, N_KV, GROUP).transpose(0, 2, 1, 3).reshape(d, N_KV, 3 * GROUP)
    wgt = jnp.pad(wgt, ((0, 0), (0, 0), (0, 128 - 3 * GROUP))).reshape(d, IN_TN)
    return dict(q=wq.astype(BF16), u=wu.astype(BF16), mg=wmg.astype(BF16), gt=wgt.astype(BF16),
                kv=wkv.astype(BF16), kv_t=wkv.T.astype(BF16))


def _pool_kernel(*refs, npg, n_tables):
    refs = refs[n_tables:]
    prev_ref = refs[0]
    page_refs = refs[1:npg + 1]
    mk_ref, mv_ref, mprev_ref, out_ref = refs[npg + 1:]
    pages = [prev_ref[0].astype(BF16)] + [page_refs[j][0].astype(BF16) for j in range(npg)]
    mpg = mk_ref.shape[0] // PAGE
    nblk = mk_ref.shape[1]
    for seg in range(npg // mpg):
        x = jnp.concatenate(pages[1 + seg * mpg:1 + (seg + 1) * mpg], axis=1)
        for half, m_ref in ((0, mk_ref), (1, mv_ref)):
            rows = slice(half * KV_COLS, (half + 1) * KV_COLS)
            head = _dot(pages[seg * mpg][rows], mprev_ref[half])
            if seg == 0:
                head = jnp.where(pl.program_id(1) > 0, head, 0.0)
            out_ref[0, rows, seg * nblk:(seg + 1) * nblk] = _dot(x[rows], m_ref[...]) + head


def _pool_matrices(pool, npg):
    nblk = npg * CMP_PER_PAGE
    cols = []
    for r in range(CMP_PER_PAGE):
        start = CMP_STRIDE * (r - 1)
        lo, hi = max(start, 0), min(start + CMP_LEN, PAGE)
        cols.append(jnp.pad(pool[:, lo - start:hi - start], ((0, 0), (lo, PAGE - hi))))
    body = jnp.stack(cols, axis=-1)
    carry = jnp.pad(pool[:, :CMP_STRIDE], ((0, 0), (PAGE - CMP_STRIDE, 0)))
    first = jnp.asarray(np.arange(CMP_PER_PAGE) == 0, F32)
    same = jnp.eye(npg, dtype=F32)[None, :, None, :, None]
    nxt = jnp.eye(npg, k=1, dtype=F32)[None, :, None, :, None]
    m = same * body[:, None, :, None, :] + nxt * (carry[:, None, :, None, None] * first)
    m = m.reshape(2, npg * PAGE, nblk).astype(BF16)
    mprev = (carry[:, :, None] * jnp.asarray(np.arange(nblk) == 0, F32)).astype(BF16)
    return m[0], m[1], mprev


def _pool_pages(pages, page_index, tables, nseq, npages, mats, npg):
    assert npg % (mats[0].shape[0] // PAGE) == 0
    nstep = npages // npg
    nblk = npg * CMP_PER_PAGE
    mk, mv, mprev = mats

    def spec(off):
        return pl.BlockSpec((1, 2 * KV_COLS, PAGE),
                            lambda b, s, *t: page_index(b, jnp.maximum(s * npg + off, 0), *t))

    const = lambda shape: pl.BlockSpec(shape, lambda b, s, *t: (0,) * len(shape))
    return pl.pallas_call(
        functools.partial(_pool_kernel, npg=npg, n_tables=len(tables)),
        grid_spec=pltpu.PrefetchScalarGridSpec(
            num_scalar_prefetch=len(tables),
            grid=(nseq, nstep),
            in_specs=[spec(j) for j in range(-1, npg)] + [const(mk.shape), const(mv.shape), const(mprev.shape)],
            out_specs=pl.BlockSpec((1, 2 * KV_COLS, nblk), lambda b, s, *t: (b, 0, s)),
        ),
        out_shape=jax.ShapeDtypeStruct((nseq, 2 * KV_COLS, npages * CMP_PER_PAGE), F32),
        compiler_params=pltpu.CompilerParams(
            dimension_semantics=("parallel", "arbitrary"), vmem_limit_bytes=VMEM_LIMIT),
        name="cmp_pool",
    )(*tables, *([pages] * (npg + 1)), mk, mv, mprev)


def _cmp_finish_kernel(pooled_ref, poolw_ref, pe_ref, phi_ref, kc_ref, vc_ref, kcr_ref):
    bias = jnp.sum(poolw_ref[...] * pe_ref[...], axis=1, keepdims=True)
    x = _gelu(pooled_ref[0] + bias)
    kc = _dot3(phi_ref[0], x[:KV_COLS])
    kc_ref[0] = kc.astype(BF16)
    kcr_ref[0] = kc.T.astype(BF16)
    vc_ref[0] = _dot3(phi_ref[1], x[KV_COLS:]).astype(BF16)


def _cmp_finish(pooled, pool, pe, phi):
    nseq, _, n = pooled.shape
    poolw = jnp.concatenate([jnp.broadcast_to(pool[s][None, :], (KV_COLS, CMP_LEN)) for s in range(2)], axis=0)
    pe_t = jnp.concatenate([jnp.tile(pe[s].T, (N_KV, 1)) for s in range(2)], axis=0)
    phi_bd = jnp.stack([jnp.kron(jnp.eye(N_KV, dtype=F32), phi[s].T) for s in range(2)])
    spec = pl.BlockSpec((1, KV_COLS, n), lambda b: (b, 0, 0))
    return pl.pallas_call(
        _cmp_finish_kernel,
        grid=(nseq,),
        in_specs=[
            pl.BlockSpec((1, 2 * KV_COLS, n), lambda b: (b, 0, 0)),
            pl.BlockSpec((2 * KV_COLS, CMP_LEN), lambda b: (0, 0)),
            pl.BlockSpec((2 * KV_COLS, CMP_LEN), lambda b: (0, 0)),
            pl.BlockSpec((2, KV_COLS, KV_COLS), lambda b: (0, 0, 0)),
        ],
        out_specs=[spec, spec, pl.BlockSpec((1, n, KV_COLS), lambda b: (b, 0, 0))],
        out_shape=[jax.ShapeDtypeStruct((nseq, KV_COLS, n), BF16)] * 2
        + [jax.ShapeDtypeStruct((nseq, n, KV_COLS), BF16)],
        compiler_params=pltpu.CompilerParams(dimension_semantics=("parallel",)),
        name="cmp_finish",
    )(pooled, poolw, pe_t, phi_bd)


ATT_TQ = 256
ATT_TK = 256
NBLK_PAD = 128
POS_ROWS = 16
AUG = HEAD_DIM + POS_ROWS


def _attn_prompt_kernel(q_ref, kca_ref, vc_ref, ks_ref, vs_ref, kw_ref, vw_ref, gt_ref, ovt_ref, ktab_ref,
                        stab_ref, o_ref, ksa_ref, vsa_ref, kwa_ref, vwa_ref, qat_ref, m_ref, acc_ref,
                        *, tq, tk, n_sel, n_top):
    qi = pl.program_id(2)
    q0 = qi * tq
    seq = ksa_ref.shape[0]

    @pl.when(qi == 0)
    def _():
        zero = jnp.zeros((128 - HEAD_DIM, seq), F32)
        ones = jnp.where(_iota((POS_ROWS, seq), 0) == 0, 1.0, 0.0).astype(BF16)
        for src, dst, lo in ((ks_ref, ksa_ref, NBLK_PAD), (kw_ref, kwa_ref, 0)):
            k_rows = jnp.concatenate([src[0], zero], axis=0).T[:, :HEAD_DIM].astype(BF16)
            dst[:, lo:] = jnp.concatenate([k_rows, ktab_ref[:, NBLK_PAD:]], axis=1)
        ksa_ref[:, 0:NBLK_PAD] = ktab_ref[:, 0:NBLK_PAD]
        for src, dst in ((vs_ref, vsa_ref), (vw_ref, vwa_ref)):
            dst[0:HEAD_DIM] = src[0].astype(BF16)
            dst[HEAD_DIM:] = ones

    q_t = q_ref[...].astype(F32).T
    q_t = jnp.concatenate([q_t[n * HEAD_DIM:(n + 1) * HEAD_DIM] for n in range(GROUP)], axis=1).astype(BF16)
    stab = stab_ref[0]
    sp_t = jnp.concatenate([jnp.broadcast_to(stab[:, n:n + 1], (POS_ROWS, tq)) for n in range(GROUP)], axis=1)
    qat_ref[NBLK_PAD:NBLK_PAD + HEAD_DIM] = q_t
    qat_ref[NBLK_PAD + HEAD_DIM:] = sp_t.astype(BF16)
    trow = q0 + _iota((1, tq), 1)
    lanes = [slice(n * tq, (n + 1) * tq) for n in range(GROUP)]

    ncmp = kca_ref.shape[2]
    nidx = _iota((ncmp, 1), 0)
    heads = lambda a: jnp.concatenate([a] * GROUP, axis=1)
    valid = (nidx >= 1) & (CMP_STRIDE * nidx + (CMP_STRIDE - 1) <= trow)
    valid = heads(jnp.where(valid, 1.0, 0.0)) > 0.5
    s = jnp.where(valid, _dot(kca_ref[0, 0], qat_ref[NBLK_PAD:]), NEG)
    e = jnp.exp2(s - jnp.max(s, axis=0, keepdims=True))
    pc = jnp.where(valid, e / jnp.sum(e, axis=0, keepdims=True), 0.0)
    o_c = _dot(vc_ref[0], pc.astype(BF16))

    ph, plo = _split((pc[:, lanes[0]] + pc[:, lanes[1]]) + (pc[:, lanes[2]] + pc[:, lanes[3]]))
    imp = (_dot(ovt_ref[...], ph) + _dot(ovt_ref[...], plo))[0:n_sel]
    blk = _iota((n_sel, 1), 0)
    cur = _div_pow2(q0 + _iota((1, tq), 1), SEL_BLOCK)
    forced = (blk == 0) | (blk == cur) | (blk == cur - 1)
    imp = jnp.where(forced, FORCE_SCORE, imp)
    imp = jnp.where(blk <= cur, imp, -1.0)
    cnt = jnp.zeros((n_sel, tq), F32)
    for j in range(n_sel):
        vj = imp[j:j + 1, :]
        tie = jnp.where(blk > j, 1.0, 0.0)
        cnt = cnt + jnp.where(vj > imp, 1.0, jnp.where(vj == imp, tie, 0.0))
    mask_t = jnp.where((cnt < n_top) & (blk <= cur), 0.0, NEG)
    mask_t = jnp.concatenate([mask_t, jnp.full((NBLK_PAD - n_sel, tq), NEG, F32)], axis=0).astype(BF16)
    qat_ref[0:NBLK_PAD] = heads(mask_t)

    m_ref[...] = jnp.full_like(m_ref, NEG)
    acc_ref[...] = jnp.zeros_like(acc_ref)

    def scores(kt):
        return _dot(ksa_ref[pl.ds(pl.multiple_of(kt * tk, tk), tk), :], qat_ref[...])

    def update(kt, s):
        m_old = m_ref[...]
        m_new = jnp.maximum(m_old, jnp.max(s, axis=0, keepdims=True))
        p = jnp.exp2(s - m_new).astype(BF16)
        pv = _dot(vsa_ref[:, pl.ds(pl.multiple_of(kt * tk, tk), tk)], p)
        acc_ref[...] = jnp.exp2(m_old - m_new) * acc_ref[...] + pv
        m_ref[...] = m_new

    def pair(j, carry):
        s0, s1 = scores(2 * j), scores(2 * j + 1)
        update(2 * j, s0)
        update(2 * j + 1, s1)
        return carry

    lax.fori_loop(0, qi // 2, pair, 0)

    @pl.when(qi % 2 == 1)
    def _():
        update(qi - 1, scores(qi - 1))

    span = WINDOW + tq
    w0 = pl.multiple_of(jnp.maximum(q0 - WINDOW, 0), tq)
    s_diag = scores(qi) + heads(jnp.where(q0 + _iota((tk, 1), 0) <= trow, 0.0, NEG))
    dist = trow - (w0 + _iota((span, 1), 0))
    s_win = (_dot(kwa_ref[pl.ds(w0, span), :], qat_ref[NBLK_PAD:])
             + heads(jnp.where((dist >= 0) & (dist < WINDOW), 0.0, NEG)))
    update(qi, s_diag)
    acc = acc_ref[...]
    o_s = acc[:HEAD_DIM] / acc[HEAD_DIM:HEAD_DIM + 1]
    ow = _dot(vwa_ref[:, pl.ds(w0, span)], jnp.exp2(s_win - jnp.max(s_win, axis=0, keepdims=True)).astype(BF16))
    o_w = ow[:HEAD_DIM] / ow[HEAD_DIM:HEAD_DIM + 1]

    gt_t = gt_ref[...].T
    outs = []
    for n in range(GROUP):
        outs.append(gt_t[n:n + 1] * o_c[:, lanes[n]] + gt_t[GROUP + n:GROUP + n + 1] * o_s[:, lanes[n]]
                    + gt_t[2 * GROUP + n:2 * GROUP + n + 1] * o_w[:, lanes[n]])
    o_ref[...] = jnp.concatenate(outs, axis=0).T.astype(BF16)


def _overlap_t(n_blk_rows, n_cmp):
    s = np.arange(n_blk_rows)[:, None]
    n = np.arange(n_cmp)[None, :]
    r = SEL_BLOCK // CMP_STRIDE
    return jnp.asarray(((n >= r * s) & (n <= r * s + r)).astype(np.float32), dtype=BF16)


def _block_expand(n_blk_rows, n_keys):
    s = np.arange(n_blk_rows)[:, None]
    k = np.arange(n_keys)[None, :]
    return jnp.asarray((k // SEL_BLOCK == s).astype(np.float32), dtype=BF16)


def _attn_prompt(slopes, q, kc_rows, vc_t, kvc_t, kvw_t, gates, nbatch, seq):
    ncmp = kc_rows.shape[1]
    tq, tk = ATT_TQ, ATT_TK
    assert tk == tq and seq % tk == 0 and WINDOW % tq == 0 and WINDOW + tq <= seq
    nq = seq // tq
    n_sel = seq // SEL_BLOCK
    ovt = _overlap_t(NBLK_PAD, ncmp)
    ktab = jnp.concatenate([_block_expand(NBLK_PAD, seq), _position_rows(seq)], axis=0).T
    cmp_end = CMP_STRIDE * np.arange(ncmp) + (CMP_STRIDE - 1)
    kca = jnp.concatenate([kc_rows.reshape(nbatch, ncmp, N_KV, HEAD_DIM).transpose(0, 2, 1, 3),
                           jnp.broadcast_to(_position_rows(cmp_end).T, (nbatch, N_KV, ncmp, POS_ROWS))], axis=-1)
    stab = _slope_table(slopes).astype(F32).transpose(0, 2, 1)
    per_slot = KV_COLS // HEAD_DIM

    def kv_spec(slot):
        return pl.BlockSpec((1, HEAD_DIM, seq), lambda b, g, i: (b, slot * per_slot + g, 0))

    q_spec = pl.BlockSpec((tq, GROUP * HEAD_DIM), lambda b, g, i: (b * nq + i, g))
    return pl.pallas_call(
        functools.partial(_attn_prompt_kernel, tq=tq, tk=tk, n_sel=n_sel, n_top=min(SEL_TOP, n_sel)),
        grid=(nbatch, N_KV, nq),
        in_specs=[
            q_spec,
            pl.BlockSpec((1, 1, ncmp, AUG), lambda b, g, i: (b, g, 0, 0)),
            pl.BlockSpec((1, HEAD_DIM, ncmp), lambda b, g, i: (b, g, 0)),
            kv_spec(2), kv_spec(3), kv_spec(0), kv_spec(1),
            pl.BlockSpec((tq, 128), lambda b, g, i: (b * nq + i, g)),
            pl.BlockSpec((NBLK_PAD, ncmp), lambda b, g, i: (0, 0)),
            pl.BlockSpec((seq, NBLK_PAD + POS_ROWS), lambda b, g, i: (0, 0)),
            pl.BlockSpec((1, POS_ROWS, GROUP), lambda b, g, i: (g, 0, 0)),
        ],
        out_specs=q_spec,
        out_shape=jax.ShapeDtypeStruct(q.shape, BF16),
        scratch_shapes=[
            pltpu.VMEM((seq, NBLK_PAD + AUG), BF16),
            pltpu.VMEM((AUG, seq), BF16),
            pltpu.VMEM((seq, AUG), BF16),
            pltpu.VMEM((AUG, seq), BF16),
            pltpu.VMEM((NBLK_PAD + AUG, GROUP * tq), BF16),
            pltpu.VMEM((1, GROUP * tq), F32),
            pltpu.VMEM((AUG, GROUP * tq), F32),
        ],
        compiler_params=pltpu.CompilerParams(
            dimension_semantics=("parallel", "parallel", "arbitrary"), vmem_limit_bytes=VMEM_LIMIT),
        name="attn_prompt",
    )(q, kca, vc_t, kvc_t, kvc_t, kvw_t, kvw_t, gates, ovt, ktab, stab)


def _diag_blocks(o, maskbd):
    o = o * maskbd
    return (o[:, 0:64] + o[:, 64:128]) + (o[:, 128:192] + o[:, 192:256])


def _attn_dec1_kernel(qbd_ref, kc_ref, vc_ref, win_ref, kvnew_ref, gates_ref, slope_ref, ovt_ref,
                      maskbd_ref, opart_ref, sel_ref, *, past, tdec, n_blk, n_top):
    qb = qbd_ref[0]
    nrows = qb.shape[0]
    tcol = past + _mod_pow2(_iota((nrows, 1), 0), tdec)
    pref = float(past + tdec)
    slope = slope_ref[...]
    maskbd = maskbd_ref[...]

    ncmp = kc_ref.shape[2]
    nrow = _iota((1, ncmp), 1)
    cend = CMP_STRIDE * nrow + (CMP_STRIDE - 1)
    valid = (nrow >= 1) & (cend <= tcol)
    s = _dot(qb, kc_ref[0])
    s = jnp.where(valid, s + slope * (cend.astype(F32) - pref), NEG)
    p = jnp.where(valid, _softmax2_rows(s), 0.0)
    o_c = _diag_blocks(_dot_nt(p.astype(BF16), vc_ref[0]), maskbd)

    per_g = GROUP * tdec
    psum = jnp.concatenate(
        [sum(p[gi * per_g + n * tdec: gi * per_g + (n + 1) * tdec] for n in range(GROUP)) for gi in range(N_KV)],
        axis=0)
    imp = _dot_nt_hl(ovt_ref[...], psum)
    nb_pad = imp.shape[0]
    blk = _iota((nb_pad, 1), 0)
    cur = _div_pow2(past + _mod_pow2(_iota((1, N_KV * tdec), 1), tdec), SEL_BLOCK)
    forced = (blk == 0) | (blk == cur) | (blk == cur - 1)
    imp = jnp.where(forced, FORCE_SCORE, imp)
    imp = jnp.where(blk <= cur, imp, -1.0)
    imp = jnp.where(blk < n_blk, imp, -2.0)
    sel = jnp.zeros(imp.shape, F32)
    for _ in range(n_top):
        mx = jnp.max(imp, axis=0, keepdims=True)
        first = jnp.min(jnp.where(imp == mx, blk, nb_pad), axis=0, keepdims=True)
        pick = blk == first
        sel = jnp.where(pick, 1.0, sel)
        imp = jnp.where(pick, -jnp.inf, imp)
    sel_ref[0] = sel

    nbuf = win_ref.shape[2]
    knew = kvnew_ref[0]
    zpad = jnp.zeros((PAGE - tdec, KV_COLS), F32)
    kn = jnp.concatenate([knew[:, :KV_COLS], zpad], axis=0).astype(BF16)
    vn = jnp.concatenate([knew[:, KV_COLS:], zpad], axis=0).astype(BF16)
    sw = jnp.concatenate([_dot(qb, win_ref[0, :KV_COLS].astype(BF16)), _dot_nt(qb, kn)], axis=1)
    idx = _iota((1, nbuf + PAGE), 1)
    kposw = past - nbuf + idx
    dist = tcol - kposw
    validw = (dist >= 0) & (dist < WINDOW) & (kposw >= 0) & (idx < nbuf + tdec)
    sw = jnp.where(validw, sw + slope * (kposw.astype(F32) - pref), NEG)
    pw = _softmax2_rows(sw).astype(BF16)
    o_w = _dot_nt(pw[:, :nbuf], win_ref[0, KV_COLS:].astype(BF16)) + _dot(pw[:, nbuf:], vn)
    o_w = _diag_blocks(o_w, maskbd)

    gates = gates_ref[0]
    opart_ref[0] = gates[:, 0:1] * o_c + gates[:, 2:3] * o_w


def _attn_dec1(qbd, kc_t, vc_t, win_t, kvnew, gates_rows, slopecol, maskbd, past, tdec, n_blk):
    nseq, nrows, _ = qbd.shape
    ncmp = kc_t.shape[2]
    nb_pad = -(-n_blk // 8) * 8
    ovt = _overlap_t(nb_pad, ncmp)
    nbuf = win_t.shape[2]
    return pl.pallas_call(
        functools.partial(_attn_dec1_kernel, past=past, tdec=tdec, n_blk=n_blk, n_top=min(SEL_TOP, n_blk)),
        grid=(nseq,),
        in_specs=[
            pl.BlockSpec((1, nrows, KV_COLS), lambda b: (b, 0, 0)),
            pl.BlockSpec((1, KV_COLS, ncmp), lambda b: (b, 0, 0)),
            pl.BlockSpec((1, KV_COLS, ncmp), lambda b: (b, 0, 0)),
            pl.BlockSpec((1, 2 * KV_COLS, nbuf), lambda b: (b, 0, 0)),
            pl.BlockSpec((1, tdec, 2 * KV_COLS), lambda b: (b, 0, 0)),
            pl.BlockSpec((1, nrows, 4), lambda b: (b, 0, 0)),
            pl.BlockSpec((nrows, 1), lambda b: (0, 0)),
            pl.BlockSpec((nb_pad, ncmp), lambda b: (0, 0)),
            pl.BlockSpec((nrows, KV_COLS), lambda b: (0, 0)),
        ],
        out_specs=[
            pl.BlockSpec((1, nrows, HEAD_DIM), lambda b: (b, 0, 0)),
            pl.BlockSpec((1, nb_pad, N_KV * tdec), lambda b: (b, 0, 0)),
        ],
        out_shape=[
            jax.ShapeDtypeStruct((nseq, nrows, HEAD_DIM), F32),
            jax.ShapeDtypeStruct((nseq, nb_pad, N_KV * tdec), F32),
        ],
        compiler_params=pltpu.CompilerParams(dimension_semantics=("parallel",), vmem_limit_bytes=VMEM_LIMIT),
        name="attn_dec_select",
    )(qbd, kc_t, vc_t, win_t, kvnew, gates_rows, slopecol, ovt, maskbd)


def _attn_dec2_kernel(pt_ref, *refs, npg, past, tdec):
    page_refs = refs[:npg]
    (qbd_ref, selrow_ref, sellast_ref, e_ref, slope_ref, knew_ref, opart_ref, gates_ref, maskbd_ref,
     o_ref, m_ref, l_ref, acc_ref) = refs[npg:]
    step = pl.program_id(1)
    nkeys = npg * PAGE
    qb = qbd_ref[0]
    nrows = qb.shape[0]
    pref = float(past + tdec)
    slope = slope_ref[...]

    @pl.when(step == 0)
    def _():
        m_ref[...] = jnp.full_like(m_ref, NEG)
        l_ref[...] = jnp.zeros_like(l_ref)
        acc_ref[...] = jnp.zeros_like(acc_ref)

    def update(s, pv_fn):
        m_old = m_ref[...]
        m_new = jnp.maximum(m_old, jnp.max(s, axis=-1, keepdims=True))
        alpha = jnp.exp2(m_old - m_new)
        p = jnp.exp2(s - m_new)
        l_ref[...] = alpha * l_ref[...] + jnp.sum(p, axis=-1, keepdims=True)
        acc_ref[...] = alpha * acc_ref[...] + pv_fn(p.astype(BF16))
        m_ref[...] = m_new

    k_t = jnp.concatenate([page_refs[j][0, :KV_COLS].astype(BF16) for j in range(npg)], axis=1)
    v_t = jnp.concatenate([page_refs[j][0, KV_COLS:].astype(BF16) for j in range(npg)], axis=1)
    kpos = step * nkeys + _iota((1, nkeys), 1)
    chosen = _dot(selrow_ref[0, 0], e_ref[...])
    s = _dot(qb, k_t) + slope * (kpos.astype(F32) - pref) + jnp.where(chosen > 0.5, 0.0, NEG)
    update(s, lambda p: _dot_nt(p, v_t))

    @pl.when(step == pl.num_programs(1) - 1)
    def _():
        tcol = past + _mod_pow2(_iota((nrows, 1), 0), tdec)
        kn = knew_ref[0]
        kposn = past + _iota((1, PAGE), 1)
        sn = _dot_nt(qb, kn[:, :KV_COLS].astype(BF16))
        ok = (sellast_ref[0, 0][:, 0:1].astype(F32) > 0.5) & (kposn <= tcol)
        sn = sn + slope * (kposn.astype(F32) - pref) + jnp.where(ok, 0.0, NEG)
        update(sn, lambda p: _dot(p, kn[:, KV_COLS:].astype(BF16)))
        o_s = _diag_blocks(acc_ref[...] / l_ref[...], maskbd_ref[...])
        o_ref[0] = opart_ref[0] + gates_ref[0][:, 1:2] * o_s


def _attn_dec2(cache_t, page_table, qbd, selrows, slopecol, knew, opart, gates_rows, maskbd, past, tdec, npg):
    nseq, npages = page_table.shape
    nstep = npages // npg
    nrows = qbd.shape[1]
    blk_per_step = npg * PAGE // SEL_BLOCK
    emat = _block_expand(blk_per_step, npg * PAGE)
    page_specs = [
        pl.BlockSpec((1, 2 * KV_COLS, PAGE), lambda b, s, pt, j=j: (pt[b, s * npg + j], 1, 0))
        for j in range(npg)
    ]
    per_seq = lambda shape: pl.BlockSpec((1,) + shape, lambda b, s, pt: (b,) + (0,) * len(shape))
    const = lambda shape: pl.BlockSpec(shape, lambda b, s, pt: (0,) * len(shape))
    return pl.pallas_call(
        functools.partial(_attn_dec2_kernel, npg=npg, past=past, tdec=tdec),
        grid_spec=pltpu.PrefetchScalarGridSpec(
            num_scalar_prefetch=1,
            grid=(nseq, nstep),
            in_specs=page_specs + [
                per_seq((nrows, KV_COLS)),
                pl.BlockSpec((1, 1, nrows, blk_per_step), lambda b, s, pt: (b, s, 0, 0)),
                pl.BlockSpec((1, 1, nrows, blk_per_step), lambda b, s, pt: (b, nstep, 0, 0)),
                const((blk_per_step, npg * PAGE)),
                const((nrows, 1)),
                per_seq((PAGE, 2 * KV_COLS)),
                per_seq((nrows, HEAD_DIM)),
                per_seq((nrows, 4)),
                const((nrows, KV_COLS)),
            ],
            out_specs=per_seq((nrows, HEAD_DIM)),
            scratch_shapes=[
                pltpu.VMEM((nrows, 1), F32),
                pltpu.VMEM((nrows, 1), F32),
                pltpu.VMEM((nrows, KV_COLS), F32),
            ],
        ),
        out_shape=jax.ShapeDtypeStruct((nseq, nrows, HEAD_DIM), F32),
        compiler_params=pltpu.CompilerParams(
            dimension_semantics=("parallel", "arbitrary"), vmem_limit_bytes=VMEM_LIMIT),
        name="attn_dec_selected",
    )(page_table, *([cache_t] * npg), qbd, selrows, selrows, emat, slopecol, knew, opart, gates_rows, maskbd)


def _s5_weights(lam_re, lam_im, log_dt, b_re, b_im, c_re, c_im, d_skip):
    hp = lax.Precision.HIGHEST
    L = S5_L
    ng, p = lam_re.shape
    c = SSM_GROUP
    ns, gs = ng // S5_SLAB, S5_SLAB
    dt = jnp.exp(log_dt)[:, None]
    lr, li = lam_re, lam_im
    mag = jnp.exp(lr * dt)
    ar = mag * jnp.cos(li * dt)
    ai = mag * jnp.sin(li * dt)
    den = lr * lr + li * li
    fr = ((ar - 1.0) * lr + ai * li) / den
    fi = (ai * lr - (ar - 1.0) * li) / den
    bt_re, bt_im = b_re.transpose(0, 2, 1), b_im.transpose(0, 2, 1)
    bbr = fr[:, None, :] * bt_re - fi[:, None, :] * bt_im
    bbi = fr[:, None, :] * bt_im + fi[:, None, :] * bt_re
    j = jnp.arange(L + 1, dtype=F32)[:, None, None]
    pmag = jnp.exp(j * (lr * dt))
    pr = pmag * jnp.cos(j * (li * dt))
    pi = pmag * jnp.sin(j * (li * dt))
    abr = pr[:, :, None, :] * bbr - pi[:, :, None, :] * bbi
    abi = pr[:, :, None, :] * bbi + pi[:, :, None, :] * bbr
    kd = (jnp.einsum('jgkp,gcp->jgkc', abr, c_re, precision=hp)
          - jnp.einsum('jgkp,gcp->jgkc', abi, c_im, precision=hp))
    kd = kd.at[0].add(d_skip.reshape(ng, 1, c) * jnp.eye(c, dtype=F32))
    kc = kd[:L].reshape(L, ns, gs * c, c)
    abc = jnp.concatenate([abr[:L], abi[:L]], axis=-1).reshape(L, ns, gs * c, 2 * p)
    cr = c_re[None] * pr[1:, :, None, :] - c_im[None] * pi[1:, :, None, :]
    ci = -(c_re[None] * pi[1:, :, None, :] + c_im[None] * pr[1:, :, None, :])
    co = jnp.stack([cr, ci], axis=1).reshape(L, 2, ns, gs, c, p)
    al = jnp.stack([pr[L].reshape(-1), pi[L].reshape(-1)])
    return kc.astype(BF16), abc.astype(BF16), co.astype(BF16), al


def _s5_expanders():
    gs, c, p = S5_SLAB, SSM_GROUP, SSM_STATE
    lane = np.arange(gs * c)
    rep_c = (np.arange(c)[:, None] == lane[None, :] % c)
    same_g = (lane[:, None] // c == lane[None, :] // c)
    st = np.arange(2 * gs * p)
    rp = np.arange(2 * p)
    rep_p = (rp[:, None] // p == st[None, :] // (gs * p)) & (rp[:, None] % p == st[None, :] % p)
    g_in = (lane[:, None] // c == (st[None, :] // p) % gs)
    as_bf = lambda a: jnp.asarray(a.astype(np.float32), dtype=BF16)
    return as_bf(rep_c), as_bf(same_g), as_bf(rep_p), as_bf(g_in)


def _chunk_lanes(u_ref):
    return jnp.concatenate([u_ref[0, :, t, :] for t in range(S5_L)], axis=1)


def _s5_local_kernel(u_ref, abc_ref, rep_p_ref, g_in_ref, pre_ref, pim_ref, wp_ref):
    for s in range(S5_L):
        blk = _dot(abc_ref[S5_L - 1 - s, 0], rep_p_ref[...]).astype(BF16) * g_in_ref[...]
        wp_ref[128 * s:128 * (s + 1), :] = blk
    r = _dot(_chunk_lanes(u_ref).astype(BF16), wp_ref[...])
    half = r.shape[1] // 2
    pre_ref[...] = r[:, :half]
    pim_ref[...] = r[:, half:]


def _s5_scan_kernel(pre_ref, pim_ref, al_ref, h0r_ref, h0i_ref, hsr_ref, hsi_ref, hfr_ref, hfi_ref):
    nchunk = pre_ref.shape[1]
    ar = al_ref[0:1, :]
    ai = al_ref[1:2, :]

    def body(k, carry):
        cr, ci = carry
        hsr_ref[:, pl.ds(k, 1), :] = cr[:, None, :]
        hsi_ref[:, pl.ds(k, 1), :] = ci[:, None, :]
        xr = pre_ref[:, pl.ds(k, 1), :][:, 0, :]
        xi = pim_ref[:, pl.ds(k, 1), :][:, 0, :]
        return ar * cr - ai * ci + xr, ar * ci + ai * cr + xi

    cr, ci = lax.fori_loop(0, nchunk, body, (h0r_ref[...], h0i_ref[...]), unroll=min(nchunk, 4))
    hfr_ref[...] = cr
    hfi_ref[...] = ci


def _s5_out_kernel(u_ref, kc_ref, co_ref, rep_c_ref, same_g_ref, hsr_ref, hsi_ref, y_ref, wt_ref, wot_ref):
    rep_c = rep_c_ref[...]
    lag = [_dot(kc_ref[dl, 0], rep_c).astype(BF16) * same_g_ref[...] for dl in range(S5_L)]
    zero = jnp.zeros((128, 128), BF16)
    for s in range(S5_L):
        for t in range(S5_L):
            wt_ref[128 * s:128 * (s + 1), 128 * t:128 * (t + 1)] = lag[t - s] if t >= s else zero
    wot_ref[...] = jnp.zeros_like(wot_ref)
    c, p = SSM_GROUP, SSM_STATE
    for t in range(S5_L):
        for r in range(2):
            for g in range(S5_SLAB):
                col = (r * S5_SLAB + g) * p
                wot_ref[128 * t + c * g:128 * t + c * (g + 1), col:col + p] = co_ref[t, r, 0, g]
    hs = jnp.concatenate([hsr_ref[...], hsi_ref[...]], axis=1).astype(BF16)
    y = _dot(_chunk_lanes(u_ref).astype(BF16), wt_ref[...]) + _dot_nt(hs, wot_ref[...])
    for t in range(S5_L):
        y_ref[0, :, t, :] = y[:, 128 * t:128 * (t + 1)]


def _s5(u3, h0r, h0i, weights, nbatch):
    kc, abc, co, al = weights
    rep_c, same_g, rep_p, g_in = _s5_expanders()
    ns, nb = u3.shape[0], u3.shape[1]
    nchunk = nb // nbatch
    gp = al.shape[1]
    lb = gp // ns
    d = S5_L * 128
    cp = pltpu.CompilerParams(dimension_semantics=("parallel",), vmem_limit_bytes=VMEM_LIMIT)
    u_spec = pl.BlockSpec((1, nb, S5_L, 128), lambda j: (j, 0, 0, 0))
    st_spec = pl.BlockSpec((nb, lb), lambda j: (0, j))
    slab = lambda a: pl.BlockSpec((a.shape[0], 1) + a.shape[2:], lambda j: (0, j) + (0,) * (a.ndim - 2))
    const = lambda a: pl.BlockSpec(a.shape, lambda j: (0,) * a.ndim)
    pre, pim = pl.pallas_call(
        _s5_local_kernel,
        grid=(ns,),
        in_specs=[u_spec, slab(abc), const(rep_p), const(g_in)],
        out_specs=[st_spec] * 2,
        out_shape=[jax.ShapeDtypeStruct((nb, gp), F32)] * 2,
        scratch_shapes=[pltpu.VMEM((d, 2 * lb), BF16)],
        compiler_params=cp,
        name="s5_local",
    )(u3, abc, rep_p, g_in)

    ls = 2 * lb
    seq3 = pl.BlockSpec((nbatch, nchunk, ls), lambda j: (0, 0, j))
    row = lambda r: pl.BlockSpec((r, ls), lambda j: (0, j))
    hsr, hsi, hfr, hfi = pl.pallas_call(
        _s5_scan_kernel,
        grid=(gp // ls,),
        in_specs=[seq3, seq3, row(2), row(nbatch), row(nbatch)],
        out_specs=[seq3, seq3, row(nbatch), row(nbatch)],
        out_shape=[jax.ShapeDtypeStruct((nbatch, nchunk, gp), F32)] * 2 + [jax.ShapeDtypeStruct((nbatch, gp), F32)] * 2,
        compiler_params=cp,
        name="s5_scan",
    )(pre.reshape(nbatch, nchunk, gp), pim.reshape(nbatch, nchunk, gp), al, h0r, h0i)

    y3 = pl.pallas_call(
        _s5_out_kernel,
        grid=(ns,),
        in_specs=[u_spec, slab(kc),
                  pl.BlockSpec(co.shape[:2] + (1,) + co.shape[3:], lambda j: (0, 0, j, 0, 0, 0)),
                  const(rep_c), const(same_g), st_spec, st_spec],
        out_specs=u_spec,
        out_shape=jax.ShapeDtypeStruct(u3.shape, F32),
        scratch_shapes=[pltpu.VMEM((d, d), BF16), pltpu.VMEM((2 * lb, d), BF16)],
        compiler_params=cp,
        name="s5_out",
    )(u3, kc, co, rep_c, same_g, hsr.reshape(nb, gp), hsi.reshape(nb, gp))
    return y3, hfr, hfi


def _merge_kernel(o_ref, y_ref, wa_ref, wg1_ref, wg2_ref, sga_ref, sgb_ref, m_ref, gy_ref):
    @pl.when(pl.program_id(1) == 0)
    def _():
        y = jnp.concatenate([y_ref[s] for s in range(y_ref.shape[0])], axis=1)
        gy_ref[...] = _gelu(y).astype(BF16)

    ya = _dot(o_ref[...], wa_ref[...])
    gy = gy_ref[...]
    yb = _dot(gy, wg1_ref[...]) * jax.nn.sigmoid(_dot(gy, wg2_ref[...]))
    m = sga_ref[...].astype(F32) * ya + sgb_ref[...].astype(F32) * yb
    m_ref[...] = m.astype(BF16)


def _merge(o, y3, w_attn_out, w_glu, sg, tm, tn=1024):
    rows, d = o.shape[0], w_attn_out.shape[1]
    nj = d // tn
    kq, ks = w_attn_out.shape[0], w_glu.shape[0]
    ns = y3.shape[0]
    return pl.pallas_call(
        _merge_kernel,
        grid=(rows // tm, nj),
        in_specs=[
            pl.BlockSpec((tm, kq), lambda i, j: (i, 0)),
            pl.BlockSpec((ns, tm, 128), lambda i, j: (0, i, 0)),
            pl.BlockSpec((kq, tn), lambda i, j: (0, j)),
            pl.BlockSpec((ks, tn), lambda i, j: (0, j)),
            pl.BlockSpec((ks, tn), lambda i, j: (0, j + nj)),
            pl.BlockSpec((tm, tn), lambda i, j: (i, j)),
            pl.BlockSpec((tm, tn), lambda i, j: (i, j + nj)),
        ],
        out_specs=pl.BlockSpec((tm, tn), lambda i, j: (i, j)),
        out_shape=jax.ShapeDtypeStruct((rows, d), BF16),
        scratch_shapes=[pltpu.VMEM((tm, ks), BF16)],
        compiler_params=pltpu.CompilerParams(
            dimension_semantics=("parallel", "arbitrary"), vmem_limit_bytes=VMEM_LIMIT),
        name="merge",
    )(o, y3, w_attn_out, w_glu, w_glu, sg, sg)


def _outproj_kernel(m_ref, x_ref, w_ref, g_ref, x1_ref, h_ref):
    x1 = x_ref[...] + _dot(m_ref[...], w_ref[...])
    x1_ref[...] = x1
    r = lax.rsqrt(jnp.mean(x1 * x1, axis=-1, keepdims=True) + EPS)
    h_ref[...] = (x1 * r * g_ref[...]).astype(BF16)


def _outproj(m, x, w_out, g_ffn, tm):
    rows, d = x.shape
    row = pl.BlockSpec((tm, d), lambda i: (i, 0))
    return pl.pallas_call(
        _outproj_kernel,
        grid=(rows // tm,),
        in_specs=[row, row, pl.BlockSpec((d, d), lambda i: (0, 0), pipeline_mode=pl.Buffered(1)),
                  pl.BlockSpec((1, d), lambda i: (0, 0))],
        out_specs=[row, row],
        out_shape=[jax.ShapeDtypeStruct((rows, d), F32), jax.ShapeDtypeStruct((rows, d), BF16)],
        compiler_params=pltpu.CompilerParams(dimension_semantics=("parallel",), vmem_limit_bytes=VMEM_LIMIT),
        name="out_proj",
    )(m, x, w_out, g_ffn.reshape(1, d))


def _ffn_kernel(h_ref, x1_ref, wg_ref, wu_ref, wd_ref, gf_ref, y_ref, *rest):
    acc_ref = rest[-1]
    f = pl.program_id(1)

    @pl.when(f == 0)
    def _():
        acc_ref[...] = jnp.zeros_like(acc_ref)

    wg, wu, wd = wg_ref[...].astype(BF16), wu_ref[...].astype(BF16), wd_ref[...].astype(BF16)
    for copy_ref, w in zip(rest[:-1], (wg, wu, wd)):
        copy_ref[...] = w
    h = h_ref[...]
    a = _dot(h, wg)
    a = (a * jax.nn.sigmoid(a)) * _dot(h, wu)
    acc_ref[...] += _dot(a.astype(BF16), wd)

    @pl.when(f == pl.num_programs(1) - 1)
    def _():
        y = x1_ref[...] + acc_ref[...]
        r = lax.rsqrt(jnp.mean(y * y, axis=-1, keepdims=True) + EPS)
        y_ref[...] = y * r * gf_ref[...]


def _ffn(h, x1, w_gate, w_up, w_down, g_final, tm, tf=512):
    rows, d = x1.shape
    dff = w_gate.shape[1]
    emit = w_gate.dtype != BF16
    assert not emit or rows == tm
    row = pl.BlockSpec((tm, d), lambda i, f: (i, 0))
    w_specs = [pl.BlockSpec((d, tf), lambda i, f: (0, f)),
               pl.BlockSpec((d, tf), lambda i, f: (0, f)),
               pl.BlockSpec((tf, d), lambda i, f: (f, 0))]
    y_shape = jax.ShapeDtypeStruct((rows, d), F32)
    copies = [jax.ShapeDtypeStruct(w.shape, BF16) for w in (w_gate, w_up, w_down)] if emit else []
    out = pl.pallas_call(
        _ffn_kernel,
        grid=(rows // tm, dff // tf),
        in_specs=[row, row] + w_specs + [pl.BlockSpec((1, d), lambda i, f: (0, 0))],
        out_specs=[row] + (w_specs if emit else []),
        out_shape=[y_shape] + copies,
        scratch_shapes=[pltpu.VMEM((tm, d), F32)],
        compiler_params=pltpu.CompilerParams(
            dimension_semantics=("parallel", "arbitrary"), vmem_limit_bytes=VMEM_LIMIT),
        name="ffn",
    )(h, x1, w_gate, w_up, w_down, g_final.reshape(1, d))
    return tuple(out) if emit else out[0]


def _alibi_slopes():
    return jnp.exp2(-8.0 * jnp.arange(1, N_HEADS + 1, dtype=F32) / N_HEADS) * LOG2E


def _slope_table(slopes):
    s1 = slopes.astype(BF16)
    r1 = slopes - s1.astype(F32)
    s2 = r1.astype(BF16)
    s3 = (r1 - s2.astype(F32)).astype(BF16)
    tab = jnp.stack([s1, s1, s2, s2, s3, s3] + [jnp.zeros_like(s1)] * 10, axis=-1)
    return tab.reshape(N_KV, GROUP, 16)


def _position_rows(pos):
    pos = np.arange(pos) if np.isscalar(pos) else np.asarray(pos)
    hi, lo = (pos // 64) * 64, pos % 64
    assert pos.max() < 64 * 256
    rows = np.stack([hi, lo, hi, lo, hi, lo] + [np.zeros_like(pos)] * 10).astype(np.float32)
    return jnp.asarray(rows, dtype=BF16)


def _layer_weights(p, l):
    w_in = _pack_w_in(p['w_in'][l], p['ssm_d'].shape[-1])
    s5 = _s5_weights(*(p[n][l] for n in ('ssm_lam_re', 'ssm_lam_im', 'ssm_log_dt', 'ssm_b_re', 'ssm_b_im',
                                          'ssm_c_re', 'ssm_c_im', 'ssm_d')))
    cast = lambda n: p[n][l].astype(BF16)
    return dict(
        g_mix=p['g_mix'][l], w_in=w_in,
        pool=p['w_cmp_pool'][l], pe=p['w_cmp_pe'][l], phi=p['w_cmp_phi'][l],
        pool_mats=_pool_matrices(p['w_cmp_pool'][l], PROMPT_PAGES_PER_STEP),
        w_attn_out=cast('w_attn_out'), w_glu=cast('w_glu'), w_out=cast('w_out'), g_ffn=p['g_ffn'][l],
        ffn_w=tuple(p[n][l] for n in ('w_gate', 'w_up', 'w_down')), s5=s5)


def _tail(x, o, y3, sg, lw, g_final, tm):
    rows = x.shape[0]
    m = _merge(o, y3.reshape(y3.shape[0], rows, 128), lw['w_attn_out'], lw['w_glu'], sg, tm)
    x1, h = _outproj(m, x, lw['w_out'], lw['g_ffn'], tm)
    return _ffn(h, x1, *lw['ffn_w'], g_final, tm)


def _feature_major_rows(kv_t, t0):
    b, f, t = kv_t.shape
    return kv_t[:, :, t0:].reshape(b, f // KV_COLS, N_KV, HEAD_DIM, t - t0).transpose(0, 4, 1, 2, 3)


def _layer_prompt(x, lw, slopes, g_out):
    b, t, d = x.shape
    rows = b * t
    assert t % (PROMPT_PAGES_PER_STEP * PAGE) == 0
    q, u3, sg, gt, kvc_t, kvw_t = _in_proj(x.reshape(rows, d), lw['g_mix'], lw['w_in'], 1024, seq_len=t)

    pooled = _pool_pages(kvc_t, lambda bi, pg: (bi, 0, pg), (), b, t // PAGE, lw['pool_mats'],
                         PROMPT_PAGES_PER_STEP)
    _, vc_t, kc_rows = _cmp_finish(pooled, lw['pool'], lw['pe'], lw['phi'])
    o = _attn_prompt(slopes, q, kc_rows, vc_t, kvc_t, kvw_t, gt, b, t)

    gp = lw['s5'][3].shape[1]
    h0 = jnp.zeros((b, gp), F32)
    y3, hr, hi = _s5(u3, h0, h0, lw['s5'], b)

    y = _tail(x.reshape(rows, d), o, y3, sg, lw, g_out, 512).reshape(b, t, d)
    n_win = min(WINDOW, t)
    ng = gp // SSM_STATE
    return (y, _feature_major_rows(kvc_t, 0), _feature_major_rows(kvw_t, t - n_win),
            hr.reshape(b, ng, SSM_STATE), hi.reshape(b, ng, SSM_STATE))


def _layer_sample(x, cache, page_table, win_buf, h_re, h_im, lw, slopes, g_out):
    b, t, d = x.shape
    rows = b * t
    npages = page_table.shape[1]
    past = npages * cache.shape[1]
    n_buf = win_buf.shape[1]
    assert cache.shape[1] == PAGE and rows % S5_L == 0 and t == S5_L
    assert past % CMP_STRIDE == 0 and t < CMP_STRIDE and past % SEL_BLOCK == 0 and t <= SEL_BLOCK
    npg = DECODE_PAGES_PER_STEP
    assert npages % npg == 0
    q, u3, sg, gt, kvc, kvw = _in_proj(x.reshape(rows, d), lw['g_mix'], lw['w_in'], rows)

    cache_t = cache.transpose(0, 2, 3, 4, 1).reshape(cache.shape[0], 4 * KV_COLS, PAGE)
    win_t = win_buf.transpose(0, 2, 3, 4, 1).reshape(b, 2 * KV_COLS, n_buf)
    pooled = _pool_pages(cache_t, lambda bi, pg, pt: (pt[bi, pg], 0, 0), (page_table,), b, npages,
                         lw['pool_mats'], npg)
    kc_t, vc_t, _ = _cmp_finish(pooled, lw['pool'], lw['pe'], lw['phi'])

    nrows = N_HEADS * t
    eye = jnp.eye(N_KV, dtype=BF16)
    q5 = q.reshape(b, t, N_KV, GROUP, HEAD_DIM).transpose(0, 2, 3, 1, 4)
    qbd = (q5[:, :, :, :, None, :] * eye[None, :, None, None, :, None]).reshape(b, nrows, KV_COLS)
    maskbd = jnp.repeat(jnp.repeat(jnp.eye(N_KV, dtype=F32), GROUP * t, axis=0), HEAD_DIM, axis=1)
    slopecol = jnp.repeat(slopes, t).reshape(nrows, 1)
    g3 = gt.reshape(b, t, N_KV, 128)[..., :3 * GROUP].reshape(b, t, N_KV, 3, GROUP)
    g3 = g3.transpose(0, 2, 4, 1, 3).reshape(b, nrows, 3)
    gates_rows = jnp.pad(g3, ((0, 0), (0, 0), (0, 1)))
    n_blk = -(-(past + t) // SEL_BLOCK)

    opart, sel_t = _attn_dec1(qbd, kc_t, vc_t, win_t, kvw.reshape(b, t, 2 * KV_COLS), gates_rows, slopecol, maskbd,
                              past, t, n_blk)
    bps = npg * PAGE // SEL_BLOCK
    nstep = npages // npg
    nb_all = (nstep + 1) * bps
    sel = jnp.pad(sel_t[:, :n_blk], ((0, 0), (0, nb_all - n_blk), (0, 0)))
    sel = sel.reshape(b, nstep + 1, bps, N_KV, 1, t).transpose(0, 1, 3, 4, 5, 2)
    selrows = jnp.broadcast_to(sel, (b, nstep + 1, N_KV, GROUP, t, bps)).reshape(b, nstep + 1, nrows, bps)
    knew = jnp.pad(kvc.reshape(b, t, 4 * KV_COLS)[:, :, 2 * KV_COLS:], ((0, 0), (0, PAGE - t), (0, 0)))
    o_rows = _attn_dec2(cache_t, page_table, qbd, selrows.astype(BF16), slopecol, knew, opart, gates_rows,
                        maskbd, past, t, npg)
    o = (o_rows.reshape(b, N_KV, GROUP, t, HEAD_DIM).transpose(0, 3, 1, 2, 4).reshape(rows, Q_COLS).astype(BF16))

    gp = h_re.shape[1] * h_re.shape[2]
    y3, hr, hi = _s5(u3, h_re.reshape(b, gp), h_im.reshape(b, gp), lw['s5'], b)

    y, *ffn_w_bf16 = _tail(x.reshape(rows, d), o, y3, sg, lw, g_out, rows)
    y = y.reshape(b, t, d)
    win_all = jnp.concatenate([win_buf, kvw.reshape(b, t, 2, N_KV, HEAD_DIM)], axis=1)
    n_keep = min(WINDOW, n_buf + t)
    return (y, kvc.reshape(b, t, 4, N_KV, HEAD_DIM), win_all[:, n_buf + t - n_keep:], hr.reshape(h_re.shape),
            hi.reshape(h_im.shape), tuple(ffn_w_bf16))


def kernel(x_prompt, x_sample, cache_kv, state_win, state_ssm_re, state_ssm_im, page_table, g_mix, w_in, w_cmp_pe, w_cmp_pool, w_cmp_phi, w_attn_out, ssm_lam_re, ssm_lam_im, ssm_log_dt, ssm_b_re, ssm_b_im, ssm_c_re, ssm_c_im, ssm_d, w_glu, w_out, g_ffn, w_gate, w_up, w_down, g_final):
    depth = g_mix.shape[0]
    assert depth == 1, "final norm is fused into the last layer's FFN; one layer supported"
    params = dict(g_mix=g_mix, w_in=w_in, w_cmp_pe=w_cmp_pe, w_cmp_pool=w_cmp_pool, w_cmp_phi=w_cmp_phi,
                  w_attn_out=w_attn_out, ssm_lam_re=ssm_lam_re, ssm_lam_im=ssm_lam_im, ssm_log_dt=ssm_log_dt,
                  ssm_b_re=ssm_b_re, ssm_b_im=ssm_b_im, ssm_c_re=ssm_c_re, ssm_c_im=ssm_c_im, ssm_d=ssm_d,
                  w_glu=w_glu, w_out=w_out, g_ffn=g_ffn, w_gate=w_gate, w_up=w_up, w_down=w_down)
    slopes = _alibi_slopes()
    outs = [[] for _ in range(8)]
    xp, xs = x_prompt, x_sample
    for l in range(depth):
        lw = _layer_weights(params, l)
        xs, kvs, wins, hrs, his, ffn_w = _layer_sample(xs, cache_kv[l], page_table, state_win[l], state_ssm_re[l],
                                                       state_ssm_im[l], lw, slopes, g_final)
        xp, kvp, winp, hrp, hip = _layer_prompt(xp, dict(lw, ffn_w=ffn_w), slopes, g_final)
        for lst, v in zip(outs, (kvp, winp, hrp, hip, kvs, wins, hrs, his)):
            lst.append(v)
    st = [jnp.stack(v) for v in outs]
    return (xp, xs, st[0], st[1], st[2], st[3], st[4], st[5], st[6], st[7])
```

```python
import functools
import math

import numpy as np
import jax
import jax.numpy as jnp
from jax import lax
from jax.experimental import pallas as pl
from jax.experimental.pallas import tpu as pltpu

F32 = jnp.float32
BF16 = jnp.bfloat16

N_HEADS = 16
HEAD_DIM = 64
N_KV = 4
GROUP = N_HEADS // N_KV
CMP_LEN = 32
CMP_STRIDE = 16
SEL_BLOCK = 64
SEL_TOP = 16
WINDOW = 512
SSM_GROUP = 16
SSM_STATE = 64
EPS = 1e-6
NEG = -1e30
FORCE_SCORE = 1e4

KV_COLS = N_KV * HEAD_DIM
Q_COLS = N_HEADS * HEAD_DIM
PAGE = 128
CMP_PER_PAGE = PAGE // CMP_STRIDE
PROMPT_PAGES_PER_STEP = 16
DECODE_PAGES_PER_STEP = 32
S5_L = 8
S5_SLAB = 128 // SSM_GROUP
VMEM_LIMIT = 56 * 1024 * 1024
LOG2E = 1.0 / math.log(2.0)
Q_SCALE = HEAD_DIM ** -0.5 * LOG2E


def _dot(a, b):
    return jnp.dot(a, b, preferred_element_type=F32)


def _dot_nt(a, b):
    return lax.dot_general(a, b, (((1,), (1,)), ((), ())), preferred_element_type=F32)


def _split(a):
    hi = a.astype(BF16)
    lo = (a - hi.astype(F32)).astype(BF16)
    return hi, lo


def _dot3(a, b):
    ah, al = _split(a)
    bh, bl = _split(b)
    return _dot(ah, bh) + _dot(ah, bl) + _dot(al, bh)


def _dot_nt_hl(w_bf16, a):
    ah, al = _split(a)
    return _dot_nt(w_bf16, ah) + _dot_nt(w_bf16, al)


def _gelu(x):
    return 0.5 * x * (1.0 + jnp.tanh(math.sqrt(2.0 / math.pi) * (x + 0.044715 * (x * x * x))))


def _iota(shape, dim):
    return lax.broadcasted_iota(jnp.int32, shape, dim)


def _log2(n):
    assert n > 0 and n & (n - 1) == 0, n
    return n.bit_length() - 1


def _div_pow2(x, n):
    return jnp.right_shift(x, _log2(n))


def _mod_pow2(x, n):
    return jnp.bitwise_and(x, (1 << _log2(n)) - 1)


def _softmax2_rows(s):
    m = jnp.max(s, axis=-1, keepdims=True)
    e = jnp.exp2(s - m)
    return e / jnp.sum(e, axis=-1, keepdims=True)


IN_TN = 512
_Q_T, _U_T, _MG_T, _KV_T = 2, 2, 8, 3
_KVC_T = 2
_U_0 = _Q_T
_MG_0 = _U_0 + _U_T
_GT_0 = _MG_0 + _MG_T
_KV_0 = _GT_0 + 1
IN_TILES = _KV_0 + _KV_T


def _inproj_kernel(x_ref, g_ref, wq_ref, wu_ref, wmg_ref, wgt_ref, wkv_ref, q_ref, u_ref, sg_ref, gt_ref, kvc_ref,
                   kvw_ref, xn_ref, *, kv_feature_major):
    j = pl.program_id(1)

    @pl.when(j == 0)
    def _():
        x = x_ref[...]
        r = lax.rsqrt(jnp.mean(x * x, axis=-1, keepdims=True) + EPS)
        xn_ref[...] = (x * r * g_ref[...]).astype(BF16)

    @pl.when(j < _U_0)
    def _():
        q_ref[...] = (_dot(xn_ref[...], wq_ref[...]) * Q_SCALE).astype(BF16)

    @pl.when((j >= _U_0) & (j < _MG_0))
    def _():
        z = _dot(xn_ref[...], wu_ref[...])
        for s in range(IN_TN // 128):
            u_ref[s] = z[:, 128 * s:128 * (s + 1)].reshape(u_ref.shape[1:])

    @pl.when((j >= _MG_0) & (j < _GT_0))
    def _():
        sg_ref[...] = jax.nn.sigmoid(_dot(xn_ref[...], wmg_ref[...])).astype(BF16)

    @pl.when(j == _GT_0)
    def _():
        gt_ref[...] = jax.nn.sigmoid(_dot(xn_ref[...], wgt_ref[...]))

    @pl.when(j >= _KV_0)
    def _():
        if kv_feature_major:
            z = _dot_nt(wkv_ref[...], xn_ref[...])
        else:
            z = _dot(xn_ref[...], wkv_ref[...])

        @pl.when(j < _KV_0 + _KVC_T)
        def _():
            kvc_ref[...] = z.reshape(kvc_ref.shape)

        @pl.when(j >= _KV_0 + _KVC_T)
        def _():
            kvw_ref[...] = z.reshape(kvw_ref.shape)


def _in_proj(x, g_mix, w, tm, seq_len=None):
    rows, d = x.shape
    feature_major = seq_len is not None

    def col(lo, n):
        return lambda i, j: (i, jnp.clip(j - lo, 0, n - 1))

    if feature_major:
        tiles_per_seq = seq_len // tm
        kvc_spec = pl.BlockSpec((1, IN_TN, tm), lambda i, j: (i // tiles_per_seq, jnp.clip(j - _KV_0, 0, _KVC_T - 1),
                                                              i % tiles_per_seq))
        kvw_spec = pl.BlockSpec((1, IN_TN, tm), lambda i, j: (i // tiles_per_seq, 0, i % tiles_per_seq))
        kvc_shape = jax.ShapeDtypeStruct((rows // seq_len, _KVC_T * IN_TN, seq_len), F32)
        kvw_shape = jax.ShapeDtypeStruct((rows // seq_len, IN_TN, seq_len), F32)
        wkv, wkv_spec = w['kv_t'], pl.BlockSpec((IN_TN, d), lambda i, j: (jnp.clip(j - _KV_0, 0, _KV_T - 1), 0))
    else:
        kvc_spec = pl.BlockSpec((tm, IN_TN), col(_KV_0, _KVC_T))
        kvw_spec = pl.BlockSpec((tm, IN_TN), lambda i, j: (i, 0))
        kvc_shape = jax.ShapeDtypeStruct((rows, _KVC_T * IN_TN), F32)
        kvw_shape = jax.ShapeDtypeStruct((rows, IN_TN), F32)
        wkv, wkv_spec = w['kv'], pl.BlockSpec((d, IN_TN), lambda i, j: (0, jnp.clip(j - _KV_0, 0, _KV_T - 1)))
    n_slab = IN_TN // 128
    return pl.pallas_call(
        functools.partial(_inproj_kernel, kv_feature_major=feature_major),
        grid=(rows // tm, IN_TILES),
        in_specs=[
            pl.BlockSpec((tm, d), lambda i, j: (i, 0), pipeline_mode=pl.Buffered(1)),
            pl.BlockSpec((1, d), lambda i, j: (0, 0)),
            pl.BlockSpec((d, IN_TN), lambda i, j: (0, jnp.clip(j, 0, _Q_T - 1))),
            pl.BlockSpec((d, IN_TN), lambda i, j: (0, jnp.clip(j - _U_0, 0, _U_T - 1))),
            pl.BlockSpec((d, IN_TN), lambda i, j: (0, jnp.clip(j - _MG_0, 0, _MG_T - 1))),
            pl.BlockSpec((d, IN_TN), lambda i, j: (0, 0), pipeline_mode=pl.Buffered(1)),
            wkv_spec,
        ],
        out_specs=[
            pl.BlockSpec((tm, IN_TN), col(0, _Q_T)),
            pl.BlockSpec((n_slab, tm // S5_L, S5_L, 128), lambda i, j: (jnp.clip(j - _U_0, 0, _U_T - 1), i, 0, 0)),
            pl.BlockSpec((tm, IN_TN), col(_MG_0, _MG_T)),
            pl.BlockSpec((tm, IN_TN), lambda i, j: (i, 0)),
            kvc_spec, kvw_spec,
        ],
        out_shape=[
            jax.ShapeDtypeStruct((rows, _Q_T * IN_TN), BF16),
            jax.ShapeDtypeStruct((_U_T * n_slab, rows // S5_L, S5_L, 128), F32),
            jax.ShapeDtypeStruct((rows, _MG_T * IN_TN), BF16),
            jax.ShapeDtypeStruct((rows, IN_TN), F32),
            kvc_shape, kvw_shape,
        ],
        scratch_shapes=[pltpu.VMEM((tm, d), BF16)],
        compiler_params=pltpu.CompilerParams(
            dimension_semantics=("parallel", "arbitrary"), vmem_limit_bytes=VMEM_LIMIT),
        name="in_proj",
    )(x, g_mix.reshape(1, d), w['q'], w['u'], w['mg'], w['gt'], wkv)


def _pack_w_in(w_in, ssm_width):
    cuts = np.cumsum([Q_COLS, 6 * KV_COLS, 3 * N_HEADS, ssm_width]).tolist()
    wq, wkv, wgt, wu, wmg = jnp.split(w_in, cuts, axis=1)
    assert wq.shape[1] == _Q_T * IN_TN and wkv.shape[1] == _KV_T * IN_TN
    assert wu.shape[1] == _U_T * IN_TN and wmg.shape[1] == _MG_T * IN_TN and IN_TN == N_KV * 128
    d = w_in.shape[0]
    wgt = wgt.reshape(d, 3, N_KV, GROUP).transpose(0, 2, 1, 3).reshape(d, N_KV, 3 * GROUP)
    wgt = jnp.pad(wgt, ((0, 0), (0, 0), (0, 128 - 3 * GROUP))).reshape(d, IN_TN)
    return dict(q=wq.astype(BF16), u=wu.astype(BF16), mg=wmg.astype(BF16), gt=wgt.astype(BF16),
                kv=wkv.astype(BF16), kv_t=wkv.T.astype(BF16))


def _pool_kernel(*refs, npg, n_tables):
    refs = refs[n_tables:]
    prev_ref = refs[0]
    page_refs = refs[1:npg + 1]
    mk_ref, mv_ref, mprev_ref, out_ref = refs[npg + 1:]
    pages = [prev_ref[0].astype(BF16)] + [page_refs[j][0].astype(BF16) for j in range(npg)]
    mpg = mk_ref.shape[0] // PAGE
    nblk = mk_ref.shape[1]
    for seg in range(npg // mpg):
        x = jnp.concatenate(pages[1 + seg * mpg:1 + (seg + 1) * mpg], axis=1)
        for half, m_ref in ((0, mk_ref), (1, mv_ref)):
            rows = slice(half * KV_COLS, (half + 1) * KV_COLS)
            head = _dot(pages[seg * mpg][rows], mprev_ref[half])
            if seg == 0:
                head = jnp.where(pl.program_id(1) > 0, head, 0.0)
            out_ref[0, rows, seg * nblk:(seg + 1) * nblk] = _dot(x[rows], m_ref[...]) + head


def _pool_matrices(pool, npg):
    nblk = npg * CMP_PER_PAGE
    cols = []
    for r in range(CMP_PER_PAGE):
        start = CMP_STRIDE * (r - 1)
        lo, hi = max(start, 0), min(start + CMP_LEN, PAGE)
        cols.append(jnp.pad(pool[:, lo - start:hi - start], ((0, 0), (lo, PAGE - hi))))
    body = jnp.stack(cols, axis=-1)
    carry = jnp.pad(pool[:, :CMP_STRIDE], ((0, 0), (PAGE - CMP_STRIDE, 0)))
    first = jnp.asarray(np.arange(CMP_PER_PAGE) == 0, F32)
    same = jnp.eye(npg, dtype=F32)[None, :, None, :, None]
    nxt = jnp.eye(npg, k=1, dtype=F32)[None, :, None, :, None]
    m = same * body[:, None, :, None, :] + nxt * (carry[:, None, :, None, None] * first)
    m = m.reshape(2, npg * PAGE, nblk).astype(BF16)
    mprev = (carry[:, :, None] * jnp.asarray(np.arange(nblk) == 0, F32)).astype(BF16)
    return m[0], m[1], mprev


def _pool_pages(pages, page_index, tables, nseq, npages, mats, npg):
    assert npg % (mats[0].shape[0] // PAGE) == 0
    nstep = npages // npg
    nblk = npg * CMP_PER_PAGE
    mk, mv, mprev = mats

    def spec(off):
        return pl.BlockSpec((1, 2 * KV_COLS, PAGE),
                            lambda b, s, *t: page_index(b, jnp.maximum(s * npg + off, 0), *t))

    const = lambda shape: pl.BlockSpec(shape, lambda b, s, *t: (0,) * len(shape))
    return pl.pallas_call(
        functools.partial(_pool_kernel, npg=npg, n_tables=len(tables)),
        grid_spec=pltpu.PrefetchScalarGridSpec(
            num_scalar_prefetch=len(tables),
            grid=(nseq, nstep),
            in_specs=[spec(j) for j in range(-1, npg)] + [const(mk.shape), const(mv.shape), const(mprev.shape)],
            out_specs=pl.BlockSpec((1, 2 * KV_COLS, nblk), lambda b, s, *t: (b, 0, s)),
        ),
        out_shape=jax.ShapeDtypeStruct((nseq, 2 * KV_COLS, npages * CMP_PER_PAGE), F32),
        compiler_params=pltpu.CompilerParams(
            dimension_semantics=("parallel", "arbitrary"), vmem_limit_bytes=VMEM_LIMIT),
        name="cmp_pool",
    )(*tables, *([pages] * (npg + 1)), mk, mv, mprev)


def _cmp_finish_kernel(pooled_ref, poolw_ref, pe_ref, phi_ref, kc_ref, vc_ref, kcr_ref):
    bias = jnp.sum(poolw_ref[...] * pe_ref[...], axis=1, keepdims=True)
    x = _gelu(pooled_ref[0] + bias)
    kc = _dot3(phi_ref[0], x[:KV_COLS])
    kc_ref[0] = kc.astype(BF16)
    kcr_ref[0] = kc.T.astype(BF16)
    vc_ref[0] = _dot3(phi_ref[1], x[KV_COLS:]).astype(BF16)


def _cmp_finish(pooled, pool, pe, phi):
    nseq, _, n = pooled.shape
    poolw = jnp.concatenate([jnp.broadcast_to(pool[s][None, :], (KV_COLS, CMP_LEN)) for s in range(2)], axis=0)
    pe_t = jnp.concatenate([jnp.tile(pe[s].T, (N_KV, 1)) for s in range(2)], axis=0)
    phi_bd = jnp.stack([jnp.kron(jnp.eye(N_KV, dtype=F32), phi[s].T) for s in range(2)])
    spec = pl.BlockSpec((1, KV_COLS, n), lambda b: (b, 0, 0))
    return pl.pallas_call(
        _cmp_finish_kernel,
        grid=(nseq,),
        in_specs=[
            pl.BlockSpec((1, 2 * KV_COLS, n), lambda b: (b, 0, 0)),
            pl.BlockSpec((2 * KV_COLS, CMP_LEN), lambda b: (0, 0)),
            pl.BlockSpec((2 * KV_COLS, CMP_LEN), lambda b: (0, 0)),
            pl.BlockSpec((2, KV_COLS, KV_COLS), lambda b: (0, 0, 0)),
        ],
        out_specs=[spec, spec, pl.BlockSpec((1, n, KV_COLS), lambda b: (b, 0, 0))],
        out_shape=[jax.ShapeDtypeStruct((nseq, KV_COLS, n), BF16)] * 2
        + [jax.ShapeDtypeStruct((nseq, n, KV_COLS), BF16)],
        compiler_params=pltpu.CompilerParams(dimension_semantics=("parallel",)),
        name="cmp_finish",
    )(pooled, poolw, pe_t, phi_bd)


ATT_TQ = 512
ATT_TK = 512
NBLK_PAD = 128
POS_ROWS = 16
AUG = HEAD_DIM + POS_ROWS


def _attn_prompt_kernel(q_ref, kca_ref, vc_ref, ks_ref, vs_ref, kw_ref, vw_ref, gt_ref, ovt_ref, ktab_ref,
                        stab_ref, o_ref, ksa_ref, vsa_ref, kwa_ref, vwa_ref, qat_ref, m_ref, acc_ref,
                        *, tq, tk, n_sel, n_top):
    qi = pl.program_id(2)
    q0 = qi * tq
    seq = ksa_ref.shape[0]

    @pl.when(qi == 0)
    def _():
        zero = jnp.zeros((128 - HEAD_DIM, seq), F32)
        ones = jnp.where(_iota((POS_ROWS, seq), 0) == 0, 1.0, 0.0).astype(BF16)
        for src, dst, lo in ((ks_ref, ksa_ref, NBLK_PAD), (kw_ref, kwa_ref, 0)):
            k_rows = jnp.concatenate([src[0], zero], axis=0).T[:, :HEAD_DIM].astype(BF16)
            dst[:, lo:] = jnp.concatenate([k_rows, ktab_ref[:, NBLK_PAD:]], axis=1)
        ksa_ref[:, 0:NBLK_PAD] = ktab_ref[:, 0:NBLK_PAD]
        for src, dst in ((vs_ref, vsa_ref), (vw_ref, vwa_ref)):
            dst[0:HEAD_DIM] = src[0].astype(BF16)
            dst[HEAD_DIM:] = ones

    q_t = q_ref[...].astype(F32).T
    q_t = jnp.concatenate([q_t[n * HEAD_DIM:(n + 1) * HEAD_DIM] for n in range(GROUP)], axis=1).astype(BF16)
    stab = stab_ref[0]
    sp_t = jnp.concatenate([jnp.broadcast_to(stab[:, n:n + 1], (POS_ROWS, tq)) for n in range(GROUP)], axis=1)
    qat_ref[NBLK_PAD:NBLK_PAD + HEAD_DIM] = q_t
    qat_ref[NBLK_PAD + HEAD_DIM:] = sp_t.astype(BF16)
    trow = q0 + _iota((1, tq), 1)
    lanes = [slice(n * tq, (n + 1) * tq) for n in range(GROUP)]

    ncmp = kca_ref.shape[2]
    nidx = _iota((ncmp, 1), 0)
    heads = lambda a: jnp.concatenate([a] * GROUP, axis=1)
    valid = (nidx >= 1) & (CMP_STRIDE * nidx + (CMP_STRIDE - 1) <= trow)
    valid = heads(jnp.where(valid, 1.0, 0.0)) > 0.5
    s = jnp.where(valid, _dot(kca_ref[0, 0], qat_ref[NBLK_PAD:]), NEG)
    e = jnp.exp2(s - jnp.max(s, axis=0, keepdims=True))
    pc = jnp.where(valid, e / jnp.sum(e, axis=0, keepdims=True), 0.0)
    o_c = _dot(vc_ref[0], pc.astype(BF16))

    ph, plo = _split((pc[:, lanes[0]] + pc[:, lanes[1]]) + (pc[:, lanes[2]] + pc[:, lanes[3]]))
    imp = (_dot(ovt_ref[...], ph) + _dot(ovt_ref[...], plo))[0:n_sel]
    blk = _iota((n_sel, 1), 0)
    cur = _div_pow2(q0 + _iota((1, tq), 1), SEL_BLOCK)
    forced = (blk == 0) | (blk == cur) | (blk == cur - 1)
    imp = jnp.where(forced, FORCE_SCORE, imp)
    imp = jnp.where(blk <= cur, imp, -1.0)
    cnt = jnp.zeros((n_sel, tq), F32)
    for j in range(n_sel):
        vj = imp[j:j + 1, :]
        tie = jnp.where(blk > j, 1.0, 0.0)
        cnt = cnt + jnp.where(vj > imp, 1.0, jnp.where(vj == imp, tie, 0.0))
    mask_t = jnp.where((cnt < n_top) & (blk <= cur), 0.0, NEG)
    mask_t = jnp.concatenate([mask_t, jnp.full((NBLK_PAD - n_sel, tq), NEG, F32)], axis=0).astype(BF16)
    qat_ref[0:NBLK_PAD] = heads(mask_t)

    m_ref[...] = jnp.full_like(m_ref, NEG)
    acc_ref[...] = jnp.zeros_like(acc_ref)

    def scores(kt):
        return _dot(ksa_ref[pl.ds(pl.multiple_of(kt * tk, tk), tk), :], qat_ref[...])

    def update(kt, s):
        m_old = m_ref[...]
        m_new = jnp.maximum(m_old, jnp.max(s, axis=0, keepdims=True))
        p = jnp.exp2(s - m_new).astype(BF16)
        pv = _dot(vsa_ref[:, pl.ds(pl.multiple_of(kt * tk, tk), tk)], p)
        acc_ref[...] = jnp.exp2(m_old - m_new) * acc_ref[...] + pv
        m_ref[...] = m_new

    def pair(j, carry):
        s0, s1 = scores(2 * j), scores(2 * j + 1)
        update(2 * j, s0)
        update(2 * j + 1, s1)
        return carry

    lax.fori_loop(0, qi // 2, pair, 0)

    @pl.when(qi % 2 == 1)
    def _():
        update(qi - 1, scores(qi - 1))

    span = WINDOW + tq
    w0 = pl.multiple_of(jnp.maximum(q0 - WINDOW, 0), tq)
    s_diag = scores(qi) + heads(jnp.where(q0 + _iota((tk, 1), 0) <= trow, 0.0, NEG))
    dist = trow - (w0 + _iota((span, 1), 0))
    s_win = (_dot(kwa_ref[pl.ds(w0, span), :], qat_ref[NBLK_PAD:])
             + heads(jnp.where((dist >= 0) & (dist < WINDOW), 0.0, NEG)))
    update(qi, s_diag)
    acc = acc_ref[...]
    o_s = acc[:HEAD_DIM] / acc[HEAD_DIM:HEAD_DIM + 1]
    ow = _dot(vwa_ref[:, pl.ds(w0, span)], jnp.exp2(s_win - jnp.max(s_win, axis=0, keepdims=True)).astype(BF16))
    o_w = ow[:HEAD_DIM] / ow[HEAD_DIM:HEAD_DIM + 1]

    gt_t = gt_ref[...].T
    outs = []
    for n in range(GROUP):
        outs.append(gt_t[n:n + 1] * o_c[:, lanes[n]] + gt_t[GROUP + n:GROUP + n + 1] * o_s[:, lanes[n]]
                    + gt_t[2 * GROUP + n:2 * GROUP + n + 1] * o_w[:, lanes[n]])
    o_ref[...] = jnp.concatenate(outs, axis=0).T.astype(BF16)


def _overlap_t(n_blk_rows, n_cmp):
    s = np.arange(n_blk_rows)[:, None]
    n = np.arange(n_cmp)[None, :]
    r = SEL_BLOCK // CMP_STRIDE
    return jnp.asarray(((n >= r * s) & (n <= r * s + r)).astype(np.float32), dtype=BF16)


def _block_expand(n_blk_rows, n_keys):
    s = np.arange(n_blk_rows)[:, None]
    k = np.arange(n_keys)[None, :]
    return jnp.asarray((k // SEL_BLOCK == s).astype(np.float32), dtype=BF16)


def _attn_prompt(slopes, q, kc_rows, vc_t, kvc_t, kvw_t, gates, nbatch, seq):
    ncmp = kc_rows.shape[1]
    tq, tk = ATT_TQ, ATT_TK
    assert tk == tq and seq % tk == 0 and WINDOW % tq == 0 and WINDOW + tq <= seq
    nq = seq // tq
    n_sel = seq // SEL_BLOCK
    ovt = _overlap_t(NBLK_PAD, ncmp)
    ktab = jnp.concatenate([_block_expand(NBLK_PAD, seq), _position_rows(seq)], axis=0).T
    cmp_end = CMP_STRIDE * np.arange(ncmp) + (CMP_STRIDE - 1)
    kca = jnp.concatenate([kc_rows.reshape(nbatch, ncmp, N_KV, HEAD_DIM).transpose(0, 2, 1, 3),
                           jnp.broadcast_to(_position_rows(cmp_end).T, (nbatch, N_KV, ncmp, POS_ROWS))], axis=-1)
    stab = _slope_table(slopes).astype(F32).transpose(0, 2, 1)
    per_slot = KV_COLS // HEAD_DIM

    def kv_spec(slot):
        return pl.BlockSpec((1, HEAD_DIM, seq), lambda b, g, i: (b, slot * per_slot + g, 0))

    q_spec = pl.BlockSpec((tq, GROUP * HEAD_DIM), lambda b, g, i: (b * nq + i, g))
    return pl.pallas_call(
        functools.partial(_attn_prompt_kernel, tq=tq, tk=tk, n_sel=n_sel, n_top=min(SEL_TOP, n_sel)),
        grid=(nbatch, N_KV, nq),
        in_specs=[
            q_spec,
            pl.BlockSpec((1, 1, ncmp, AUG), lambda b, g, i: (b, g, 0, 0)),
            pl.BlockSpec((1, HEAD_DIM, ncmp), lambda b, g, i: (b, g, 0)),
            kv_spec(2), kv_spec(3), kv_spec(0), kv_spec(1),
            pl.BlockSpec((tq, 128), lambda b, g, i: (b * nq + i, g)),
            pl.BlockSpec((NBLK_PAD, ncmp), lambda b, g, i: (0, 0)),
            pl.BlockSpec((seq, NBLK_PAD + POS_ROWS), lambda b, g, i: (0, 0)),
            pl.BlockSpec((1, POS_ROWS, GROUP), lambda b, g, i: (g, 0, 0)),
        ],
        out_specs=q_spec,
        out_shape=jax.ShapeDtypeStruct(q.shape, BF16),
        scratch_shapes=[
            pltpu.VMEM((seq, NBLK_PAD + AUG), BF16),
            pltpu.VMEM((AUG, seq), BF16),
            pltpu.VMEM((seq, AUG), BF16),
            pltpu.VMEM((AUG, seq), BF16),
            pltpu.VMEM((NBLK_PAD + AUG, GROUP * tq), BF16),
            pltpu.VMEM((1, GROUP * tq), F32),
            pltpu.VMEM((AUG, GROUP * tq), F32),
        ],
        compiler_params=pltpu.CompilerParams(
            dimension_semantics=("parallel", "parallel", "arbitrary"), vmem_limit_bytes=VMEM_LIMIT),
        name="attn_prompt",
    )(q, kca, vc_t, kvc_t, kvc_t, kvw_t, kvw_t, gates, ovt, ktab, stab)


def _diag_blocks(o, maskbd):
    o = o * maskbd
    return (o[:, 0:64] + o[:, 64:128]) + (o[:, 128:192] + o[:, 192:256])


def _attn_dec1_kernel(qbd_ref, kc_ref, vc_ref, win_ref, kvnew_ref, gates_ref, slope_ref, ovt_ref,
                      maskbd_ref, opart_ref, sel_ref, *, past, tdec, n_blk, n_top):
    qb = qbd_ref[0]
    nrows = qb.shape[0]
    tcol = past + _mod_pow2(_iota((nrows, 1), 0), tdec)
    pref = float(past + tdec)
    slope = slope_ref[...]
    maskbd = maskbd_ref[...]

    ncmp = kc_ref.shape[2]
    nrow = _iota((1, ncmp), 1)
    cend = CMP_STRIDE * nrow + (CMP_STRIDE - 1)
    valid = (nrow >= 1) & (cend <= tcol)
    s = _dot(qb, kc_ref[0])
    s = jnp.where(valid, s + slope * (cend.astype(F32) - pref), NEG)
    p = jnp.where(valid, _softmax2_rows(s), 0.0)
    o_c = _diag_blocks(_dot_nt(p.astype(BF16), vc_ref[0]), maskbd)

    per_g = GROUP * tdec
    psum = jnp.concatenate(
        [sum(p[gi * per_g + n * tdec: gi * per_g + (n + 1) * tdec] for n in range(GROUP)) for gi in range(N_KV)],
        axis=0)
    imp = _dot_nt_hl(ovt_ref[...], psum)
    nb_pad = imp.shape[0]
    blk = _iota((nb_pad, 1), 0)
    cur = _div_pow2(past + _mod_pow2(_iota((1, N_KV * tdec), 1), tdec), SEL_BLOCK)
    forced = (blk == 0) | (blk == cur) | (blk == cur - 1)
    imp = jnp.where(forced, FORCE_SCORE, imp)
    imp = jnp.where(blk <= cur, imp, -1.0)
    imp = jnp.where(blk < n_blk, imp, -2.0)
    sel = jnp.zeros(imp.shape, F32)
    for _ in range(n_top):
        mx = jnp.max(imp, axis=0, keepdims=True)
        first = jnp.min(jnp.where(imp == mx, blk, nb_pad), axis=0, keepdims=True)
        pick = blk == first
        sel = jnp.where(pick, 1.0, sel)
        imp = jnp.where(pick, -jnp.inf, imp)
    sel_ref[0] = sel

    nbuf = win_ref.shape[2]
    knew = kvnew_ref[0]
    zpad = jnp.zeros((PAGE - tdec, KV_COLS), F32)
    kn = jnp.concatenate([knew[:, :KV_COLS], zpad], axis=0).astype(BF16)
    vn = jnp.concatenate([knew[:, KV_COLS:], zpad], axis=0).astype(BF16)
    sw = jnp.concatenate([_dot(qb, win_ref[0, :KV_COLS].astype(BF16)), _dot_nt(qb, kn)], axis=1)
    idx = _iota((1, nbuf + PAGE), 1)
    kposw = past - nbuf + idx
    dist = tcol - kposw
    validw = (dist >= 0) & (dist < WINDOW) & (kposw >= 0) & (idx < nbuf + tdec)
    sw = jnp.where(validw, sw + slope * (kposw.astype(F32) - pref), NEG)
    pw = _softmax2_rows(sw).astype(BF16)
    o_w = _dot_nt(pw[:, :nbuf], win_ref[0, KV_COLS:].astype(BF16)) + _dot(pw[:, nbuf:], vn)
    o_w = _diag_blocks(o_w, maskbd)

    gates = gates_ref[0]
    opart_ref[0] = gates[:, 0:1] * o_c + gates[:, 2:3] * o_w


def _attn_dec1(qbd, kc_t, vc_t, win_t, kvnew, gates_rows, slopecol, maskbd, past, tdec, n_blk):
    nseq, nrows, _ = qbd.shape
    ncmp = kc_t.shape[2]
    nb_pad = -(-n_blk // 8) * 8
    ovt = _overlap_t(nb_pad, ncmp)
    nbuf = win_t.shape[2]
    return pl.pallas_call(
        functools.partial(_attn_dec1_kernel, past=past, tdec=tdec, n_blk=n_blk, n_top=min(SEL_TOP, n_blk)),
        grid=(nseq,),
        in_specs=[
            pl.BlockSpec((1, nrows, KV_COLS), lambda b: (b, 0, 0)),
            pl.BlockSpec((1, KV_COLS, ncmp), lambda b: (b, 0, 0)),
            pl.BlockSpec((1, KV_COLS, ncmp), lambda b: (b, 0, 0)),
            pl.BlockSpec((1, 2 * KV_COLS, nbuf), lambda b: (b, 0, 0)),
            pl.BlockSpec((1, tdec, 2 * KV_COLS), lambda b: (b, 0, 0)),
            pl.BlockSpec((1, nrows, 4), lambda b: (b, 0, 0)),
            pl.BlockSpec((nrows, 1), lambda b: (0, 0)),
            pl.BlockSpec((nb_pad, ncmp), lambda b: (0, 0)),
            pl.BlockSpec((nrows, KV_COLS), lambda b: (0, 0)),
        ],
        out_specs=[
            pl.BlockSpec((1, nrows, HEAD_DIM), lambda b: (b, 0, 0)),
            pl.BlockSpec((1, nb_pad, N_KV * tdec), lambda b: (b, 0, 0)),
        ],
        out_shape=[
            jax.ShapeDtypeStruct((nseq, nrows, HEAD_DIM), F32),
            jax.ShapeDtypeStruct((nseq, nb_pad, N_KV * tdec), F32),
        ],
        compiler_params=pltpu.CompilerParams(dimension_semantics=("parallel",), vmem_limit_bytes=VMEM_LIMIT),
        name="attn_dec_select",
    )(qbd, kc_t, vc_t, win_t, kvnew, gates_rows, slopecol, ovt, maskbd)


def _attn_dec2_kernel(pt_ref, *refs, npg, past, tdec):
    page_refs = refs[:npg]
    (qbd_ref, selrow_ref, sellast_ref, e_ref, slope_ref, knew_ref, opart_ref, gates_ref, maskbd_ref,
     o_ref, m_ref, l_ref, acc_ref) = refs[npg:]
    step = pl.program_id(1)
    nkeys = npg * PAGE
    qb = qbd_ref[0]
    nrows = qb.shape[0]
    pref = float(past + tdec)
    slope = slope_ref[...]

    @pl.when(step == 0)
    def _():
        m_ref[...] = jnp.full_like(m_ref, NEG)
        l_ref[...] = jnp.zeros_like(l_ref)
        acc_ref[...] = jnp.zeros_like(acc_ref)

    def update(s, pv_fn):
        m_old = m_ref[...]
        m_new = jnp.maximum(m_old, jnp.max(s, axis=-1, keepdims=True))
        alpha = jnp.exp2(m_old - m_new)
        p = jnp.exp2(s - m_new)
        l_ref[...] = alpha * l_ref[...] + jnp.sum(p, axis=-1, keepdims=True)
        acc_ref[...] = alpha * acc_ref[...] + pv_fn(p.astype(BF16))
        m_ref[...] = m_new

    k_t = jnp.concatenate([page_refs[j][0, :KV_COLS].astype(BF16) for j in range(npg)], axis=1)
    v_t = jnp.concatenate([page_refs[j][0, KV_COLS:].astype(BF16) for j in range(npg)], axis=1)
    kpos = step * nkeys + _iota((1, nkeys), 1)
    chosen = _dot(selrow_ref[0, 0], e_ref[...])
    s = _dot(qb, k_t) + slope * (kpos.astype(F32) - pref) + jnp.where(chosen > 0.5, 0.0, NEG)
    update(s, lambda p: _dot_nt(p, v_t))

    @pl.when(step == pl.num_programs(1) - 1)
    def _():
        tcol = past + _mod_pow2(_iota((nrows, 1), 0), tdec)
        kn = knew_ref[0]
        kposn = past + _iota((1, PAGE), 1)
        sn = _dot_nt(qb, kn[:, :KV_COLS].astype(BF16))
        ok = (sellast_ref[0, 0][:, 0:1].astype(F32) > 0.5) & (kposn <= tcol)
        sn = sn + slope * (kposn.astype(F32) - pref) + jnp.where(ok, 0.0, NEG)
        update(sn, lambda p: _dot(p, kn[:, KV_COLS:].astype(BF16)))
        o_s = _diag_blocks(acc_ref[...] / l_ref[...], maskbd_ref[...])
        o_ref[0] = opart_ref[0] + gates_ref[0][:, 1:2] * o_s


def _attn_dec2(cache_t, page_table, qbd, selrows, slopecol, knew, opart, gates_rows, maskbd, past, tdec, npg):
    nseq, npages = page_table.shape
    nstep = npages // npg
    nrows = qbd.shape[1]
    blk_per_step = npg * PAGE // SEL_BLOCK
    emat = _block_expand(blk_per_step, npg * PAGE)
    page_specs = [
        pl.BlockSpec((1, 2 * KV_COLS, PAGE), lambda b, s, pt, j=j: (pt[b, s * npg + j], 1, 0))
        for j in range(npg)
    ]
    per_seq = lambda shape: pl.BlockSpec((1,) + shape, lambda b, s, pt: (b,) + (0,) * len(shape))
    const = lambda shape: pl.BlockSpec(shape, lambda b, s, pt: (0,) * len(shape))
    return pl.pallas_call(
        functools.partial(_attn_dec2_kernel, npg=npg, past=past, tdec=tdec),
        grid_spec=pltpu.PrefetchScalarGridSpec(
            num_scalar_prefetch=1,
            grid=(nseq, nstep),
            in_specs=page_specs + [
                per_seq((nrows, KV_COLS)),
                pl.BlockSpec((1, 1, nrows, blk_per_step), lambda b, s, pt: (b, s, 0, 0)),
                pl.BlockSpec((1, 1, nrows, blk_per_step), lambda b, s, pt: (b, nstep, 0, 0)),
                const((blk_per_step, npg * PAGE)),
                const((nrows, 1)),
                per_seq((PAGE, 2 * KV_COLS)),
                per_seq((nrows, HEAD_DIM)),
                per_seq((nrows, 4)),
                const((nrows, KV_COLS)),
            ],
            out_specs=per_seq((nrows, HEAD_DIM)),
            scratch_shapes=[
                pltpu.VMEM((nrows, 1), F32),
                pltpu.VMEM((nrows, 1), F32),
                pltpu.VMEM((nrows, KV_COLS), F32),
            ],
        ),
        out_shape=jax.ShapeDtypeStruct((nseq, nrows, HEAD_DIM), F32),
        compiler_params=pltpu.CompilerParams(
            dimension_semantics=("parallel", "arbitrary"), vmem_limit_bytes=VMEM_LIMIT),
        name="attn_dec_selected",
    )(page_table, *([cache_t] * npg), qbd, selrows, selrows, emat, slopecol, knew, opart, gates_rows, maskbd)


def _s5_weights(lam_re, lam_im, log_dt, b_re, b_im, c_re, c_im, d_skip):
    hp = lax.Precision.HIGHEST
    L = S5_L
    ng, p = lam_re.shape
    c = SSM_GROUP
    ns, gs = ng // S5_SLAB, S5_SLAB
    dt = jnp.exp(log_dt)[:, None]
    lr, li = lam_re, lam_im
    mag = jnp.exp(lr * dt)
    ar = mag * jnp.cos(li * dt)
    ai = mag * jnp.sin(li * dt)
    den = lr * lr + li * li
    fr = ((ar - 1.0) * lr + ai * li) / den
    fi = (ai * lr - (ar - 1.0) * li) / den
    bt_re, bt_im = b_re.transpose(0, 2, 1), b_im.transpose(0, 2, 1)
    bbr = fr[:, None, :] * bt_re - fi[:, None, :] * bt_im
    bbi = fr[:, None, :] * bt_im + fi[:, None, :] * bt_re
    j = jnp.arange(L + 1, dtype=F32)[:, None, None]
    pmag = jnp.exp(j * (lr * dt))
    pr = pmag * jnp.cos(j * (li * dt))
    pi = pmag * jnp.sin(j * (li * dt))
    abr = pr[:, :, None, :] * bbr - pi[:, :, None, :] * bbi
    abi = pr[:, :, None, :] * bbi + pi[:, :, None, :] * bbr
    kd = (jnp.einsum('jgkp,gcp->jgkc', abr, c_re, precision=hp)
          - jnp.einsum('jgkp,gcp->jgkc', abi, c_im, precision=hp))
    kd = kd.at[0].add(d_skip.reshape(ng, 1, c) * jnp.eye(c, dtype=F32))
    kc = kd[:L].reshape(L, ns, gs * c, c)
    abc = jnp.concatenate([abr[:L], abi[:L]], axis=-1).reshape(L, ns, gs * c, 2 * p)
    cr = c_re[None] * pr[1:, :, None, :] - c_im[None] * pi[1:, :, None, :]
    ci = -(c_re[None] * pi[1:, :, None, :] + c_im[None] * pr[1:, :, None, :])
    co = jnp.stack([cr, ci], axis=1).reshape(L, 2, ns, gs, c, p)
    al = jnp.stack([pr[L].reshape(-1), pi[L].reshape(-1)])
    return kc.astype(BF16), abc.astype(BF16), co.astype(BF16), al


def _s5_expanders():
    gs, c, p = S5_SLAB, SSM_GROUP, SSM_STATE
    lane = np.arange(gs * c)
    rep_c = (np.arange(c)[:, None] == lane[None, :] % c)
    same_g = (lane[:, None] // c == lane[None, :] // c)
    st = np.arange(2 * gs * p)
    rp = np.arange(2 * p)
    rep_p = (rp[:, None] // p == st[None, :] // (gs * p)) & (rp[:, None] % p == st[None, :] % p)
    g_in = (lane[:, None] // c == (st[None, :] // p) % gs)
    as_bf = lambda a: jnp.asarray(a.astype(np.float32), dtype=BF16)
    return as_bf(rep_c), as_bf(same_g), as_bf(rep_p), as_bf(g_in)


def _chunk_lanes(u_ref):
    return jnp.concatenate([u_ref[0, :, t, :] for t in range(S5_L)], axis=1)


def _s5_local_kernel(u_ref, abc_ref, rep_p_ref, g_in_ref, pre_ref, pim_ref, wp_ref):
    for s in range(S5_L):
        blk = _dot(abc_ref[S5_L - 1 - s, 0], rep_p_ref[...]).astype(BF16) * g_in_ref[...]
        wp_ref[128 * s:128 * (s + 1), :] = blk
    r = _dot(_chunk_lanes(u_ref).astype(BF16), wp_ref[...])
    half = r.shape[1] // 2
    pre_ref[...] = r[:, :half]
    pim_ref[...] = r[:, half:]


def _s5_scan_kernel(pre_ref, pim_ref, al_ref, h0r_ref, h0i_ref, hsr_ref, hsi_ref, hfr_ref, hfi_ref):
    nchunk = pre_ref.shape[1]
    ar = al_ref[0:1, :]
    ai = al_ref[1:2, :]

    def body(k, carry):
        cr, ci = carry
        hsr_ref[:, pl.ds(k, 1), :] = cr[:, None, :]
        hsi_ref[:, pl.ds(k, 1), :] = ci[:, None, :]
        xr = pre_ref[:, pl.ds(k, 1), :][:, 0, :]
        xi = pim_ref[:, pl.ds(k, 1), :][:, 0, :]
        return ar * cr - ai * ci + xr, ar * ci + ai * cr + xi

    cr, ci = lax.fori_loop(0, nchunk, body, (h0r_ref[...], h0i_ref[...]), unroll=min(nchunk, 4))
    hfr_ref[...] = cr
    hfi_ref[...] = ci


def _s5_out_kernel(u_ref, kc_ref, co_ref, rep_c_ref, same_g_ref, hsr_ref, hsi_ref, y_ref, wt_ref, wot_ref):
    rep_c = rep_c_ref[...]
    lag = [_dot(kc_ref[dl, 0], rep_c).astype(BF16) * same_g_ref[...] for dl in range(S5_L)]
    zero = jnp.zeros((128, 128), BF16)
    for s in range(S5_L):
        for t in range(S5_L):
            wt_ref[128 * s:128 * (s + 1), 128 * t:128 * (t + 1)] = lag[t - s] if t >= s else zero
    wot_ref[...] = jnp.zeros_like(wot_ref)
    c, p = SSM_GROUP, SSM_STATE
    for t in range(S5_L):
        for r in range(2):
            for g in range(S5_SLAB):
                col = (r * S5_SLAB + g) * p
                wot_ref[128 * t + c * g:128 * t + c * (g + 1), col:col + p] = co_ref[t, r, 0, g]
    hs = jnp.concatenate([hsr_ref[...], hsi_ref[...]], axis=1).astype(BF16)
    y = _dot(_chunk_lanes(u_ref).astype(BF16), wt_ref[...]) + _dot_nt(hs, wot_ref[...])
    for t in range(S5_L):
        y_ref[0, :, t, :] = y[:, 128 * t:128 * (t + 1)]


def _s5(u3, h0r, h0i, weights, nbatch):
    kc, abc, co, al = weights
    rep_c, same_g, rep_p, g_in = _s5_expanders()
    ns, nb = u3.shape[0], u3.shape[1]
    nchunk = nb // nbatch
    gp = al.shape[1]
    lb = gp // ns
    d = S5_L * 128
    cp = pltpu.CompilerParams(dimension_semantics=("parallel",), vmem_limit_bytes=VMEM_LIMIT)
    u_spec = pl.BlockSpec((1, nb, S5_L, 128), lambda j: (j, 0, 0, 0))
    st_spec = pl.BlockSpec((nb, lb), lambda j: (0, j))
    slab = lambda a: pl.BlockSpec((a.shape[0], 1) + a.shape[2:], lambda j: (0, j) + (0,) * (a.ndim - 2))
    const = lambda a: pl.BlockSpec(a.shape, lambda j: (0,) * a.ndim)
    pre, pim = pl.pallas_call(
        _s5_local_kernel,
        grid=(ns,),
        in_specs=[u_spec, slab(abc), const(rep_p), const(g_in)],
        out_specs=[st_spec] * 2,
        out_shape=[jax.ShapeDtypeStruct((nb, gp), F32)] * 2,
        scratch_shapes=[pltpu.VMEM((d, 2 * lb), BF16)],
        compiler_params=cp,
        name="s5_local",
    )(u3, abc, rep_p, g_in)

    ls = 2 * lb
    seq3 = pl.BlockSpec((nbatch, nchunk, ls), lambda j: (0, 0, j))
    row = lambda r: pl.BlockSpec((r, ls), lambda j: (0, j))
    hsr, hsi, hfr, hfi = pl.pallas_call(
        _s5_scan_kernel,
        grid=(gp // ls,),
        in_specs=[seq3, seq3, row(2), row(nbatch), row(nbatch)],
        out_specs=[seq3, seq3, row(nbatch), row(nbatch)],
        out_shape=[jax.ShapeDtypeStruct((nbatch, nchunk, gp), F32)] * 2 + [jax.ShapeDtypeStruct((nbatch, gp), F32)] * 2,
        compiler_params=cp,
        name="s5_scan",
    )(pre.reshape(nbatch, nchunk, gp), pim.reshape(nbatch, nchunk, gp), al, h0r, h0i)

    y3 = pl.pallas_call(
        _s5_out_kernel,
        grid=(ns,),
        in_specs=[u_spec, slab(kc),
                  pl.BlockSpec(co.shape[:2] + (1,) + co.shape[3:], lambda j: (0, 0, j, 0, 0, 0)),
                  const(rep_c), const(same_g), st_spec, st_spec],
        out_specs=u_spec,
        out_shape=jax.ShapeDtypeStruct(u3.shape, F32),
        scratch_shapes=[pltpu.VMEM((d, d), BF16), pltpu.VMEM((2 * lb, d), BF16)],
        compiler_params=cp,
        name="s5_out",
    )(u3, kc, co, rep_c, same_g, hsr.reshape(nb, gp), hsi.reshape(nb, gp))
    return y3, hfr, hfi


def _merge_kernel(o_ref, y_ref, wa_ref, wg1_ref, wg2_ref, sga_ref, sgb_ref, m_ref, gy_ref):
    @pl.when(pl.program_id(1) == 0)
    def _():
        y = jnp.concatenate([y_ref[s] for s in range(y_ref.shape[0])], axis=1)
        gy_ref[...] = _gelu(y).astype(BF16)

    ya = _dot(o_ref[...], wa_ref[...])
    gy = gy_ref[...]
    yb = _dot(gy, wg1_ref[...]) * jax.nn.sigmoid(_dot(gy, wg2_ref[...]))
    m = sga_ref[...].astype(F32) * ya + sgb_ref[...].astype(F32) * yb
    m_ref[...] = m.astype(BF16)


def _merge(o, y3, w_attn_out, w_glu, sg, tm, tn=1024):
    rows, d = o.shape[0], w_attn_out.shape[1]
    nj = d // tn
    kq, ks = w_attn_out.shape[0], w_glu.shape[0]
    ns = y3.shape[0]
    return pl.pallas_call(
        _merge_kernel,
        grid=(rows // tm, nj),
        in_specs=[
            pl.BlockSpec((tm, kq), lambda i, j: (i, 0)),
            pl.BlockSpec((ns, tm, 128), lambda i, j: (0, i, 0)),
            pl.BlockSpec((kq, tn), lambda i, j: (0, j)),
            pl.BlockSpec((ks, tn), lambda i, j: (0, j)),
            pl.BlockSpec((ks, tn), lambda i, j: (0, j + nj)),
            pl.BlockSpec((tm, tn), lambda i, j: (i, j)),
            pl.BlockSpec((tm, tn), lambda i, j: (i, j + nj)),
        ],
        out_specs=pl.BlockSpec((tm, tn), lambda i, j: (i, j)),
        out_shape=jax.ShapeDtypeStruct((rows, d), BF16),
        scratch_shapes=[pltpu.VMEM((tm, ks), BF16)],
        compiler_params=pltpu.CompilerParams(
            dimension_semantics=("parallel", "arbitrary"), vmem_limit_bytes=VMEM_LIMIT),
        name="merge",
    )(o, y3, w_attn_out, w_glu, w_glu, sg, sg)


def _outproj_kernel(m_ref, x_ref, w_ref, g_ref, x1_ref, h_ref):
    x1 = x_ref[...] + _dot(m_ref[...], w_ref[...])
    x1_ref[...] = x1
    r = lax.rsqrt(jnp.mean(x1 * x1, axis=-1, keepdims=True) + EPS)
    h_ref[...] = (x1 * r * g_ref[...]).astype(BF16)


def _outproj(m, x, w_out, g_ffn, tm):
    rows, d = x.shape
    row = pl.BlockSpec((tm, d), lambda i: (i, 0))
    return pl.pallas_call(
        _outproj_kernel,
        grid=(rows // tm,),
        in_specs=[row, row, pl.BlockSpec((d, d), lambda i: (0, 0), pipeline_mode=pl.Buffered(1)),
                  pl.BlockSpec((1, d), lambda i: (0, 0))],
        out_specs=[row, row],
        out_shape=[jax.ShapeDtypeStruct((rows, d), F32), jax.ShapeDtypeStruct((rows, d), BF16)],
        compiler_params=pltpu.CompilerParams(dimension_semantics=("parallel",), vmem_limit_bytes=VMEM_LIMIT),
        name="out_proj",
    )(m, x, w_out, g_ffn.reshape(1, d))


def _ffn_kernel(h_ref, x1_ref, wg_ref, wu_ref, wd_ref, gf_ref, y_ref, *rest):
    acc_ref = rest[-1]
    f = pl.program_id(1)

    @pl.when(f == 0)
    def _():
        acc_ref[...] = jnp.zeros_like(acc_ref)

    wg, wu, wd = wg_ref[...].astype(BF16), wu_ref[...].astype(BF16), wd_ref[...].astype(BF16)
    for copy_ref, w in zip(rest[:-1], (wg, wu, wd)):
        copy_ref[...] = w
    h = h_ref[...]
    a = _dot(h, wg)
    a = (a * jax.nn.sigmoid(a)) * _dot(h, wu)
    acc_ref[...] += _dot(a.astype(BF16), wd)

    @pl.when(f == pl.num_programs(1) - 1)
    def _():
        y = x1_ref[...] + acc_ref[...]
        r = lax.rsqrt(jnp.mean(y * y, axis=-1, keepdims=True) + EPS)
        y_ref[...] = y * r * gf_ref[...]


def _ffn(h, x1, w_gate, w_up, w_down, g_final, tm, tf=512):
    rows, d = x1.shape
    dff = w_gate.shape[1]
    emit = w_gate.dtype != BF16
    assert not emit or rows == tm
    row = pl.BlockSpec((tm, d), lambda i, f: (i, 0))
    w_specs = [pl.BlockSpec((d, tf), lambda i, f: (0, f)),
               pl.BlockSpec((d, tf), lambda i, f: (0, f)),
               pl.BlockSpec((tf, d), lambda i, f: (f, 0))]
    y_shape = jax.ShapeDtypeStruct((rows, d), F32)
    copies = [jax.ShapeDtypeStruct(w.shape, BF16) for w in (w_gate, w_up, w_down)] if emit else []
    out = pl.pallas_call(
        _ffn_kernel,
        grid=(rows // tm, dff // tf),
        in_specs=[row, row] + w_specs + [pl.BlockSpec((1, d), lambda i, f: (0, 0))],
        out_specs=[row] + (w_specs if emit else []),
        out_shape=[y_shape] + copies,
        scratch_shapes=[pltpu.VMEM((tm, d), F32)],
        compiler_params=pltpu.CompilerParams(
            dimension_semantics=("parallel", "arbitrary"), vmem_limit_bytes=VMEM_LIMIT),
        name="ffn",
    )(h, x1, w_gate, w_up, w_down, g_final.reshape(1, d))
    return tuple(out) if emit else out[0]


def _alibi_slopes():
    return jnp.exp2(-8.0 * jnp.arange(1, N_HEADS + 1, dtype=F32) / N_HEADS) * LOG2E


def _slope_table(slopes):
    s1 = slopes.astype(BF16)
    r1 = slopes - s1.astype(F32)
    s2 = r1.astype(BF16)
    s3 = (r1 - s2.astype(F32)).astype(BF16)
    tab = jnp.stack([s1, s1, s2, s2, s3, s3] + [jnp.zeros_like(s1)] * 10, axis=-1)
    return tab.reshape(N_KV, GROUP, 16)


def _position_rows(pos):
    pos = np.arange(pos) if np.isscalar(pos) else np.asarray(pos)
    hi, lo = (pos // 64) * 64, pos % 64
    assert pos.max() < 64 * 256
    rows = np.stack([hi, lo, hi, lo, hi, lo] + [np.zeros_like(pos)] * 10).astype(np.float32)
    return jnp.asarray(rows, dtype=BF16)


def _layer_weights(p, l):
    w_in = _pack_w_in(p['w_in'][l], p['ssm_d'].shape[-1])
    s5 = _s5_weights(*(p[n][l] for n in ('ssm_lam_re', 'ssm_lam_im', 'ssm_log_dt', 'ssm_b_re', 'ssm_b_im',
                                          'ssm_c_re', 'ssm_c_im', 'ssm_d')))
    cast = lambda n: p[n][l].astype(BF16)
    return dict(
        g_mix=p['g_mix'][l], w_in=w_in,
        pool=p['w_cmp_pool'][l], pe=p['w_cmp_pe'][l], phi=p['w_cmp_phi'][l],
        pool_mats=_pool_matrices(p['w_cmp_pool'][l], PROMPT_PAGES_PER_STEP),
        w_attn_out=cast('w_attn_out'), w_glu=cast('w_glu'), w_out=cast('w_out'), g_ffn=p['g_ffn'][l],
        ffn_w=tuple(p[n][l] for n in ('w_gate', 'w_up', 'w_down')), s5=s5)


def _tail(x, o, y3, sg, lw, g_final, tm):
    rows = x.shape[0]
    m = _merge(o, y3.reshape(y3.shape[0], rows, 128), lw['w_attn_out'], lw['w_glu'], sg, tm)
    x1, h = _outproj(m, x, lw['w_out'], lw['g_ffn'], tm)
    return _ffn(h, x1, *lw['ffn_w'], g_final, tm)


def _feature_major_rows(kv_t, t0):
    b, f, t = kv_t.shape
    return kv_t[:, :, t0:].reshape(b, f // KV_COLS, N_KV, HEAD_DIM, t - t0).transpose(0, 4, 1, 2, 3)


def _layer_prompt(x, lw, slopes, g_out):
    b, t, d = x.shape
    rows = b * t
    assert t % (PROMPT_PAGES_PER_STEP * PAGE) == 0
    q, u3, sg, gt, kvc_t, kvw_t = _in_proj(x.reshape(rows, d), lw['g_mix'], lw['w_in'], 1024, seq_len=t)

    pooled = _pool_pages(kvc_t, lambda bi, pg: (bi, 0, pg), (), b, t // PAGE, lw['pool_mats'],
                         PROMPT_PAGES_PER_STEP)
    _, vc_t, kc_rows = _cmp_finish(pooled, lw['pool'], lw['pe'], lw['phi'])
    o = _attn_prompt(slopes, q, kc_rows, vc_t, kvc_t, kvw_t, gt, b, t)

    gp = lw['s5'][3].shape[1]
    h0 = jnp.zeros((b, gp), F32)
    y3, hr, hi = _s5(u3, h0, h0, lw['s5'], b)

    y = _tail(x.reshape(rows, d), o, y3, sg, lw, g_out, 512).reshape(b, t, d)
    n_win = min(WINDOW, t)
    ng = gp // SSM_STATE
    return (y, _feature_major_rows(kvc_t, 0), _feature_major_rows(kvw_t, t - n_win),
            hr.reshape(b, ng, SSM_STATE), hi.reshape(b, ng, SSM_STATE))


def _layer_sample(x, cache, page_table, win_buf, h_re, h_im, lw, slopes, g_out):
    b, t, d = x.shape
    rows = b * t
    npages = page_table.shape[1]
    past = npages * cache.shape[1]
    n_buf = win_buf.shape[1]
    assert cache.shape[1] == PAGE and rows % S5_L == 0 and t == S5_L
    assert past % CMP_STRIDE == 0 and t < CMP_STRIDE and past % SEL_BLOCK == 0 and t <= SEL_BLOCK
    npg = DECODE_PAGES_PER_STEP
    assert npages % npg == 0
    q, u3, sg, gt, kvc, kvw = _in_proj(x.reshape(rows, d), lw['g_mix'], lw['w_in'], rows)

    cache_t = cache.transpose(0, 2, 3, 4, 1).reshape(cache.shape[0], 4 * KV_COLS, PAGE)
    win_t = win_buf.transpose(0, 2, 3, 4, 1).reshape(b, 2 * KV_COLS, n_buf)
    pooled = _pool_pages(cache_t, lambda bi, pg, pt: (pt[bi, pg], 0, 0), (page_table,), b, npages,
                         lw['pool_mats'], npg)
    kc_t, vc_t, _ = _cmp_finish(pooled, lw['pool'], lw['pe'], lw['phi'])

    nrows = N_HEADS * t
    eye = jnp.eye(N_KV, dtype=BF16)
    q5 = q.reshape(b, t, N_KV, GROUP, HEAD_DIM).transpose(0, 2, 3, 1, 4)
    qbd = (q5[:, :, :, :, None, :] * eye[None, :, None, None, :, None]).reshape(b, nrows, KV_COLS)
    maskbd = jnp.repeat(jnp.repeat(jnp.eye(N_KV, dtype=F32), GROUP * t, axis=0), HEAD_DIM, axis=1)
    slopecol = jnp.repeat(slopes, t).reshape(nrows, 1)
    g3 = gt.reshape(b, t, N_KV, 128)[..., :3 * GROUP].reshape(b, t, N_KV, 3, GROUP)
    g3 = g3.transpose(0, 2, 4, 1, 3).reshape(b, nrows, 3)
    gates_rows = jnp.pad(g3, ((0, 0), (0, 0), (0, 1)))
    n_blk = -(-(past + t) // SEL_BLOCK)

    opart, sel_t = _attn_dec1(qbd, kc_t, vc_t, win_t, kvw.reshape(b, t, 2 * KV_COLS), gates_rows, slopecol, maskbd,
                              past, t, n_blk)
    bps = npg * PAGE // SEL_BLOCK
    nstep = npages // npg
    nb_all = (nstep + 1) * bps
    sel = jnp.pad(sel_t[:, :n_blk], ((0, 0), (0, nb_all - n_blk), (0, 0)))
    sel = sel.reshape(b, nstep + 1, bps, N_KV, 1, t).transpose(0, 1, 3, 4, 5, 2)
    selrows = jnp.broadcast_to(sel, (b, nstep + 1, N_KV, GROUP, t, bps)).reshape(b, nstep + 1, nrows, bps)
    knew = jnp.pad(kvc.reshape(b, t, 4 * KV_COLS)[:, :, 2 * KV_COLS:], ((0, 0), (0, PAGE - t), (0, 0)))
    o_rows = _attn_dec2(cache_t, page_table, qbd, selrows.astype(BF16), slopecol, knew, opart, gates_rows,
                        maskbd, past, t, npg)
    o = (o_rows.reshape(b, N_KV, GROUP, t, HEAD_DIM).transpose(0, 3, 1, 2, 4).reshape(rows, Q_COLS).astype(BF16))

    gp = h_re.shape[1] * h_re.shape[2]
    y3, hr, hi = _s5(u3, h_re.reshape(b, gp), h_im.reshape(b, gp), lw['s5'], b)

    y, *ffn_w_bf16 = _tail(x.reshape(rows, d), o, y3, sg, lw, g_out, rows)
    y = y.reshape(b, t, d)
    win_all = jnp.concatenate([win_buf, kvw.reshape(b, t, 2, N_KV, HEAD_DIM)], axis=1)
    n_keep = min(WINDOW, n_buf + t)
    return (y, kvc.reshape(b, t, 4, N_KV, HEAD_DIM), win_all[:, n_buf + t - n_keep:], hr.reshape(h_re.shape),
            hi.reshape(h_im.shape), tuple(ffn_w_bf16))


def kernel(x_prompt, x_sample, cache_kv, state_win, state_ssm_re, state_ssm_im, page_table, g_mix, w_in, w_cmp_pe, w_cmp_pool, w_cmp_phi, w_attn_out, ssm_lam_re, ssm_lam_im, ssm_log_dt, ssm_b_re, ssm_b_im, ssm_c_re, ssm_c_im, ssm_d, w_glu, w_out, g_ffn, w_gate, w_up, w_down, g_final):
    depth = g_mix.shape[0]
    assert depth == 1, "final norm is fused into the last layer's FFN; one layer supported"
    params = dict(g_mix=g_mix, w_in=w_in, w_cmp_pe=w_cmp_pe, w_cmp_pool=w_cmp_pool, w_cmp_phi=w_cmp_phi,
                  w_attn_out=w_attn_out, ssm_lam_re=ssm_lam_re, ssm_lam_im=ssm_lam_im, ssm_log_dt=ssm_log_dt,
                  ssm_b_re=ssm_b_re, ssm_b_im=ssm_b_im, ssm_c_re=ssm_c_re, ssm_c_im=ssm_c_im, ssm_d=ssm_d,
                  w_glu=w_glu, w_out=w_out, g_ffn=g_ffn, w_gate=w_gate, w_up=w_up, w_down=w_down)
    slopes = _alibi_slopes()
    outs = [[] for _ in range(8)]
    xp, xs = x_prompt, x_sample
    for l in range(depth):
        lw = _layer_weights(params, l)
        xs, kvs, wins, hrs, his, ffn_w = _layer_sample(xs, cache_kv[l], page_table, state_win[l], state_ssm_re[l],
                                                       state_ssm_im[l], lw, slopes, g_final)
        xp, kvp, winp, hrp, hip = _layer_prompt(xp, dict(lw, ffn_w=ffn_w), slopes, g_final)
        for lst, v in zip(outs, (kvp, winp, hrp, hip, kvs, wins, hrs, his)):
            lst.append(v)
    st = [jnp.stack(v) for v in outs]
    return (xp, xs, st[0], st[1], st[2], st[3], st[4], st[5], st[6], st[7])
```

```python
import functools
import math

import numpy as np
import jax
import jax.numpy as jnp
from jax import lax
from jax.experimental import pallas as pl
from jax.experimental.pallas import tpu as pltpu

F32 = jnp.float32
BF16 = jnp.bfloat16

N_HEADS = 16
HEAD_DIM = 64
N_KV = 4
GROUP = N_HEADS // N_KV
CMP_LEN = 32
CMP_STRIDE = 16
SEL_BLOCK = 64
SEL_TOP = 16
WINDOW = 512
SSM_GROUP = 16
SSM_STATE = 64
EPS = 1e-6
NEG = -1e30
FORCE_SCORE = 1e4

KV_COLS = N_KV * HEAD_DIM
Q_COLS = N_HEADS * HEAD_DIM
PAGE = 128
CMP_PER_PAGE = PAGE // CMP_STRIDE
PROMPT_PAGES_PER_STEP = 16
DECODE_PAGES_PER_STEP = 32
S5_L = 8
S5_SLAB = 128 // SSM_GROUP
VMEM_LIMIT = 56 * 1024 * 1024
LOG2E = 1.0 / math.log(2.0)
Q_SCALE = HEAD_DIM ** -0.5 * LOG2E


def _dot(a, b):
    return jnp.dot(a, b, preferred_element_type=F32)


def _dot_nt(a, b):
    return lax.dot_general(a, b, (((1,), (1,)), ((), ())), preferred_element_type=F32)


def _split(a):
    hi = a.astype(BF16)
    lo = (a - hi.astype(F32)).astype(BF16)
    return hi, lo


def _dot3(a, b):
    ah, al = _split(a)
    bh, bl = _split(b)
    return _dot(ah, bh) + _dot(ah, bl) + _dot(al, bh)


def _dot_nt_hl(w_bf16, a):
    ah, al = _split(a)
    return _dot_nt(w_bf16, ah) + _dot_nt(w_bf16, al)


def _gelu(x):
    return 0.5 * x * (1.0 + jnp.tanh(math.sqrt(2.0 / math.pi) * (x + 0.044715 * (x * x * x))))


def _iota(shape, dim):
    return lax.broadcasted_iota(jnp.int32, shape, dim)


def _log2(n):
    assert n > 0 and n & (n - 1) == 0, n
    return n.bit_length() - 1


def _div_pow2(x, n):
    return jnp.right_shift(x, _log2(n))


def _mod_pow2(x, n):
    return jnp.bitwise_and(x, (1 << _log2(n)) - 1)


def _softmax2_rows(s):
    m = jnp.max(s, axis=-1, keepdims=True)
    e = jnp.exp2(s - m)
    return e / jnp.sum(e, axis=-1, keepdims=True)


IN_TN = 512
_Q_T, _U_T, _MG_T, _KV_T = 2, 2, 8, 3
_KVC_T = 2
_U_0 = _Q_T
_MG_0 = _U_0 + _U_T
_GT_0 = _MG_0 + _MG_T
_KV_0 = _GT_0 + 1
IN_TILES = _KV_0 + _KV_T


def _inproj_kernel(x_ref, g_ref, wq_ref, wu_ref, wmg_ref, wgt_ref, wkv_ref, q_ref, u_ref, sg_ref, gt_ref, kvc_ref,
                   kvw_ref, xn_ref, *, kv_feature_major):
    j = pl.program_id(1)

    @pl.when(j == 0)
    def _():
        x = x_ref[...]
        r = lax.rsqrt(jnp.mean(x * x, axis=-1, keepdims=True) + EPS)
        xn_ref[...] = (x * r * g_ref[...]).astype(BF16)

    @pl.when(j < _U_0)
    def _():
        q_ref[...] = (_dot(xn_ref[...], wq_ref[...]) * Q_SCALE).astype(BF16)

    @pl.when((j >= _U_0) & (j < _MG_0))
    def _():
        z = _dot(xn_ref[...], wu_ref[...])
        for s in range(IN_TN // 128):
            u_ref[s] = z[:, 128 * s:128 * (s + 1)].reshape(u_ref.shape[1:])

    @pl.when((j >= _MG_0) & (j < _GT_0))
    def _():
        sg_ref[...] = jax.nn.sigmoid(_dot(xn_ref[...], wmg_ref[...])).astype(BF16)

    @pl.when(j == _GT_0)
    def _():
        gt_ref[...] = jax.nn.sigmoid(_dot(xn_ref[...], wgt_ref[...]))

    @pl.when(j >= _KV_0)
    def _():
        if kv_feature_major:
            z = _dot_nt(wkv_ref[...], xn_ref[...])
        else:
            z = _dot(xn_ref[...], wkv_ref[...])

        @pl.when(j < _KV_0 + _KVC_T)
        def _():
            kvc_ref[...] = z.reshape(kvc_ref.shape)

        @pl.when(j >= _KV_0 + _KVC_T)
        def _():
            kvw_ref[...] = z.reshape(kvw_ref.shape)


def _in_proj(x, g_mix, w, tm, seq_len=None):
    rows, d = x.shape
    feature_major = seq_len is not None

    def col(lo, n):
        return lambda i, j: (i, jnp.clip(j - lo, 0, n - 1))

    if feature_major:
        tiles_per_seq = seq_len // tm
        kvc_spec = pl.BlockSpec((1, IN_TN, tm), lambda i, j: (i // tiles_per_seq, jnp.clip(j - _KV_0, 0, _KVC_T - 1),
                                                              i % tiles_per_seq))
        kvw_spec = pl.BlockSpec((1, IN_TN, tm), lambda i, j: (i // tiles_per_seq, 0, i % tiles_per_seq))
        kvc_shape = jax.ShapeDtypeStruct((rows // seq_len, _KVC_T * IN_TN, seq_len), F32)
        kvw_shape = jax.ShapeDtypeStruct((rows // seq_len, IN_TN, seq_len), F32)
        wkv, wkv_spec = w['kv_t'], pl.BlockSpec((IN_TN, d), lambda i, j: (jnp.clip(j - _KV_0, 0, _KV_T - 1), 0))
    else:
        kvc_spec = pl.BlockSpec((tm, IN_TN), col(_KV_0, _KVC_T))
        kvw_spec = pl.BlockSpec((tm, IN_TN), lambda i, j: (i, 0))
        kvc_shape = jax.ShapeDtypeStruct((rows, _KVC_T * IN_TN), F32)
        kvw_shape = jax.ShapeDtypeStruct((rows, IN_TN), F32)
        wkv, wkv_spec = w['kv'], pl.BlockSpec((d, IN_TN), lambda i, j: (0, jnp.clip(j - _KV_0, 0, _KV_T - 1)))
    n_slab = IN_TN // 128
    return pl.pallas_call(
        functools.partial(_inproj_kernel, kv_feature_major=feature_major),
        grid=(rows // tm, IN_TILES),
        in_specs=[
            pl.BlockSpec((tm, d), lambda i, j: (i, 0), pipeline_mode=pl.Buffered(1)),
            pl.BlockSpec((1, d), lambda i, j: (0, 0)),
            pl.BlockSpec((d, IN_TN), lambda i, j: (0, jnp.clip(j, 0, _Q_T - 1))),
            pl.BlockSpec((d, IN_TN), lambda i, j: (0, jnp.clip(j - _U_0, 0, _U_T - 1))),
            pl.BlockSpec((d, IN_TN), lambda i, j: (0, jnp.clip(j - _MG_0, 0, _MG_T - 1))),
            pl.BlockSpec((d, IN_TN), lambda i, j: (0, 0), pipeline_mode=pl.Buffered(1)),
            wkv_spec,
        ],
        out_specs=[
            pl.BlockSpec((tm, IN_TN), col(0, _Q_T)),
            pl.BlockSpec((n_slab, tm // S5_L, S5_L, 128), lambda i, j: (jnp.clip(j - _U_0, 0, _U_T - 1), i, 0, 0)),
            pl.BlockSpec((tm, IN_TN), col(_MG_0, _MG_T)),
            pl.BlockSpec((tm, IN_TN), lambda i, j: (i, 0)),
            kvc_spec, kvw_spec,
        ],
        out_shape=[
            jax.ShapeDtypeStruct((rows, _Q_T * IN_TN), BF16),
            jax.ShapeDtypeStruct((_U_T * n_slab, rows // S5_L, S5_L, 128), F32),
            jax.ShapeDtypeStruct((rows, _MG_T * IN_TN), BF16),
            jax.ShapeDtypeStruct((rows, IN_TN), F32),
            kvc_shape, kvw_shape,
        ],
        scratch_shapes=[pltpu.VMEM((tm, d), BF16)],
        compiler_params=pltpu.CompilerParams(
            dimension_semantics=("parallel", "arbitrary"), vmem_limit_bytes=VMEM_LIMIT),
        name="in_proj",
    )(x, g_mix.reshape(1, d), w['q'], w['u'], w['mg'], w['gt'], wkv)


def _pack_w_in(w_in, ssm_width):
    cuts = np.cumsum([Q_COLS, 6 * KV_COLS, 3 * N_HEADS, ssm_width]).tolist()
    wq, wkv, wgt, wu, wmg = jnp.split(w_in, cuts, axis=1)
    assert wq.shape[1] == _Q_T * IN_TN and wkv.shape[1] == _KV_T * IN_TN
    assert wu.shape[1] == _U_T * IN_TN and wmg.shape[1] == _MG_T * IN_TN and IN_TN == N_KV * 128
    d = w_in.shape[0]
    wgt = wgt.reshape(d, 3, N_KV, GROUP).transpose(0, 2, 1, 3).reshape(d, N_KV, 3 * GROUP)
    wgt = jnp.pad(wgt, ((0, 0), (0, 0), (0, 128 - 3 * GROUP))).reshape(d, IN_TN)
    return dict(q=wq.astype(BF16), u=wu.astype(BF16), mg=wmg.astype(BF16), gt=wgt.astype(BF16),
                kv=wkv.astype(BF16), kv_t=wkv.T.astype(BF16))


def _pool_kernel(*refs, npg, n_tables):
    refs = refs[n_tables:]
    prev_ref = refs[0]
    page_refs = refs[1:npg + 1]
    mk_ref, mv_ref, mprev_ref, out_ref = refs[npg + 1:]
    pages = [prev_ref[0].astype(BF16)] + [page_refs[j][0].astype(BF16) for j in range(npg)]
    mpg = mk_ref.shape[0] // PAGE
    nblk = mk_ref.shape[1]
    for seg in range(npg // mpg):
        x = jnp.concatenate(pages[1 + seg * mpg:1 + (seg + 1) * mpg], axis=1)
        for half, m_ref in ((0, mk_ref), (1, mv_ref)):
            rows = slice(half * KV_COLS, (half + 1) * KV_COLS)
            head = _dot(pages[seg * mpg][rows], mprev_ref[half])
            if seg == 0:
                head = jnp.where(pl.program_id(1) > 0, head, 0.0)
            out_ref[0, rows, seg * nblk:(seg + 1) * nblk] = _dot(x[rows], m_ref[...]) + head


def _pool_matrices(pool, npg):
    nblk = npg * CMP_PER_PAGE
    cols = []
    for r in range(CMP_PER_PAGE):
        start = CMP_STRIDE * (r - 1)
        lo, hi = max(start, 0), min(start + CMP_LEN, PAGE)
        cols.append(jnp.pad(pool[:, lo - start:hi - start], ((0, 0), (lo, PAGE - hi))))
    body = jnp.stack(cols, axis=-1)
    carry = jnp.pad(pool[:, :CMP_STRIDE], ((0, 0), (PAGE - CMP_STRIDE, 0)))
    first = jnp.asarray(np.arange(CMP_PER_PAGE) == 0, F32)
    same = jnp.eye(npg, dtype=F32)[None, :, None, :, None]
    nxt = jnp.eye(npg, k=1, dtype=F32)[None, :, None, :, None]
    m = same * body[:, None, :, None, :] + nxt * (carry[:, None, :, None, None] * first)
    m = m.reshape(2, npg * PAGE, nblk).astype(BF16)
    mprev = (carry[:, :, None] * jnp.asarray(np.arange(nblk) == 0, F32)).astype(BF16)
    return m[0], m[1], mprev


def _pool_pages(pages, page_index, tables, nseq, npages, mats, npg):
    assert npg % (mats[0].shape[0] // PAGE) == 0
    nstep = npages // npg
    nblk = npg * CMP_PER_PAGE
    mk, mv, mprev = mats

    def spec(off):
        return pl.BlockSpec((1, 2 * KV_COLS, PAGE),
                            lambda b, s, *t: page_index(b, jnp.maximum(s * npg + off, 0), *t))

    const = lambda shape: pl.BlockSpec(shape, lambda b, s, *t: (0,) * len(shape))
    return pl.pallas_call(
        functools.partial(_pool_kernel, npg=npg, n_tables=len(tables)),
        grid_spec=pltpu.PrefetchScalarGridSpec(
            num_scalar_prefetch=len(tables),
            grid=(nseq, nstep),
            in_specs=[spec(j) for j in range(-1, npg)] + [const(mk.shape), const(mv.shape), const(mprev.shape)],
            out_specs=pl.BlockSpec((1, 2 * KV_COLS, nblk), lambda b, s, *t: (b, 0, s)),
        ),
        out_shape=jax.ShapeDtypeStruct((nseq, 2 * KV_COLS, npages * CMP_PER_PAGE), F32),
        compiler_params=pltpu.CompilerParams(
            dimension_semantics=("parallel", "arbitrary"), vmem_limit_bytes=VMEM_LIMIT),
        name="cmp_pool",
    )(*tables, *([pages] * (npg + 1)), mk, mv, mprev)


def _cmp_finish_kernel(pooled_ref, poolw_ref, pe_ref, phi_ref, kc_ref, vc_ref, kcr_ref):
    bias = jnp.sum(poolw_ref[...] * pe_ref[...], axis=1, keepdims=True)
    x = _gelu(pooled_ref[0] + bias)
    kc = _dot3(phi_ref[0], x[:KV_COLS])
    kc_ref[0] = kc.astype(BF16)
    kcr_ref[0] = kc.T.astype(BF16)
    vc_ref[0] = _dot3(phi_ref[1], x[KV_COLS:]).astype(BF16)


def _cmp_finish(pooled, pool, pe, phi):
    nseq, _, n = pooled.shape
    poolw = jnp.concatenate([jnp.broadcast_to(pool[s][None, :], (KV_COLS, CMP_LEN)) for s in range(2)], axis=0)
    pe_t = jnp.concatenate([jnp.tile(pe[s].T, (N_KV, 1)) for s in range(2)], axis=0)
    phi_bd = jnp.stack([jnp.kron(jnp.eye(N_KV, dtype=F32), phi[s].T) for s in range(2)])
    spec = pl.BlockSpec((1, KV_COLS, n), lambda b: (b, 0, 0))
    return pl.pallas_call(
        _cmp_finish_kernel,
        grid=(nseq,),
        in_specs=[
            pl.BlockSpec((1, 2 * KV_COLS, n), lambda b: (b, 0, 0)),
            pl.BlockSpec((2 * KV_COLS, CMP_LEN), lambda b: (0, 0)),
            pl.BlockSpec((2 * KV_COLS, CMP_LEN), lambda b: (0, 0)),
            pl.BlockSpec((2, KV_COLS, KV_COLS), lambda b: (0, 0, 0)),
        ],
        out_specs=[spec, spec, pl.BlockSpec((1, n, KV_COLS), lambda b: (b, 0, 0))],
        out_shape=[jax.ShapeDtypeStruct((nseq, KV_COLS, n), BF16)] * 2
        + [jax.ShapeDtypeStruct((nseq, n, KV_COLS), BF16)],
        compiler_params=pltpu.CompilerParams(dimension_semantics=("parallel",)),
        name="cmp_finish",
    )(pooled, poolw, pe_t, phi_bd)


ATT_TQ = 512
ATT_TK = 512
NBLK_PAD = 128
POS_ROWS = 16
AUG = HEAD_DIM + POS_ROWS


def _attn_prompt_kernel(q_ref, kca_ref, vc_ref, ks_ref, vs_ref, kw_ref, vw_ref, gt_ref, ovt_ref, ktab_ref,
                        stab_ref, o_ref, ksa_ref, vsa_ref, kwa_ref, vwa_ref, qat_ref, m_ref, acc_ref,
                        *, tq, tk, n_sel, n_top):
    qi = pl.program_id(2)
    q0 = qi * tq
    seq = ksa_ref.shape[0]

    @pl.when(qi == 0)
    def _():
        zero = jnp.zeros((128 - HEAD_DIM, seq), F32)
        ones = jnp.where(_iota((POS_ROWS, seq), 0) == 0, 1.0, 0.0).astype(BF16)
        for src, dst, lo in ((ks_ref, ksa_ref, NBLK_PAD), (kw_ref, kwa_ref, 0)):
            k_rows = jnp.concatenate([src[0], zero], axis=0).T[:, :HEAD_DIM].astype(BF16)
            dst[:, lo:] = jnp.concatenate([k_rows, ktab_ref[:, NBLK_PAD:]], axis=1)
        ksa_ref[:, 0:NBLK_PAD] = ktab_ref[:, 0:NBLK_PAD]
        for src, dst in ((vs_ref, vsa_ref), (vw_ref, vwa_ref)):
            dst[0:HEAD_DIM] = src[0].astype(BF16)
            dst[HEAD_DIM:] = ones

    q_t = q_ref[...].astype(F32).T
    q_t = jnp.concatenate([q_t[n * HEAD_DIM:(n + 1) * HEAD_DIM] for n in range(GROUP)], axis=1).astype(BF16)
    stab = stab_ref[0]
    sp_t = jnp.concatenate([jnp.broadcast_to(stab[:, n:n + 1], (POS_ROWS, tq)) for n in range(GROUP)], axis=1)
    qat_ref[NBLK_PAD:NBLK_PAD + HEAD_DIM] = q_t
    qat_ref[NBLK_PAD + HEAD_DIM:] = sp_t.astype(BF16)
    trow = q0 + _iota((1, tq), 1)
    lanes = [slice(n * tq, (n + 1) * tq) for n in range(GROUP)]

    ncmp = kca_ref.shape[2]
    nidx = _iota((ncmp, 1), 0)
    heads = lambda a: jnp.concatenate([a] * GROUP, axis=1)
    valid = (nidx >= 1) & (CMP_STRIDE * nidx + (CMP_STRIDE - 1) <= trow)
    valid = heads(jnp.where(valid, 1.0, 0.0)) > 0.5
    s = jnp.where(valid, _dot(kca_ref[0, 0], qat_ref[NBLK_PAD:]), NEG)
    e = jnp.exp2(s - jnp.max(s, axis=0, keepdims=True))
    pc = jnp.where(valid, e / jnp.sum(e, axis=0, keepdims=True), 0.0)
    o_c = _dot(vc_ref[0], pc.astype(BF16))

    ph, plo = _split((pc[:, lanes[0]] + pc[:, lanes[1]]) + (pc[:, lanes[2]] + pc[:, lanes[3]]))
    imp = (_dot(ovt_ref[...], ph) + _dot(ovt_ref[...], plo))[0:n_sel]
    blk = _iota((n_sel, 1), 0)
    cur = _div_pow2(q0 + _iota((1, tq), 1), SEL_BLOCK)
    forced = (blk == 0) | (blk == cur) | (blk == cur - 1)
    imp = jnp.where(forced, FORCE_SCORE, imp)
    imp = jnp.where(blk <= cur, imp, -1.0)
    cnt = jnp.zeros((n_sel, tq), F32)
    for j in range(n_sel):
        vj = imp[j:j + 1, :]
        tie = jnp.where(blk > j, 1.0, 0.0)
        cnt = cnt + jnp.where(vj > imp, 1.0, jnp.where(vj == imp, tie, 0.0))
    mask_t = jnp.where((cnt < n_top) & (blk <= cur), 0.0, NEG)
    mask_t = jnp.concatenate([mask_t, jnp.full((NBLK_PAD - n_sel, tq), NEG, F32)], axis=0).astype(BF16)
    qat_ref[0:NBLK_PAD] = heads(mask_t)

    m_ref[...] = jnp.full_like(m_ref, NEG)
    acc_ref[...] = jnp.zeros_like(acc_ref)

    def scores(kt):
        return _dot(ksa_ref[pl.ds(pl.multiple_of(kt * tk, tk), tk), :], qat_ref[...])

    def update(kt, s):
        m_old = m_ref[...]
        m_new = jnp.maximum(m_old, jnp.max(s, axis=0, keepdims=True))
        p = jnp.exp2(s - m_new).astype(BF16)
        pv = _dot(vsa_ref[:, pl.ds(pl.multiple_of(kt * tk, tk), tk)], p)
        acc_ref[...] = jnp.exp2(m_old - m_new) * acc_ref[...] + pv
        m_ref[...] = m_new

    def pair(j, carry):
        s0, s1 = scores(2 * j), scores(2 * j + 1)
        update(2 * j, s0)
        update(2 * j + 1, s1)
        return carry

    lax.fori_loop(0, qi // 2, pair, 0)

    @pl.when(qi % 2 == 1)
    def _():
        update(qi - 1, scores(qi - 1))

    s_diag = scores(qi) + heads(jnp.where(q0 + _iota((tk, 1), 0) <= trow, 0.0, NEG))
    hq = tq // 2
    span = WINDOW + hq
    o_w_half = []
    for h in range(2):
        qh = q0 + h * hq
        w0 = pl.multiple_of(jnp.maximum(qh - WINDOW, 0), hq)
        dist = (qh + _iota((1, hq), 1)) - (w0 + _iota((span, 1), 0))
        qa_h = jnp.concatenate([qat_ref[NBLK_PAD:, n * tq + h * hq:n * tq + (h + 1) * hq] for n in range(GROUP)], axis=1)
        s_win = _dot(kwa_ref[pl.ds(w0, span), :], qa_h) + heads(jnp.where((dist >= 0) & (dist < WINDOW), 0.0, NEG))
        ow = _dot(vwa_ref[:, pl.ds(w0, span)],
                  jnp.exp2(s_win - jnp.max(s_win, axis=0, keepdims=True)).astype(BF16))
        o_w_half.append(ow[:HEAD_DIM] / ow[HEAD_DIM:HEAD_DIM + 1])
    o_w = jnp.concatenate([o_w_half[h][:, n * hq:(n + 1) * hq] for n in range(GROUP) for h in range(2)], axis=1)
    update(qi, s_diag)
    acc = acc_ref[...]
    o_s = acc[:HEAD_DIM] / acc[HEAD_DIM:HEAD_DIM + 1]

    gt_t = gt_ref[...].T
    outs = []
    for n in range(GROUP):
        outs.append(gt_t[n:n + 1] * o_c[:, lanes[n]] + gt_t[GROUP + n:GROUP + n + 1] * o_s[:, lanes[n]]
                    + gt_t[2 * GROUP + n:2 * GROUP + n + 1] * o_w[:, lanes[n]])
    o_ref[...] = jnp.concatenate(outs, axis=0).T.astype(BF16)


def _overlap_t(n_blk_rows, n_cmp):
    s = np.arange(n_blk_rows)[:, None]
    n = np.arange(n_cmp)[None, :]
    r = SEL_BLOCK // CMP_STRIDE
    return jnp.asarray(((n >= r * s) & (n <= r * s + r)).astype(np.float32), dtype=BF16)


def _block_expand(n_blk_rows, n_keys):
    s = np.arange(n_blk_rows)[:, None]
    k = np.arange(n_keys)[None, :]
    return jnp.asarray((k // SEL_BLOCK == s).astype(np.float32), dtype=BF16)


def _attn_prompt(slopes, q, kc_rows, vc_t, kvc_t, kvw_t, gates, nbatch, seq):
    ncmp = kc_rows.shape[1]
    tq, tk = ATT_TQ, ATT_TK
    assert tk == tq and seq % tk == 0 and WINDOW % tq == 0 and WINDOW + tq <= seq
    nq = seq // tq
    n_sel = seq // SEL_BLOCK
    ovt = _overlap_t(NBLK_PAD, ncmp)
    ktab = jnp.concatenate([_block_expand(NBLK_PAD, seq), _position_rows(seq)], axis=0).T
    cmp_end = CMP_STRIDE * np.arange(ncmp) + (CMP_STRIDE - 1)
    kca = jnp.concatenate([kc_rows.reshape(nbatch, ncmp, N_KV, HEAD_DIM).transpose(0, 2, 1, 3),
                           jnp.broadcast_to(_position_rows(cmp_end).T, (nbatch, N_KV, ncmp, POS_ROWS))], axis=-1)
    stab = _slope_table(slopes).astype(F32).transpose(0, 2, 1)
    per_slot = KV_COLS // HEAD_DIM

    def kv_spec(slot):
        return pl.BlockSpec((1, HEAD_DIM, seq), lambda b, g, i: (b, slot * per_slot + g, 0))

    q_spec = pl.BlockSpec((tq, GROUP * HEAD_DIM), lambda b, g, i: (b * nq + i, g))
    return pl.pallas_call(
        functools.partial(_attn_prompt_kernel, tq=tq, tk=tk, n_sel=n_sel, n_top=min(SEL_TOP, n_sel)),
        grid=(nbatch, N_KV, nq),
        in_specs=[
            q_spec,
            pl.BlockSpec((1, 1, ncmp, AUG), lambda b, g, i: (b, g, 0, 0)),
            pl.BlockSpec((1, HEAD_DIM, ncmp), lambda b, g, i: (b, g, 0)),
            kv_spec(2), kv_spec(3), kv_spec(0), kv_spec(1),
            pl.BlockSpec((tq, 128), lambda b, g, i: (b * nq + i, g)),
            pl.BlockSpec((NBLK_PAD, ncmp), lambda b, g, i: (0, 0)),
            pl.BlockSpec((seq, NBLK_PAD + POS_ROWS), lambda b, g, i: (0, 0)),
            pl.BlockSpec((1, POS_ROWS, GROUP), lambda b, g, i: (g, 0, 0)),
        ],
        out_specs=q_spec,
        out_shape=jax.ShapeDtypeStruct(q.shape, BF16),
        scratch_shapes=[
            pltpu.VMEM((seq, NBLK_PAD + AUG), BF16),
            pltpu.VMEM((AUG, seq), BF16),
            pltpu.VMEM((seq, AUG), BF16),
            pltpu.VMEM((AUG, seq), BF16),
            pltpu.VMEM((NBLK_PAD + AUG, GROUP * tq), BF16),
            pltpu.VMEM((1, GROUP * tq), F32),
            pltpu.VMEM((AUG, GROUP * tq), F32),
        ],
        compiler_params=pltpu.CompilerParams(
            dimension_semantics=("parallel", "parallel", "arbitrary"), vmem_limit_bytes=VMEM_LIMIT),
        name="attn_prompt",
    )(q, kca, vc_t, kvc_t, kvc_t, kvw_t, kvw_t, gates, ovt, ktab, stab)


def _diag_blocks(o, maskbd):
    o = o * maskbd
    return (o[:, 0:64] + o[:, 64:128]) + (o[:, 128:192] + o[:, 192:256])


def _attn_dec1_kernel(qbd_ref, kc_ref, vc_ref, win_ref, kvnew_ref, gates_ref, slope_ref, ovt_ref,
                      maskbd_ref, opart_ref, sel_ref, *, past, tdec, n_blk, n_top):
    qb = qbd_ref[0]
    nrows = qb.shape[0]
    tcol = past + _mod_pow2(_iota((nrows, 1), 0), tdec)
    pref = float(past + tdec)
    slope = slope_ref[...]
    maskbd = maskbd_ref[...]

    ncmp = kc_ref.shape[2]
    nrow = _iota((1, ncmp), 1)
    cend = CMP_STRIDE * nrow + (CMP_STRIDE - 1)
    valid = (nrow >= 1) & (cend <= tcol)
    s = _dot(qb, kc_ref[0])
    s = jnp.where(valid, s + slope * (cend.astype(F32) - pref), NEG)
    p = jnp.where(valid, _softmax2_rows(s), 0.0)
    o_c = _diag_blocks(_dot_nt(p.astype(BF16), vc_ref[0]), maskbd)

    per_g = GROUP * tdec
    psum = jnp.concatenate(
        [sum(p[gi * per_g + n * tdec: gi * per_g + (n + 1) * tdec] for n in range(GROUP)) for gi in range(N_KV)],
        axis=0)
    imp = _dot_nt_hl(ovt_ref[...], psum)
    nb_pad = imp.shape[0]
    blk = _iota((nb_pad, 1), 0)
    cur = _div_pow2(past + _mod_pow2(_iota((1, N_KV * tdec), 1), tdec), SEL_BLOCK)
    forced = (blk == 0) | (blk == cur) | (blk == cur - 1)
    imp = jnp.where(forced, FORCE_SCORE, imp)
    imp = jnp.where(blk <= cur, imp, -1.0)
    imp = jnp.where(blk < n_blk, imp, -2.0)
    sel = jnp.zeros(imp.shape, F32)
    for _ in range(n_top):
        mx = jnp.max(imp, axis=0, keepdims=True)
        first = jnp.min(jnp.where(imp == mx, blk, nb_pad), axis=0, keepdims=True)
        pick = blk == first
        sel = jnp.where(pick, 1.0, sel)
        imp = jnp.where(pick, -jnp.inf, imp)
    sel_ref[0] = sel

    nbuf = win_ref.shape[2]
    knew = kvnew_ref[0]
    zpad = jnp.zeros((PAGE - tdec, KV_COLS), F32)
    kn = jnp.concatenate([knew[:, :KV_COLS], zpad], axis=0).astype(BF16)
    vn = jnp.concatenate([knew[:, KV_COLS:], zpad], axis=0).astype(BF16)
    sw = jnp.concatenate([_dot(qb, win_ref[0, :KV_COLS].astype(BF16)), _dot_nt(qb, kn)], axis=1)
    idx = _iota((1, nbuf + PAGE), 1)
    kposw = past - nbuf + idx
    dist = tcol - kposw
    validw = (dist >= 0) & (dist < WINDOW) & (kposw >= 0) & (idx < nbuf + tdec)
    sw = jnp.where(validw, sw + slope * (kposw.astype(F32) - pref), NEG)
    pw = _softmax2_rows(sw).astype(BF16)
    o_w = _dot_nt(pw[:, :nbuf], win_ref[0, KV_COLS:].astype(BF16)) + _dot(pw[:, nbuf:], vn)
    o_w = _diag_blocks(o_w, maskbd)

    gates = gates_ref[0]
    opart_ref[0] = gates[:, 0:1] * o_c + gates[:, 2:3] * o_w


def _attn_dec1(qbd, kc_t, vc_t, win_t, kvnew, gates_rows, slopecol, maskbd, past, tdec, n_blk):
    nseq, nrows, _ = qbd.shape
    ncmp = kc_t.shape[2]
    nb_pad = -(-n_blk // 8) * 8
    ovt = _overlap_t(nb_pad, ncmp)
    nbuf = win_t.shape[2]
    return pl.pallas_call(
        functools.partial(_attn_dec1_kernel, past=past, tdec=tdec, n_blk=n_blk, n_top=min(SEL_TOP, n_blk)),
        grid=(nseq,),
        in_specs=[
            pl.BlockSpec((1, nrows, KV_COLS), lambda b: (b, 0, 0)),
            pl.BlockSpec((1, KV_COLS, ncmp), lambda b: (b, 0, 0)),
            pl.BlockSpec((1, KV_COLS, ncmp), lambda b: (b, 0, 0)),
            pl.BlockSpec((1, 2 * KV_COLS, nbuf), lambda b: (b, 0, 0)),
            pl.BlockSpec((1, tdec, 2 * KV_COLS), lambda b: (b, 0, 0)),
            pl.BlockSpec((1, nrows, 4), lambda b: (b, 0, 0)),
            pl.BlockSpec((nrows, 1), lambda b: (0, 0)),
            pl.BlockSpec((nb_pad, ncmp), lambda b: (0, 0)),
            pl.BlockSpec((nrows, KV_COLS), lambda b: (0, 0)),
        ],
        out_specs=[
            pl.BlockSpec((1, nrows, HEAD_DIM), lambda b: (b, 0, 0)),
            pl.BlockSpec((1, nb_pad, N_KV * tdec), lambda b: (b, 0, 0)),
        ],
        out_shape=[
            jax.ShapeDtypeStruct((nseq, nrows, HEAD_DIM), F32),
            jax.ShapeDtypeStruct((nseq, nb_pad, N_KV * tdec), F32),
        ],
        compiler_params=pltpu.CompilerParams(dimension_semantics=("parallel",), vmem_limit_bytes=VMEM_LIMIT),
        name="attn_dec_select",
    )(qbd, kc_t, vc_t, win_t, kvnew, gates_rows, slopecol, ovt, maskbd)


def _attn_dec2_kernel(pt_ref, *refs, npg, past, tdec):
    page_refs = refs[:npg]
    (qbd_ref, selrow_ref, sellast_ref, e_ref, slope_ref, knew_ref, opart_ref, gates_ref, maskbd_ref,
     o_ref, m_ref, l_ref, acc_ref) = refs[npg:]
    step = pl.program_id(1)
    nkeys = npg * PAGE
    qb = qbd_ref[0]
    nrows = qb.shape[0]
    pref = float(past + tdec)
    slope = slope_ref[...]

    @pl.when(step == 0)
    def _():
        m_ref[...] = jnp.full_like(m_ref, NEG)
        l_ref[...] = jnp.zeros_like(l_ref)
        acc_ref[...] = jnp.zeros_like(acc_ref)

    def update(s, pv_fn):
        m_old = m_ref[...]
        m_new = jnp.maximum(m_old, jnp.max(s, axis=-1, keepdims=True))
        alpha = jnp.exp2(m_old - m_new)
        p = jnp.exp2(s - m_new)
        l_ref[...] = alpha * l_ref[...] + jnp.sum(p, axis=-1, keepdims=True)
        acc_ref[...] = alpha * acc_ref[...] + pv_fn(p.astype(BF16))
        m_ref[...] = m_new

    k_t = jnp.concatenate([page_refs[j][0, :KV_COLS].astype(BF16) for j in range(npg)], axis=1)
    v_t = jnp.concatenate([page_refs[j][0, KV_COLS:].astype(BF16) for j in range(npg)], axis=1)
    kpos = step * nkeys + _iota((1, nkeys), 1)
    chosen = _dot(selrow_ref[0, 0], e_ref[...])
    s = _dot(qb, k_t) + slope * (kpos.astype(F32) - pref) + jnp.where(chosen > 0.5, 0.0, NEG)
    update(s, lambda p: _dot_nt(p, v_t))

    @pl.when(step == pl.num_programs(1) - 1)
    def _():
        tcol = past + _mod_pow2(_iota((nrows, 1), 0), tdec)
        kn = knew_ref[0]
        kposn = past + _iota((1, PAGE), 1)
        sn = _dot_nt(qb, kn[:, :KV_COLS].astype(BF16))
        ok = (sellast_ref[0, 0][:, 0:1].astype(F32) > 0.5) & (kposn <= tcol)
        sn = sn + slope * (kposn.astype(F32) - pref) + jnp.where(ok, 0.0, NEG)
        update(sn, lambda p: _dot(p, kn[:, KV_COLS:].astype(BF16)))
        o_s = _diag_blocks(acc_ref[...] / l_ref[...], maskbd_ref[...])
        o_ref[0] = opart_ref[0] + gates_ref[0][:, 1:2] * o_s


def _attn_dec2(cache_t, page_table, qbd, selrows, slopecol, knew, opart, gates_rows, maskbd, past, tdec, npg):
    nseq, npages = page_table.shape
    nstep = npages // npg
    nrows = qbd.shape[1]
    blk_per_step = npg * PAGE // SEL_BLOCK
    emat = _block_expand(blk_per_step, npg * PAGE)
    page_specs = [
        pl.BlockSpec((1, 2 * KV_COLS, PAGE), lambda b, s, pt, j=j: (pt[b, s * npg + j], 1, 0))
        for j in range(npg)
    ]
    per_seq = lambda shape: pl.BlockSpec((1,) + shape, lambda b, s, pt: (b,) + (0,) * len(shape))
    const = lambda shape: pl.BlockSpec(shape, lambda b, s, pt: (0,) * len(shape))
    return pl.pallas_call(
        functools.partial(_attn_dec2_kernel, npg=npg, past=past, tdec=tdec),
        grid_spec=pltpu.PrefetchScalarGridSpec(
            num_scalar_prefetch=1,
            grid=(nseq, nstep),
            in_specs=page_specs + [
                per_seq((nrows, KV_COLS)),
                pl.BlockSpec((1, 1, nrows, blk_per_step), lambda b, s, pt: (b, s, 0, 0)),
                pl.BlockSpec((1, 1, nrows, blk_per_step), lambda b, s, pt: (b, nstep, 0, 0)),
                const((blk_per_step, npg * PAGE)),
                const((nrows, 1)),
                per_seq((PAGE, 2 * KV_COLS)),
                per_seq((nrows, HEAD_DIM)),
                per_seq((nrows, 4)),
                const((nrows, KV_COLS)),
            ],
            out_specs=per_seq((nrows, HEAD_DIM)),
            scratch_shapes=[
                pltpu.VMEM((nrows, 1), F32),
                pltpu.VMEM((nrows, 1), F32),
                pltpu.VMEM((nrows, KV_COLS), F32),
            ],
        ),
        out_shape=jax.ShapeDtypeStruct((nseq, nrows, HEAD_DIM), F32),
        compiler_params=pltpu.CompilerParams(
            dimension_semantics=("parallel", "arbitrary"), vmem_limit_bytes=VMEM_LIMIT),
        name="attn_dec_selected",
    )(page_table, *([cache_t] * npg), qbd, selrows, selrows, emat, slopecol, knew, opart, gates_rows, maskbd)


def _s5_weights(lam_re, lam_im, log_dt, b_re, b_im, c_re, c_im, d_skip):
    hp = lax.Precision.HIGHEST
    L = S5_L
    ng, p = lam_re.shape
    c = SSM_GROUP
    ns, gs = ng // S5_SLAB, S5_SLAB
    dt = jnp.exp(log_dt)[:, None]
    lr, li = lam_re, lam_im
    mag = jnp.exp(lr * dt)
    ar = mag * jnp.cos(li * dt)
    ai = mag * jnp.sin(li * dt)
    den = lr * lr + li * li
    fr = ((ar - 1.0) * lr + ai * li) / den
    fi = (ai * lr - (ar - 1.0) * li) / den
    bt_re, bt_im = b_re.transpose(0, 2, 1), b_im.transpose(0, 2, 1)
    bbr = fr[:, None, :] * bt_re - fi[:, None, :] * bt_im
    bbi = fr[:, None, :] * bt_im + fi[:, None, :] * bt_re
    j = jnp.arange(L + 1, dtype=F32)[:, None, None]
    pmag = jnp.exp(j * (lr * dt))
    pr = pmag * jnp.cos(j * (li * dt))
    pi = pmag * jnp.sin(j * (li * dt))
    abr = pr[:, :, None, :] * bbr - pi[:, :, None, :] * bbi
    abi = pr[:, :, None, :] * bbi + pi[:, :, None, :] * bbr
    kd = (jnp.einsum('jgkp,gcp->jgkc', abr, c_re, precision=hp)
          - jnp.einsum('jgkp,gcp->jgkc', abi, c_im, precision=hp))
    kd = kd.at[0].add(d_skip.reshape(ng, 1, c) * jnp.eye(c, dtype=F32))
    kc = kd[:L].reshape(L, ns, gs * c, c)
    abc = jnp.concatenate([abr[:L], abi[:L]], axis=-1).reshape(L, ns, gs * c, 2 * p)
    cr = c_re[None] * pr[1:, :, None, :] - c_im[None] * pi[1:, :, None, :]
    ci = -(c_re[None] * pi[1:, :, None, :] + c_im[None] * pr[1:, :, None, :])
    co = jnp.stack([cr, ci], axis=1).reshape(L, 2, ns, gs, c, p)
    al = jnp.stack([pr[L].reshape(-1), pi[L].reshape(-1)])
    return kc.astype(BF16), abc.astype(BF16), co.astype(BF16), al


def _s5_expanders():
    gs, c, p = S5_SLAB, SSM_GROUP, SSM_STATE
    lane = np.arange(gs * c)
    rep_c = (np.arange(c)[:, None] == lane[None, :] % c)
    same_g = (lane[:, None] // c == lane[None, :] // c)
    st = np.arange(2 * gs * p)
    rp = np.arange(2 * p)
    rep_p = (rp[:, None] // p == st[None, :] // (gs * p)) & (rp[:, None] % p == st[None, :] % p)
    g_in = (lane[:, None] // c == (st[None, :] // p) % gs)
    as_bf = lambda a: jnp.asarray(a.astype(np.float32), dtype=BF16)
    return as_bf(rep_c), as_bf(same_g), as_bf(rep_p), as_bf(g_in)


def _chunk_lanes(u_ref):
    return jnp.concatenate([u_ref[0, :, t, :] for t in range(S5_L)], axis=1)


def _s5_local_kernel(u_ref, abc_ref, rep_p_ref, g_in_ref, pre_ref, pim_ref, wp_ref):
    for s in range(S5_L):
        blk = _dot(abc_ref[S5_L - 1 - s, 0], rep_p_ref[...]).astype(BF16) * g_in_ref[...]
        wp_ref[128 * s:128 * (s + 1), :] = blk
    r = _dot(_chunk_lanes(u_ref).astype(BF16), wp_ref[...])
    half = r.shape[1] // 2
    pre_ref[...] = r[:, :half]
    pim_ref[...] = r[:, half:]


def _s5_scan_kernel(pre_ref, pim_ref, al_ref, h0r_ref, h0i_ref, hsr_ref, hsi_ref, hfr_ref, hfi_ref):
    nchunk = pre_ref.shape[1]
    ar = al_ref[0:1, :]
    ai = al_ref[1:2, :]

    def body(k, carry):
        cr, ci = carry
        hsr_ref[:, pl.ds(k, 1), :] = cr[:, None, :]
        hsi_ref[:, pl.ds(k, 1), :] = ci[:, None, :]
        xr = pre_ref[:, pl.ds(k, 1), :][:, 0, :]
        xi = pim_ref[:, pl.ds(k, 1), :][:, 0, :]
        return ar * cr - ai * ci + xr, ar * ci + ai * cr + xi

    cr, ci = lax.fori_loop(0, nchunk, body, (h0r_ref[...], h0i_ref[...]), unroll=min(nchunk, 4))
    hfr_ref[...] = cr
    hfi_ref[...] = ci


def _s5_out_kernel(u_ref, kc_ref, co_ref, rep_c_ref, same_g_ref, hsr_ref, hsi_ref, y_ref, wt_ref, wot_ref):
    rep_c = rep_c_ref[...]
    lag = [_dot(kc_ref[dl, 0], rep_c).astype(BF16) * same_g_ref[...] for dl in range(S5_L)]
    zero = jnp.zeros((128, 128), BF16)
    for s in range(S5_L):
        for t in range(S5_L):
            wt_ref[128 * s:128 * (s + 1), 128 * t:128 * (t + 1)] = lag[t - s] if t >= s else zero
    wot_ref[...] = jnp.zeros_like(wot_ref)
    c, p = SSM_GROUP, SSM_STATE
    for t in range(S5_L):
        for r in range(2):
            for g in range(S5_SLAB):
                col = (r * S5_SLAB + g) * p
                wot_ref[128 * t + c * g:128 * t + c * (g + 1), col:col + p] = co_ref[t, r, 0, g]
    hs = jnp.concatenate([hsr_ref[...], hsi_ref[...]], axis=1).astype(BF16)
    y = _dot(_chunk_lanes(u_ref).astype(BF16), wt_ref[...]) + _dot_nt(hs, wot_ref[...])
    for t in range(S5_L):
        y_ref[0, :, t, :] = y[:, 128 * t:128 * (t + 1)]


def _s5(u3, h0r, h0i, weights, nbatch):
    kc, abc, co, al = weights
    rep_c, same_g, rep_p, g_in = _s5_expanders()
    ns, nb = u3.shape[0], u3.shape[1]
    nchunk = nb // nbatch
    gp = al.shape[1]
    lb = gp // ns
    d = S5_L * 128
    cp = pltpu.CompilerParams(dimension_semantics=("parallel",), vmem_limit_bytes=VMEM_LIMIT)
    u_spec = pl.BlockSpec((1, nb, S5_L, 128), lambda j: (j, 0, 0, 0))
    st_spec = pl.BlockSpec((nb, lb), lambda j: (0, j))
    slab = lambda a: pl.BlockSpec((a.shape[0], 1) + a.shape[2:], lambda j: (0, j) + (0,) * (a.ndim - 2))
    const = lambda a: pl.BlockSpec(a.shape, lambda j: (0,) * a.ndim)
    pre, pim = pl.pallas_call(
        _s5_local_kernel,
        grid=(ns,),
        in_specs=[u_spec, slab(abc), const(rep_p), const(g_in)],
        out_specs=[st_spec] * 2,
        out_shape=[jax.ShapeDtypeStruct((nb, gp), F32)] * 2,
        scratch_shapes=[pltpu.VMEM((d, 2 * lb), BF16)],
        compiler_params=cp,
        name="s5_local",
    )(u3, abc, rep_p, g_in)

    ls = 2 * lb
    seq3 = pl.BlockSpec((nbatch, nchunk, ls), lambda j: (0, 0, j))
    row = lambda r: pl.BlockSpec((r, ls), lambda j: (0, j))
    hsr, hsi, hfr, hfi = pl.pallas_call(
        _s5_scan_kernel,
        grid=(gp // ls,),
        in_specs=[seq3, seq3, row(2), row(nbatch), row(nbatch)],
        out_specs=[seq3, seq3, row(nbatch), row(nbatch)],
        out_shape=[jax.ShapeDtypeStruct((nbatch, nchunk, gp), F32)] * 2 + [jax.ShapeDtypeStruct((nbatch, gp), F32)] * 2,
        compiler_params=cp,
        name="s5_scan",
    )(pre.reshape(nbatch, nchunk, gp), pim.reshape(nbatch, nchunk, gp), al, h0r, h0i)

    y3 = pl.pallas_call(
        _s5_out_kernel,
        grid=(ns,),
        in_specs=[u_spec, slab(kc),
                  pl.BlockSpec(co.shape[:2] + (1,) + co.shape[3:], lambda j: (0, 0, j, 0, 0, 0)),
                  const(rep_c), const(same_g), st_spec, st_spec],
        out_specs=u_spec,
        out_shape=jax.ShapeDtypeStruct(u3.shape, F32),
        scratch_shapes=[pltpu.VMEM((d, d), BF16), pltpu.VMEM((2 * lb, d), BF16)],
        compiler_params=cp,
        name="s5_out",
    )(u3, kc, co, rep_c, same_g, hsr.reshape(nb, gp), hsi.reshape(nb, gp))
    return y3, hfr, hfi


def _merge_kernel(o_ref, y_ref, wa_ref, wg1_ref, wg2_ref, sga_ref, sgb_ref, m_ref, gy_ref):
    @pl.when(pl.program_id(1) == 0)
    def _():
        y = jnp.concatenate([y_ref[s] for s in range(y_ref.shape[0])], axis=1)
        gy_ref[...] = _gelu(y).astype(BF16)

    ya = _dot(o_ref[...], wa_ref[...])
    gy = gy_ref[...]
    yb = _dot(gy, wg1_ref[...]) * jax.nn.sigmoid(_dot(gy, wg2_ref[...]))
    m = sga_ref[...].astype(F32) * ya + sgb_ref[...].astype(F32) * yb
    m_ref[...] = m.astype(BF16)


def _merge(o, y3, w_attn_out, w_glu, sg, tm, tn=1024):
    rows, d = o.shape[0], w_attn_out.shape[1]
    nj = d // tn
    kq, ks = w_attn_out.shape[0], w_glu.shape[0]
    ns = y3.shape[0]
    return pl.pallas_call(
        _merge_kernel,
        grid=(rows // tm, nj),
        in_specs=[
            pl.BlockSpec((tm, kq), lambda i, j: (i, 0)),
            pl.BlockSpec((ns, tm, 128), lambda i, j: (0, i, 0)),
            pl.BlockSpec((kq, tn), lambda i, j: (0, j)),
            pl.BlockSpec((ks, tn), lambda i, j: (0, j)),
            pl.BlockSpec((ks, tn), lambda i, j: (0, j + nj)),
            pl.BlockSpec((tm, tn), lambda i, j: (i, j)),
            pl.BlockSpec((tm, tn), lambda i, j: (i, j + nj)),
        ],
        out_specs=pl.BlockSpec((tm, tn), lambda i, j: (i, j)),
        out_shape=jax.ShapeDtypeStruct((rows, d), BF16),
        scratch_shapes=[pltpu.VMEM((tm, ks), BF16)],
        compiler_params=pltpu.CompilerParams(
            dimension_semantics=("parallel", "arbitrary"), vmem_limit_bytes=VMEM_LIMIT),
        name="merge",
    )(o, y3, w_attn_out, w_glu, w_glu, sg, sg)


def _outproj_kernel(m_ref, x_ref, w_ref, g_ref, x1_ref, h_ref):
    x1 = x_ref[...] + _dot(m_ref[...], w_ref[...])
    x1_ref[...] = x1
    r = lax.rsqrt(jnp.mean(x1 * x1, axis=-1, keepdims=True) + EPS)
    h_ref[...] = (x1 * r * g_ref[...]).astype(BF16)


def _outproj(m, x, w_out, g_ffn, tm):
    rows, d = x.shape
    row = pl.BlockSpec((tm, d), lambda i: (i, 0))
    return pl.pallas_call(
        _outproj_kernel,
        grid=(rows // tm,),
        in_specs=[row, row, pl.BlockSpec((d, d), lambda i: (0, 0), pipeline_mode=pl.Buffered(1)),
                  pl.BlockSpec((1, d), lambda i: (0, 0))],
        out_specs=[row, row],
        out_shape=[jax.ShapeDtypeStruct((rows, d), F32), jax.ShapeDtypeStruct((rows, d), BF16)],
        compiler_params=pltpu.CompilerParams(dimension_semantics=("parallel",), vmem_limit_bytes=VMEM_LIMIT),
        name="out_proj",
    )(m, x, w_out, g_ffn.reshape(1, d))


def _ffn_kernel(h_ref, x1_ref, wg_ref, wu_ref, wd_ref, gf_ref, y_ref, *rest):
    acc_ref = rest[-1]
    f = pl.program_id(1)

    @pl.when(f == 0)
    def _():
        acc_ref[...] = jnp.zeros_like(acc_ref)

    wg, wu, wd = wg_ref[...].astype(BF16), wu_ref[...].astype(BF16), wd_ref[...].astype(BF16)
    for copy_ref, w in zip(rest[:-1], (wg, wu, wd)):
        copy_ref[...] = w
    h = h_ref[...]
    a = _dot(h, wg)
    a = (a * jax.nn.sigmoid(a)) * _dot(h, wu)
    acc_ref[...] += _dot(a.astype(BF16), wd)

    @pl.when(f == pl.num_programs(1) - 1)
    def _():
        y = x1_ref[...] + acc_ref[...]
        r = lax.rsqrt(jnp.mean(y * y, axis=-1, keepdims=True) + EPS)
        y_ref[...] = y * r * gf_ref[...]


def _ffn(h, x1, w_gate, w_up, w_down, g_final, tm, tf=512):
    rows, d = x1.shape
    dff = w_gate.shape[1]
    emit = w_gate.dtype != BF16
    assert not emit or rows == tm
    row = pl.BlockSpec((tm, d), lambda i, f: (i, 0))
    w_specs = [pl.BlockSpec((d, tf), lambda i, f: (0, f)),
               pl.BlockSpec((d, tf), lambda i, f: (0, f)),
               pl.BlockSpec((tf, d), lambda i, f: (f, 0))]
    y_shape = jax.ShapeDtypeStruct((rows, d), F32)
    copies = [jax.ShapeDtypeStruct(w.shape, BF16) for w in (w_gate, w_up, w_down)] if emit else []
    out = pl.pallas_call(
        _ffn_kernel,
        grid=(rows // tm, dff // tf),
        in_specs=[row, row] + w_specs + [pl.BlockSpec((1, d), lambda i, f: (0, 0))],
        out_specs=[row] + (w_specs if emit else []),
        out_shape=[y_shape] + copies,
        scratch_shapes=[pltpu.VMEM((tm, d), F32)],
        compiler_params=pltpu.CompilerParams(
            dimension_semantics=("parallel", "arbitrary"), vmem_limit_bytes=VMEM_LIMIT),
        name="ffn",
    )(h, x1, w_gate, w_up, w_down, g_final.reshape(1, d))
    return tuple(out) if emit else out[0]


def _alibi_slopes():
    return jnp.exp2(-8.0 * jnp.arange(1, N_HEADS + 1, dtype=F32) / N_HEADS) * LOG2E


def _slope_table(slopes):
    s1 = slopes.astype(BF16)
    r1 = slopes - s1.astype(F32)
    s2 = r1.astype(BF16)
    s3 = (r1 - s2.astype(F32)).astype(BF16)
    tab = jnp.stack([s1, s1, s2, s2, s3, s3] + [jnp.zeros_like(s1)] * 10, axis=-1)
    return tab.reshape(N_KV, GROUP, 16)


def _position_rows(pos):
    pos = np.arange(pos) if np.isscalar(pos) else np.asarray(pos)
    hi, lo = (pos // 64) * 64, pos % 64
    assert pos.max() < 64 * 256
    rows = np.stack([hi, lo, hi, lo, hi, lo] + [np.zeros_like(pos)] * 10).astype(np.float32)
    return jnp.asarray(rows, dtype=BF16)


def _layer_weights(p, l):
    w_in = _pack_w_in(p['w_in'][l], p['ssm_d'].shape[-1])
    s5 = _s5_weights(*(p[n][l] for n in ('ssm_lam_re', 'ssm_lam_im', 'ssm_log_dt', 'ssm_b_re', 'ssm_b_im',
                                          'ssm_c_re', 'ssm_c_im', 'ssm_d')))
    cast = lambda n: p[n][l].astype(BF16)
    return dict(
        g_mix=p['g_mix'][l], w_in=w_in,
        pool=p['w_cmp_pool'][l], pe=p['w_cmp_pe'][l], phi=p['w_cmp_phi'][l],
        pool_mats=_pool_matrices(p['w_cmp_pool'][l], PROMPT_PAGES_PER_STEP),
        w_attn_out=cast('w_attn_out'), w_glu=cast('w_glu'), w_out=cast('w_out'), g_ffn=p['g_ffn'][l],
        ffn_w=tuple(p[n][l] for n in ('w_gate', 'w_up', 'w_down')), s5=s5)


def _tail(x, o, y3, sg, lw, g_final, tm):
    rows = x.shape[0]
    m = _merge(o, y3.reshape(y3.shape[0], rows, 128), lw['w_attn_out'], lw['w_glu'], sg, tm)
    x1, h = _outproj(m, x, lw['w_out'], lw['g_ffn'], tm)
    return _ffn(h, x1, *lw['ffn_w'], g_final, tm)


def _feature_major_rows(kv_t, t0):
    b, f, t = kv_t.shape
    return kv_t[:, :, t0:].reshape(b, f // KV_COLS, N_KV, HEAD_DIM, t - t0).transpose(0, 4, 1, 2, 3)


def _layer_prompt(x, lw, slopes, g_out):
    b, t, d = x.shape
    rows = b * t
    assert t % (PROMPT_PAGES_PER_STEP * PAGE) == 0
    q, u3, sg, gt, kvc_t, kvw_t = _in_proj(x.reshape(rows, d), lw['g_mix'], lw['w_in'], 1024, seq_len=t)

    pooled = _pool_pages(kvc_t, lambda bi, pg: (bi, 0, pg), (), b, t // PAGE, lw['pool_mats'],
                         PROMPT_PAGES_PER_STEP)
    _, vc_t, kc_rows = _cmp_finish(pooled, lw['pool'], lw['pe'], lw['phi'])
    o = _attn_prompt(slopes, q, kc_rows, vc_t, kvc_t, kvw_t, gt, b, t)

    gp = lw['s5'][3].shape[1]
    h0 = jnp.zeros((b, gp), F32)
    y3, hr, hi = _s5(u3, h0, h0, lw['s5'], b)

    y = _tail(x.reshape(rows, d), o, y3, sg, lw, g_out, 512).reshape(b, t, d)
    n_win = min(WINDOW, t)
    ng = gp // SSM_STATE
    return (y, _feature_major_rows(kvc_t, 0), _feature_major_rows(kvw_t, t - n_win),
            hr.reshape(b, ng, SSM_STATE), hi.reshape(b, ng, SSM_STATE))


def _layer_sample(x, cache, page_table, win_buf, h_re, h_im, lw, slopes, g_out):
    b, t, d = x.shape
    rows = b * t
    npages = page_table.shape[1]
    past = npages * cache.shape[1]
    n_buf = win_buf.shape[1]
    assert cache.shape[1] == PAGE and rows % S5_L == 0 and t == S5_L
    assert past % CMP_STRIDE == 0 and t < CMP_STRIDE and past % SEL_BLOCK == 0 and t <= SEL_BLOCK
    npg = DECODE_PAGES_PER_STEP
    assert npages % npg == 0
    q, u3, sg, gt, kvc, kvw = _in_proj(x.reshape(rows, d), lw['g_mix'], lw['w_in'], rows)

    cache_t = cache.transpose(0, 2, 3, 4, 1).reshape(cache.shape[0], 4 * KV_COLS, PAGE)
    win_t = win_buf.transpose(0, 2, 3, 4, 1).reshape(b, 2 * KV_COLS, n_buf)
    pooled = _pool_pages(cache_t, lambda bi, pg, pt: (pt[bi, pg], 0, 0), (page_table,), b, npages,
                         lw['pool_mats'], npg)
    kc_t, vc_t, _ = _cmp_finish(pooled, lw['pool'], lw['pe'], lw['phi'])

    nrows = N_HEADS * t
    eye = jnp.eye(N_KV, dtype=BF16)
    q5 = q.reshape(b, t, N_KV, GROUP, HEAD_DIM).transpose(0, 2, 3, 1, 4)
    qbd = (q5[:, :, :, :, None, :] * eye[None, :, None, None, :, None]).reshape(b, nrows, KV_COLS)
    maskbd = jnp.repeat(jnp.repeat(jnp.eye(N_KV, dtype=F32), GROUP * t, axis=0), HEAD_DIM, axis=1)
    slopecol = jnp.repeat(slopes, t).reshape(nrows, 1)
    g3 = gt.reshape(b, t, N_KV, 128)[..., :3 * GROUP].reshape(b, t, N_KV, 3, GROUP)
    g3 = g3.transpose(0, 2, 4, 1, 3).reshape(b, nrows, 3)
    gates_rows = jnp.pad(g3, ((0, 0), (0, 0), (0, 1)))
    n_blk = -(-(past + t) // SEL_BLOCK)

    opart, sel_t = _attn_dec1(qbd, kc_t, vc_t, win_t, kvw.reshape(b, t, 2 * KV_COLS), gates_rows, slopecol, maskbd,
                              past, t, n_blk)
    bps = npg * PAGE // SEL_BLOCK
    nstep = npages // npg
    nb_all = (nstep + 1) * bps
    sel = jnp.pad(sel_t[:, :n_blk], ((0, 0), (0, nb_all - n_blk), (0, 0)))
    sel = sel.reshape(b, nstep + 1, bps, N_KV, 1, t).transpose(0, 1, 3, 4, 5, 2)
    selrows = jnp.broadcast_to(sel, (b, nstep + 1, N_KV, GROUP, t, bps)).reshape(b, nstep + 1, nrows, bps)
    knew = jnp.pad(kvc.reshape(b, t, 4 * KV_COLS)[:, :, 2 * KV_COLS:], ((0, 0), (0, PAGE - t), (0, 0)))
    o_rows = _attn_dec2(cache_t, page_table, qbd, selrows.astype(BF16), slopecol, knew, opart, gates_rows,
                        maskbd, past, t, npg)
    o = (o_rows.reshape(b, N_KV, GROUP, t, HEAD_DIM).transpose(0, 3, 1, 2, 4).reshape(rows, Q_COLS).astype(BF16))

    gp = h_re.shape[1] * h_re.shape[2]
    y3, hr, hi = _s5(u3, h_re.reshape(b, gp), h_im.reshape(b, gp), lw['s5'], b)

    y, *ffn_w_bf16 = _tail(x.reshape(rows, d), o, y3, sg, lw, g_out, rows)
    y = y.reshape(b, t, d)
    win_all = jnp.concatenate([win_buf, kvw.reshape(b, t, 2, N_KV, HEAD_DIM)], axis=1)
    n_keep = min(WINDOW, n_buf + t)
    return (y, kvc.reshape(b, t, 4, N_KV, HEAD_DIM), win_all[:, n_buf + t - n_keep:], hr.reshape(h_re.shape),
            hi.reshape(h_im.shape), tuple(ffn_w_bf16))


def kernel(x_prompt, x_sample, cache_kv, state_win, state_ssm_re, state_ssm_im, page_table, g_mix, w_in, w_cmp_pe, w_cmp_pool, w_cmp_phi, w_attn_out, ssm_lam_re, ssm_lam_im, ssm_log_dt, ssm_b_re, ssm_b_im, ssm_c_re, ssm_c_im, ssm_d, w_glu, w_out, g_ffn, w_gate, w_up, w_down, g_final):
    depth = g_mix.shape[0]
    assert depth == 1, "final norm is fused into the last layer's FFN; one layer supported"
    params = dict(g_mix=g_mix, w_in=w_in, w_cmp_pe=w_cmp_pe, w_cmp_pool=w_cmp_pool, w_cmp_phi=w_cmp_phi,
                  w_attn_out=w_attn_out, ssm_lam_re=ssm_lam_re, ssm_lam_im=ssm_lam_im, ssm_log_dt=ssm_log_dt,
                  ssm_b_re=ssm_b_re, ssm_b_im=ssm_b_im, ssm_c_re=ssm_c_re, ssm_c_im=ssm_c_im, ssm_d=ssm_d,
                  w_glu=w_glu, w_out=w_out, g_ffn=g_ffn, w_gate=w_gate, w_up=w_up, w_down=w_down)
    slopes = _alibi_slopes()
    outs = [[] for _ in range(8)]
    xp, xs = x_prompt, x_sample
    for l in range(depth):
        lw = _layer_weights(params, l)
        xs, kvs, wins, hrs, his, ffn_w = _layer_sample(xs, cache_kv[l], page_table, state_win[l], state_ssm_re[l],
                                                       state_ssm_im[l], lw, slopes, g_final)
        xp, kvp, winp, hrp, hip = _layer_prompt(xp, dict(lw, ffn_w=ffn_w), slopes, g_final)
        for lst, v in zip(outs, (kvp, winp, hrp, hip, kvs, wins, hrs, his)):
            lst.append(v)
    st = [jnp.stack(v) for v in outs]
    return (xp, xs, st[0], st[1], st[2], st[3], st[4], st[5], st[6], st[7])
```

```python
import functools
import math

import numpy as np
import jax
import jax.numpy as jnp
from jax import lax
from jax.experimental import pallas as pl
from jax.experimental.pallas import tpu as pltpu

F32 = jnp.float32
BF16 = jnp.bfloat16

N_HEADS = 16
HEAD_DIM = 64
N_KV = 4
GROUP = N_HEADS // N_KV
CMP_LEN = 32
CMP_STRIDE = 16
SEL_BLOCK = 64
SEL_TOP = 16
WINDOW = 512
SSM_GROUP = 16
SSM_STATE = 64
EPS = 1e-6
NEG = -1e30
FORCE_SCORE = 1e4

KV_COLS = N_KV * HEAD_DIM
Q_COLS = N_HEADS * HEAD_DIM
PAGE = 128
CMP_PER_PAGE = PAGE // CMP_STRIDE
PROMPT_PAGES_PER_STEP = 16
DECODE_PAGES_PER_STEP = 32
S5_L = 8
S5_SLAB = 128 // SSM_GROUP
VMEM_LIMIT = 56 * 1024 * 1024
LOG2E = 1.0 / math.log(2.0)
Q_SCALE = HEAD_DIM ** -0.5 * LOG2E


def _dot(a, b):
    return jnp.dot(a, b, preferred_element_type=F32)


def _dot_nt(a, b):
    return lax.dot_general(a, b, (((1,), (1,)), ((), ())), preferred_element_type=F32)


def _split(a):
    hi = a.astype(BF16)
    lo = (a - hi.astype(F32)).astype(BF16)
    return hi, lo


def _dot3(a, b):
    ah, al = _split(a)
    bh, bl = _split(b)
    return _dot(ah, bh) + _dot(ah, bl) + _dot(al, bh)


def _dot_nt_hl(w_bf16, a):
    ah, al = _split(a)
    return _dot_nt(w_bf16, ah) + _dot_nt(w_bf16, al)


def _gelu(x):
    return 0.5 * x * (1.0 + jnp.tanh(math.sqrt(2.0 / math.pi) * (x + 0.044715 * (x * x * x))))


def _iota(shape, dim):
    return lax.broadcasted_iota(jnp.int32, shape, dim)


def _log2(n):
    assert n > 0 and n & (n - 1) == 0, n
    return n.bit_length() - 1


def _div_pow2(x, n):
    return jnp.right_shift(x, _log2(n))


def _mod_pow2(x, n):
    return jnp.bitwise_and(x, (1 << _log2(n)) - 1)


def _softmax2_rows(s):
    m = jnp.max(s, axis=-1, keepdims=True)
    e = jnp.exp2(s - m)
    return e / jnp.sum(e, axis=-1, keepdims=True)


IN_TN = 512
_Q_T, _U_T, _MG_T, _KV_T = 2, 2, 8, 3
_KVC_T = 2
_U_0 = _Q_T
_MG_0 = _U_0 + _U_T
_GT_0 = _MG_0 + _MG_T
_KV_0 = _GT_0 + 1
IN_TILES = _KV_0 + _KV_T


def _inproj_kernel(x_ref, g_ref, wq_ref, wu_ref, wmg_ref, wgt_ref, wkv_ref, q_ref, u_ref, sg_ref, gt_ref, kvc_ref,
                   kvw_ref, xn_ref, *, kv_feature_major):
    j = pl.program_id(1)

    @pl.when(j == 0)
    def _():
        x = x_ref[...]
        r = lax.rsqrt(jnp.mean(x * x, axis=-1, keepdims=True) + EPS)
        xn_ref[...] = (x * r * g_ref[...]).astype(BF16)

    @pl.when(j < _U_0)
    def _():
        q_ref[...] = (_dot(xn_ref[...], wq_ref[...]) * Q_SCALE).astype(BF16)

    @pl.when((j >= _U_0) & (j < _MG_0))
    def _():
        z = _dot(xn_ref[...], wu_ref[...])
        for s in range(IN_TN // 128):
            u_ref[s] = z[:, 128 * s:128 * (s + 1)].reshape(u_ref.shape[1:])

    @pl.when((j >= _MG_0) & (j < _GT_0))
    def _():
        sg_ref[...] = jax.nn.sigmoid(_dot(xn_ref[...], wmg_ref[...])).astype(BF16)

    @pl.when(j == _GT_0)
    def _():
        gt_ref[...] = jax.nn.sigmoid(_dot(xn_ref[...], wgt_ref[...]))

    @pl.when(j >= _KV_0)
    def _():
        if kv_feature_major:
            z = _dot_nt(wkv_ref[...], xn_ref[...])
        else:
            z = _dot(xn_ref[...], wkv_ref[...])

        @pl.when(j < _KV_0 + _KVC_T)
        def _():
            kvc_ref[...] = z.reshape(kvc_ref.shape)

        @pl.when(j >= _KV_0 + _KVC_T)
        def _():
            kvw_ref[...] = z.reshape(kvw_ref.shape)


def _in_proj(x, g_mix, w, tm, seq_len=None):
    rows, d = x.shape
    feature_major = seq_len is not None

    def col(lo, n):
        return lambda i, j: (i, jnp.clip(j - lo, 0, n - 1))

    if feature_major:
        tiles_per_seq = seq_len // tm
        kvc_spec = pl.BlockSpec((1, IN_TN, tm), lambda i, j: (i // tiles_per_seq, jnp.clip(j - _KV_0, 0, _KVC_T - 1),
                                                              i % tiles_per_seq))
        kvw_spec = pl.BlockSpec((1, IN_TN, tm), lambda i, j: (i // tiles_per_seq, 0, i % tiles_per_seq))
        kvc_shape = jax.ShapeDtypeStruct((rows // seq_len, _KVC_T * IN_TN, seq_len), F32)
        kvw_shape = jax.ShapeDtypeStruct((rows // seq_len, IN_TN, seq_len), F32)
        wkv, wkv_spec = w['kv_t'], pl.BlockSpec((IN_TN, d), lambda i, j: (jnp.clip(j - _KV_0, 0, _KV_T - 1), 0))
    else:
        kvc_spec = pl.BlockSpec((tm, IN_TN), col(_KV_0, _KVC_T))
        kvw_spec = pl.BlockSpec((tm, IN_TN), lambda i, j: (i, 0))
        kvc_shape = jax.ShapeDtypeStruct((rows, _KVC_T * IN_TN), F32)
        kvw_shape = jax.ShapeDtypeStruct((rows, IN_TN), F32)
        wkv, wkv_spec = w['kv'], pl.BlockSpec((d, IN_TN), lambda i, j: (0, jnp.clip(j - _KV_0, 0, _KV_T - 1)))
    n_slab = IN_TN // 128
    return pl.pallas_call(
        functools.partial(_inproj_kernel, kv_feature_major=feature_major),
        grid=(rows // tm, IN_TILES),
        in_specs=[
            pl.BlockSpec((tm, d), lambda i, j: (i, 0), pipeline_mode=pl.Buffered(1)),
            pl.BlockSpec((1, d), lambda i, j: (0, 0)),
            pl.BlockSpec((d, IN_TN), lambda i, j: (0, jnp.clip(j, 0, _Q_T - 1))),
            pl.BlockSpec((d, IN_TN), lambda i, j: (0, jnp.clip(j - _U_0, 0, _U_T - 1))),
            pl.BlockSpec((d, IN_TN), lambda i, j: (0, jnp.clip(j - _MG_0, 0, _MG_T - 1))),
            pl.BlockSpec((d, IN_TN), lambda i, j: (0, 0), pipeline_mode=pl.Buffered(1)),
            wkv_spec,
        ],
        out_specs=[
            pl.BlockSpec((tm, IN_TN), col(0, _Q_T)),
            pl.BlockSpec((n_slab, tm // S5_L, S5_L, 128), lambda i, j: (jnp.clip(j - _U_0, 0, _U_T - 1), i, 0, 0)),
            pl.BlockSpec((tm, IN_TN), col(_MG_0, _MG_T)),
            pl.BlockSpec((tm, IN_TN), lambda i, j: (i, 0)),
            kvc_spec, kvw_spec,
        ],
        out_shape=[
            jax.ShapeDtypeStruct((rows, _Q_T * IN_TN), BF16),
            jax.ShapeDtypeStruct((_U_T * n_slab, rows // S5_L, S5_L, 128), F32),
            jax.ShapeDtypeStruct((rows, _MG_T * IN_TN), BF16),
            jax.ShapeDtypeStruct((rows, IN_TN), F32),
            kvc_shape, kvw_shape,
        ],
        scratch_shapes=[pltpu.VMEM((tm, d), BF16)],
        compiler_params=pltpu.CompilerParams(
            dimension_semantics=("parallel", "arbitrary"), vmem_limit_bytes=VMEM_LIMIT),
        name="in_proj",
    )(x, g_mix.reshape(1, d), w['q'], w['u'], w['mg'], w['gt'], wkv)


def _pack_w_in(w_in, ssm_width):
    cuts = np.cumsum([Q_COLS, 6 * KV_COLS, 3 * N_HEADS, ssm_width]).tolist()
    wq, wkv, wgt, wu, wmg = jnp.split(w_in, cuts, axis=1)
    assert wq.shape[1] == _Q_T * IN_TN and wkv.shape[1] == _KV_T * IN_TN
    assert wu.shape[1] == _U_T * IN_TN and wmg.shape[1] == _MG_T * IN_TN and IN_TN == N_KV * 128
    d = w_in.shape[0]
    wgt = wgt.reshape(d, 3, N_KV, GROUP).transpose(0, 2, 1, 3).reshape(d, N_KV, 3 * GROUP)
    wgt = jnp.pad(wgt, ((0, 0), (0, 0), (0, 128 - 3 * GROUP))).reshape(d, IN_TN)
    return dict(q=wq.astype(BF16), u=wu.astype(BF16), mg=wmg.astype(BF16), gt=wgt.astype(BF16),
                kv=wkv.astype(BF16), kv_t=wkv.T.astype(BF16))


def _pool_kernel(*refs, npg, n_tables):
    refs = refs[n_tables:]
    prev_ref = refs[0]
    page_refs = refs[1:npg + 1]
    mk_ref, mv_ref, mprev_ref, out_ref = refs[npg + 1:]
    pages = [prev_ref[0].astype(BF16)] + [page_refs[j][0].astype(BF16) for j in range(npg)]
    mpg = mk_ref.shape[0] // PAGE
    nblk = mk_ref.shape[1]
    for seg in range(npg // mpg):
        x = jnp.concatenate(pages[1 + seg * mpg:1 + (seg + 1) * mpg], axis=1)
        for half, m_ref in ((0, mk_ref), (1, mv_ref)):
            rows = slice(half * KV_COLS, (half + 1) * KV_COLS)
            head = _dot(pages[seg * mpg][rows], mprev_ref[half])
            if seg == 0:
                head = jnp.where(pl.program_id(1) > 0, head, 0.0)
            out_ref[0, rows, seg * nblk:(seg + 1) * nblk] = _dot(x[rows], m_ref[...]) + head


def _pool_matrices(pool, npg):
    nblk = npg * CMP_PER_PAGE
    cols = []
    for r in range(CMP_PER_PAGE):
        start = CMP_STRIDE * (r - 1)
        lo, hi = max(start, 0), min(start + CMP_LEN, PAGE)
        cols.append(jnp.pad(pool[:, lo - start:hi - start], ((0, 0), (lo, PAGE - hi))))
    body = jnp.stack(cols, axis=-1)
    carry = jnp.pad(pool[:, :CMP_STRIDE], ((0, 0), (PAGE - CMP_STRIDE, 0)))
    first = jnp.asarray(np.arange(CMP_PER_PAGE) == 0, F32)
    same = jnp.eye(npg, dtype=F32)[None, :, None, :, None]
    nxt = jnp.eye(npg, k=1, dtype=F32)[None, :, None, :, None]
    m = same * body[:, None, :, None, :] + nxt * (carry[:, None, :, None, None] * first)
    m = m.reshape(2, npg * PAGE, nblk).astype(BF16)
    mprev = (carry[:, :, None] * jnp.asarray(np.arange(nblk) == 0, F32)).astype(BF16)
    return m[0], m[1], mprev


def _pool_pages(pages, page_index, tables, nseq, npages, mats, npg):
    assert npg % (mats[0].shape[0] // PAGE) == 0
    nstep = npages // npg
    nblk = npg * CMP_PER_PAGE
    mk, mv, mprev = mats

    def spec(off):
        return pl.BlockSpec((1, 2 * KV_COLS, PAGE),
                            lambda b, s, *t: page_index(b, jnp.maximum(s * npg + off, 0), *t))

    const = lambda shape: pl.BlockSpec(shape, lambda b, s, *t: (0,) * len(shape))
    return pl.pallas_call(
        functools.partial(_pool_kernel, npg=npg, n_tables=len(tables)),
        grid_spec=pltpu.PrefetchScalarGridSpec(
            num_scalar_prefetch=len(tables),
            grid=(nseq, nstep),
            in_specs=[spec(j) for j in range(-1, npg)] + [const(mk.shape), const(mv.shape), const(mprev.shape)],
            out_specs=pl.BlockSpec((1, 2 * KV_COLS, nblk), lambda b, s, *t: (b, 0, s)),
        ),
        out_shape=jax.ShapeDtypeStruct((nseq, 2 * KV_COLS, npages * CMP_PER_PAGE), F32),
        compiler_params=pltpu.CompilerParams(
            dimension_semantics=("parallel", "arbitrary"), vmem_limit_bytes=VMEM_LIMIT),
        name="cmp_pool",
    )(*tables, *([pages] * (npg + 1)), mk, mv, mprev)


def _cmp_finish_kernel(pooled_ref, poolw_ref, pe_ref, phi_ref, kc_ref, vc_ref, kcr_ref):
    bias = jnp.sum(poolw_ref[...] * pe_ref[...], axis=1, keepdims=True)
    x = _gelu(pooled_ref[0] + bias)
    kc = _dot3(phi_ref[0], x[:KV_COLS])
    kc_ref[0] = kc.astype(BF16)
    kcr_ref[0] = kc.T.astype(BF16)
    vc_ref[0] = _dot3(phi_ref[1], x[KV_COLS:]).astype(BF16)


def _cmp_finish(pooled, pool, pe, phi):
    nseq, _, n = pooled.shape
    poolw = jnp.concatenate([jnp.broadcast_to(pool[s][None, :], (KV_COLS, CMP_LEN)) for s in range(2)], axis=0)
    pe_t = jnp.concatenate([jnp.tile(pe[s].T, (N_KV, 1)) for s in range(2)], axis=0)
    phi_bd = jnp.stack([jnp.kron(jnp.eye(N_KV, dtype=F32), phi[s].T) for s in range(2)])
    spec = pl.BlockSpec((1, KV_COLS, n), lambda b: (b, 0, 0))
    return pl.pallas_call(
        _cmp_finish_kernel,
        grid=(nseq,),
        in_specs=[
            pl.BlockSpec((1, 2 * KV_COLS, n), lambda b: (b, 0, 0)),
            pl.BlockSpec((2 * KV_COLS, CMP_LEN), lambda b: (0, 0)),
            pl.BlockSpec((2 * KV_COLS, CMP_LEN), lambda b: (0, 0)),
            pl.BlockSpec((2, KV_COLS, KV_COLS), lambda b: (0, 0, 0)),
        ],
        out_specs=[spec, spec, pl.BlockSpec((1, n, KV_COLS), lambda b: (b, 0, 0))],
        out_shape=[jax.ShapeDtypeStruct((nseq, KV_COLS, n), BF16)] * 2
        + [jax.ShapeDtypeStruct((nseq, n, KV_COLS), BF16)],
        compiler_params=pltpu.CompilerParams(dimension_semantics=("parallel",)),
        name="cmp_finish",
    )(pooled, poolw, pe_t, phi_bd)


ATT_TQ = 512
ATT_TK = 512
NBLK_PAD = 128
POS_ROWS = 16
AUG = HEAD_DIM + POS_ROWS


def _attn_prompt_kernel(q_ref, kca_ref, vc_ref, ks_ref, vs_ref, kw_ref, vw_ref, gt_ref, ovt_ref, ktab_ref,
                        stab_ref, o_ref, ksa_ref, vsa_ref, kwa_ref, vwa_ref, qat_ref, m_ref, acc_ref,
                        *, tq, tk, n_sel, n_top):
    qi = pl.program_id(2)
    q0 = qi * tq
    seq = ksa_ref.shape[0]

    @pl.when(qi == 0)
    def _():
        zero = jnp.zeros((128 - HEAD_DIM, seq), F32)
        ones = jnp.where(_iota((POS_ROWS, seq), 0) == 0, 1.0, 0.0).astype(BF16)
        for src, dst, lo in ((ks_ref, ksa_ref, NBLK_PAD), (kw_ref, kwa_ref, 0)):
            k_rows = jnp.concatenate([src[0], zero], axis=0).T[:, :HEAD_DIM].astype(BF16)
            dst[:, lo:] = jnp.concatenate([k_rows, ktab_ref[:, NBLK_PAD:]], axis=1)
        ksa_ref[:, 0:NBLK_PAD] = ktab_ref[:, 0:NBLK_PAD]
        for src, dst in ((vs_ref, vsa_ref), (vw_ref, vwa_ref)):
            dst[0:HEAD_DIM] = src[0].astype(BF16)
            dst[HEAD_DIM:] = ones

    q_t = q_ref[...].astype(F32).T
    q_t = jnp.concatenate([q_t[n * HEAD_DIM:(n + 1) * HEAD_DIM] for n in range(GROUP)], axis=1).astype(BF16)
    stab = stab_ref[0]
    sp_t = jnp.concatenate([jnp.broadcast_to(stab[:, n:n + 1], (POS_ROWS, tq)) for n in range(GROUP)], axis=1)
    qat_ref[NBLK_PAD:NBLK_PAD + HEAD_DIM] = q_t
    qat_ref[NBLK_PAD + HEAD_DIM:] = sp_t.astype(BF16)
    trow = q0 + _iota((1, tq), 1)
    lanes = [slice(n * tq, (n + 1) * tq) for n in range(GROUP)]

    ncmp = kca_ref.shape[2]
    nidx = _iota((ncmp, 1), 0)
    heads = lambda a: jnp.concatenate([a] * GROUP, axis=1)
    valid = (nidx >= 1) & (CMP_STRIDE * nidx + (CMP_STRIDE - 1) <= trow)
    valid = heads(jnp.where(valid, 1.0, 0.0)) > 0.5
    s = jnp.where(valid, _dot(kca_ref[0, 0], qat_ref[NBLK_PAD:]), NEG)
    e = jnp.exp2(s - jnp.max(s, axis=0, keepdims=True))
    pc = jnp.where(valid, e / jnp.sum(e, axis=0, keepdims=True), 0.0)
    o_c = _dot(vc_ref[0], pc.astype(BF16))

    ph, plo = _split((pc[:, lanes[0]] + pc[:, lanes[1]]) + (pc[:, lanes[2]] + pc[:, lanes[3]]))
    imp = (_dot(ovt_ref[...], ph) + _dot(ovt_ref[...], plo))[0:n_sel]
    blk = _iota((n_sel, 1), 0)
    cur = _div_pow2(q0 + _iota((1, tq), 1), SEL_BLOCK)
    forced = (blk == 0) | (blk == cur) | (blk == cur - 1)
    imp = jnp.where(forced, FORCE_SCORE, imp)
    imp = jnp.where(blk <= cur, imp, -1.0)
    cnt = jnp.zeros((n_sel, tq), F32)
    for j in range(n_sel):
        vj = imp[j:j + 1, :]
        tie = jnp.where(blk > j, 1.0, 0.0)
        cnt = cnt + jnp.where(vj > imp, 1.0, jnp.where(vj == imp, tie, 0.0))
    mask_t = jnp.where((cnt < n_top) & (blk <= cur), 0.0, NEG)
    mask_t = jnp.concatenate([mask_t, jnp.full((NBLK_PAD - n_sel, tq), NEG, F32)], axis=0).astype(BF16)
    qat_ref[0:NBLK_PAD] = heads(mask_t)

    m_ref[...] = jnp.full_like(m_ref, NEG)
    acc_ref[...] = jnp.zeros_like(acc_ref)

    def scores(kt):
        return _dot(ksa_ref[pl.ds(pl.multiple_of(kt * tk, tk), tk), :], qat_ref[...])

    def update(kt, s):
        update_keys(pl.multiple_of(kt * tk, tk), tk, s, None)

    def update_keys(k0, nk, s, lane_sl):
        gather = (lambda ref: ref[...]) if lane_sl is None else (
            lambda ref: jnp.concatenate([ref[:, sl] for sl in lane_sl], axis=1))
        m_old = gather(m_ref)
        m_new = jnp.maximum(m_old, jnp.max(s, axis=0, keepdims=True))
        p = jnp.exp2(s - m_new).astype(BF16)
        acc_new = jnp.exp2(m_old - m_new) * gather(acc_ref) + _dot(vsa_ref[:, pl.ds(k0, nk)], p)
        if lane_sl is None:
            acc_ref[...] = acc_new
            m_ref[...] = m_new
        else:
            w = acc_new.shape[1] // len(lane_sl)
            for i, sl in enumerate(lane_sl):
                acc_ref[:, sl] = acc_new[:, i * w:(i + 1) * w]
                m_ref[:, sl] = m_new[:, i * w:(i + 1) * w]

    def pair(j, carry):
        s0, s1 = scores(2 * j), scores(2 * j + 1)
        update(2 * j, s0)
        update(2 * j + 1, s1)
        return carry

    lax.fori_loop(0, qi // 2, pair, 0)

    @pl.when(qi % 2 == 1)
    def _():
        update(qi - 1, scores(qi - 1))

    hq = tq // 2
    upper = [slice(n * tq + hq, (n + 1) * tq) for n in range(GROUP)]
    k_mid = pl.multiple_of(q0 + hq, hq)
    s_d0 = (_dot(ksa_ref[pl.ds(pl.multiple_of(q0, hq), hq), :], qat_ref[...])
            + heads(jnp.where(q0 + _iota((hq, 1), 0) <= trow, 0.0, NEG)))
    s_d1 = (_dot(ksa_ref[pl.ds(k_mid, hq), :], jnp.concatenate([qat_ref[:, sl] for sl in upper], axis=1))
            + jnp.concatenate([jnp.where(_iota((hq, 1), 0) <= _iota((1, hq), 1), 0.0, NEG)] * GROUP, axis=1))

    span = WINDOW + hq
    o_w_half = []
    for h in range(2):
        qh = q0 + h * hq
        w0 = pl.multiple_of(jnp.maximum(qh - WINDOW, 0), hq)
        dist = (qh + _iota((1, hq), 1)) - (w0 + _iota((span, 1), 0))
        qa_h = jnp.concatenate([qat_ref[NBLK_PAD:, n * tq + h * hq:n * tq + (h + 1) * hq] for n in range(GROUP)], axis=1)
        s_win = _dot(kwa_ref[pl.ds(w0, span), :], qa_h) + heads(jnp.where((dist >= 0) & (dist < WINDOW), 0.0, NEG))
        ow = _dot(vwa_ref[:, pl.ds(w0, span)],
                  jnp.exp2(s_win - jnp.max(s_win, axis=0, keepdims=True)).astype(BF16))
        o_w_half.append(ow[:HEAD_DIM] / ow[HEAD_DIM:HEAD_DIM + 1])
    o_w = jnp.concatenate([o_w_half[h][:, n * hq:(n + 1) * hq] for n in range(GROUP) for h in range(2)], axis=1)
    update_keys(pl.multiple_of(q0, hq), hq, s_d0, None)
    update_keys(k_mid, hq, s_d1, upper)
    acc = acc_ref[...]
    o_s = acc[:HEAD_DIM] / acc[HEAD_DIM:HEAD_DIM + 1]

    gt_t = gt_ref[...].T
    outs = []
    for n in range(GROUP):
        outs.append(gt_t[n:n + 1] * o_c[:, lanes[n]] + gt_t[GROUP + n:GROUP + n + 1] * o_s[:, lanes[n]]
                    + gt_t[2 * GROUP + n:2 * GROUP + n + 1] * o_w[:, lanes[n]])
    o_ref[...] = jnp.concatenate(outs, axis=0).T.astype(BF16)


def _overlap_t(n_blk_rows, n_cmp):
    s = np.arange(n_blk_rows)[:, None]
    n = np.arange(n_cmp)[None, :]
    r = SEL_BLOCK // CMP_STRIDE
    return jnp.asarray(((n >= r * s) & (n <= r * s + r)).astype(np.float32), dtype=BF16)


def _block_expand(n_blk_rows, n_keys):
    s = np.arange(n_blk_rows)[:, None]
    k = np.arange(n_keys)[None, :]
    return jnp.asarray((k // SEL_BLOCK == s).astype(np.float32), dtype=BF16)


def _attn_prompt(slopes, q, kc_rows, vc_t, kvc_t, kvw_t, gates, nbatch, seq):
    ncmp = kc_rows.shape[1]
    tq, tk = ATT_TQ, ATT_TK
    assert tk == tq and seq % tk == 0 and WINDOW % tq == 0 and WINDOW + tq <= seq
    nq = seq // tq
    n_sel = seq // SEL_BLOCK
    ovt = _overlap_t(NBLK_PAD, ncmp)
    ktab = jnp.concatenate([_block_expand(NBLK_PAD, seq), _position_rows(seq)], axis=0).T
    cmp_end = CMP_STRIDE * np.arange(ncmp) + (CMP_STRIDE - 1)
    kca = jnp.concatenate([kc_rows.reshape(nbatch, ncmp, N_KV, HEAD_DIM).transpose(0, 2, 1, 3),
                           jnp.broadcast_to(_position_rows(cmp_end).T, (nbatch, N_KV, ncmp, POS_ROWS))], axis=-1)
    stab = _slope_table(slopes).astype(F32).transpose(0, 2, 1)
    per_slot = KV_COLS // HEAD_DIM

    def kv_spec(slot):
        return pl.BlockSpec((1, HEAD_DIM, seq), lambda b, g, i: (b, slot * per_slot + g, 0))

    q_spec = pl.BlockSpec((tq, GROUP * HEAD_DIM), lambda b, g, i: (b * nq + i, g))
    return pl.pallas_call(
        functools.partial(_attn_prompt_kernel, tq=tq, tk=tk, n_sel=n_sel, n_top=min(SEL_TOP, n_sel)),
        grid=(nbatch, N_KV, nq),
        in_specs=[
            q_spec,
            pl.BlockSpec((1, 1, ncmp, AUG), lambda b, g, i: (b, g, 0, 0)),
            pl.BlockSpec((1, HEAD_DIM, ncmp), lambda b, g, i: (b, g, 0)),
            kv_spec(2), kv_spec(3), kv_spec(0), kv_spec(1),
            pl.BlockSpec((tq, 128), lambda b, g, i: (b * nq + i, g)),
            pl.BlockSpec((NBLK_PAD, ncmp), lambda b, g, i: (0, 0)),
            pl.BlockSpec((seq, NBLK_PAD + POS_ROWS), lambda b, g, i: (0, 0)),
            pl.BlockSpec((1, POS_ROWS, GROUP), lambda b, g, i: (g, 0, 0)),
        ],
        out_specs=q_spec,
        out_shape=jax.ShapeDtypeStruct(q.shape, BF16),
        scratch_shapes=[
            pltpu.VMEM((seq, NBLK_PAD + AUG), BF16),
            pltpu.VMEM((AUG, seq), BF16),
            pltpu.VMEM((seq, AUG), BF16),
            pltpu.VMEM((AUG, seq), BF16),
            pltpu.VMEM((NBLK_PAD + AUG, GROUP * tq), BF16),
            pltpu.VMEM((1, GROUP * tq), F32),
            pltpu.VMEM((AUG, GROUP * tq), F32),
        ],
        compiler_params=pltpu.CompilerParams(
            dimension_semantics=("parallel", "parallel", "arbitrary"), vmem_limit_bytes=VMEM_LIMIT),
        name="attn_prompt",
    )(q, kca, vc_t, kvc_t, kvc_t, kvw_t, kvw_t, gates, ovt, ktab, stab)


def _diag_blocks(o, maskbd):
    o = o * maskbd
    return (o[:, 0:64] + o[:, 64:128]) + (o[:, 128:192] + o[:, 192:256])


def _attn_dec1_kernel(qbd_ref, kc_ref, vc_ref, win_ref, kvnew_ref, gates_ref, slope_ref, ovt_ref,
                      maskbd_ref, opart_ref, sel_ref, *, past, tdec, n_blk, n_top):
    qb = qbd_ref[0]
    nrows = qb.shape[0]
    tcol = past + _mod_pow2(_iota((nrows, 1), 0), tdec)
    pref = float(past + tdec)
    slope = slope_ref[...]
    maskbd = maskbd_ref[...]

    ncmp = kc_ref.shape[2]
    nrow = _iota((1, ncmp), 1)
    cend = CMP_STRIDE * nrow + (CMP_STRIDE - 1)
    valid = (nrow >= 1) & (cend <= tcol)
    s = _dot(qb, kc_ref[0])
    s = jnp.where(valid, s + slope * (cend.astype(F32) - pref), NEG)
    p = jnp.where(valid, _softmax2_rows(s), 0.0)
    o_c = _diag_blocks(_dot_nt(p.astype(BF16), vc_ref[0]), maskbd)

    per_g = GROUP * tdec
    psum = jnp.concatenate(
        [sum(p[gi * per_g + n * tdec: gi * per_g + (n + 1) * tdec] for n in range(GROUP)) for gi in range(N_KV)],
        axis=0)
    imp = _dot_nt_hl(ovt_ref[...], psum)
    nb_pad = imp.shape[0]
    blk = _iota((nb_pad, 1), 0)
    cur = _div_pow2(past + _mod_pow2(_iota((1, N_KV * tdec), 1), tdec), SEL_BLOCK)
    forced = (blk == 0) | (blk == cur) | (blk == cur - 1)
    imp = jnp.where(forced, FORCE_SCORE, imp)
    imp = jnp.where(blk <= cur, imp, -1.0)
    imp = jnp.where(blk < n_blk, imp, -2.0)
    sel = jnp.zeros(imp.shape, F32)
    for _ in range(n_top):
        mx = jnp.max(imp, axis=0, keepdims=True)
        first = jnp.min(jnp.where(imp == mx, blk, nb_pad), axis=0, keepdims=True)
        pick = blk == first
        sel = jnp.where(pick, 1.0, sel)
        imp = jnp.where(pick, -jnp.inf, imp)
    sel_ref[0] = sel

    nbuf = win_ref.shape[2]
    knew = kvnew_ref[0]
    zpad = jnp.zeros((PAGE - tdec, KV_COLS), F32)
    kn = jnp.concatenate([knew[:, :KV_COLS], zpad], axis=0).astype(BF16)
    vn = jnp.concatenate([knew[:, KV_COLS:], zpad], axis=0).astype(BF16)
    sw = jnp.concatenate([_dot(qb, win_ref[0, :KV_COLS].astype(BF16)), _dot_nt(qb, kn)], axis=1)
    idx = _iota((1, nbuf + PAGE), 1)
    kposw = past - nbuf + idx
    dist = tcol - kposw
    validw = (dist >= 0) & (dist < WINDOW) & (kposw >= 0) & (idx < nbuf + tdec)
    sw = jnp.where(validw, sw + slope * (kposw.astype(F32) - pref), NEG)
    pw = _softmax2_rows(sw).astype(BF16)
    o_w = _dot_nt(pw[:, :nbuf], win_ref[0, KV_COLS:].astype(BF16)) + _dot(pw[:, nbuf:], vn)
    o_w = _diag_blocks(o_w, maskbd)

    gates = gates_ref[0]
    opart_ref[0] = gates[:, 0:1] * o_c + gates[:, 2:3] * o_w


def _attn_dec1(qbd, kc_t, vc_t, win_t, kvnew, gates_rows, slopecol, maskbd, past, tdec, n_blk):
    nseq, nrows, _ = qbd.shape
    ncmp = kc_t.shape[2]
    nb_pad = -(-n_blk // 8) * 8
    ovt = _overlap_t(nb_pad, ncmp)
    nbuf = win_t.shape[2]
    return pl.pallas_call(
        functools.partial(_attn_dec1_kernel, past=past, tdec=tdec, n_blk=n_blk, n_top=min(SEL_TOP, n_blk)),
        grid=(nseq,),
        in_specs=[
            pl.BlockSpec((1, nrows, KV_COLS), lambda b: (b, 0, 0)),
            pl.BlockSpec((1, KV_COLS, ncmp), lambda b: (b, 0, 0)),
            pl.BlockSpec((1, KV_COLS, ncmp), lambda b: (b, 0, 0)),
            pl.BlockSpec((1, 2 * KV_COLS, nbuf), lambda b: (b, 0, 0)),
            pl.BlockSpec((1, tdec, 2 * KV_COLS), lambda b: (b, 0, 0)),
            pl.BlockSpec((1, nrows, 4), lambda b: (b, 0, 0)),
            pl.BlockSpec((nrows, 1), lambda b: (0, 0)),
            pl.BlockSpec((nb_pad, ncmp), lambda b: (0, 0)),
            pl.BlockSpec((nrows, KV_COLS), lambda b: (0, 0)),
        ],
        out_specs=[
            pl.BlockSpec((1, nrows, HEAD_DIM), lambda b: (b, 0, 0)),
            pl.BlockSpec((1, nb_pad, N_KV * tdec), lambda b: (b, 0, 0)),
        ],
        out_shape=[
            jax.ShapeDtypeStruct((nseq, nrows, HEAD_DIM), F32),
            jax.ShapeDtypeStruct((nseq, nb_pad, N_KV * tdec), F32),
        ],
        compiler_params=pltpu.CompilerParams(dimension_semantics=("parallel",), vmem_limit_bytes=VMEM_LIMIT),
        name="attn_dec_select",
    )(qbd, kc_t, vc_t, win_t, kvnew, gates_rows, slopecol, ovt, maskbd)


def _attn_dec2_kernel(pt_ref, *refs, npg, past, tdec):
    page_refs = refs[:npg]
    (qbd_ref, selrow_ref, sellast_ref, e_ref, slope_ref, knew_ref, opart_ref, gates_ref, maskbd_ref,
     o_ref, m_ref, l_ref, acc_ref) = refs[npg:]
    step = pl.program_id(1)
    nkeys = npg * PAGE
    qb = qbd_ref[0]
    nrows = qb.shape[0]
    pref = float(past + tdec)
    slope = slope_ref[...]

    @pl.when(step == 0)
    def _():
        m_ref[...] = jnp.full_like(m_ref, NEG)
        l_ref[...] = jnp.zeros_like(l_ref)
        acc_ref[...] = jnp.zeros_like(acc_ref)

    def update(s, pv_fn):
        m_old = m_ref[...]
        m_new = jnp.maximum(m_old, jnp.max(s, axis=-1, keepdims=True))
        alpha = jnp.exp2(m_old - m_new)
        p = jnp.exp2(s - m_new)
        l_ref[...] = alpha * l_ref[...] + jnp.sum(p, axis=-1, keepdims=True)
        acc_ref[...] = alpha * acc_ref[...] + pv_fn(p.astype(BF16))
        m_ref[...] = m_new

    k_t = jnp.concatenate([page_refs[j][0, :KV_COLS].astype(BF16) for j in range(npg)], axis=1)
    v_t = jnp.concatenate([page_refs[j][0, KV_COLS:].astype(BF16) for j in range(npg)], axis=1)
    kpos = step * nkeys + _iota((1, nkeys), 1)
    chosen = _dot(selrow_ref[0, 0], e_ref[...])
    s = _dot(qb, k_t) + slope * (kpos.astype(F32) - pref) + jnp.where(chosen > 0.5, 0.0, NEG)
    update(s, lambda p: _dot_nt(p, v_t))

    @pl.when(step == pl.num_programs(1) - 1)
    def _():
        tcol = past + _mod_pow2(_iota((nrows, 1), 0), tdec)
        kn = knew_ref[0]
        kposn = past + _iota((1, PAGE), 1)
        sn = _dot_nt(qb, kn[:, :KV_COLS].astype(BF16))
        ok = (sellast_ref[0, 0][:, 0:1].astype(F32) > 0.5) & (kposn <= tcol)
        sn = sn + slope * (kposn.astype(F32) - pref) + jnp.where(ok, 0.0, NEG)
        update(sn, lambda p: _dot(p, kn[:, KV_COLS:].astype(BF16)))
        o_s = _diag_blocks(acc_ref[...] / l_ref[...], maskbd_ref[...])
        o_ref[0] = opart_ref[0] + gates_ref[0][:, 1:2] * o_s


def _attn_dec2(cache_t, page_table, qbd, selrows, slopecol, knew, opart, gates_rows, maskbd, past, tdec, npg):
    nseq, npages = page_table.shape
    nstep = npages // npg
    nrows = qbd.shape[1]
    blk_per_step = npg * PAGE // SEL_BLOCK
    emat = _block_expand(blk_per_step, npg * PAGE)
    page_specs = [
        pl.BlockSpec((1, 2 * KV_COLS, PAGE), lambda b, s, pt, j=j: (pt[b, s * npg + j], 1, 0))
        for j in range(npg)
    ]
    per_seq = lambda shape: pl.BlockSpec((1,) + shape, lambda b, s, pt: (b,) + (0,) * len(shape))
    const = lambda shape: pl.BlockSpec(shape, lambda b, s, pt: (0,) * len(shape))
    return pl.pallas_call(
        functools.partial(_attn_dec2_kernel, npg=npg, past=past, tdec=tdec),
        grid_spec=pltpu.PrefetchScalarGridSpec(
            num_scalar_prefetch=1,
            grid=(nseq, nstep),
            in_specs=page_specs + [
                per_seq((nrows, KV_COLS)),
                pl.BlockSpec((1, 1, nrows, blk_per_step), lambda b, s, pt: (b, s, 0, 0)),
                pl.BlockSpec((1, 1, nrows, blk_per_step), lambda b, s, pt: (b, nstep, 0, 0)),
                const((blk_per_step, npg * PAGE)),
                const((nrows, 1)),
                per_seq((PAGE, 2 * KV_COLS)),
                per_seq((nrows, HEAD_DIM)),
                per_seq((nrows, 4)),
                const((nrows, KV_COLS)),
            ],
            out_specs=per_seq((nrows, HEAD_DIM)),
            scratch_shapes=[
                pltpu.VMEM((nrows, 1), F32),
                pltpu.VMEM((nrows, 1), F32),
                pltpu.VMEM((nrows, KV_COLS), F32),
            ],
        ),
        out_shape=jax.ShapeDtypeStruct((nseq, nrows, HEAD_DIM), F32),
        compiler_params=pltpu.CompilerParams(
            dimension_semantics=("parallel", "arbitrary"), vmem_limit_bytes=VMEM_LIMIT),
        name="attn_dec_selected",
    )(page_table, *([cache_t] * npg), qbd, selrows, selrows, emat, slopecol, knew, opart, gates_rows, maskbd)


def _s5_weights(lam_re, lam_im, log_dt, b_re, b_im, c_re, c_im, d_skip):
    hp = lax.Precision.HIGHEST
    L = S5_L
    ng, p = lam_re.shape
    c = SSM_GROUP
    ns, gs = ng // S5_SLAB, S5_SLAB
    dt = jnp.exp(log_dt)[:, None]
    lr, li = lam_re, lam_im
    mag = jnp.exp(lr * dt)
    ar = mag * jnp.cos(li * dt)
    ai = mag * jnp.sin(li * dt)
    den = lr * lr + li * li
    fr = ((ar - 1.0) * lr + ai * li) / den
    fi = (ai * lr - (ar - 1.0) * li) / den
    bt_re, bt_im = b_re.transpose(0, 2, 1), b_im.transpose(0, 2, 1)
    bbr = fr[:, None, :] * bt_re - fi[:, None, :] * bt_im
    bbi = fr[:, None, :] * bt_im + fi[:, None, :] * bt_re
    j = jnp.arange(L + 1, dtype=F32)[:, None, None]
    pmag = jnp.exp(j * (lr * dt))
    pr = pmag * jnp.cos(j * (li * dt))
    pi = pmag * jnp.sin(j * (li * dt))
    abr = pr[:, :, None, :] * bbr - pi[:, :, None, :] * bbi
    abi = pr[:, :, None, :] * bbi + pi[:, :, None, :] * bbr
    kd = (jnp.einsum('jgkp,gcp->jgkc', abr, c_re, precision=hp)
          - jnp.einsum('jgkp,gcp->jgkc', abi, c_im, precision=hp))
    kd = kd.at[0].add(d_skip.reshape(ng, 1, c) * jnp.eye(c, dtype=F32))
    kc = kd[:L].reshape(L, ns, gs * c, c)
    abc = jnp.concatenate([abr[:L], abi[:L]], axis=-1).reshape(L, ns, gs * c, 2 * p)
    cr = c_re[None] * pr[1:, :, None, :] - c_im[None] * pi[1:, :, None, :]
    ci = -(c_re[None] * pi[1:, :, None, :] + c_im[None] * pr[1:, :, None, :])
    co = jnp.stack([cr, ci], axis=1).reshape(L, 2, ns, gs, c, p)
    al = jnp.stack([pr[L].reshape(-1), pi[L].reshape(-1)])
    return kc.astype(BF16), abc.astype(BF16), co.astype(BF16), al


def _s5_expanders():
    gs, c, p = S5_SLAB, SSM_GROUP, SSM_STATE
    lane = np.arange(gs * c)
    rep_c = (np.arange(c)[:, None] == lane[None, :] % c)
    same_g = (lane[:, None] // c == lane[None, :] // c)
    st = np.arange(2 * gs * p)
    rp = np.arange(2 * p)
    rep_p = (rp[:, None] // p == st[None, :] // (gs * p)) & (rp[:, None] % p == st[None, :] % p)
    g_in = (lane[:, None] // c == (st[None, :] // p) % gs)
    as_bf = lambda a: jnp.asarray(a.astype(np.float32), dtype=BF16)
    return as_bf(rep_c), as_bf(same_g), as_bf(rep_p), as_bf(g_in)


def _chunk_lanes(u_ref):
    return jnp.concatenate([u_ref[0, :, t, :] for t in range(S5_L)], axis=1)


def _s5_local_kernel(u_ref, abc_ref, rep_p_ref, g_in_ref, pre_ref, pim_ref, wp_ref):
    for s in range(S5_L):
        blk = _dot(abc_ref[S5_L - 1 - s, 0], rep_p_ref[...]).astype(BF16) * g_in_ref[...]
        wp_ref[128 * s:128 * (s + 1), :] = blk
    r = _dot(_chunk_lanes(u_ref).astype(BF16), wp_ref[...])
    half = r.shape[1] // 2
    pre_ref[...] = r[:, :half]
    pim_ref[...] = r[:, half:]


def _s5_scan_kernel(pre_ref, pim_ref, al_ref, h0r_ref, h0i_ref, hsr_ref, hsi_ref, hfr_ref, hfi_ref):
    nchunk = pre_ref.shape[1]
    ar = al_ref[0:1, :]
    ai = al_ref[1:2, :]

    def body(k, carry):
        cr, ci = carry
        hsr_ref[:, pl.ds(k, 1), :] = cr[:, None, :]
        hsi_ref[:, pl.ds(k, 1), :] = ci[:, None, :]
        xr = pre_ref[:, pl.ds(k, 1), :][:, 0, :]
        xi = pim_ref[:, pl.ds(k, 1), :][:, 0, :]
        return ar * cr - ai * ci + xr, ar * ci + ai * cr + xi

    cr, ci = lax.fori_loop(0, nchunk, body, (h0r_ref[...], h0i_ref[...]), unroll=min(nchunk, 4))
    hfr_ref[...] = cr
    hfi_ref[...] = ci


def _s5_out_kernel(u_ref, kc_ref, co_ref, rep_c_ref, same_g_ref, hsr_ref, hsi_ref, y_ref, wt_ref, wot_ref):
    rep_c = rep_c_ref[...]
    lag = [_dot(kc_ref[dl, 0], rep_c).astype(BF16) * same_g_ref[...] for dl in range(S5_L)]
    zero = jnp.zeros((128, 128), BF16)
    for s in range(S5_L):
        for t in range(S5_L):
            wt_ref[128 * s:128 * (s + 1), 128 * t:128 * (t + 1)] = lag[t - s] if t >= s else zero
    wot_ref[...] = jnp.zeros_like(wot_ref)
    c, p = SSM_GROUP, SSM_STATE
    for t in range(S5_L):
        for r in range(2):
            for g in range(S5_SLAB):
                col = (r * S5_SLAB + g) * p
                wot_ref[128 * t + c * g:128 * t + c * (g + 1), col:col + p] = co_ref[t, r, 0, g]
    hs = jnp.concatenate([hsr_ref[...], hsi_ref[...]], axis=1).astype(BF16)
    y = _dot(_chunk_lanes(u_ref).astype(BF16), wt_ref[...]) + _dot_nt(hs, wot_ref[...])
    for t in range(S5_L):
        y_ref[0, :, t, :] = y[:, 128 * t:128 * (t + 1)]


def _s5(u3, h0r, h0i, weights, nbatch):
    kc, abc, co, al = weights
    rep_c, same_g, rep_p, g_in = _s5_expanders()
    ns, nb = u3.shape[0], u3.shape[1]
    nchunk = nb // nbatch
    gp = al.shape[1]
    lb = gp // ns
    d = S5_L * 128
    cp = pltpu.CompilerParams(dimension_semantics=("parallel",), vmem_limit_bytes=VMEM_LIMIT)
    u_spec = pl.BlockSpec((1, nb, S5_L, 128), lambda j: (j, 0, 0, 0))
    st_spec = pl.BlockSpec((nb, lb), lambda j: (0, j))
    slab = lambda a: pl.BlockSpec((a.shape[0], 1) + a.shape[2:], lambda j: (0, j) + (0,) * (a.ndim - 2))
    const = lambda a: pl.BlockSpec(a.shape, lambda j: (0,) * a.ndim)
    pre, pim = pl.pallas_call(
        _s5_local_kernel,
        grid=(ns,),
        in_specs=[u_spec, slab(abc), const(rep_p), const(g_in)],
        out_specs=[st_spec] * 2,
        out_shape=[jax.ShapeDtypeStruct((nb, gp), F32)] * 2,
        scratch_shapes=[pltpu.VMEM((d, 2 * lb), BF16)],
        compiler_params=cp,
        name="s5_local",
    )(u3, abc, rep_p, g_in)

    ls = 2 * lb
    seq3 = pl.BlockSpec((nbatch, nchunk, ls), lambda j: (0, 0, j))
    row = lambda r: pl.BlockSpec((r, ls), lambda j: (0, j))
    hsr, hsi, hfr, hfi = pl.pallas_call(
        _s5_scan_kernel,
        grid=(gp // ls,),
        in_specs=[seq3, seq3, row(2), row(nbatch), row(nbatch)],
        out_specs=[seq3, seq3, row(nbatch), row(nbatch)],
        out_shape=[jax.ShapeDtypeStruct((nbatch, nchunk, gp), F32)] * 2 + [jax.ShapeDtypeStruct((nbatch, gp), F32)] * 2,
        compiler_params=cp,
        name="s5_scan",
    )(pre.reshape(nbatch, nchunk, gp), pim.reshape(nbatch, nchunk, gp), al, h0r, h0i)

    y3 = pl.pallas_call(
        _s5_out_kernel,
        grid=(ns,),
        in_specs=[u_spec, slab(kc),
                  pl.BlockSpec(co.shape[:2] + (1,) + co.shape[3:], lambda j: (0, 0, j, 0, 0, 0)),
                  const(rep_c), const(same_g), st_spec, st_spec],
        out_specs=u_spec,
        out_shape=jax.ShapeDtypeStruct(u3.shape, F32),
        scratch_shapes=[pltpu.VMEM((d, d), BF16), pltpu.VMEM((2 * lb, d), BF16)],
        compiler_params=cp,
        name="s5_out",
    )(u3, kc, co, rep_c, same_g, hsr.reshape(nb, gp), hsi.reshape(nb, gp))
    return y3, hfr, hfi


def _merge_kernel(o_ref, y_ref, wa_ref, wg1_ref, wg2_ref, sga_ref, sgb_ref, m_ref, gy_ref):
    @pl.when(pl.program_id(1) == 0)
    def _():
        y = jnp.concatenate([y_ref[s] for s in range(y_ref.shape[0])], axis=1)
        gy_ref[...] = _gelu(y).astype(BF16)

    ya = _dot(o_ref[...], wa_ref[...])
    gy = gy_ref[...]
    yb = _dot(gy, wg1_ref[...]) * jax.nn.sigmoid(_dot(gy, wg2_ref[...]))
    m = sga_ref[...].astype(F32) * ya + sgb_ref[...].astype(F32) * yb
    m_ref[...] = m.astype(BF16)


def _merge(o, y3, w_attn_out, w_glu, sg, tm, tn=1024):
    rows, d = o.shape[0], w_attn_out.shape[1]
    nj = d // tn
    kq, ks = w_attn_out.shape[0], w_glu.shape[0]
    ns = y3.shape[0]
    return pl.pallas_call(
        _merge_kernel,
        grid=(rows // tm, nj),
        in_specs=[
            pl.BlockSpec((tm, kq), lambda i, j: (i, 0)),
            pl.BlockSpec((ns, tm, 128), lambda i, j: (0, i, 0)),
            pl.BlockSpec((kq, tn), lambda i, j: (0, j)),
            pl.BlockSpec((ks, tn), lambda i, j: (0, j)),
            pl.BlockSpec((ks, tn), lambda i, j: (0, j + nj)),
            pl.BlockSpec((tm, tn), lambda i, j: (i, j)),
            pl.BlockSpec((tm, tn), lambda i, j: (i, j + nj)),
        ],
        out_specs=pl.BlockSpec((tm, tn), lambda i, j: (i, j)),
        out_shape=jax.ShapeDtypeStruct((rows, d), BF16),
        scratch_shapes=[pltpu.VMEM((tm, ks), BF16)],
        compiler_params=pltpu.CompilerParams(
            dimension_semantics=("parallel", "arbitrary"), vmem_limit_bytes=VMEM_LIMIT),
        name="merge",
    )(o, y3, w_attn_out, w_glu, w_glu, sg, sg)


def _outproj_kernel(m_ref, x_ref, w_ref, g_ref, x1_ref, h_ref):
    x1 = x_ref[...] + _dot(m_ref[...], w_ref[...])
    x1_ref[...] = x1
    r = lax.rsqrt(jnp.mean(x1 * x1, axis=-1, keepdims=True) + EPS)
    h_ref[...] = (x1 * r * g_ref[...]).astype(BF16)


def _outproj(m, x, w_out, g_ffn, tm):
    rows, d = x.shape
    row = pl.BlockSpec((tm, d), lambda i: (i, 0))
    return pl.pallas_call(
        _outproj_kernel,
        grid=(rows // tm,),
        in_specs=[row, row, pl.BlockSpec((d, d), lambda i: (0, 0), pipeline_mode=pl.Buffered(1)),
                  pl.BlockSpec((1, d), lambda i: (0, 0))],
        out_specs=[row, row],
        out_shape=[jax.ShapeDtypeStruct((rows, d), F32), jax.ShapeDtypeStruct((rows, d), BF16)],
        compiler_params=pltpu.CompilerParams(dimension_semantics=("parallel",), vmem_limit_bytes=VMEM_LIMIT),
        name="out_proj",
    )(m, x, w_out, g_ffn.reshape(1, d))


def _ffn_kernel(h_ref, x1_ref, wg_ref, wu_ref, wd_ref, gf_ref, y_ref, *rest):
    acc_ref = rest[-1]
    f = pl.program_id(1)

    @pl.when(f == 0)
    def _():
        acc_ref[...] = jnp.zeros_like(acc_ref)

    wg, wu, wd = wg_ref[...].astype(BF16), wu_ref[...].astype(BF16), wd_ref[...].astype(BF16)
    for copy_ref, w in zip(rest[:-1], (wg, wu, wd)):
        copy_ref[...] = w
    h = h_ref[...]
    a = _dot(h, wg)
    a = (a * jax.nn.sigmoid(a)) * _dot(h, wu)
    acc_ref[...] += _dot(a.astype(BF16), wd)

    @pl.when(f == pl.num_programs(1) - 1)
    def _():
        y = x1_ref[...] + acc_ref[...]
        r = lax.rsqrt(jnp.mean(y * y, axis=-1, keepdims=True) + EPS)
        y_ref[...] = y * r * gf_ref[...]


def _ffn(h, x1, w_gate, w_up, w_down, g_final, tm, tf=512):
    rows, d = x1.shape
    dff = w_gate.shape[1]
    emit = w_gate.dtype != BF16
    assert not emit or rows == tm
    row = pl.BlockSpec((tm, d), lambda i, f: (i, 0))
    w_specs = [pl.BlockSpec((d, tf), lambda i, f: (0, f)),
               pl.BlockSpec((d, tf), lambda i, f: (0, f)),
               pl.BlockSpec((tf, d), lambda i, f: (f, 0))]
    y_shape = jax.ShapeDtypeStruct((rows, d), F32)
    copies = [jax.ShapeDtypeStruct(w.shape, BF16) for w in (w_gate, w_up, w_down)] if emit else []
    out = pl.pallas_call(
        _ffn_kernel,
        grid=(rows // tm, dff // tf),
        in_specs=[row, row] + w_specs + [pl.BlockSpec((1, d), lambda i, f: (0, 0))],
        out_specs=[row] + (w_specs if emit else []),
        out_shape=[y_shape] + copies,
        scratch_shapes=[pltpu.VMEM((tm, d), F32)],
        compiler_params=pltpu.CompilerParams(
            dimension_semantics=("parallel", "arbitrary"), vmem_limit_bytes=VMEM_LIMIT),
        name="ffn",
    )(h, x1, w_gate, w_up, w_down, g_final.reshape(1, d))
    return tuple(out) if emit else out[0]


def _alibi_slopes():
    return jnp.exp2(-8.0 * jnp.arange(1, N_HEADS + 1, dtype=F32) / N_HEADS) * LOG2E


def _slope_table(slopes):
    s1 = slopes.astype(BF16)
    r1 = slopes - s1.astype(F32)
    s2 = r1.astype(BF16)
    s3 = (r1 - s2.astype(F32)).astype(BF16)
    tab = jnp.stack([s1, s1, s2, s2, s3, s3] + [jnp.zeros_like(s1)] * 10, axis=-1)
    return tab.reshape(N_KV, GROUP, 16)


def _position_rows(pos):
    pos = np.arange(pos) if np.isscalar(pos) else np.asarray(pos)
    hi, lo = (pos // 64) * 64, pos % 64
    assert pos.max() < 64 * 256
    rows = np.stack([hi, lo, hi, lo, hi, lo] + [np.zeros_like(pos)] * 10).astype(np.float32)
    return jnp.asarray(rows, dtype=BF16)


def _layer_weights(p, l):
    w_in = _pack_w_in(p['w_in'][l], p['ssm_d'].shape[-1])
    s5 = _s5_weights(*(p[n][l] for n in ('ssm_lam_re', 'ssm_lam_im', 'ssm_log_dt', 'ssm_b_re', 'ssm_b_im',
                                          'ssm_c_re', 'ssm_c_im', 'ssm_d')))
    cast = lambda n: p[n][l].astype(BF16)
    return dict(
        g_mix=p['g_mix'][l], w_in=w_in,
        pool=p['w_cmp_pool'][l], pe=p['w_cmp_pe'][l], phi=p['w_cmp_phi'][l],
        pool_mats=_pool_matrices(p['w_cmp_pool'][l], PROMPT_PAGES_PER_STEP),
        w_attn_out=cast('w_attn_out'), w_glu=cast('w_glu'), w_out=cast('w_out'), g_ffn=p['g_ffn'][l],
        ffn_w=tuple(p[n][l] for n in ('w_gate', 'w_up', 'w_down')), s5=s5)


def _tail(x, o, y3, sg, lw, g_final, tm):
    rows = x.shape[0]
    m = _merge(o, y3.reshape(y3.shape[0], rows, 128), lw['w_attn_out'], lw['w_glu'], sg, tm)
    x1, h = _outproj(m, x, lw['w_out'], lw['g_ffn'], tm)
    return _ffn(h, x1, *lw['ffn_w'], g_final, tm)


def _feature_major_rows(kv_t, t0):
    b, f, t = kv_t.shape
    return kv_t[:, :, t0:].reshape(b, f // KV_COLS, N_KV, HEAD_DIM, t - t0).transpose(0, 4, 1, 2, 3)


def _layer_prompt(x, lw, slopes, g_out):
    b, t, d = x.shape
    rows = b * t
    assert t % (PROMPT_PAGES_PER_STEP * PAGE) == 0
    q, u3, sg, gt, kvc_t, kvw_t = _in_proj(x.reshape(rows, d), lw['g_mix'], lw['w_in'], 1024, seq_len=t)

    pooled = _pool_pages(kvc_t, lambda bi, pg: (bi, 0, pg), (), b, t // PAGE, lw['pool_mats'],
                         PROMPT_PAGES_PER_STEP)
    _, vc_t, kc_rows = _cmp_finish(pooled, lw['pool'], lw['pe'], lw['phi'])
    o = _attn_prompt(slopes, q, kc_rows, vc_t, kvc_t, kvw_t, gt, b, t)

    gp = lw['s5'][3].shape[1]
    h0 = jnp.zeros((b, gp), F32)
    y3, hr, hi = _s5(u3, h0, h0, lw['s5'], b)

    y = _tail(x.reshape(rows, d), o, y3, sg, lw, g_out, 512).reshape(b, t, d)
    n_win = min(WINDOW, t)
    ng = gp // SSM_STATE
    return (y, _feature_major_rows(kvc_t, 0), _feature_major_rows(kvw_t, t - n_win),
            hr.reshape(b, ng, SSM_STATE), hi.reshape(b, ng, SSM_STATE))


def _layer_sample(x, cache, page_table, win_buf, h_re, h_im, lw, slopes, g_out):
    b, t, d = x.shape
    rows = b * t
    npages = page_table.shape[1]
    past = npages * cache.shape[1]
    n_buf = win_buf.shape[1]
    assert cache.shape[1] == PAGE and rows % S5_L == 0 and t == S5_L
    assert past % CMP_STRIDE == 0 and t < CMP_STRIDE and past % SEL_BLOCK == 0 and t <= SEL_BLOCK
    npg = DECODE_PAGES_PER_STEP
    assert npages % npg == 0
    q, u3, sg, gt, kvc, kvw = _in_proj(x.reshape(rows, d), lw['g_mix'], lw['w_in'], rows)

    cache_t = cache.transpose(0, 2, 3, 4, 1).reshape(cache.shape[0], 4 * KV_COLS, PAGE)
    win_t = win_buf.transpose(0, 2, 3, 4, 1).reshape(b, 2 * KV_COLS, n_buf)
    pooled = _pool_pages(cache_t, lambda bi, pg, pt: (pt[bi, pg], 0, 0), (page_table,), b, npages,
                         lw['pool_mats'], npg)
    kc_t, vc_t, _ = _cmp_finish(pooled, lw['pool'], lw['pe'], lw['phi'])

    nrows = N_HEADS * t
    eye = jnp.eye(N_KV, dtype=BF16)
    q5 = q.reshape(b, t, N_KV, GROUP, HEAD_DIM).transpose(0, 2, 3, 1, 4)
    qbd = (q5[:, :, :, :, None, :] * eye[None, :, None, None, :, None]).reshape(b, nrows, KV_COLS)
    maskbd = jnp.repeat(jnp.repeat(jnp.eye(N_KV, dtype=F32), GROUP * t, axis=0), HEAD_DIM, axis=1)
    slopecol = jnp.repeat(slopes, t).reshape(nrows, 1)
    g3 = gt.reshape(b, t, N_KV, 128)[..., :3 * GROUP].reshape(b, t, N_KV, 3, GROUP)
    g3 = g3.transpose(0, 2, 4, 1, 3).reshape(b, nrows, 3)
    gates_rows = jnp.pad(g3, ((0, 0), (0, 0), (0, 1)))
    n_blk = -(-(past + t) // SEL_BLOCK)

    opart, sel_t = _attn_dec1(qbd, kc_t, vc_t, win_t, kvw.reshape(b, t, 2 * KV_COLS), gates_rows, slopecol, maskbd,
                              past, t, n_blk)
    bps = npg * PAGE // SEL_BLOCK
    nstep = npages // npg
    nb_all = (nstep + 1) * bps
    sel = jnp.pad(sel_t[:, :n_blk], ((0, 0), (0, nb_all - n_blk), (0, 0)))
    sel = sel.reshape(b, nstep + 1, bps, N_KV, 1, t).transpose(0, 1, 3, 4, 5, 2)
    selrows = jnp.broadcast_to(sel, (b, nstep + 1, N_KV, GROUP, t, bps)).reshape(b, nstep + 1, nrows, bps)
    knew = jnp.pad(kvc.reshape(b, t, 4 * KV_COLS)[:, :, 2 * KV_COLS:], ((0, 0), (0, PAGE - t), (0, 0)))
    o_rows = _attn_dec2(cache_t, page_table, qbd, selrows.astype(BF16), slopecol, knew, opart, gates_rows,
                        maskbd, past, t, npg)
    o = (o_rows.reshape(b, N_KV, GROUP, t, HEAD_DIM).transpose(0, 3, 1, 2, 4).reshape(rows, Q_COLS).astype(BF16))

    gp = h_re.shape[1] * h_re.shape[2]
    y3, hr, hi = _s5(u3, h_re.reshape(b, gp), h_im.reshape(b, gp), lw['s5'], b)

    y, *ffn_w_bf16 = _tail(x.reshape(rows, d), o, y3, sg, lw, g_out, rows)
    y = y.reshape(b, t, d)
    win_all = jnp.concatenate([win_buf, kvw.reshape(b, t, 2, N_KV, HEAD_DIM)], axis=1)
    n_keep = min(WINDOW, n_buf + t)
    return (y, kvc.reshape(b, t, 4, N_KV, HEAD_DIM), win_all[:, n_buf + t - n_keep:], hr.reshape(h_re.shape),
            hi.reshape(h_im.shape), tuple(ffn_w_bf16))


def kernel(x_prompt, x_sample, cache_kv, state_win, state_ssm_re, state_ssm_im, page_table, g_mix, w_in, w_cmp_pe, w_cmp_pool, w_cmp_phi, w_attn_out, ssm_lam_re, ssm_lam_im, ssm_log_dt, ssm_b_re, ssm_b_im, ssm_c_re, ssm_c_im, ssm_d, w_glu, w_out, g_ffn, w_gate, w_up, w_down, g_final):
    depth = g_mix.shape[0]
    assert depth == 1, "final norm is fused into the last layer's FFN; one layer supported"
    params = dict(g_mix=g_mix, w_in=w_in, w_cmp_pe=w_cmp_pe, w_cmp_pool=w_cmp_pool, w_cmp_phi=w_cmp_phi,
                  w_attn_out=w_attn_out, ssm_lam_re=ssm_lam_re, ssm_lam_im=ssm_lam_im, ssm_log_dt=ssm_log_dt,
                  ssm_b_re=ssm_b_re, ssm_b_im=ssm_b_im, ssm_c_re=ssm_c_re, ssm_c_im=ssm_c_im, ssm_d=ssm_d,
                  w_glu=w_glu, w_out=w_out, g_ffn=g_ffn, w_gate=w_gate, w_up=w_up, w_down=w_down)
    slopes = _alibi_slopes()
    outs = [[] for _ in range(8)]
    xp, xs = x_prompt, x_sample
    for l in range(depth):
        lw = _layer_weights(params, l)
        xs, kvs, wins, hrs, his, ffn_w = _layer_sample(xs, cache_kv[l], page_table, state_win[l], state_ssm_re[l],
                                                       state_ssm_im[l], lw, slopes, g_final)
        xp, kvp, winp, hrp, hip = _layer_prompt(xp, dict(lw, ffn_w=ffn_w), slopes, g_final)
        for lst, v in zip(outs, (kvp, winp, hrp, hip, kvs, wins, hrs, his)):
            lst.append(v)
    st = [jnp.stack(v) for v in outs]
    return (xp, xs, st[0], st[1], st[2], st[3], st[4], st[5], st[6], st[7])
```

```python
import functools
import math

import numpy as np
import jax
import jax.numpy as jnp
from jax import lax
from jax.experimental import pallas as pl
from jax.experimental.pallas import tpu as pltpu

F32 = jnp.float32
BF16 = jnp.bfloat16

N_HEADS = 16
HEAD_DIM = 64
N_KV = 4
GROUP = N_HEADS // N_KV
CMP_LEN = 32
CMP_STRIDE = 16
SEL_BLOCK = 64
SEL_TOP = 16
WINDOW = 512
SSM_GROUP = 16
SSM_STATE = 64
EPS = 1e-6
NEG = -1e30
FORCE_SCORE = 1e4

KV_COLS = N_KV * HEAD_DIM
Q_COLS = N_HEADS * HEAD_DIM
PAGE = 128
CMP_PER_PAGE = PAGE // CMP_STRIDE
PROMPT_PAGES_PER_STEP = 16
DECODE_PAGES_PER_STEP = 64
S5_L = 8
S5_SLAB = 128 // SSM_GROUP
VMEM_LIMIT = 56 * 1024 * 1024
LOG2E = 1.0 / math.log(2.0)
Q_SCALE = HEAD_DIM ** -0.5 * LOG2E


def _dot(a, b):
    return jnp.dot(a, b, preferred_element_type=F32)


def _dot_nt(a, b):
    return lax.dot_general(a, b, (((1,), (1,)), ((), ())), preferred_element_type=F32)


def _split(a):
    hi = a.astype(BF16)
    lo = (a - hi.astype(F32)).astype(BF16)
    return hi, lo


def _dot3(a, b):
    ah, al = _split(a)
    bh, bl = _split(b)
    return _dot(ah, bh) + _dot(ah, bl) + _dot(al, bh)


def _dot_nt_hl(w_bf16, a):
    ah, al = _split(a)
    return _dot_nt(w_bf16, ah) + _dot_nt(w_bf16, al)


def _gelu(x):
    return 0.5 * x * (1.0 + jnp.tanh(math.sqrt(2.0 / math.pi) * (x + 0.044715 * (x * x * x))))


def _iota(shape, dim):
    return lax.broadcasted_iota(jnp.int32, shape, dim)


def _log2(n):
    assert n > 0 and n & (n - 1) == 0, n
    return n.bit_length() - 1


def _div_pow2(x, n):
    return jnp.right_shift(x, _log2(n))


def _mod_pow2(x, n):
    return jnp.bitwise_and(x, (1 << _log2(n)) - 1)


def _softmax2_rows(s):
    m = jnp.max(s, axis=-1, keepdims=True)
    e = jnp.exp2(s - m)
    return e / jnp.sum(e, axis=-1, keepdims=True)


IN_TN = 512
_Q_T, _U_T, _MG_T, _KV_T = 2, 2, 8, 3
_KVC_T = 2
_U_0 = _Q_T
_MG_0 = _U_0 + _U_T
_GT_0 = _MG_0 + _MG_T
_KV_0 = _GT_0 + 1
IN_TILES = _KV_0 + _KV_T


def _inproj_kernel(x_ref, g_ref, wq_ref, wu_ref, wmg_ref, wgt_ref, wkv_ref, q_ref, u_ref, sg_ref, gt_ref, kvc_ref,
                   kvw_ref, xn_ref, *, kv_feature_major):
    j = pl.program_id(1)

    @pl.when(j == 0)
    def _():
        x = x_ref[...]
        r = lax.rsqrt(jnp.mean(x * x, axis=-1, keepdims=True) + EPS)
        xn_ref[...] = (x * r * g_ref[...]).astype(BF16)

    @pl.when(j < _U_0)
    def _():
        q_ref[...] = (_dot(xn_ref[...], wq_ref[...]) * Q_SCALE).astype(BF16)

    @pl.when((j >= _U_0) & (j < _MG_0))
    def _():
        z = _dot(xn_ref[...], wu_ref[...])
        for s in range(IN_TN // 128):
            u_ref[s] = z[:, 128 * s:128 * (s + 1)].reshape(u_ref.shape[1:])

    @pl.when((j >= _MG_0) & (j < _GT_0))
    def _():
        sg_ref[...] = jax.nn.sigmoid(_dot(xn_ref[...], wmg_ref[...])).astype(BF16)

    @pl.when(j == _GT_0)
    def _():
        gt_ref[...] = jax.nn.sigmoid(_dot(xn_ref[...], wgt_ref[...]))

    @pl.when(j >= _KV_0)
    def _():
        if kv_feature_major:
            z = _dot_nt(wkv_ref[...], xn_ref[...])
        else:
            z = _dot(xn_ref[...], wkv_ref[...])

        @pl.when(j < _KV_0 + _KVC_T)
        def _():
            kvc_ref[...] = z.reshape(kvc_ref.shape)

        @pl.when(j >= _KV_0 + _KVC_T)
        def _():
            kvw_ref[...] = z.reshape(kvw_ref.shape)


def _in_proj(x, g_mix, w, tm, seq_len=None):
    rows, d = x.shape
    feature_major = seq_len is not None

    def col(lo, n):
        return lambda i, j: (i, jnp.clip(j - lo, 0, n - 1))

    if feature_major:
        tiles_per_seq = seq_len // tm
        kvc_spec = pl.BlockSpec((1, IN_TN, tm), lambda i, j: (i // tiles_per_seq, jnp.clip(j - _KV_0, 0, _KVC_T - 1),
                                                              i % tiles_per_seq))
        kvw_spec = pl.BlockSpec((1, IN_TN, tm), lambda i, j: (i // tiles_per_seq, 0, i % tiles_per_seq))
        kvc_shape = jax.ShapeDtypeStruct((rows // seq_len, _KVC_T * IN_TN, seq_len), F32)
        kvw_shape = jax.ShapeDtypeStruct((rows // seq_len, IN_TN, seq_len), F32)
        wkv, wkv_spec = w['kv_t'], pl.BlockSpec((IN_TN, d), lambda i, j: (jnp.clip(j - _KV_0, 0, _KV_T - 1), 0))
    else:
        kvc_spec = pl.BlockSpec((tm, IN_TN), col(_KV_0, _KVC_T))
        kvw_spec = pl.BlockSpec((tm, IN_TN), lambda i, j: (i, 0))
        kvc_shape = jax.ShapeDtypeStruct((rows, _KVC_T * IN_TN), F32)
        kvw_shape = jax.ShapeDtypeStruct((rows, IN_TN), F32)
        wkv, wkv_spec = w['kv'], pl.BlockSpec((d, IN_TN), lambda i, j: (0, jnp.clip(j - _KV_0, 0, _KV_T - 1)))
    n_slab = IN_TN // 128
    return pl.pallas_call(
        functools.partial(_inproj_kernel, kv_feature_major=feature_major),
        grid=(rows // tm, IN_TILES),
        in_specs=[
            pl.BlockSpec((tm, d), lambda i, j: (i, 0), pipeline_mode=pl.Buffered(1)),
            pl.BlockSpec((1, d), lambda i, j: (0, 0)),
            pl.BlockSpec((d, IN_TN), lambda i, j: (0, jnp.clip(j, 0, _Q_T - 1))),
            pl.BlockSpec((d, IN_TN), lambda i, j: (0, jnp.clip(j - _U_0, 0, _U_T - 1))),
            pl.BlockSpec((d, IN_TN), lambda i, j: (0, jnp.clip(j - _MG_0, 0, _MG_T - 1))),
            pl.BlockSpec((d, IN_TN), lambda i, j: (0, 0), pipeline_mode=pl.Buffered(1)),
            wkv_spec,
        ],
        out_specs=[
            pl.BlockSpec((tm, IN_TN), col(0, _Q_T)),
            pl.BlockSpec((n_slab, tm // S5_L, S5_L, 128), lambda i, j: (jnp.clip(j - _U_0, 0, _U_T - 1), i, 0, 0)),
            pl.BlockSpec((tm, IN_TN), col(_MG_0, _MG_T)),
            pl.BlockSpec((tm, IN_TN), lambda i, j: (i, 0)),
            kvc_spec, kvw_spec,
        ],
        out_shape=[
            jax.ShapeDtypeStruct((rows, _Q_T * IN_TN), BF16),
            jax.ShapeDtypeStruct((_U_T * n_slab, rows // S5_L, S5_L, 128), F32),
            jax.ShapeDtypeStruct((rows, _MG_T * IN_TN), BF16),
            jax.ShapeDtypeStruct((rows, IN_TN), F32),
            kvc_shape, kvw_shape,
        ],
        scratch_shapes=[pltpu.VMEM((tm, d), BF16)],
        compiler_params=pltpu.CompilerParams(
            dimension_semantics=("parallel", "arbitrary"), vmem_limit_bytes=VMEM_LIMIT),
        name="in_proj",
    )(x, g_mix.reshape(1, d), w['q'], w['u'], w['mg'], w['gt'], wkv)


def _pack_w_in(w_in, ssm_width):
    cuts = np.cumsum([Q_COLS, 6 * KV_COLS, 3 * N_HEADS, ssm_width]).tolist()
    wq, wkv, wgt, wu, wmg = jnp.split(w_in, cuts, axis=1)
    assert wq.shape[1] == _Q_T * IN_TN and wkv.shape[1] == _KV_T * IN_TN
    assert wu.shape[1] == _U_T * IN_TN and wmg.shape[1] == _MG_T * IN_TN and IN_TN == N_KV * 128
    d = w_in.shape[0]
    wgt = wgt.reshape(d, 3, N_KV, GROUP).transpose(0, 2, 1, 3).reshape(d, N_KV, 3 * GROUP)
    wgt = jnp.pad(wgt, ((0, 0), (0, 0), (0, 128 - 3 * GROUP))).reshape(d, IN_TN)
    return dict(q=wq.astype(BF16), u=wu.astype(BF16), mg=wmg.astype(BF16), gt=wgt.astype(BF16),
                kv=wkv.astype(BF16), kv_t=wkv.T.astype(BF16))


def _pool_kernel(*refs, npg, n_tables):
    refs = refs[n_tables:]
    prev_ref = refs[0]
    page_refs = refs[1:npg + 1]
    mk_ref, mv_ref, mprev_ref, out_ref = refs[npg + 1:]
    pages = [prev_ref[0].astype(BF16)] + [page_refs[j][0].astype(BF16) for j in range(npg)]
    mpg = mk_ref.shape[0] // PAGE
    nblk = mk_ref.shape[1]
    for seg in range(npg // mpg):
        x = jnp.concatenate(pages[1 + seg * mpg:1 + (seg + 1) * mpg], axis=1)
        for half, m_ref in ((0, mk_ref), (1, mv_ref)):
            rows = slice(half * KV_COLS, (half + 1) * KV_COLS)
            head = _dot(pages[seg * mpg][rows], mprev_ref[half])
            if seg == 0:
                head = jnp.where(pl.program_id(1) > 0, head, 0.0)
            out_ref[0, rows, seg * nblk:(seg + 1) * nblk] = _dot(x[rows], m_ref[...]) + head


def _pool_matrices(pool, npg):
    nblk = npg * CMP_PER_PAGE
    cols = []
    for r in range(CMP_PER_PAGE):
        start = CMP_STRIDE * (r - 1)
        lo, hi = max(start, 0), min(start + CMP_LEN, PAGE)
        cols.append(jnp.pad(pool[:, lo - start:hi - start], ((0, 0), (lo, PAGE - hi))))
    body = jnp.stack(cols, axis=-1)
    carry = jnp.pad(pool[:, :CMP_STRIDE], ((0, 0), (PAGE - CMP_STRIDE, 0)))
    first = jnp.asarray(np.arange(CMP_PER_PAGE) == 0, F32)
    same = jnp.eye(npg, dtype=F32)[None, :, None, :, None]
    nxt = jnp.eye(npg, k=1, dtype=F32)[None, :, None, :, None]
    m = same * body[:, None, :, None, :] + nxt * (carry[:, None, :, None, None] * first)
    m = m.reshape(2, npg * PAGE, nblk).astype(BF16)
    mprev = (carry[:, :, None] * jnp.asarray(np.arange(nblk) == 0, F32)).astype(BF16)
    return m[0], m[1], mprev


def _pool_pages(pages, page_index, tables, nseq, npages, mats, npg):
    assert npg % (mats[0].shape[0] // PAGE) == 0
    nstep = npages // npg
    nblk = npg * CMP_PER_PAGE
    mk, mv, mprev = mats

    def spec(off):
        return pl.BlockSpec((1, 2 * KV_COLS, PAGE),
                            lambda b, s, *t: page_index(b, jnp.maximum(s * npg + off, 0), *t))

    const = lambda shape: pl.BlockSpec(shape, lambda b, s, *t: (0,) * len(shape))
    return pl.pallas_call(
        functools.partial(_pool_kernel, npg=npg, n_tables=len(tables)),
        grid_spec=pltpu.PrefetchScalarGridSpec(
            num_scalar_prefetch=len(tables),
            grid=(nseq, nstep),
            in_specs=[spec(j) for j in range(-1, npg)] + [const(mk.shape), const(mv.shape), const(mprev.shape)],
            out_specs=pl.BlockSpec((1, 2 * KV_COLS, nblk), lambda b, s, *t: (b, 0, s)),
        ),
        out_shape=jax.ShapeDtypeStruct((nseq, 2 * KV_COLS, npages * CMP_PER_PAGE), F32),
        compiler_params=pltpu.CompilerParams(
            dimension_semantics=("parallel", "arbitrary"), vmem_limit_bytes=VMEM_LIMIT),
        name="cmp_pool",
    )(*tables, *([pages] * (npg + 1)), mk, mv, mprev)


def _cmp_finish_kernel(pooled_ref, poolw_ref, pe_ref, phi_ref, kc_ref, vc_ref, kcr_ref):
    bias = jnp.sum(poolw_ref[...] * pe_ref[...], axis=1, keepdims=True)
    x = _gelu(pooled_ref[0] + bias)
    kc = _dot3(phi_ref[0], x[:KV_COLS])
    kc_ref[0] = kc.astype(BF16)
    kcr_ref[0] = kc.T.astype(BF16)
    vc_ref[0] = _dot3(phi_ref[1], x[KV_COLS:]).astype(BF16)


def _cmp_finish(pooled, pool, pe, phi):
    nseq, _, n = pooled.shape
    poolw = jnp.concatenate([jnp.broadcast_to(pool[s][None, :], (KV_COLS, CMP_LEN)) for s in range(2)], axis=0)
    pe_t = jnp.concatenate([jnp.tile(pe[s].T, (N_KV, 1)) for s in range(2)], axis=0)
    phi_bd = jnp.stack([jnp.kron(jnp.eye(N_KV, dtype=F32), phi[s].T) for s in range(2)])
    spec = pl.BlockSpec((1, KV_COLS, n), lambda b: (b, 0, 0))
    return pl.pallas_call(
        _cmp_finish_kernel,
        grid=(nseq,),
        in_specs=[
            pl.BlockSpec((1, 2 * KV_COLS, n), lambda b: (b, 0, 0)),
            pl.BlockSpec((2 * KV_COLS, CMP_LEN), lambda b: (0, 0)),
            pl.BlockSpec((2 * KV_COLS, CMP_LEN), lambda b: (0, 0)),
            pl.BlockSpec((2, KV_COLS, KV_COLS), lambda b: (0, 0, 0)),
        ],
        out_specs=[spec, spec, pl.BlockSpec((1, n, KV_COLS), lambda b: (b, 0, 0))],
        out_shape=[jax.ShapeDtypeStruct((nseq, KV_COLS, n), BF16)] * 2
        + [jax.ShapeDtypeStruct((nseq, n, KV_COLS), BF16)],
        compiler_params=pltpu.CompilerParams(dimension_semantics=("parallel",)),
        name="cmp_finish",
    )(pooled, poolw, pe_t, phi_bd)


ATT_TQ = 512
ATT_TK = 512
NBLK_PAD = 128
POS_ROWS = 16
AUG = HEAD_DIM + POS_ROWS


def _attn_prompt_kernel(q_ref, kca_ref, vc_ref, ks_ref, vs_ref, kw_ref, vw_ref, gt_ref, ovt_ref, ktab_ref,
                        stab_ref, o_ref, ksa_ref, vsa_ref, kwa_ref, vwa_ref, qat_ref, m_ref, acc_ref,
                        *, tq, tk, n_sel, n_top):
    qi = pl.program_id(2)
    q0 = qi * tq
    seq = ksa_ref.shape[0]

    @pl.when(qi == 0)
    def _():
        zero = jnp.zeros((128 - HEAD_DIM, seq), F32)
        ones = jnp.where(_iota((POS_ROWS, seq), 0) == 0, 1.0, 0.0).astype(BF16)
        for src, dst, lo in ((ks_ref, ksa_ref, NBLK_PAD), (kw_ref, kwa_ref, 0)):
            k_rows = jnp.concatenate([src[0], zero], axis=0).T[:, :HEAD_DIM].astype(BF16)
            dst[:, lo:] = jnp.concatenate([k_rows, ktab_ref[:, NBLK_PAD:]], axis=1)
        ksa_ref[:, 0:NBLK_PAD] = ktab_ref[:, 0:NBLK_PAD]
        for src, dst in ((vs_ref, vsa_ref), (vw_ref, vwa_ref)):
            dst[0:HEAD_DIM] = src[0].astype(BF16)
            dst[HEAD_DIM:] = ones

    q_t = q_ref[...].astype(F32).T
    q_t = jnp.concatenate([q_t[n * HEAD_DIM:(n + 1) * HEAD_DIM] for n in range(GROUP)], axis=1).astype(BF16)
    stab = stab_ref[0]
    sp_t = jnp.concatenate([jnp.broadcast_to(stab[:, n:n + 1], (POS_ROWS, tq)) for n in range(GROUP)], axis=1)
    qat_ref[NBLK_PAD:NBLK_PAD + HEAD_DIM] = q_t
    qat_ref[NBLK_PAD + HEAD_DIM:] = sp_t.astype(BF16)
    trow = q0 + _iota((1, tq), 1)
    lanes = [slice(n * tq, (n + 1) * tq) for n in range(GROUP)]

    ncmp = kca_ref.shape[2]
    nidx = _iota((ncmp, 1), 0)
    heads = lambda a: jnp.concatenate([a] * GROUP, axis=1)
    valid = (nidx >= 1) & (CMP_STRIDE * nidx + (CMP_STRIDE - 1) <= trow)
    valid = heads(jnp.where(valid, 1.0, 0.0)) > 0.5
    s = jnp.where(valid, _dot(kca_ref[0, 0], qat_ref[NBLK_PAD:]), NEG)
    e = jnp.exp2(s - jnp.max(s, axis=0, keepdims=True))
    pc = jnp.where(valid, e / jnp.sum(e, axis=0, keepdims=True), 0.0)
    o_c = _dot(vc_ref[0], pc.astype(BF16))

    ph, plo = _split((pc[:, lanes[0]] + pc[:, lanes[1]]) + (pc[:, lanes[2]] + pc[:, lanes[3]]))
    imp = (_dot(ovt_ref[...], ph) + _dot(ovt_ref[...], plo))[0:n_sel]
    blk = _iota((n_sel, 1), 0)
    cur = _div_pow2(q0 + _iota((1, tq), 1), SEL_BLOCK)
    forced = (blk == 0) | (blk == cur) | (blk == cur - 1)
    imp = jnp.where(forced, FORCE_SCORE, imp)
    imp = jnp.where(blk <= cur, imp, -1.0)
    cnt = jnp.zeros((n_sel, tq), F32)
    for j in range(n_sel):
        vj = imp[j:j + 1, :]
        tie = jnp.where(blk > j, 1.0, 0.0)
        cnt = cnt + jnp.where(vj > imp, 1.0, jnp.where(vj == imp, tie, 0.0))
    mask_t = jnp.where((cnt < n_top) & (blk <= cur), 0.0, NEG)
    mask_t = jnp.concatenate([mask_t, jnp.full((NBLK_PAD - n_sel, tq), NEG, F32)], axis=0).astype(BF16)
    qat_ref[0:NBLK_PAD] = heads(mask_t)

    m_ref[...] = jnp.full_like(m_ref, NEG)
    acc_ref[...] = jnp.zeros_like(acc_ref)

    def scores(kt):
        return _dot(ksa_ref[pl.ds(pl.multiple_of(kt * tk, tk), tk), :], qat_ref[...])

    def update(kt, s):
        update_keys(pl.multiple_of(kt * tk, tk), tk, s, None)

    def update_keys(k0, nk, s, lane_sl):
        gather = (lambda ref: ref[...]) if lane_sl is None else (
            lambda ref: jnp.concatenate([ref[:, sl] for sl in lane_sl], axis=1))
        m_old = gather(m_ref)
        m_new = jnp.maximum(m_old, jnp.max(s, axis=0, keepdims=True))
        p = jnp.exp2(s - m_new).astype(BF16)
        acc_new = jnp.exp2(m_old - m_new) * gather(acc_ref) + _dot(vsa_ref[:, pl.ds(k0, nk)], p)
        if lane_sl is None:
            acc_ref[...] = acc_new
            m_ref[...] = m_new
        else:
            w = acc_new.shape[1] // len(lane_sl)
            for i, sl in enumerate(lane_sl):
                acc_ref[:, sl] = acc_new[:, i * w:(i + 1) * w]
                m_ref[:, sl] = m_new[:, i * w:(i + 1) * w]

    def pair(j, carry):
        s0, s1 = scores(2 * j), scores(2 * j + 1)
        update(2 * j, s0)
        update(2 * j + 1, s1)
        return carry

    lax.fori_loop(0, qi // 2, pair, 0)

    @pl.when(qi % 2 == 1)
    def _():
        update(qi - 1, scores(qi - 1))

    hq = tq // 2
    upper = [slice(n * tq + hq, (n + 1) * tq) for n in range(GROUP)]
    k_mid = pl.multiple_of(q0 + hq, hq)
    s_d0 = (_dot(ksa_ref[pl.ds(pl.multiple_of(q0, hq), hq), :], qat_ref[...])
            + heads(jnp.where(q0 + _iota((hq, 1), 0) <= trow, 0.0, NEG)))
    s_d1 = (_dot(ksa_ref[pl.ds(k_mid, hq), :], jnp.concatenate([qat_ref[:, sl] for sl in upper], axis=1))
            + jnp.concatenate([jnp.where(_iota((hq, 1), 0) <= _iota((1, hq), 1), 0.0, NEG)] * GROUP, axis=1))

    span = WINDOW + hq
    o_w_half = []
    for h in range(2):
        qh = q0 + h * hq
        w0 = pl.multiple_of(jnp.maximum(qh - WINDOW, 0), hq)
        dist = (qh + _iota((1, hq), 1)) - (w0 + _iota((span, 1), 0))
        qa_h = jnp.concatenate([qat_ref[NBLK_PAD:, n * tq + h * hq:n * tq + (h + 1) * hq] for n in range(GROUP)], axis=1)
        s_win = _dot(kwa_ref[pl.ds(w0, span), :], qa_h) + heads(jnp.where((dist >= 0) & (dist < WINDOW), 0.0, NEG))
        ow = _dot(vwa_ref[:, pl.ds(w0, span)],
                  jnp.exp2(s_win - jnp.max(s_win, axis=0, keepdims=True)).astype(BF16))
        o_w_half.append(ow[:HEAD_DIM] / ow[HEAD_DIM:HEAD_DIM + 1])
    o_w = jnp.concatenate([o_w_half[h][:, n * hq:(n + 1) * hq] for n in range(GROUP) for h in range(2)], axis=1)
    update_keys(pl.multiple_of(q0, hq), hq, s_d0, None)
    update_keys(k_mid, hq, s_d1, upper)
    acc = acc_ref[...]
    o_s = acc[:HEAD_DIM] / acc[HEAD_DIM:HEAD_DIM + 1]

    gt_t = gt_ref[...].T
    outs = []
    for n in range(GROUP):
        outs.append(gt_t[n:n + 1] * o_c[:, lanes[n]] + gt_t[GROUP + n:GROUP + n + 1] * o_s[:, lanes[n]]
                    + gt_t[2 * GROUP + n:2 * GROUP + n + 1] * o_w[:, lanes[n]])
    o_ref[...] = jnp.concatenate(outs, axis=0).T.astype(BF16)


def _overlap_t(n_blk_rows, n_cmp):
    s = np.arange(n_blk_rows)[:, None]
    n = np.arange(n_cmp)[None, :]
    r = SEL_BLOCK // CMP_STRIDE
    return jnp.asarray(((n >= r * s) & (n <= r * s + r)).astype(np.float32), dtype=BF16)


def _block_expand(n_blk_rows, n_keys):
    s = np.arange(n_blk_rows)[:, None]
    k = np.arange(n_keys)[None, :]
    return jnp.asarray((k // SEL_BLOCK == s).astype(np.float32), dtype=BF16)


def _attn_prompt(slopes, q, kc_rows, vc_t, kvc_t, kvw_t, gates, nbatch, seq):
    ncmp = kc_rows.shape[1]
    tq, tk = ATT_TQ, ATT_TK
    assert tk == tq and seq % tk == 0 and WINDOW % tq == 0 and WINDOW + tq <= seq
    nq = seq // tq
    n_sel = seq // SEL_BLOCK
    ovt = _overlap_t(NBLK_PAD, ncmp)
    ktab = jnp.concatenate([_block_expand(NBLK_PAD, seq), _position_rows(seq)], axis=0).T
    cmp_end = CMP_STRIDE * np.arange(ncmp) + (CMP_STRIDE - 1)
    kca = jnp.concatenate([kc_rows.reshape(nbatch, ncmp, N_KV, HEAD_DIM).transpose(0, 2, 1, 3),
                           jnp.broadcast_to(_position_rows(cmp_end).T, (nbatch, N_KV, ncmp, POS_ROWS))], axis=-1)
    stab = _slope_table(slopes).astype(F32).transpose(0, 2, 1)
    per_slot = KV_COLS // HEAD_DIM

    def kv_spec(slot):
        return pl.BlockSpec((1, HEAD_DIM, seq), lambda b, g, i: (b, slot * per_slot + g, 0))

    q_spec = pl.BlockSpec((tq, GROUP * HEAD_DIM), lambda b, g, i: (b * nq + i, g))
    return pl.pallas_call(
        functools.partial(_attn_prompt_kernel, tq=tq, tk=tk, n_sel=n_sel, n_top=min(SEL_TOP, n_sel)),
        grid=(nbatch, N_KV, nq),
        in_specs=[
            q_spec,
            pl.BlockSpec((1, 1, ncmp, AUG), lambda b, g, i: (b, g, 0, 0)),
            pl.BlockSpec((1, HEAD_DIM, ncmp), lambda b, g, i: (b, g, 0)),
            kv_spec(2), kv_spec(3), kv_spec(0), kv_spec(1),
            pl.BlockSpec((tq, 128), lambda b, g, i: (b * nq + i, g)),
            pl.BlockSpec((NBLK_PAD, ncmp), lambda b, g, i: (0, 0)),
            pl.BlockSpec((seq, NBLK_PAD + POS_ROWS), lambda b, g, i: (0, 0)),
            pl.BlockSpec((1, POS_ROWS, GROUP), lambda b, g, i: (g, 0, 0)),
        ],
        out_specs=q_spec,
        out_shape=jax.ShapeDtypeStruct(q.shape, BF16),
        scratch_shapes=[
            pltpu.VMEM((seq, NBLK_PAD + AUG), BF16),
            pltpu.VMEM((AUG, seq), BF16),
            pltpu.VMEM((seq, AUG), BF16),
            pltpu.VMEM((AUG, seq), BF16),
            pltpu.VMEM((NBLK_PAD + AUG, GROUP * tq), BF16),
            pltpu.VMEM((1, GROUP * tq), F32),
            pltpu.VMEM((AUG, GROUP * tq), F32),
        ],
        compiler_params=pltpu.CompilerParams(
            dimension_semantics=("parallel", "parallel", "arbitrary"), vmem_limit_bytes=VMEM_LIMIT),
        name="attn_prompt",
    )(q, kca, vc_t, kvc_t, kvc_t, kvw_t, kvw_t, gates, ovt, ktab, stab)


def _diag_blocks(o, maskbd):
    o = o * maskbd
    return (o[:, 0:64] + o[:, 64:128]) + (o[:, 128:192] + o[:, 192:256])


def _attn_dec1_kernel(qbd_ref, kc_ref, vc_ref, win_ref, kvnew_ref, gates_ref, slope_ref, ovt_ref,
                      maskbd_ref, opart_ref, sel_ref, *, past, tdec, n_blk, n_top):
    qb = qbd_ref[0]
    nrows = qb.shape[0]
    tcol = past + _mod_pow2(_iota((nrows, 1), 0), tdec)
    pref = float(past + tdec)
    slope = slope_ref[...]
    maskbd = maskbd_ref[...]

    ncmp = kc_ref.shape[2]
    nrow = _iota((1, ncmp), 1)
    cend = CMP_STRIDE * nrow + (CMP_STRIDE - 1)
    valid = (nrow >= 1) & (cend <= tcol)
    s = _dot(qb, kc_ref[0])
    s = jnp.where(valid, s + slope * (cend.astype(F32) - pref), NEG)
    p = jnp.where(valid, _softmax2_rows(s), 0.0)
    o_c = _diag_blocks(_dot_nt(p.astype(BF16), vc_ref[0]), maskbd)

    per_g = GROUP * tdec
    psum = jnp.concatenate(
        [sum(p[gi * per_g + n * tdec: gi * per_g + (n + 1) * tdec] for n in range(GROUP)) for gi in range(N_KV)],
        axis=0)
    imp = _dot_nt_hl(ovt_ref[...], psum)
    nb_pad = imp.shape[0]
    blk = _iota((nb_pad, 1), 0)
    cur = _div_pow2(past + _mod_pow2(_iota((1, N_KV * tdec), 1), tdec), SEL_BLOCK)
    forced = (blk == 0) | (blk == cur) | (blk == cur - 1)
    imp = jnp.where(forced, FORCE_SCORE, imp)
    imp = jnp.where(blk <= cur, imp, -1.0)
    imp = jnp.where(blk < n_blk, imp, -2.0)
    sel = jnp.zeros(imp.shape, F32)
    for _ in range(n_top):
        mx = jnp.max(imp, axis=0, keepdims=True)
        first = jnp.min(jnp.where(imp == mx, blk, nb_pad), axis=0, keepdims=True)
        pick = blk == first
        sel = jnp.where(pick, 1.0, sel)
        imp = jnp.where(pick, -jnp.inf, imp)
    sel_ref[0] = sel

    nbuf = win_ref.shape[2]
    knew = kvnew_ref[0]
    zpad = jnp.zeros((PAGE - tdec, KV_COLS), F32)
    kn = jnp.concatenate([knew[:, :KV_COLS], zpad], axis=0).astype(BF16)
    vn = jnp.concatenate([knew[:, KV_COLS:], zpad], axis=0).astype(BF16)
    sw = jnp.concatenate([_dot(qb, win_ref[0, :KV_COLS].astype(BF16)), _dot_nt(qb, kn)], axis=1)
    idx = _iota((1, nbuf + PAGE), 1)
    kposw = past - nbuf + idx
    dist = tcol - kposw
    validw = (dist >= 0) & (dist < WINDOW) & (kposw >= 0) & (idx < nbuf + tdec)
    sw = jnp.where(validw, sw + slope * (kposw.astype(F32) - pref), NEG)
    pw = _softmax2_rows(sw).astype(BF16)
    o_w = _dot_nt(pw[:, :nbuf], win_ref[0, KV_COLS:].astype(BF16)) + _dot(pw[:, nbuf:], vn)
    o_w = _diag_blocks(o_w, maskbd)

    gates = gates_ref[0]
    opart_ref[0] = gates[:, 0:1] * o_c + gates[:, 2:3] * o_w


def _attn_dec1(qbd, kc_t, vc_t, win_t, kvnew, gates_rows, slopecol, maskbd, past, tdec, n_blk):
    nseq, nrows, _ = qbd.shape
    ncmp = kc_t.shape[2]
    nb_pad = -(-n_blk // 8) * 8
    ovt = _overlap_t(nb_pad, ncmp)
    nbuf = win_t.shape[2]
    return pl.pallas_call(
        functools.partial(_attn_dec1_kernel, past=past, tdec=tdec, n_blk=n_blk, n_top=min(SEL_TOP, n_blk)),
        grid=(nseq,),
        in_specs=[
            pl.BlockSpec((1, nrows, KV_COLS), lambda b: (b, 0, 0)),
            pl.BlockSpec((1, KV_COLS, ncmp), lambda b: (b, 0, 0)),
            pl.BlockSpec((1, KV_COLS, ncmp), lambda b: (b, 0, 0)),
            pl.BlockSpec((1, 2 * KV_COLS, nbuf), lambda b: (b, 0, 0)),
            pl.BlockSpec((1, tdec, 2 * KV_COLS), lambda b: (b, 0, 0)),
            pl.BlockSpec((1, nrows, 4), lambda b: (b, 0, 0)),
            pl.BlockSpec((nrows, 1), lambda b: (0, 0)),
            pl.BlockSpec((nb_pad, ncmp), lambda b: (0, 0)),
            pl.BlockSpec((nrows, KV_COLS), lambda b: (0, 0)),
        ],
        out_specs=[
            pl.BlockSpec((1, nrows, HEAD_DIM), lambda b: (b, 0, 0)),
            pl.BlockSpec((1, nb_pad, N_KV * tdec), lambda b: (b, 0, 0)),
        ],
        out_shape=[
            jax.ShapeDtypeStruct((nseq, nrows, HEAD_DIM), F32),
            jax.ShapeDtypeStruct((nseq, nb_pad, N_KV * tdec), F32),
        ],
        compiler_params=pltpu.CompilerParams(dimension_semantics=("parallel",), vmem_limit_bytes=VMEM_LIMIT),
        name="attn_dec_select",
    )(qbd, kc_t, vc_t, win_t, kvnew, gates_rows, slopecol, ovt, maskbd)


def _attn_dec2_kernel(pt_ref, *refs, npg, past, tdec):
    page_refs = refs[:npg]
    (qbd_ref, selrow_ref, sellast_ref, e_ref, slope_ref, knew_ref, opart_ref, gates_ref, maskbd_ref,
     o_ref, m_ref, l_ref, acc_ref) = refs[npg:]
    step = pl.program_id(1)
    nkeys = npg * PAGE
    qb = qbd_ref[0]
    nrows = qb.shape[0]
    pref = float(past + tdec)
    slope = slope_ref[...]

    @pl.when(step == 0)
    def _():
        m_ref[...] = jnp.full_like(m_ref, NEG)
        l_ref[...] = jnp.zeros_like(l_ref)
        acc_ref[...] = jnp.zeros_like(acc_ref)

    def update(s, pv_fn):
        m_old = m_ref[...]
        m_new = jnp.maximum(m_old, jnp.max(s, axis=-1, keepdims=True))
        alpha = jnp.exp2(m_old - m_new)
        p = jnp.exp2(s - m_new)
        l_ref[...] = alpha * l_ref[...] + jnp.sum(p, axis=-1, keepdims=True)
        acc_ref[...] = alpha * acc_ref[...] + pv_fn(p.astype(BF16))
        m_ref[...] = m_new

    k_t = jnp.concatenate([page_refs[j][0, :KV_COLS].astype(BF16) for j in range(npg)], axis=1)
    v_t = jnp.concatenate([page_refs[j][0, KV_COLS:].astype(BF16) for j in range(npg)], axis=1)
    kpos = step * nkeys + _iota((1, nkeys), 1)
    chosen = _dot(selrow_ref[0, 0], e_ref[...])
    s = _dot(qb, k_t) + slope * (kpos.astype(F32) - pref) + jnp.where(chosen > 0.5, 0.0, NEG)
    update(s, lambda p: _dot_nt(p, v_t))

    @pl.when(step == pl.num_programs(1) - 1)
    def _():
        tcol = past + _mod_pow2(_iota((nrows, 1), 0), tdec)
        kn = knew_ref[0]
        kposn = past + _iota((1, PAGE), 1)
        sn = _dot_nt(qb, kn[:, :KV_COLS].astype(BF16))
        ok = (sellast_ref[0, 0][:, 0:1].astype(F32) > 0.5) & (kposn <= tcol)
        sn = sn + slope * (kposn.astype(F32) - pref) + jnp.where(ok, 0.0, NEG)
        update(sn, lambda p: _dot(p, kn[:, KV_COLS:].astype(BF16)))
        o_s = _diag_blocks(acc_ref[...] / l_ref[...], maskbd_ref[...])
        o_ref[0] = opart_ref[0] + gates_ref[0][:, 1:2] * o_s


def _attn_dec2(cache_t, page_table, qbd, selrows, slopecol, knew, opart, gates_rows, maskbd, past, tdec, npg):
    nseq, npages = page_table.shape
    nstep = npages // npg
    nrows = qbd.shape[1]
    blk_per_step = npg * PAGE // SEL_BLOCK
    emat = _block_expand(blk_per_step, npg * PAGE)
    page_specs = [
        pl.BlockSpec((1, 2 * KV_COLS, PAGE), lambda b, s, pt, j=j: (pt[b, s * npg + j], 1, 0))
        for j in range(npg)
    ]
    per_seq = lambda shape: pl.BlockSpec((1,) + shape, lambda b, s, pt: (b,) + (0,) * len(shape))
    const = lambda shape: pl.BlockSpec(shape, lambda b, s, pt: (0,) * len(shape))
    return pl.pallas_call(
        functools.partial(_attn_dec2_kernel, npg=npg, past=past, tdec=tdec),
        grid_spec=pltpu.PrefetchScalarGridSpec(
            num_scalar_prefetch=1,
            grid=(nseq, nstep),
            in_specs=page_specs + [
                per_seq((nrows, KV_COLS)),
                pl.BlockSpec((1, 1, nrows, blk_per_step), lambda b, s, pt: (b, s, 0, 0)),
                pl.BlockSpec((1, 1, nrows, blk_per_step), lambda b, s, pt: (b, nstep, 0, 0)),
                const((blk_per_step, npg * PAGE)),
                const((nrows, 1)),
                per_seq((PAGE, 2 * KV_COLS)),
                per_seq((nrows, HEAD_DIM)),
                per_seq((nrows, 4)),
                const((nrows, KV_COLS)),
            ],
            out_specs=per_seq((nrows, HEAD_DIM)),
            scratch_shapes=[
                pltpu.VMEM((nrows, 1), F32),
                pltpu.VMEM((nrows, 1), F32),
                pltpu.VMEM((nrows, KV_COLS), F32),
            ],
        ),
        out_shape=jax.ShapeDtypeStruct((nseq, nrows, HEAD_DIM), F32),
        compiler_params=pltpu.CompilerParams(
            dimension_semantics=("parallel", "arbitrary"), vmem_limit_bytes=VMEM_LIMIT),
        name="attn_dec_selected",
    )(page_table, *([cache_t] * npg), qbd, selrows, selrows, emat, slopecol, knew, opart, gates_rows, maskbd)


def _s5_weights(lam_re, lam_im, log_dt, b_re, b_im, c_re, c_im, d_skip):
    hp = lax.Precision.HIGHEST
    L = S5_L
    ng, p = lam_re.shape
    c = SSM_GROUP
    ns, gs = ng // S5_SLAB, S5_SLAB
    dt = jnp.exp(log_dt)[:, None]
    lr, li = lam_re, lam_im
    mag = jnp.exp(lr * dt)
    ar = mag * jnp.cos(li * dt)
    ai = mag * jnp.sin(li * dt)
    den = lr * lr + li * li
    fr = ((ar - 1.0) * lr + ai * li) / den
    fi = (ai * lr - (ar - 1.0) * li) / den
    bt_re, bt_im = b_re.transpose(0, 2, 1), b_im.transpose(0, 2, 1)
    bbr = fr[:, None, :] * bt_re - fi[:, None, :] * bt_im
    bbi = fr[:, None, :] * bt_im + fi[:, None, :] * bt_re
    j = jnp.arange(L + 1, dtype=F32)[:, None, None]
    pmag = jnp.exp(j * (lr * dt))
    pr = pmag * jnp.cos(j * (li * dt))
    pi = pmag * jnp.sin(j * (li * dt))
    abr = pr[:, :, None, :] * bbr - pi[:, :, None, :] * bbi
    abi = pr[:, :, None, :] * bbi + pi[:, :, None, :] * bbr
    kd = (jnp.einsum('jgkp,gcp->jgkc', abr, c_re, precision=hp)
          - jnp.einsum('jgkp,gcp->jgkc', abi, c_im, precision=hp))
    kd = kd.at[0].add(d_skip.reshape(ng, 1, c) * jnp.eye(c, dtype=F32))
    kc = kd[:L].reshape(L, ns, gs * c, c)
    abc = jnp.concatenate([abr[:L], abi[:L]], axis=-1).reshape(L, ns, gs * c, 2 * p)
    cr = c_re[None] * pr[1:, :, None, :] - c_im[None] * pi[1:, :, None, :]
    ci = -(c_re[None] * pi[1:, :, None, :] + c_im[None] * pr[1:, :, None, :])
    co = jnp.stack([cr, ci], axis=1).reshape(L, 2, ns, gs, c, p)
    al = jnp.stack([pr[L].reshape(-1), pi[L].reshape(-1)])
    return kc.astype(BF16), abc.astype(BF16), co.astype(BF16), al


def _s5_expanders():
    gs, c, p = S5_SLAB, SSM_GROUP, SSM_STATE
    lane = np.arange(gs * c)
    rep_c = (np.arange(c)[:, None] == lane[None, :] % c)
    same_g = (lane[:, None] // c == lane[None, :] // c)
    st = np.arange(2 * gs * p)
    rp = np.arange(2 * p)
    rep_p = (rp[:, None] // p == st[None, :] // (gs * p)) & (rp[:, None] % p == st[None, :] % p)
    g_in = (lane[:, None] // c == (st[None, :] // p) % gs)
    as_bf = lambda a: jnp.asarray(a.astype(np.float32), dtype=BF16)
    return as_bf(rep_c), as_bf(same_g), as_bf(rep_p), as_bf(g_in)


def _chunk_lanes(u_ref):
    return jnp.concatenate([u_ref[0, :, t, :] for t in range(S5_L)], axis=1)


def _s5_local_kernel(u_ref, abc_ref, rep_p_ref, g_in_ref, pre_ref, pim_ref, wp_ref):
    for s in range(S5_L):
        blk = _dot(abc_ref[S5_L - 1 - s, 0], rep_p_ref[...]).astype(BF16) * g_in_ref[...]
        wp_ref[128 * s:128 * (s + 1), :] = blk
    r = _dot(_chunk_lanes(u_ref).astype(BF16), wp_ref[...])
    half = r.shape[1] // 2
    pre_ref[...] = r[:, :half]
    pim_ref[...] = r[:, half:]


def _s5_scan_kernel(pre_ref, pim_ref, al_ref, h0r_ref, h0i_ref, hsr_ref, hsi_ref, hfr_ref, hfi_ref):
    nchunk = pre_ref.shape[1]
    ar = al_ref[0:1, :]
    ai = al_ref[1:2, :]

    def body(k, carry):
        cr, ci = carry
        hsr_ref[:, pl.ds(k, 1), :] = cr[:, None, :]
        hsi_ref[:, pl.ds(k, 1), :] = ci[:, None, :]
        xr = pre_ref[:, pl.ds(k, 1), :][:, 0, :]
        xi = pim_ref[:, pl.ds(k, 1), :][:, 0, :]
        return ar * cr - ai * ci + xr, ar * ci + ai * cr + xi

    cr, ci = lax.fori_loop(0, nchunk, body, (h0r_ref[...], h0i_ref[...]), unroll=min(nchunk, 4))
    hfr_ref[...] = cr
    hfi_ref[...] = ci


def _s5_out_kernel(u_ref, kc_ref, co_ref, rep_c_ref, same_g_ref, hsr_ref, hsi_ref, y_ref, wt_ref, wot_ref):
    rep_c = rep_c_ref[...]
    lag = [_dot(kc_ref[dl, 0], rep_c).astype(BF16) * same_g_ref[...] for dl in range(S5_L)]
    zero = jnp.zeros((128, 128), BF16)
    for s in range(S5_L):
        for t in range(S5_L):
            wt_ref[128 * s:128 * (s + 1), 128 * t:128 * (t + 1)] = lag[t - s] if t >= s else zero
    wot_ref[...] = jnp.zeros_like(wot_ref)
    c, p = SSM_GROUP, SSM_STATE
    for t in range(S5_L):
        for r in range(2):
            for g in range(S5_SLAB):
                col = (r * S5_SLAB + g) * p
                wot_ref[128 * t + c * g:128 * t + c * (g + 1), col:col + p] = co_ref[t, r, 0, g]
    hs = jnp.concatenate([hsr_ref[...], hsi_ref[...]], axis=1).astype(BF16)
    y = _dot(_chunk_lanes(u_ref).astype(BF16), wt_ref[...]) + _dot_nt(hs, wot_ref[...])
    for t in range(S5_L):
        y_ref[0, :, t, :] = y[:, 128 * t:128 * (t + 1)]


def _s5(u3, h0r, h0i, weights, nbatch):
    kc, abc, co, al = weights
    rep_c, same_g, rep_p, g_in = _s5_expanders()
    ns, nb = u3.shape[0], u3.shape[1]
    nchunk = nb // nbatch
    gp = al.shape[1]
    lb = gp // ns
    d = S5_L * 128
    cp = pltpu.CompilerParams(dimension_semantics=("parallel",), vmem_limit_bytes=VMEM_LIMIT)
    u_spec = pl.BlockSpec((1, nb, S5_L, 128), lambda j: (j, 0, 0, 0))
    st_spec = pl.BlockSpec((nb, lb), lambda j: (0, j))
    slab = lambda a: pl.BlockSpec((a.shape[0], 1) + a.shape[2:], lambda j: (0, j) + (0,) * (a.ndim - 2))
    const = lambda a: pl.BlockSpec(a.shape, lambda j: (0,) * a.ndim)
    pre, pim = pl.pallas_call(
        _s5_local_kernel,
        grid=(ns,),
        in_specs=[u_spec, slab(abc), const(rep_p), const(g_in)],
        out_specs=[st_spec] * 2,
        out_shape=[jax.ShapeDtypeStruct((nb, gp), F32)] * 2,
        scratch_shapes=[pltpu.VMEM((d, 2 * lb), BF16)],
        compiler_params=cp,
        name="s5_local",
    )(u3, abc, rep_p, g_in)

    ls = 2 * lb
    seq3 = pl.BlockSpec((nbatch, nchunk, ls), lambda j: (0, 0, j))
    row = lambda r: pl.BlockSpec((r, ls), lambda j: (0, j))
    hsr, hsi, hfr, hfi = pl.pallas_call(
        _s5_scan_kernel,
        grid=(gp // ls,),
        in_specs=[seq3, seq3, row(2), row(nbatch), row(nbatch)],
        out_specs=[seq3, seq3, row(nbatch), row(nbatch)],
        out_shape=[jax.ShapeDtypeStruct((nbatch, nchunk, gp), F32)] * 2 + [jax.ShapeDtypeStruct((nbatch, gp), F32)] * 2,
        compiler_params=cp,
        name="s5_scan",
    )(pre.reshape(nbatch, nchunk, gp), pim.reshape(nbatch, nchunk, gp), al, h0r, h0i)

    y3 = pl.pallas_call(
        _s5_out_kernel,
        grid=(ns,),
        in_specs=[u_spec, slab(kc),
                  pl.BlockSpec(co.shape[:2] + (1,) + co.shape[3:], lambda j: (0, 0, j, 0, 0, 0)),
                  const(rep_c), const(same_g), st_spec, st_spec],
        out_specs=u_spec,
        out_shape=jax.ShapeDtypeStruct(u3.shape, F32),
        scratch_shapes=[pltpu.VMEM((d, d), BF16), pltpu.VMEM((2 * lb, d), BF16)],
        compiler_params=cp,
        name="s5_out",
    )(u3, kc, co, rep_c, same_g, hsr.reshape(nb, gp), hsi.reshape(nb, gp))
    return y3, hfr, hfi


def _merge_kernel(o_ref, y_ref, wa_ref, wg1_ref, wg2_ref, sga_ref, sgb_ref, m_ref, gy_ref):
    @pl.when(pl.program_id(1) == 0)
    def _():
        y = jnp.concatenate([y_ref[s] for s in range(y_ref.shape[0])], axis=1)
        gy_ref[...] = _gelu(y).astype(BF16)

    ya = _dot(o_ref[...], wa_ref[...])
    gy = gy_ref[...]
    yb = _dot(gy, wg1_ref[...]) * jax.nn.sigmoid(_dot(gy, wg2_ref[...]))
    m = sga_ref[...].astype(F32) * ya + sgb_ref[...].astype(F32) * yb
    m_ref[...] = m.astype(BF16)


def _merge(o, y3, w_attn_out, w_glu, sg, tm, tn=1024):
    rows, d = o.shape[0], w_attn_out.shape[1]
    nj = d // tn
    kq, ks = w_attn_out.shape[0], w_glu.shape[0]
    ns = y3.shape[0]
    return pl.pallas_call(
        _merge_kernel,
        grid=(rows // tm, nj),
        in_specs=[
            pl.BlockSpec((tm, kq), lambda i, j: (i, 0)),
            pl.BlockSpec((ns, tm, 128), lambda i, j: (0, i, 0)),
            pl.BlockSpec((kq, tn), lambda i, j: (0, j)),
            pl.BlockSpec((ks, tn), lambda i, j: (0, j)),
            pl.BlockSpec((ks, tn), lambda i, j: (0, j + nj)),
            pl.BlockSpec((tm, tn), lambda i, j: (i, j)),
            pl.BlockSpec((tm, tn), lambda i, j: (i, j + nj)),
        ],
        out_specs=pl.BlockSpec((tm, tn), lambda i, j: (i, j)),
        out_shape=jax.ShapeDtypeStruct((rows, d), BF16),
        scratch_shapes=[pltpu.VMEM((tm, ks), BF16)],
        compiler_params=pltpu.CompilerParams(
            dimension_semantics=("parallel", "arbitrary"), vmem_limit_bytes=VMEM_LIMIT),
        name="merge",
    )(o, y3, w_attn_out, w_glu, w_glu, sg, sg)


def _outproj_kernel(m_ref, x_ref, w_ref, g_ref, x1_ref, h_ref):
    x1 = x_ref[...] + _dot(m_ref[...], w_ref[...])
    x1_ref[...] = x1
    r = lax.rsqrt(jnp.mean(x1 * x1, axis=-1, keepdims=True) + EPS)
    h_ref[...] = (x1 * r * g_ref[...]).astype(BF16)


def _outproj(m, x, w_out, g_ffn, tm):
    rows, d = x.shape
    row = pl.BlockSpec((tm, d), lambda i: (i, 0))
    return pl.pallas_call(
        _outproj_kernel,
        grid=(rows // tm,),
        in_specs=[row, row, pl.BlockSpec((d, d), lambda i: (0, 0), pipeline_mode=pl.Buffered(1)),
                  pl.BlockSpec((1, d), lambda i: (0, 0))],
        out_specs=[row, row],
        out_shape=[jax.ShapeDtypeStruct((rows, d), F32), jax.ShapeDtypeStruct((rows, d), BF16)],
        compiler_params=pltpu.CompilerParams(dimension_semantics=("parallel",), vmem_limit_bytes=VMEM_LIMIT),
        name="out_proj",
    )(m, x, w_out, g_ffn.reshape(1, d))


def _ffn_kernel(h_ref, x1_ref, wg_ref, wu_ref, wd_ref, gf_ref, y_ref, *rest):
    acc_ref = rest[-1]
    f = pl.program_id(1)

    @pl.when(f == 0)
    def _():
        acc_ref[...] = jnp.zeros_like(acc_ref)

    wg, wu, wd = wg_ref[...].astype(BF16), wu_ref[...].astype(BF16), wd_ref[...].astype(BF16)
    for copy_ref, w in zip(rest[:-1], (wg, wu, wd)):
        copy_ref[...] = w
    h = h_ref[...]
    a = _dot(h, wg)
    a = (a * jax.nn.sigmoid(a)) * _dot(h, wu)
    acc_ref[...] += _dot(a.astype(BF16), wd)

    @pl.when(f == pl.num_programs(1) - 1)
    def _():
        y = x1_ref[...] + acc_ref[...]
        r = lax.rsqrt(jnp.mean(y * y, axis=-1, keepdims=True) + EPS)
        y_ref[...] = y * r * gf_ref[...]


def _ffn(h, x1, w_gate, w_up, w_down, g_final, tm, tf=512):
    rows, d = x1.shape
    dff = w_gate.shape[1]
    emit = w_gate.dtype != BF16
    assert not emit or rows == tm
    row = pl.BlockSpec((tm, d), lambda i, f: (i, 0))
    w_specs = [pl.BlockSpec((d, tf), lambda i, f: (0, f)),
               pl.BlockSpec((d, tf), lambda i, f: (0, f)),
               pl.BlockSpec((tf, d), lambda i, f: (f, 0))]
    y_shape = jax.ShapeDtypeStruct((rows, d), F32)
    copies = [jax.ShapeDtypeStruct(w.shape, BF16) for w in (w_gate, w_up, w_down)] if emit else []
    out = pl.pallas_call(
        _ffn_kernel,
        grid=(rows // tm, dff // tf),
        in_specs=[row, row] + w_specs + [pl.BlockSpec((1, d), lambda i, f: (0, 0))],
        out_specs=[row] + (w_specs if emit else []),
        out_shape=[y_shape] + copies,
        scratch_shapes=[pltpu.VMEM((tm, d), F32)],
        compiler_params=pltpu.CompilerParams(
            dimension_semantics=("parallel", "arbitrary"), vmem_limit_bytes=VMEM_LIMIT),
        name="ffn",
    )(h, x1, w_gate, w_up, w_down, g_final.reshape(1, d))
    return tuple(out) if emit else out[0]


def _alibi_slopes():
    return jnp.exp2(-8.0 * jnp.arange(1, N_HEADS + 1, dtype=F32) / N_HEADS) * LOG2E


def _slope_table(slopes):
    s1 = slopes.astype(BF16)
    r1 = slopes - s1.astype(F32)
    s2 = r1.astype(BF16)
    s3 = (r1 - s2.astype(F32)).astype(BF16)
    tab = jnp.stack([s1, s1, s2, s2, s3, s3] + [jnp.zeros_like(s1)] * 10, axis=-1)
    return tab.reshape(N_KV, GROUP, 16)


def _position_rows(pos):
    pos = np.arange(pos) if np.isscalar(pos) else np.asarray(pos)
    hi, lo = (pos // 64) * 64, pos % 64
    assert pos.max() < 64 * 256
    rows = np.stack([hi, lo, hi, lo, hi, lo] + [np.zeros_like(pos)] * 10).astype(np.float32)
    return jnp.asarray(rows, dtype=BF16)


def _layer_weights(p, l):
    w_in = _pack_w_in(p['w_in'][l], p['ssm_d'].shape[-1])
    s5 = _s5_weights(*(p[n][l] for n in ('ssm_lam_re', 'ssm_lam_im', 'ssm_log_dt', 'ssm_b_re', 'ssm_b_im',
                                          'ssm_c_re', 'ssm_c_im', 'ssm_d')))
    cast = lambda n: p[n][l].astype(BF16)
    return dict(
        g_mix=p['g_mix'][l], w_in=w_in,
        pool=p['w_cmp_pool'][l], pe=p['w_cmp_pe'][l], phi=p['w_cmp_phi'][l],
        pool_mats=_pool_matrices(p['w_cmp_pool'][l], PROMPT_PAGES_PER_STEP),
        w_attn_out=cast('w_attn_out'), w_glu=cast('w_glu'), w_out=cast('w_out'), g_ffn=p['g_ffn'][l],
        ffn_w=tuple(p[n][l] for n in ('w_gate', 'w_up', 'w_down')), s5=s5)


def _tail(x, o, y3, sg, lw, g_final, tm):
    rows = x.shape[0]
    m = _merge(o, y3.reshape(y3.shape[0], rows, 128), lw['w_attn_out'], lw['w_glu'], sg, tm)
    x1, h = _outproj(m, x, lw['w_out'], lw['g_ffn'], tm)
    return _ffn(h, x1, *lw['ffn_w'], g_final, tm)


def _feature_major_rows(kv_t, t0):
    b, f, t = kv_t.shape
    return kv_t[:, :, t0:].reshape(b, f // KV_COLS, N_KV, HEAD_DIM, t - t0).transpose(0, 4, 1, 2, 3)


def _layer_prompt(x, lw, slopes, g_out):
    b, t, d = x.shape
    rows = b * t
    assert t % (PROMPT_PAGES_PER_STEP * PAGE) == 0
    q, u3, sg, gt, kvc_t, kvw_t = _in_proj(x.reshape(rows, d), lw['g_mix'], lw['w_in'], 1024, seq_len=t)

    pooled = _pool_pages(kvc_t, lambda bi, pg: (bi, 0, pg), (), b, t // PAGE, lw['pool_mats'],
                         PROMPT_PAGES_PER_STEP)
    _, vc_t, kc_rows = _cmp_finish(pooled, lw['pool'], lw['pe'], lw['phi'])
    o = _attn_prompt(slopes, q, kc_rows, vc_t, kvc_t, kvw_t, gt, b, t)

    gp = lw['s5'][3].shape[1]
    h0 = jnp.zeros((b, gp), F32)
    y3, hr, hi = _s5(u3, h0, h0, lw['s5'], b)

    y = _tail(x.reshape(rows, d), o, y3, sg, lw, g_out, 512).reshape(b, t, d)
    n_win = min(WINDOW, t)
    ng = gp // SSM_STATE
    return (y, _feature_major_rows(kvc_t, 0), _feature_major_rows(kvw_t, t - n_win),
            hr.reshape(b, ng, SSM_STATE), hi.reshape(b, ng, SSM_STATE))


def _layer_sample(x, cache, page_table, win_buf, h_re, h_im, lw, slopes, g_out):
    b, t, d = x.shape
    rows = b * t
    npages = page_table.shape[1]
    past = npages * cache.shape[1]
    n_buf = win_buf.shape[1]
    assert cache.shape[1] == PAGE and rows % S5_L == 0 and t == S5_L
    assert past % CMP_STRIDE == 0 and t < CMP_STRIDE and past % SEL_BLOCK == 0 and t <= SEL_BLOCK
    npg = DECODE_PAGES_PER_STEP
    assert npages % npg == 0
    q, u3, sg, gt, kvc, kvw = _in_proj(x.reshape(rows, d), lw['g_mix'], lw['w_in'], rows)

    cache_t = cache.transpose(0, 2, 3, 4, 1).reshape(cache.shape[0], 4 * KV_COLS, PAGE)
    win_t = win_buf.transpose(0, 2, 3, 4, 1).reshape(b, 2 * KV_COLS, n_buf)
    pooled = _pool_pages(cache_t, lambda bi, pg, pt: (pt[bi, pg], 0, 0), (page_table,), b, npages,
                         lw['pool_mats'], npg)
    kc_t, vc_t, _ = _cmp_finish(pooled, lw['pool'], lw['pe'], lw['phi'])

    nrows = N_HEADS * t
    eye = jnp.eye(N_KV, dtype=BF16)
    q5 = q.reshape(b, t, N_KV, GROUP, HEAD_DIM).transpose(0, 2, 3, 1, 4)
    qbd = (q5[:, :, :, :, None, :] * eye[None, :, None, None, :, None]).reshape(b, nrows, KV_COLS)
    maskbd = jnp.repeat(jnp.repeat(jnp.eye(N_KV, dtype=F32), GROUP * t, axis=0), HEAD_DIM, axis=1)
    slopecol = jnp.repeat(slopes, t).reshape(nrows, 1)
    g3 = gt.reshape(b, t, N_KV, 128)[..., :3 * GROUP].reshape(b, t, N_KV, 3, GROUP)
    g3 = g3.transpose(0, 2, 4, 1, 3).reshape(b, nrows, 3)
    gates_rows = jnp.pad(g3, ((0, 0), (0, 0), (0, 1)))
    n_blk = -(-(past + t) // SEL_BLOCK)

    opart, sel_t = _attn_dec1(qbd, kc_t, vc_t, win_t, kvw.reshape(b, t, 2 * KV_COLS), gates_rows, slopecol, maskbd,
                              past, t, n_blk)
    bps = npg * PAGE // SEL_BLOCK
    nstep = npages // npg
    nb_all = (nstep + 1) * bps
    sel = jnp.pad(sel_t[:, :n_blk], ((0, 0), (0, nb_all - n_blk), (0, 0)))
    sel = sel.reshape(b, nstep + 1, bps, N_KV, 1, t).transpose(0, 1, 3, 4, 5, 2)
    selrows = jnp.broadcast_to(sel, (b, nstep + 1, N_KV, GROUP, t, bps)).reshape(b, nstep + 1, nrows, bps)
    knew = jnp.pad(kvc.reshape(b, t, 4 * KV_COLS)[:, :, 2 * KV_COLS:], ((0, 0), (0, PAGE - t), (0, 0)))
    o_rows = _attn_dec2(cache_t, page_table, qbd, selrows.astype(BF16), slopecol, knew, opart, gates_rows,
                        maskbd, past, t, npg)
    o = (o_rows.reshape(b, N_KV, GROUP, t, HEAD_DIM).transpose(0, 3, 1, 2, 4).reshape(rows, Q_COLS).astype(BF16))

    gp = h_re.shape[1] * h_re.shape[2]
    y3, hr, hi = _s5(u3, h_re.reshape(b, gp), h_im.reshape(b, gp), lw['s5'], b)

    y, *ffn_w_bf16 = _tail(x.reshape(rows, d), o, y3, sg, lw, g_out, rows)
    y = y.reshape(b, t, d)
    win_all = jnp.concatenate([win_buf, kvw.reshape(b, t, 2, N_KV, HEAD_DIM)], axis=1)
    n_keep = min(WINDOW, n_buf + t)
    return (y, kvc.reshape(b, t, 4, N_KV, HEAD_DIM), win_all[:, n_buf + t - n_keep:], hr.reshape(h_re.shape),
            hi.reshape(h_im.shape), tuple(ffn_w_bf16))


def kernel(x_prompt, x_sample, cache_kv, state_win, state_ssm_re, state_ssm_im, page_table, g_mix, w_in, w_cmp_pe, w_cmp_pool, w_cmp_phi, w_attn_out, ssm_lam_re, ssm_lam_im, ssm_log_dt, ssm_b_re, ssm_b_im, ssm_c_re, ssm_c_im, ssm_d, w_glu, w_out, g_ffn, w_gate, w_up, w_down, g_final):
    depth = g_mix.shape[0]
    assert depth == 1, "final norm is fused into the last layer's FFN; one layer supported"
    params = dict(g_mix=g_mix, w_in=w_in, w_cmp_pe=w_cmp_pe, w_cmp_pool=w_cmp_pool, w_cmp_phi=w_cmp_phi,
                  w_attn_out=w_attn_out, ssm_lam_re=ssm_lam_re, ssm_lam_im=ssm_lam_im, ssm_log_dt=ssm_log_dt,
                  ssm_b_re=ssm_b_re, ssm_b_im=ssm_b_im, ssm_c_re=ssm_c_re, ssm_c_im=ssm_c_im, ssm_d=ssm_d,
                  w_glu=w_glu, w_out=w_out, g_ffn=g_ffn, w_gate=w_gate, w_up=w_up, w_down=w_down)
    slopes = _alibi_slopes()
    outs = [[] for _ in range(8)]
    xp, xs = x_prompt, x_sample
    for l in range(depth):
        lw = _layer_weights(params, l)
        xs, kvs, wins, hrs, his, ffn_w = _layer_sample(xs, cache_kv[l], page_table, state_win[l], state_ssm_re[l],
                                                       state_ssm_im[l], lw, slopes, g_final)
        xp, kvp, winp, hrp, hip = _layer_prompt(xp, dict(lw, ffn_w=ffn_w), slopes, g_final)
        for lst, v in zip(outs, (kvp, winp, hrp, hip, kvs, wins, hrs, his)):
            lst.append(v)
    st = [jnp.stack(v) for v in outs]
    return (xp, xs, st[0], st[1], st[2], st[3], st[4], st[5], st[6], st[7])
```

```python
import functools
import math

import numpy as np
import jax
import jax.numpy as jnp
from jax import lax
from jax.experimental import pallas as pl
from jax.experimental.pallas import tpu as pltpu

F32 = jnp.float32
BF16 = jnp.bfloat16

N_HEADS = 16
HEAD_DIM = 64
N_KV = 4
GROUP = N_HEADS // N_KV
CMP_LEN = 32
CMP_STRIDE = 16
SEL_BLOCK = 64
SEL_TOP = 16
WINDOW = 512
SSM_GROUP = 16
SSM_STATE = 64
EPS = 1e-6
NEG = -1e30
FORCE_SCORE = 1e4

KV_COLS = N_KV * HEAD_DIM
Q_COLS = N_HEADS * HEAD_DIM
PAGE = 128
CMP_PER_PAGE = PAGE // CMP_STRIDE
PROMPT_PAGES_PER_STEP = 16
DECODE_PAGES_PER_STEP = 64
S5_L = 8
S5_SLAB = 128 // SSM_GROUP
VMEM_LIMIT = 56 * 1024 * 1024
LOG2E = 1.0 / math.log(2.0)
Q_SCALE = HEAD_DIM ** -0.5 * LOG2E


def _dot(a, b):
    return jnp.dot(a, b, preferred_element_type=F32)


def _dot_nt(a, b):
    return lax.dot_general(a, b, (((1,), (1,)), ((), ())), preferred_element_type=F32)


def _split(a):
    hi = a.astype(BF16)
    lo = (a - hi.astype(F32)).astype(BF16)
    return hi, lo


def _dot3(a, b):
    ah, al = _split(a)
    bh, bl = _split(b)
    return _dot(ah, bh) + _dot(ah, bl) + _dot(al, bh)


def _dot_nt_hl(w_bf16, a):
    ah, al = _split(a)
    return _dot_nt(w_bf16, ah) + _dot_nt(w_bf16, al)


def _gelu(x):
    return 0.5 * x * (1.0 + jnp.tanh(math.sqrt(2.0 / math.pi) * (x + 0.044715 * (x * x * x))))


def _iota(shape, dim):
    return lax.broadcasted_iota(jnp.int32, shape, dim)


def _log2(n):
    assert n > 0 and n & (n - 1) == 0, n
    return n.bit_length() - 1


def _div_pow2(x, n):
    return jnp.right_shift(x, _log2(n))


def _mod_pow2(x, n):
    return jnp.bitwise_and(x, (1 << _log2(n)) - 1)


def _softmax2_rows(s):
    m = jnp.max(s, axis=-1, keepdims=True)
    e = jnp.exp2(s - m)
    return e / jnp.sum(e, axis=-1, keepdims=True)


IN_TN = 512
_Q_T, _U_T, _MG_T, _KV_T = 2, 2, 8, 3
_KVC_T = 2
_U_0 = _Q_T
_MG_0 = _U_0 + _U_T
_GT_0 = _MG_0 + _MG_T
_KV_0 = _GT_0 + 1
IN_TILES = _KV_0 + _KV_T


def _inproj_kernel(x_ref, g_ref, wq_ref, wu_ref, wmg_ref, wgt_ref, wkv_ref, q_ref, u_ref, sg_ref, gt_ref, kvc_ref,
                   kvw_ref, xn_ref, *, kv_feature_major):
    j = pl.program_id(1)

    @pl.when(j == 0)
    def _():
        x = x_ref[...]
        r = lax.rsqrt(jnp.mean(x * x, axis=-1, keepdims=True) + EPS)
        xn_ref[...] = (x * r * g_ref[...]).astype(BF16)

    @pl.when(j < _U_0)
    def _():
        q_ref[...] = (_dot(xn_ref[...], wq_ref[...]) * Q_SCALE).astype(BF16)

    @pl.when((j >= _U_0) & (j < _MG_0))
    def _():
        z = _dot(xn_ref[...], wu_ref[...])
        for s in range(IN_TN // 128):
            u_ref[s] = z[:, 128 * s:128 * (s + 1)].reshape(u_ref.shape[1:])

    @pl.when((j >= _MG_0) & (j < _GT_0))
    def _():
        sg_ref[...] = jax.nn.sigmoid(_dot(xn_ref[...], wmg_ref[...])).astype(BF16)

    @pl.when(j == _GT_0)
    def _():
        gt_ref[...] = jax.nn.sigmoid(_dot(xn_ref[...], wgt_ref[...]))

    @pl.when(j >= _KV_0)
    def _():
        if kv_feature_major:
            z = _dot_nt(wkv_ref[...], xn_ref[...])
        else:
            z = _dot_nt(xn_ref[...], wkv_ref[...])

        @pl.when(j < _KV_0 + _KVC_T)
        def _():
            kvc_ref[...] = z.reshape(kvc_ref.shape)

        @pl.when(j >= _KV_0 + _KVC_T)
        def _():
            kvw_ref[...] = z.reshape(kvw_ref.shape)


def _in_proj(x, g_mix, w, tm, seq_len=None):
    rows, d = x.shape
    feature_major = seq_len is not None

    def col(lo, n):
        return lambda i, j: (i, jnp.clip(j - lo, 0, n - 1))

    if feature_major:
        tiles_per_seq = seq_len // tm
        kvc_spec = pl.BlockSpec((1, IN_TN, tm), lambda i, j: (i // tiles_per_seq, jnp.clip(j - _KV_0, 0, _KVC_T - 1),
                                                              i % tiles_per_seq))
        kvw_spec = pl.BlockSpec((1, IN_TN, tm), lambda i, j: (i // tiles_per_seq, 0, i % tiles_per_seq))
        kvc_shape = jax.ShapeDtypeStruct((rows // seq_len, _KVC_T * IN_TN, seq_len), F32)
        kvw_shape = jax.ShapeDtypeStruct((rows // seq_len, IN_TN, seq_len), F32)
    else:
        kvc_spec = pl.BlockSpec((tm, IN_TN), col(_KV_0, _KVC_T))
        kvw_spec = pl.BlockSpec((tm, IN_TN), lambda i, j: (i, 0))
        kvc_shape = jax.ShapeDtypeStruct((rows, _KVC_T * IN_TN), F32)
        kvw_shape = jax.ShapeDtypeStruct((rows, IN_TN), F32)
    wkv_spec = pl.BlockSpec((IN_TN, d), lambda i, j: (jnp.clip(j - _KV_0, 0, _KV_T - 1), 0))
    n_slab = IN_TN // 128
    return pl.pallas_call(
        functools.partial(_inproj_kernel, kv_feature_major=feature_major),
        grid=(rows // tm, IN_TILES),
        in_specs=[
            pl.BlockSpec((tm, d), lambda i, j: (i, 0), pipeline_mode=pl.Buffered(1)),
            pl.BlockSpec((1, d), lambda i, j: (0, 0)),
            pl.BlockSpec((d, IN_TN), lambda i, j: (0, jnp.clip(j, 0, _Q_T - 1))),
            pl.BlockSpec((d, IN_TN), lambda i, j: (0, jnp.clip(j - _U_0, 0, _U_T - 1))),
            pl.BlockSpec((d, IN_TN), lambda i, j: (0, jnp.clip(j - _MG_0, 0, _MG_T - 1))),
            pl.BlockSpec((d, IN_TN), lambda i, j: (0, 0), pipeline_mode=pl.Buffered(1)),
            wkv_spec,
        ],
        out_specs=[
            pl.BlockSpec((tm, IN_TN), col(0, _Q_T)),
            pl.BlockSpec((n_slab, tm // S5_L, S5_L, 128), lambda i, j: (jnp.clip(j - _U_0, 0, _U_T - 1), i, 0, 0)),
            pl.BlockSpec((tm, IN_TN), col(_MG_0, _MG_T)),
            pl.BlockSpec((tm, IN_TN), lambda i, j: (i, 0)),
            kvc_spec, kvw_spec,
        ],
        out_shape=[
            jax.ShapeDtypeStruct((rows, _Q_T * IN_TN), BF16),
            jax.ShapeDtypeStruct((_U_T * n_slab, rows // S5_L, S5_L, 128), F32),
            jax.ShapeDtypeStruct((rows, _MG_T * IN_TN), BF16),
            jax.ShapeDtypeStruct((rows, IN_TN), F32),
            kvc_shape, kvw_shape,
        ],
        scratch_shapes=[pltpu.VMEM((tm, d), BF16)],
        compiler_params=pltpu.CompilerParams(
            dimension_semantics=("parallel", "arbitrary"), vmem_limit_bytes=VMEM_LIMIT),
        name="in_proj",
    )(x, g_mix.reshape(1, d), w['q'], w['u'], w['mg'], w['gt'], w['kv_t'])


def _pack_w_in(w_in, ssm_width):
    cuts = np.cumsum([Q_COLS, 6 * KV_COLS, 3 * N_HEADS, ssm_width]).tolist()
    wq, wkv, wgt, wu, wmg = jnp.split(w_in, cuts, axis=1)
    assert wq.shape[1] == _Q_T * IN_TN and wkv.shape[1] == _KV_T * IN_TN
    assert wu.shape[1] == _U_T * IN_TN and wmg.shape[1] == _MG_T * IN_TN and IN_TN == N_KV * 128
    d = w_in.shape[0]
    wgt = wgt.reshape(d, 3, N_KV, GROUP).transpose(0, 2, 1, 3).reshape(d, N_KV, 3 * GROUP)
    wgt = jnp.pad(wgt, ((0, 0), (0, 0), (0, 128 - 3 * GROUP))).reshape(d, IN_TN)
    return dict(q=wq.astype(BF16), u=wu.astype(BF16), mg=wmg.astype(BF16), gt=wgt.astype(BF16),
                kv_t=wkv.T.astype(BF16))


def _pool_kernel(*refs, npg, n_tables):
    refs = refs[n_tables:]
    prev_ref = refs[0]
    page_refs = refs[1:npg + 1]
    mk_ref, mv_ref, mprev_ref, out_ref = refs[npg + 1:]
    pages = [prev_ref[0].astype(BF16)] + [page_refs[j][0].astype(BF16) for j in range(npg)]
    mpg = mk_ref.shape[0] // PAGE
    nblk = mk_ref.shape[1]
    for seg in range(npg // mpg):
        x = jnp.concatenate(pages[1 + seg * mpg:1 + (seg + 1) * mpg], axis=1)
        for half, m_ref in ((0, mk_ref), (1, mv_ref)):
            rows = slice(half * KV_COLS, (half + 1) * KV_COLS)
            head = _dot(pages[seg * mpg][rows], mprev_ref[half])
            if seg == 0:
                head = jnp.where(pl.program_id(1) > 0, head, 0.0)
            out_ref[0, rows, seg * nblk:(seg + 1) * nblk] = _dot(x[rows], m_ref[...]) + head


def _pool_matrices(pool, npg):
    nblk = npg * CMP_PER_PAGE
    cols = []
    for r in range(CMP_PER_PAGE):
        start = CMP_STRIDE * (r - 1)
        lo, hi = max(start, 0), min(start + CMP_LEN, PAGE)
        cols.append(jnp.pad(pool[:, lo - start:hi - start], ((0, 0), (lo, PAGE - hi))))
    body = jnp.stack(cols, axis=-1)
    carry = jnp.pad(pool[:, :CMP_STRIDE], ((0, 0), (PAGE - CMP_STRIDE, 0)))
    first = jnp.asarray(np.arange(CMP_PER_PAGE) == 0, F32)
    same = jnp.eye(npg, dtype=F32)[None, :, None, :, None]
    nxt = jnp.eye(npg, k=1, dtype=F32)[None, :, None, :, None]
    m = same * body[:, None, :, None, :] + nxt * (carry[:, None, :, None, None] * first)
    m = m.reshape(2, npg * PAGE, nblk).astype(BF16)
    mprev = (carry[:, :, None] * jnp.asarray(np.arange(nblk) == 0, F32)).astype(BF16)
    return m[0], m[1], mprev


def _pool_pages(pages, page_index, tables, nseq, npages, mats, npg):
    assert npg % (mats[0].shape[0] // PAGE) == 0
    nstep = npages // npg
    nblk = npg * CMP_PER_PAGE
    mk, mv, mprev = mats

    def spec(off):
        return pl.BlockSpec((1, 2 * KV_COLS, PAGE),
                            lambda b, s, *t: page_index(b, jnp.maximum(s * npg + off, 0), *t))

    const = lambda shape: pl.BlockSpec(shape, lambda b, s, *t: (0,) * len(shape))
    return pl.pallas_call(
        functools.partial(_pool_kernel, npg=npg, n_tables=len(tables)),
        grid_spec=pltpu.PrefetchScalarGridSpec(
            num_scalar_prefetch=len(tables),
            grid=(nseq, nstep),
            in_specs=[spec(j) for j in range(-1, npg)] + [const(mk.shape), const(mv.shape), const(mprev.shape)],
            out_specs=pl.BlockSpec((1, 2 * KV_COLS, nblk), lambda b, s, *t: (b, 0, s)),
        ),
        out_shape=jax.ShapeDtypeStruct((nseq, 2 * KV_COLS, npages * CMP_PER_PAGE), F32),
        compiler_params=pltpu.CompilerParams(
            dimension_semantics=("parallel", "arbitrary"), vmem_limit_bytes=VMEM_LIMIT),
        name="cmp_pool",
    )(*tables, *([pages] * (npg + 1)), mk, mv, mprev)


def _cmp_finish_kernel(pooled_ref, poolw_ref, pe_ref, phi_ref, kc_ref, vc_ref, kcr_ref):
    bias = jnp.sum(poolw_ref[...] * pe_ref[...], axis=1, keepdims=True)
    x = _gelu(pooled_ref[0] + bias)
    kc = _dot3(phi_ref[0], x[:KV_COLS])
    kc_ref[0] = kc.astype(BF16)
    kcr_ref[0] = kc.T.astype(BF16)
    vc_ref[0] = _dot3(phi_ref[1], x[KV_COLS:]).astype(BF16)


def _cmp_finish(pooled, pool, pe, phi):
    nseq, _, n = pooled.shape
    poolw = jnp.concatenate([jnp.broadcast_to(pool[s][None, :], (KV_COLS, CMP_LEN)) for s in range(2)], axis=0)
    pe_t = jnp.concatenate([jnp.tile(pe[s].T, (N_KV, 1)) for s in range(2)], axis=0)
    phi_bd = jnp.stack([jnp.kron(jnp.eye(N_KV, dtype=F32), phi[s].T) for s in range(2)])
    spec = pl.BlockSpec((1, KV_COLS, n), lambda b: (b, 0, 0))
    return pl.pallas_call(
        _cmp_finish_kernel,
        grid=(nseq,),
        in_specs=[
            pl.BlockSpec((1, 2 * KV_COLS, n), lambda b: (b, 0, 0)),
            pl.BlockSpec((2 * KV_COLS, CMP_LEN), lambda b: (0, 0)),
            pl.BlockSpec((2 * KV_COLS, CMP_LEN), lambda b: (0, 0)),
            pl.BlockSpec((2, KV_COLS, KV_COLS), lambda b: (0, 0, 0)),
        ],
        out_specs=[spec, spec, pl.BlockSpec((1, n, KV_COLS), lambda b: (b, 0, 0))],
        out_shape=[jax.ShapeDtypeStruct((nseq, KV_COLS, n), BF16)] * 2
        + [jax.ShapeDtypeStruct((nseq, n, KV_COLS), BF16)],
        compiler_params=pltpu.CompilerParams(dimension_semantics=("parallel",)),
        name="cmp_finish",
    )(pooled, poolw, pe_t, phi_bd)


ATT_TQ = 512
ATT_TK = 512
NBLK_PAD = 128
POS_ROWS = 16
AUG = HEAD_DIM + POS_ROWS


def _attn_prompt_kernel(q_ref, kca_ref, vc_ref, ks_ref, vs_ref, kw_ref, vw_ref, gt_ref, ovt_ref, ktab_ref,
                        stab_ref, o_ref, ksa_ref, vsa_ref, kwa_ref, vwa_ref, qat_ref, m_ref, acc_ref,
                        *, tq, tk, n_sel, n_top):
    qi = pl.program_id(2)
    q0 = qi * tq
    seq = ksa_ref.shape[0]

    @pl.when(qi == 0)
    def _():
        zero = jnp.zeros((128 - HEAD_DIM, seq), F32)
        ones = jnp.where(_iota((POS_ROWS, seq), 0) == 0, 1.0, 0.0).astype(BF16)
        for src, dst, lo in ((ks_ref, ksa_ref, NBLK_PAD), (kw_ref, kwa_ref, 0)):
            k_rows = jnp.concatenate([src[0], zero], axis=0).T[:, :HEAD_DIM].astype(BF16)
            dst[:, lo:] = jnp.concatenate([k_rows, ktab_ref[:, NBLK_PAD:]], axis=1)
        ksa_ref[:, 0:NBLK_PAD] = ktab_ref[:, 0:NBLK_PAD]
        for src, dst in ((vs_ref, vsa_ref), (vw_ref, vwa_ref)):
            dst[0:HEAD_DIM] = src[0].astype(BF16)
            dst[HEAD_DIM:] = ones

    q_t = q_ref[...].astype(F32).T
    q_t = jnp.concatenate([q_t[n * HEAD_DIM:(n + 1) * HEAD_DIM] for n in range(GROUP)], axis=1).astype(BF16)
    stab = stab_ref[0]
    sp_t = jnp.concatenate([jnp.broadcast_to(stab[:, n:n + 1], (POS_ROWS, tq)) for n in range(GROUP)], axis=1)
    qat_ref[NBLK_PAD:NBLK_PAD + HEAD_DIM] = q_t
    qat_ref[NBLK_PAD + HEAD_DIM:] = sp_t.astype(BF16)
    trow = q0 + _iota((1, tq), 1)
    lanes = [slice(n * tq, (n + 1) * tq) for n in range(GROUP)]

    ncmp = kca_ref.shape[2]
    nidx = _iota((ncmp, 1), 0)
    heads = lambda a: jnp.concatenate([a] * GROUP, axis=1)
    valid = (nidx >= 1) & (CMP_STRIDE * nidx + (CMP_STRIDE - 1) <= trow)
    valid = heads(jnp.where(valid, 1.0, 0.0)) > 0.5
    s = jnp.where(valid, _dot(kca_ref[0, 0], qat_ref[NBLK_PAD:]), NEG)
    e = jnp.exp2(s - jnp.max(s, axis=0, keepdims=True))
    pc = jnp.where(valid, e / jnp.sum(e, axis=0, keepdims=True), 0.0)
    o_c = _dot(vc_ref[0], pc.astype(BF16))

    ph, plo = _split((pc[:, lanes[0]] + pc[:, lanes[1]]) + (pc[:, lanes[2]] + pc[:, lanes[3]]))
    imp = (_dot(ovt_ref[...], ph) + _dot(ovt_ref[...], plo))[0:n_sel]
    blk = _iota((n_sel, 1), 0)
    cur = _div_pow2(q0 + _iota((1, tq), 1), SEL_BLOCK)
    forced = (blk == 0) | (blk == cur) | (blk == cur - 1)
    imp = jnp.where(forced, FORCE_SCORE, imp)
    imp = jnp.where(blk <= cur, imp, -1.0)
    cnt = jnp.zeros((n_sel, tq), F32)
    for j in range(n_sel):
        vj = imp[j:j + 1, :]
        tie = jnp.where(blk > j, 1.0, 0.0)
        cnt = cnt + jnp.where(vj > imp, 1.0, jnp.where(vj == imp, tie, 0.0))
    mask_t = jnp.where((cnt < n_top) & (blk <= cur), 0.0, NEG)
    mask_t = jnp.concatenate([mask_t, jnp.full((NBLK_PAD - n_sel, tq), NEG, F32)], axis=0).astype(BF16)
    qat_ref[0:NBLK_PAD] = heads(mask_t)

    m_ref[...] = jnp.full_like(m_ref, NEG)
    acc_ref[...] = jnp.zeros_like(acc_ref)

    def scores(kt):
        return _dot(ksa_ref[pl.ds(pl.multiple_of(kt * tk, tk), tk), :], qat_ref[...])

    def update(kt, s):
        update_keys(pl.multiple_of(kt * tk, tk), tk, s, None)

    def update_keys(k0, nk, s, lane_sl):
        gather = (lambda ref: ref[...]) if lane_sl is None else (
            lambda ref: jnp.concatenate([ref[:, sl] for sl in lane_sl], axis=1))
        m_old = gather(m_ref)
        m_new = jnp.maximum(m_old, jnp.max(s, axis=0, keepdims=True))
        p = jnp.exp2(s - m_new).astype(BF16)
        acc_new = jnp.exp2(m_old - m_new) * gather(acc_ref) + _dot(vsa_ref[:, pl.ds(k0, nk)], p)
        if lane_sl is None:
            acc_ref[...] = acc_new
            m_ref[...] = m_new
        else:
            w = acc_new.shape[1] // len(lane_sl)
            for i, sl in enumerate(lane_sl):
                acc_ref[:, sl] = acc_new[:, i * w:(i + 1) * w]
                m_ref[:, sl] = m_new[:, i * w:(i + 1) * w]

    def pair(j, carry):
        s0, s1 = scores(2 * j), scores(2 * j + 1)
        update(2 * j, s0)
        update(2 * j + 1, s1)
        return carry

    lax.fori_loop(0, qi // 2, pair, 0)

    @pl.when(qi % 2 == 1)
    def _():
        update(qi - 1, scores(qi - 1))

    hq = tq // 2
    upper = [slice(n * tq + hq, (n + 1) * tq) for n in range(GROUP)]
    k_mid = pl.multiple_of(q0 + hq, hq)
    s_d0 = (_dot(ksa_ref[pl.ds(pl.multiple_of(q0, hq), hq), :], qat_ref[...])
            + heads(jnp.where(q0 + _iota((hq, 1), 0) <= trow, 0.0, NEG)))
    s_d1 = (_dot(ksa_ref[pl.ds(k_mid, hq), :], jnp.concatenate([qat_ref[:, sl] for sl in upper], axis=1))
            + jnp.concatenate([jnp.where(_iota((hq, 1), 0) <= _iota((1, hq), 1), 0.0, NEG)] * GROUP, axis=1))

    span = WINDOW + hq
    o_w_half = []
    for h in range(2):
        qh = q0 + h * hq
        w0 = pl.multiple_of(jnp.maximum(qh - WINDOW, 0), hq)
        dist = (qh + _iota((1, hq), 1)) - (w0 + _iota((span, 1), 0))
        qa_h = jnp.concatenate([qat_ref[NBLK_PAD:, n * tq + h * hq:n * tq + (h + 1) * hq] for n in range(GROUP)], axis=1)
        s_win = _dot(kwa_ref[pl.ds(w0, span), :], qa_h) + heads(jnp.where((dist >= 0) & (dist < WINDOW), 0.0, NEG))
        ow = _dot(vwa_ref[:, pl.ds(w0, span)],
                  jnp.exp2(s_win - jnp.max(s_win, axis=0, keepdims=True)).astype(BF16))
        o_w_half.append(ow[:HEAD_DIM] / ow[HEAD_DIM:HEAD_DIM + 1])
    o_w = jnp.concatenate([o_w_half[h][:, n * hq:(n + 1) * hq] for n in range(GROUP) for h in range(2)], axis=1)
    update_keys(pl.multiple_of(q0, hq), hq, s_d0, None)
    update_keys(k_mid, hq, s_d1, upper)
    acc = acc_ref[...]
    o_s = acc[:HEAD_DIM] / acc[HEAD_DIM:HEAD_DIM + 1]

    gt_t = gt_ref[...].T
    outs = []
    for n in range(GROUP):
        outs.append(gt_t[n:n + 1] * o_c[:, lanes[n]] + gt_t[GROUP + n:GROUP + n + 1] * o_s[:, lanes[n]]
                    + gt_t[2 * GROUP + n:2 * GROUP + n + 1] * o_w[:, lanes[n]])
    o_ref[...] = jnp.concatenate(outs, axis=0).T.astype(BF16)


def _overlap_t(n_blk_rows, n_cmp):
    s = np.arange(n_blk_rows)[:, None]
    n = np.arange(n_cmp)[None, :]
    r = SEL_BLOCK // CMP_STRIDE
    return jnp.asarray(((n >= r * s) & (n <= r * s + r)).astype(np.float32), dtype=BF16)


def _block_expand(n_blk_rows, n_keys):
    s = np.arange(n_blk_rows)[:, None]
    k = np.arange(n_keys)[None, :]
    return jnp.asarray((k // SEL_BLOCK == s).astype(np.float32), dtype=BF16)


def _attn_prompt(slopes, q, kc_rows, vc_t, kvc_t, kvw_t, gates, nbatch, seq):
    ncmp = kc_rows.shape[1]
    tq, tk = ATT_TQ, ATT_TK
    assert tk == tq and seq % tk == 0 and WINDOW % tq == 0 and WINDOW + tq <= seq
    nq = seq // tq
    n_sel = seq // SEL_BLOCK
    ovt = _overlap_t(NBLK_PAD, ncmp)
    ktab = jnp.concatenate([_block_expand(NBLK_PAD, seq), _position_rows(seq)], axis=0).T
    cmp_end = CMP_STRIDE * np.arange(ncmp) + (CMP_STRIDE - 1)
    kca = jnp.concatenate([kc_rows.reshape(nbatch, ncmp, N_KV, HEAD_DIM).transpose(0, 2, 1, 3),
                           jnp.broadcast_to(_position_rows(cmp_end).T, (nbatch, N_KV, ncmp, POS_ROWS))], axis=-1)
    stab = _slope_table(slopes).astype(F32).transpose(0, 2, 1)
    per_slot = KV_COLS // HEAD_DIM

    def kv_spec(slot):
        return pl.BlockSpec((1, HEAD_DIM, seq), lambda b, g, i: (b, slot * per_slot + g, 0))

    q_spec = pl.BlockSpec((tq, GROUP * HEAD_DIM), lambda b, g, i: (b * nq + i, g))
    return pl.pallas_call(
        functools.partial(_attn_prompt_kernel, tq=tq, tk=tk, n_sel=n_sel, n_top=min(SEL_TOP, n_sel)),
        grid=(nbatch, N_KV, nq),
        in_specs=[
            q_spec,
            pl.BlockSpec((1, 1, ncmp, AUG), lambda b, g, i: (b, g, 0, 0)),
            pl.BlockSpec((1, HEAD_DIM, ncmp), lambda b, g, i: (b, g, 0)),
            kv_spec(2), kv_spec(3), kv_spec(0), kv_spec(1),
            pl.BlockSpec((tq, 128), lambda b, g, i: (b * nq + i, g)),
            pl.BlockSpec((NBLK_PAD, ncmp), lambda b, g, i: (0, 0)),
            pl.BlockSpec((seq, NBLK_PAD + POS_ROWS), lambda b, g, i: (0, 0)),
            pl.BlockSpec((1, POS_ROWS, GROUP), lambda b, g, i: (g, 0, 0)),
        ],
        out_specs=q_spec,
        out_shape=jax.ShapeDtypeStruct(q.shape, BF16),
        scratch_shapes=[
            pltpu.VMEM((seq, NBLK_PAD + AUG), BF16),
            pltpu.VMEM((AUG, seq), BF16),
            pltpu.VMEM((seq, AUG), BF16),
            pltpu.VMEM((AUG, seq), BF16),
            pltpu.VMEM((NBLK_PAD + AUG, GROUP * tq), BF16),
            pltpu.VMEM((1, GROUP * tq), F32),
            pltpu.VMEM((AUG, GROUP * tq), F32),
        ],
        compiler_params=pltpu.CompilerParams(
            dimension_semantics=("parallel", "parallel", "arbitrary"), vmem_limit_bytes=VMEM_LIMIT),
        name="attn_prompt",
    )(q, kca, vc_t, kvc_t, kvc_t, kvw_t, kvw_t, gates, ovt, ktab, stab)


def _diag_blocks(o, maskbd):
    o = o * maskbd
    return (o[:, 0:64] + o[:, 64:128]) + (o[:, 128:192] + o[:, 192:256])


def _attn_dec1_kernel(qbd_ref, kc_ref, vc_ref, win_ref, kvnew_ref, gates_ref, slope_ref, ovt_ref,
                      maskbd_ref, opart_ref, sel_ref, *, past, tdec, n_blk, n_top):
    qb = qbd_ref[0]
    nrows = qb.shape[0]
    tcol = past + _mod_pow2(_iota((nrows, 1), 0), tdec)
    pref = float(past + tdec)
    slope = slope_ref[...]
    maskbd = maskbd_ref[...]

    ncmp = kc_ref.shape[2]
    nrow = _iota((1, ncmp), 1)
    cend = CMP_STRIDE * nrow + (CMP_STRIDE - 1)
    valid = (nrow >= 1) & (cend <= tcol)
    s = _dot(qb, kc_ref[0])
    s = jnp.where(valid, s + slope * (cend.astype(F32) - pref), NEG)
    p = jnp.where(valid, _softmax2_rows(s), 0.0)
    o_c = _diag_blocks(_dot_nt(p.astype(BF16), vc_ref[0]), maskbd)

    per_g = GROUP * tdec
    psum = jnp.concatenate(
        [sum(p[gi * per_g + n * tdec: gi * per_g + (n + 1) * tdec] for n in range(GROUP)) for gi in range(N_KV)],
        axis=0)
    imp = _dot_nt_hl(ovt_ref[...], psum)
    nb_pad = imp.shape[0]
    blk = _iota((nb_pad, 1), 0)
    cur = _div_pow2(past + _mod_pow2(_iota((1, N_KV * tdec), 1), tdec), SEL_BLOCK)
    forced = (blk == 0) | (blk == cur) | (blk == cur - 1)
    imp = jnp.where(forced, FORCE_SCORE, imp)
    imp = jnp.where(blk <= cur, imp, -1.0)
    imp = jnp.where(blk < n_blk, imp, -2.0)
    sel = jnp.zeros(imp.shape, F32)
    for _ in range(n_top):
        mx = jnp.max(imp, axis=0, keepdims=True)
        first = jnp.min(jnp.where(imp == mx, blk, nb_pad), axis=0, keepdims=True)
        pick = blk == first
        sel = jnp.where(pick, 1.0, sel)
        imp = jnp.where(pick, -jnp.inf, imp)
    sel_ref[0] = sel

    nbuf = win_ref.shape[2]
    knew = kvnew_ref[0]
    zpad = jnp.zeros((PAGE - tdec, KV_COLS), F32)
    kn = jnp.concatenate([knew[:, :KV_COLS], zpad], axis=0).astype(BF16)
    vn = jnp.concatenate([knew[:, KV_COLS:], zpad], axis=0).astype(BF16)
    sw = jnp.concatenate([_dot(qb, win_ref[0, :KV_COLS].astype(BF16)), _dot_nt(qb, kn)], axis=1)
    idx = _iota((1, nbuf + PAGE), 1)
    kposw = past - nbuf + idx
    dist = tcol - kposw
    validw = (dist >= 0) & (dist < WINDOW) & (kposw >= 0) & (idx < nbuf + tdec)
    sw = jnp.where(validw, sw + slope * (kposw.astype(F32) - pref), NEG)
    pw = _softmax2_rows(sw).astype(BF16)
    o_w = _dot_nt(pw[:, :nbuf], win_ref[0, KV_COLS:].astype(BF16)) + _dot(pw[:, nbuf:], vn)
    o_w = _diag_blocks(o_w, maskbd)

    gates = gates_ref[0]
    opart_ref[0] = gates[:, 0:1] * o_c + gates[:, 2:3] * o_w


def _attn_dec1(qbd, kc_t, vc_t, win_t, kvnew, gates_rows, slopecol, maskbd, past, tdec, n_blk):
    nseq, nrows, _ = qbd.shape
    ncmp = kc_t.shape[2]
    nb_pad = -(-n_blk // 8) * 8
    ovt = _overlap_t(nb_pad, ncmp)
    nbuf = win_t.shape[2]
    return pl.pallas_call(
        functools.partial(_attn_dec1_kernel, past=past, tdec=tdec, n_blk=n_blk, n_top=min(SEL_TOP, n_blk)),
        grid=(nseq,),
        in_specs=[
            pl.BlockSpec((1, nrows, KV_COLS), lambda b: (b, 0, 0)),
            pl.BlockSpec((1, KV_COLS, ncmp), lambda b: (b, 0, 0)),
            pl.BlockSpec((1, KV_COLS, ncmp), lambda b: (b, 0, 0)),
            pl.BlockSpec((1, 2 * KV_COLS, nbuf), lambda b: (b, 0, 0)),
            pl.BlockSpec((1, tdec, 2 * KV_COLS), lambda b: (b, 0, 0)),
            pl.BlockSpec((1, nrows, 4), lambda b: (b, 0, 0)),
            pl.BlockSpec((nrows, 1), lambda b: (0, 0)),
            pl.BlockSpec((nb_pad, ncmp), lambda b: (0, 0)),
            pl.BlockSpec((nrows, KV_COLS), lambda b: (0, 0)),
        ],
        out_specs=[
            pl.BlockSpec((1, nrows, HEAD_DIM), lambda b: (b, 0, 0)),
            pl.BlockSpec((1, nb_pad, N_KV * tdec), lambda b: (b, 0, 0)),
        ],
        out_shape=[
            jax.ShapeDtypeStruct((nseq, nrows, HEAD_DIM), F32),
            jax.ShapeDtypeStruct((nseq, nb_pad, N_KV * tdec), F32),
        ],
        compiler_params=pltpu.CompilerParams(dimension_semantics=("parallel",), vmem_limit_bytes=VMEM_LIMIT),
        name="attn_dec_select",
    )(qbd, kc_t, vc_t, win_t, kvnew, gates_rows, slopecol, ovt, maskbd)


def _attn_dec2_kernel(pt_ref, *refs, npg, past, tdec):
    page_refs = refs[:npg]
    (qbd_ref, selrow_ref, sellast_ref, e_ref, slope_ref, knew_ref, opart_ref, gates_ref, maskbd_ref,
     o_ref, m_ref, l_ref, acc_ref) = refs[npg:]
    step = pl.program_id(1)
    nkeys = npg * PAGE
    qb = qbd_ref[0]
    nrows = qb.shape[0]
    pref = float(past + tdec)
    slope = slope_ref[...]

    @pl.when(step == 0)
    def _():
        m_ref[...] = jnp.full_like(m_ref, NEG)
        l_ref[...] = jnp.zeros_like(l_ref)
        acc_ref[...] = jnp.zeros_like(acc_ref)

    def update(s, pv_fn):
        m_old = m_ref[...]
        m_new = jnp.maximum(m_old, jnp.max(s, axis=-1, keepdims=True))
        alpha = jnp.exp2(m_old - m_new)
        p = jnp.exp2(s - m_new)
        l_ref[...] = alpha * l_ref[...] + jnp.sum(p, axis=-1, keepdims=True)
        acc_ref[...] = alpha * acc_ref[...] + pv_fn(p.astype(BF16))
        m_ref[...] = m_new

    k_t = jnp.concatenate([page_refs[j][0, :KV_COLS].astype(BF16) for j in range(npg)], axis=1)
    v_t = jnp.concatenate([page_refs[j][0, KV_COLS:].astype(BF16) for j in range(npg)], axis=1)
    kpos = step * nkeys + _iota((1, nkeys), 1)
    chosen = _dot(selrow_ref[0, 0], e_ref[...])
    s = _dot(qb, k_t) + slope * (kpos.astype(F32) - pref) + jnp.where(chosen > 0.5, 0.0, NEG)
    update(s, lambda p: _dot_nt(p, v_t))

    @pl.when(step == pl.num_programs(1) - 1)
    def _():
        tcol = past + _mod_pow2(_iota((nrows, 1), 0), tdec)
        kn = knew_ref[0]
        kposn = past + _iota((1, PAGE), 1)
        sn = _dot_nt(qb, kn[:, :KV_COLS].astype(BF16))
        ok = (sellast_ref[0, 0][:, 0:1].astype(F32) > 0.5) & (kposn <= tcol)
        sn = sn + slope * (kposn.astype(F32) - pref) + jnp.where(ok, 0.0, NEG)
        update(sn, lambda p: _dot(p, kn[:, KV_COLS:].astype(BF16)))
        o_s = _diag_blocks(acc_ref[...] / l_ref[...], maskbd_ref[...])
        o_ref[0] = opart_ref[0] + gates_ref[0][:, 1:2] * o_s


def _attn_dec2(cache_t, page_table, qbd, selrows, slopecol, knew, opart, gates_rows, maskbd, past, tdec, npg):
    nseq, npages = page_table.shape
    nstep = npages // npg
    nrows = qbd.shape[1]
    blk_per_step = npg * PAGE // SEL_BLOCK
    emat = _block_expand(blk_per_step, npg * PAGE)
    page_specs = [
        pl.BlockSpec((1, 2 * KV_COLS, PAGE), lambda b, s, pt, j=j: (pt[b, s * npg + j], 1, 0))
        for j in range(npg)
    ]
    per_seq = lambda shape: pl.BlockSpec((1,) + shape, lambda b, s, pt: (b,) + (0,) * len(shape))
    const = lambda shape: pl.BlockSpec(shape, lambda b, s, pt: (0,) * len(shape))
    return pl.pallas_call(
        functools.partial(_attn_dec2_kernel, npg=npg, past=past, tdec=tdec),
        grid_spec=pltpu.PrefetchScalarGridSpec(
            num_scalar_prefetch=1,
            grid=(nseq, nstep),
            in_specs=page_specs + [
                per_seq((nrows, KV_COLS)),
                pl.BlockSpec((1, 1, nrows, blk_per_step), lambda b, s, pt: (b, s, 0, 0)),
                pl.BlockSpec((1, 1, nrows, blk_per_step), lambda b, s, pt: (b, nstep, 0, 0)),
                const((blk_per_step, npg * PAGE)),
                const((nrows, 1)),
                per_seq((PAGE, 2 * KV_COLS)),
                per_seq((nrows, HEAD_DIM)),
                per_seq((nrows, 4)),
                const((nrows, KV_COLS)),
            ],
            out_specs=per_seq((nrows, HEAD_DIM)),
            scratch_shapes=[
                pltpu.VMEM((nrows, 1), F32),
                pltpu.VMEM((nrows, 1), F32),
                pltpu.VMEM((nrows, KV_COLS), F32),
            ],
        ),
        out_shape=jax.ShapeDtypeStruct((nseq, nrows, HEAD_DIM), F32),
        compiler_params=pltpu.CompilerParams(
            dimension_semantics=("parallel", "arbitrary"), vmem_limit_bytes=VMEM_LIMIT),
        name="attn_dec_selected",
    )(page_table, *([cache_t] * npg), qbd, selrows, selrows, emat, slopecol, knew, opart, gates_rows, maskbd)


def _s5_weights(lam_re, lam_im, log_dt, b_re, b_im, c_re, c_im, d_skip):
    hp = lax.Precision.HIGHEST
    L = S5_L
    ng, p = lam_re.shape
    c = SSM_GROUP
    ns, gs = ng // S5_SLAB, S5_SLAB
    dt = jnp.exp(log_dt)[:, None]
    lr, li = lam_re, lam_im
    mag = jnp.exp(lr * dt)
    ar = mag * jnp.cos(li * dt)
    ai = mag * jnp.sin(li * dt)
    den = lr * lr + li * li
    fr = ((ar - 1.0) * lr + ai * li) / den
    fi = (ai * lr - (ar - 1.0) * li) / den
    bt_re, bt_im = b_re.transpose(0, 2, 1), b_im.transpose(0, 2, 1)
    bbr = fr[:, None, :] * bt_re - fi[:, None, :] * bt_im
    bbi = fr[:, None, :] * bt_im + fi[:, None, :] * bt_re
    j = jnp.arange(L + 1, dtype=F32)[:, None, None]
    pmag = jnp.exp(j * (lr * dt))
    pr = pmag * jnp.cos(j * (li * dt))
    pi = pmag * jnp.sin(j * (li * dt))
    abr = pr[:, :, None, :] * bbr - pi[:, :, None, :] * bbi
    abi = pr[:, :, None, :] * bbi + pi[:, :, None, :] * bbr
    kd = (jnp.einsum('jgkp,gcp->jgkc', abr, c_re, precision=hp)
          - jnp.einsum('jgkp,gcp->jgkc', abi, c_im, precision=hp))
    kd = kd.at[0].add(d_skip.reshape(ng, 1, c) * jnp.eye(c, dtype=F32))
    kc = kd[:L].reshape(L, ns, gs * c, c)
    abc = jnp.concatenate([abr[:L], abi[:L]], axis=-1).reshape(L, ns, gs * c, 2 * p)
    cr = c_re[None] * pr[1:, :, None, :] - c_im[None] * pi[1:, :, None, :]
    ci = -(c_re[None] * pi[1:, :, None, :] + c_im[None] * pr[1:, :, None, :])
    co = jnp.stack([cr, ci], axis=1).reshape(L, 2, ns, gs, c, p)
    al = jnp.stack([pr[L].reshape(-1), pi[L].reshape(-1)])
    return kc.astype(BF16), abc.astype(BF16), co.astype(BF16), al


def _s5_expanders():
    gs, c, p = S5_SLAB, SSM_GROUP, SSM_STATE
    lane = np.arange(gs * c)
    rep_c = (np.arange(c)[:, None] == lane[None, :] % c)
    same_g = (lane[:, None] // c == lane[None, :] // c)
    st = np.arange(2 * gs * p)
    rp = np.arange(2 * p)
    rep_p = (rp[:, None] // p == st[None, :] // (gs * p)) & (rp[:, None] % p == st[None, :] % p)
    g_in = (lane[:, None] // c == (st[None, :] // p) % gs)
    as_bf = lambda a: jnp.asarray(a.astype(np.float32), dtype=BF16)
    return as_bf(rep_c), as_bf(same_g), as_bf(rep_p), as_bf(g_in)


def _chunk_lanes(u_ref):
    return jnp.concatenate([u_ref[0, :, t, :] for t in range(S5_L)], axis=1)


def _s5_local_kernel(u_ref, abc_ref, rep_p_ref, g_in_ref, pre_ref, pim_ref, wp_ref):
    for s in range(S5_L):
        blk = _dot(abc_ref[S5_L - 1 - s, 0], rep_p_ref[...]).astype(BF16) * g_in_ref[...]
        wp_ref[128 * s:128 * (s + 1), :] = blk
    r = _dot(_chunk_lanes(u_ref).astype(BF16), wp_ref[...])
    half = r.shape[1] // 2
    pre_ref[...] = r[:, :half]
    pim_ref[...] = r[:, half:]


def _s5_scan_kernel(pre_ref, pim_ref, al_ref, h0r_ref, h0i_ref, hsr_ref, hsi_ref, hfr_ref, hfi_ref):
    nchunk = pre_ref.shape[1]
    ar = al_ref[0:1, :]
    ai = al_ref[1:2, :]

    def body(k, carry):
        cr, ci = carry
        hsr_ref[:, pl.ds(k, 1), :] = cr[:, None, :]
        hsi_ref[:, pl.ds(k, 1), :] = ci[:, None, :]
        xr = pre_ref[:, pl.ds(k, 1), :][:, 0, :]
        xi = pim_ref[:, pl.ds(k, 1), :][:, 0, :]
        return ar * cr - ai * ci + xr, ar * ci + ai * cr + xi

    cr, ci = lax.fori_loop(0, nchunk, body, (h0r_ref[...], h0i_ref[...]), unroll=min(nchunk, 4))
    hfr_ref[...] = cr
    hfi_ref[...] = ci


def _s5_out_kernel(u_ref, kc_ref, co_ref, rep_c_ref, same_g_ref, hsr_ref, hsi_ref, y_ref, wt_ref, wot_ref):
    rep_c = rep_c_ref[...]
    lag = [_dot(kc_ref[dl, 0], rep_c).astype(BF16) * same_g_ref[...] for dl in range(S5_L)]
    zero = jnp.zeros((128, 128), BF16)
    for s in range(S5_L):
        for t in range(S5_L):
            wt_ref[128 * s:128 * (s + 1), 128 * t:128 * (t + 1)] = lag[t - s] if t >= s else zero
    wot_ref[...] = jnp.zeros_like(wot_ref)
    c, p = SSM_GROUP, SSM_STATE
    for t in range(S5_L):
        for r in range(2):
            for g in range(S5_SLAB):
                col = (r * S5_SLAB + g) * p
                wot_ref[128 * t + c * g:128 * t + c * (g + 1), col:col + p] = co_ref[t, r, 0, g]
    hs = jnp.concatenate([hsr_ref[...], hsi_ref[...]], axis=1).astype(BF16)
    y = _dot(_chunk_lanes(u_ref).astype(BF16), wt_ref[...]) + _dot_nt(hs, wot_ref[...])
    for t in range(S5_L):
        y_ref[0, :, t, :] = y[:, 128 * t:128 * (t + 1)]


def _s5(u3, h0r, h0i, weights, nbatch):
    kc, abc, co, al = weights
    rep_c, same_g, rep_p, g_in = _s5_expanders()
    ns, nb = u3.shape[0], u3.shape[1]
    nchunk = nb // nbatch
    gp = al.shape[1]
    lb = gp // ns
    d = S5_L * 128
    cp = pltpu.CompilerParams(dimension_semantics=("parallel",), vmem_limit_bytes=VMEM_LIMIT)
    u_spec = pl.BlockSpec((1, nb, S5_L, 128), lambda j: (j, 0, 0, 0))
    st_spec = pl.BlockSpec((nb, lb), lambda j: (0, j))
    slab = lambda a: pl.BlockSpec((a.shape[0], 1) + a.shape[2:], lambda j: (0, j) + (0,) * (a.ndim - 2))
    const = lambda a: pl.BlockSpec(a.shape, lambda j: (0,) * a.ndim)
    pre, pim = pl.pallas_call(
        _s5_local_kernel,
        grid=(ns,),
        in_specs=[u_spec, slab(abc), const(rep_p), const(g_in)],
        out_specs=[st_spec] * 2,
        out_shape=[jax.ShapeDtypeStruct((nb, gp), F32)] * 2,
        scratch_shapes=[pltpu.VMEM((d, 2 * lb), BF16)],
        compiler_params=cp,
        name="s5_local",
    )(u3, abc, rep_p, g_in)

    ls = 2 * lb
    seq3 = pl.BlockSpec((nbatch, nchunk, ls), lambda j: (0, 0, j))
    row = lambda r: pl.BlockSpec((r, ls), lambda j: (0, j))
    hsr, hsi, hfr, hfi = pl.pallas_call(
        _s5_scan_kernel,
        grid=(gp // ls,),
        in_specs=[seq3, seq3, row(2), row(nbatch), row(nbatch)],
        out_specs=[seq3, seq3, row(nbatch), row(nbatch)],
        out_shape=[jax.ShapeDtypeStruct((nbatch, nchunk, gp), F32)] * 2 + [jax.ShapeDtypeStruct((nbatch, gp), F32)] * 2,
        compiler_params=cp,
        name="s5_scan",
    )(pre.reshape(nbatch, nchunk, gp), pim.reshape(nbatch, nchunk, gp), al, h0r, h0i)

    y3 = pl.pallas_call(
        _s5_out_kernel,
        grid=(ns,),
        in_specs=[u_spec, slab(kc),
                  pl.BlockSpec(co.shape[:2] + (1,) + co.shape[3:], lambda j: (0, 0, j, 0, 0, 0)),
                  const(rep_c), const(same_g), st_spec, st_spec],
        out_specs=u_spec,
        out_shape=jax.ShapeDtypeStruct(u3.shape, F32),
        scratch_shapes=[pltpu.VMEM((d, d), BF16), pltpu.VMEM((2 * lb, d), BF16)],
        compiler_params=cp,
        name="s5_out",
    )(u3, kc, co, rep_c, same_g, hsr.reshape(nb, gp), hsi.reshape(nb, gp))
    return y3, hfr, hfi


def _merge_kernel(o_ref, y_ref, wa_ref, wg1_ref, wg2_ref, sga_ref, sgb_ref, m_ref, gy_ref):
    @pl.when(pl.program_id(1) == 0)
    def _():
        y = jnp.concatenate([y_ref[s] for s in range(y_ref.shape[0])], axis=1)
        gy_ref[...] = _gelu(y).astype(BF16)

    ya = _dot(o_ref[...], wa_ref[...])
    gy = gy_ref[...]
    yb = _dot(gy, wg1_ref[...]) * jax.nn.sigmoid(_dot(gy, wg2_ref[...]))
    m = sga_ref[...].astype(F32) * ya + sgb_ref[...].astype(F32) * yb
    m_ref[...] = m.astype(BF16)


def _merge(o, y3, w_attn_out, w_glu, sg, tm, tn=1024):
    rows, d = o.shape[0], w_attn_out.shape[1]
    nj = d // tn
    kq, ks = w_attn_out.shape[0], w_glu.shape[0]
    ns = y3.shape[0]
    return pl.pallas_call(
        _merge_kernel,
        grid=(rows // tm, nj),
        in_specs=[
            pl.BlockSpec((tm, kq), lambda i, j: (i, 0)),
            pl.BlockSpec((ns, tm, 128), lambda i, j: (0, i, 0)),
            pl.BlockSpec((kq, tn), lambda i, j: (0, j)),
            pl.BlockSpec((ks, tn), lambda i, j: (0, j)),
            pl.BlockSpec((ks, tn), lambda i, j: (0, j + nj)),
            pl.BlockSpec((tm, tn), lambda i, j: (i, j)),
            pl.BlockSpec((tm, tn), lambda i, j: (i, j + nj)),
        ],
        out_specs=pl.BlockSpec((tm, tn), lambda i, j: (i, j)),
        out_shape=jax.ShapeDtypeStruct((rows, d), BF16),
        scratch_shapes=[pltpu.VMEM((tm, ks), BF16)],
        compiler_params=pltpu.CompilerParams(
            dimension_semantics=("parallel", "arbitrary"), vmem_limit_bytes=VMEM_LIMIT),
        name="merge",
    )(o, y3, w_attn_out, w_glu, w_glu, sg, sg)


def _outproj_kernel(m_ref, x_ref, w_ref, g_ref, x1_ref, h_ref):
    x1 = x_ref[...] + _dot(m_ref[...], w_ref[...])
    x1_ref[...] = x1
    r = lax.rsqrt(jnp.mean(x1 * x1, axis=-1, keepdims=True) + EPS)
    h_ref[...] = (x1 * r * g_ref[...]).astype(BF16)


def _outproj(m, x, w_out, g_ffn, tm):
    rows, d = x.shape
    row = pl.BlockSpec((tm, d), lambda i: (i, 0))
    return pl.pallas_call(
        _outproj_kernel,
        grid=(rows // tm,),
        in_specs=[row, row, pl.BlockSpec((d, d), lambda i: (0, 0), pipeline_mode=pl.Buffered(1)),
                  pl.BlockSpec((1, d), lambda i: (0, 0))],
        out_specs=[row, row],
        out_shape=[jax.ShapeDtypeStruct((rows, d), F32), jax.ShapeDtypeStruct((rows, d), BF16)],
        compiler_params=pltpu.CompilerParams(dimension_semantics=("parallel",), vmem_limit_bytes=VMEM_LIMIT),
        name="out_proj",
    )(m, x, w_out, g_ffn.reshape(1, d))


def _ffn_kernel(h_ref, x1_ref, wg_ref, wu_ref, wd_ref, gf_ref, y_ref, *rest):
    acc_ref = rest[-1]
    f = pl.program_id(1)

    @pl.when(f == 0)
    def _():
        acc_ref[...] = jnp.zeros_like(acc_ref)

    wg, wu, wd = wg_ref[...].astype(BF16), wu_ref[...].astype(BF16), wd_ref[...].astype(BF16)
    for copy_ref, w in zip(rest[:-1], (wg, wu, wd)):
        copy_ref[...] = w
    h = h_ref[...]
    a = _dot(h, wg)
    a = (a * jax.nn.sigmoid(a)) * _dot(h, wu)
    acc_ref[...] += _dot(a.astype(BF16), wd)

    @pl.when(f == pl.num_programs(1) - 1)
    def _():
        y = x1_ref[...] + acc_ref[...]
        r = lax.rsqrt(jnp.mean(y * y, axis=-1, keepdims=True) + EPS)
        y_ref[...] = y * r * gf_ref[...]


def _ffn(h, x1, w_gate, w_up, w_down, g_final, tm, tf=512):
    rows, d = x1.shape
    dff = w_gate.shape[1]
    emit = w_gate.dtype != BF16
    assert not emit or rows == tm
    row = pl.BlockSpec((tm, d), lambda i, f: (i, 0))
    w_specs = [pl.BlockSpec((d, tf), lambda i, f: (0, f)),
               pl.BlockSpec((d, tf), lambda i, f: (0, f)),
               pl.BlockSpec((tf, d), lambda i, f: (f, 0))]
    y_shape = jax.ShapeDtypeStruct((rows, d), F32)
    copies = [jax.ShapeDtypeStruct(w.shape, BF16) for w in (w_gate, w_up, w_down)] if emit else []
    out = pl.pallas_call(
        _ffn_kernel,
        grid=(rows // tm, dff // tf),
        in_specs=[row, row] + w_specs + [pl.BlockSpec((1, d), lambda i, f: (0, 0))],
        out_specs=[row] + (w_specs if emit else []),
        out_shape=[y_shape] + copies,
        scratch_shapes=[pltpu.VMEM((tm, d), F32)],
        compiler_params=pltpu.CompilerParams(
            dimension_semantics=("parallel", "arbitrary"), vmem_limit_bytes=VMEM_LIMIT),
        name="ffn",
    )(h, x1, w_gate, w_up, w_down, g_final.reshape(1, d))
    return tuple(out) if emit else out[0]


def _alibi_slopes():
    return jnp.exp2(-8.0 * jnp.arange(1, N_HEADS + 1, dtype=F32) / N_HEADS) * LOG2E


def _slope_table(slopes):
    s1 = slopes.astype(BF16)
    r1 = slopes - s1.astype(F32)
    s2 = r1.astype(BF16)
    s3 = (r1 - s2.astype(F32)).astype(BF16)
    tab = jnp.stack([s1, s1, s2, s2, s3, s3] + [jnp.zeros_like(s1)] * 10, axis=-1)
    return tab.reshape(N_KV, GROUP, 16)


def _position_rows(pos):
    pos = np.arange(pos) if np.isscalar(pos) else np.asarray(pos)
    hi, lo = (pos // 64) * 64, pos % 64
    assert pos.max() < 64 * 256
    rows = np.stack([hi, lo, hi, lo, hi, lo] + [np.zeros_like(pos)] * 10).astype(np.float32)
    return jnp.asarray(rows, dtype=BF16)


def _layer_weights(p, l):
    w_in = _pack_w_in(p['w_in'][l], p['ssm_d'].shape[-1])
    s5 = _s5_weights(*(p[n][l] for n in ('ssm_lam_re', 'ssm_lam_im', 'ssm_log_dt', 'ssm_b_re', 'ssm_b_im',
                                          'ssm_c_re', 'ssm_c_im', 'ssm_d')))
    cast = lambda n: p[n][l].astype(BF16)
    return dict(
        g_mix=p['g_mix'][l], w_in=w_in,
        pool=p['w_cmp_pool'][l], pe=p['w_cmp_pe'][l], phi=p['w_cmp_phi'][l],
        pool_mats=_pool_matrices(p['w_cmp_pool'][l], PROMPT_PAGES_PER_STEP),
        w_attn_out=cast('w_attn_out'), w_glu=cast('w_glu'), w_out=cast('w_out'), g_ffn=p['g_ffn'][l],
        ffn_w=tuple(p[n][l] for n in ('w_gate', 'w_up', 'w_down')), s5=s5)


def _tail(x, o, y3, sg, lw, g_final, tm):
    rows = x.shape[0]
    m = _merge(o, y3.reshape(y3.shape[0], rows, 128), lw['w_attn_out'], lw['w_glu'], sg, tm)
    x1, h = _outproj(m, x, lw['w_out'], lw['g_ffn'], tm)
    return _ffn(h, x1, *lw['ffn_w'], g_final, tm)


def _feature_major_rows(kv_t, t0):
    b, f, t = kv_t.shape
    return kv_t[:, :, t0:].reshape(b, f // KV_COLS, N_KV, HEAD_DIM, t - t0).transpose(0, 4, 1, 2, 3)


def _layer_prompt(x, lw, slopes, g_out):
    b, t, d = x.shape
    rows = b * t
    assert t % (PROMPT_PAGES_PER_STEP * PAGE) == 0
    q, u3, sg, gt, kvc_t, kvw_t = _in_proj(x.reshape(rows, d), lw['g_mix'], lw['w_in'], 1024, seq_len=t)

    pooled = _pool_pages(kvc_t, lambda bi, pg: (bi, 0, pg), (), b, t // PAGE, lw['pool_mats'],
                         PROMPT_PAGES_PER_STEP)
    _, vc_t, kc_rows = _cmp_finish(pooled, lw['pool'], lw['pe'], lw['phi'])
    o = _attn_prompt(slopes, q, kc_rows, vc_t, kvc_t, kvw_t, gt, b, t)

    gp = lw['s5'][3].shape[1]
    h0 = jnp.zeros((b, gp), F32)
    y3, hr, hi = _s5(u3, h0, h0, lw['s5'], b)

    y = _tail(x.reshape(rows, d), o, y3, sg, lw, g_out, 512).reshape(b, t, d)
    n_win = min(WINDOW, t)
    ng = gp // SSM_STATE
    return (y, _feature_major_rows(kvc_t, 0), _feature_major_rows(kvw_t, t - n_win),
            hr.reshape(b, ng, SSM_STATE), hi.reshape(b, ng, SSM_STATE))


def _layer_sample(x, cache, page_table, win_buf, h_re, h_im, lw, slopes, g_out):
    b, t, d = x.shape
    rows = b * t
    npages = page_table.shape[1]
    past = npages * cache.shape[1]
    n_buf = win_buf.shape[1]
    assert cache.shape[1] == PAGE and rows % S5_L == 0 and t == S5_L
    assert past % CMP_STRIDE == 0 and t < CMP_STRIDE and past % SEL_BLOCK == 0 and t <= SEL_BLOCK
    npg = DECODE_PAGES_PER_STEP
    assert npages % npg == 0
    q, u3, sg, gt, kvc, kvw = _in_proj(x.reshape(rows, d), lw['g_mix'], lw['w_in'], rows)

    cache_t = cache.transpose(0, 2, 3, 4, 1).reshape(cache.shape[0], 4 * KV_COLS, PAGE)
    win_t = win_buf.transpose(0, 2, 3, 4, 1).reshape(b, 2 * KV_COLS, n_buf)
    pooled = _pool_pages(cache_t, lambda bi, pg, pt: (pt[bi, pg], 0, 0), (page_table,), b, npages,
                         lw['pool_mats'], npg)
    kc_t, vc_t, _ = _cmp_finish(pooled, lw['pool'], lw['pe'], lw['phi'])

    nrows = N_HEADS * t
    eye = jnp.eye(N_KV, dtype=BF16)
    q5 = q.reshape(b, t, N_KV, GROUP, HEAD_DIM).transpose(0, 2, 3, 1, 4)
    qbd = (q5[:, :, :, :, None, :] * eye[None, :, None, None, :, None]).reshape(b, nrows, KV_COLS)
    maskbd = jnp.repeat(jnp.repeat(jnp.eye(N_KV, dtype=F32), GROUP * t, axis=0), HEAD_DIM, axis=1)
    slopecol = jnp.repeat(slopes, t).reshape(nrows, 1)
    g3 = gt.reshape(b, t, N_KV, 128)[..., :3 * GROUP].reshape(b, t, N_KV, 3, GROUP)
    g3 = g3.transpose(0, 2, 4, 1, 3).reshape(b, nrows, 3)
    gates_rows = jnp.pad(g3, ((0, 0), (0, 0), (0, 1)))
    n_blk = -(-(past + t) // SEL_BLOCK)

    opart, sel_t = _attn_dec1(qbd, kc_t, vc_t, win_t, kvw.reshape(b, t, 2 * KV_COLS), gates_rows, slopecol, maskbd,
                              past, t, n_blk)
    bps = npg * PAGE // SEL_BLOCK
    nstep = npages // npg
    nb_all = (nstep + 1) * bps
    sel = jnp.pad(sel_t[:, :n_blk], ((0, 0), (0, nb_all - n_blk), (0, 0)))
    sel = sel.reshape(b, nstep + 1, bps, N_KV, 1, t).transpose(0, 1, 3, 4, 5, 2)
    selrows = jnp.broadcast_to(sel, (b, nstep + 1, N_KV, GROUP, t, bps)).reshape(b, nstep + 1, nrows, bps)
    knew = jnp.pad(kvc.reshape(b, t, 4 * KV_COLS)[:, :, 2 * KV_COLS:], ((0, 0), (0, PAGE - t), (0, 0)))
    o_rows = _attn_dec2(cache_t, page_table, qbd, selrows.astype(BF16), slopecol, knew, opart, gates_rows,
                        maskbd, past, t, npg)
    o = (o_rows.reshape(b, N_KV, GROUP, t, HEAD_DIM).transpose(0, 3, 1, 2, 4).reshape(rows, Q_COLS).astype(BF16))

    gp = h_re.shape[1] * h_re.shape[2]
    y3, hr, hi = _s5(u3, h_re.reshape(b, gp), h_im.reshape(b, gp), lw['s5'], b)

    y, *ffn_w_bf16 = _tail(x.reshape(rows, d), o, y3, sg, lw, g_out, rows)
    y = y.reshape(b, t, d)
    win_all = jnp.concatenate([win_buf, kvw.reshape(b, t, 2, N_KV, HEAD_DIM)], axis=1)
    n_keep = min(WINDOW, n_buf + t)
    return (y, kvc.reshape(b, t, 4, N_KV, HEAD_DIM), win_all[:, n_buf + t - n_keep:], hr.reshape(h_re.shape),
            hi.reshape(h_im.shape), tuple(ffn_w_bf16))


def kernel(x_prompt, x_sample, cache_kv, state_win, state_ssm_re, state_ssm_im, page_table, g_mix, w_in, w_cmp_pe, w_cmp_pool, w_cmp_phi, w_attn_out, ssm_lam_re, ssm_lam_im, ssm_log_dt, ssm_b_re, ssm_b_im, ssm_c_re, ssm_c_im, ssm_d, w_glu, w_out, g_ffn, w_gate, w_up, w_down, g_final):
    depth = g_mix.shape[0]
    assert depth == 1, "final norm is fused into the last layer's FFN; one layer supported"
    params = dict(g_mix=g_mix, w_in=w_in, w_cmp_pe=w_cmp_pe, w_cmp_pool=w_cmp_pool, w_cmp_phi=w_cmp_phi,
                  w_attn_out=w_attn_out, ssm_lam_re=ssm_lam_re, ssm_lam_im=ssm_lam_im, ssm_log_dt=ssm_log_dt,
                  ssm_b_re=ssm_b_re, ssm_b_im=ssm_b_im, ssm_c_re=ssm_c_re, ssm_c_im=ssm_c_im, ssm_d=ssm_d,
                  w_glu=w_glu, w_out=w_out, g_ffn=g_ffn, w_gate=w_gate, w_up=w_up, w_down=w_down)
    slopes = _alibi_slopes()
    outs = [[] for _ in range(8)]
    xp, xs = x_prompt, x_sample
    for l in range(depth):
        lw = _layer_weights(params, l)
        xs, kvs, wins, hrs, his, ffn_w = _layer_sample(xs, cache_kv[l], page_table, state_win[l], state_ssm_re[l],
                                                       state_ssm_im[l], lw, slopes, g_final)
        xp, kvp, winp, hrp, hip = _layer_prompt(xp, dict(lw, ffn_w=ffn_w), slopes, g_final)
        for lst, v in zip(outs, (kvp, winp, hrp, hip, kvs, wins, hrs, his)):
            lst.append(v)
    st = [jnp.stack(v) for v in outs]
    return (xp, xs, st[0], st[1], st[2], st[3], st[4], st[5], st[6], st[7])
```

```python
import functools
import math

import numpy as np
import jax
import jax.numpy as jnp
from jax import lax
from jax.experimental import pallas as pl
from jax.experimental.pallas import tpu as pltpu

F32 = jnp.float32
BF16 = jnp.bfloat16

N_HEADS = 16
HEAD_DIM = 64
N_KV = 4
GROUP = N_HEADS // N_KV
CMP_LEN = 32
CMP_STRIDE = 16
SEL_BLOCK = 64
SEL_TOP = 16
WINDOW = 512
SSM_GROUP = 16
SSM_STATE = 64
EPS = 1e-6
NEG = -1e30
FORCE_SCORE = 1e4

KV_COLS = N_KV * HEAD_DIM
Q_COLS = N_HEADS * HEAD_DIM
PAGE = 128
CMP_PER_PAGE = PAGE // CMP_STRIDE
PROMPT_PAGES_PER_STEP = 16
DECODE_PAGES_PER_STEP = 64
S5_L = 8
S5_SLAB = 128 // SSM_GROUP
VMEM_LIMIT = 56 * 1024 * 1024
LOG2E = 1.0 / math.log(2.0)
Q_SCALE = HEAD_DIM ** -0.5 * LOG2E


def _dot(a, b):
    return jnp.dot(a, b, preferred_element_type=F32)


def _dot_nt(a, b):
    return lax.dot_general(a, b, (((1,), (1,)), ((), ())), preferred_element_type=F32)


def _split(a):
    hi = a.astype(BF16)
    lo = (a - hi.astype(F32)).astype(BF16)
    return hi, lo


def _dot3(a, b):
    ah, al = _split(a)
    bh, bl = _split(b)
    return _dot(ah, bh) + _dot(ah, bl) + _dot(al, bh)


def _dot_nt_hl(w_bf16, a):
    ah, al = _split(a)
    return _dot_nt(w_bf16, ah) + _dot_nt(w_bf16, al)


def _gelu(x):
    return 0.5 * x * (1.0 + jnp.tanh(math.sqrt(2.0 / math.pi) * (x + 0.044715 * (x * x * x))))


def _iota(shape, dim):
    return lax.broadcasted_iota(jnp.int32, shape, dim)


def _log2(n):
    assert n > 0 and n & (n - 1) == 0, n
    return n.bit_length() - 1


def _div_pow2(x, n):
    return jnp.right_shift(x, _log2(n))


def _mod_pow2(x, n):
    return jnp.bitwise_and(x, (1 << _log2(n)) - 1)


def _softmax2_rows(s):
    m = jnp.max(s, axis=-1, keepdims=True)
    e = jnp.exp2(s - m)
    return e / jnp.sum(e, axis=-1, keepdims=True)


IN_TN = 512
_Q_T, _U_T, _MG_T, _KV_T = 2, 2, 8, 3
_KVC_T = 2
_U_0 = _Q_T
_MG_0 = _U_0 + _U_T
_GT_0 = _MG_0 + _MG_T
_KV_0 = _GT_0 + 1
IN_TILES = _KV_0 + _KV_T


def _inproj_kernel(x_ref, g_ref, wq_ref, wu_ref, wmg_ref, wgt_ref, wkv_ref, q_ref, u_ref, sg_ref, gt_ref, kvc_ref,
                   kvw_ref, xn_ref, *, kv_feature_major):
    j = pl.program_id(1)

    @pl.when(j == 0)
    def _():
        x = x_ref[...]
        r = lax.rsqrt(jnp.mean(x * x, axis=-1, keepdims=True) + EPS)
        xn_ref[...] = (x * r * g_ref[...]).astype(BF16)

    @pl.when(j < _U_0)
    def _():
        q_ref[...] = (_dot(xn_ref[...], wq_ref[...]) * Q_SCALE).astype(BF16)

    @pl.when((j >= _U_0) & (j < _MG_0))
    def _():
        z = _dot(xn_ref[...], wu_ref[...])
        for s in range(IN_TN // 128):
            u_ref[s] = z[:, 128 * s:128 * (s + 1)].reshape(u_ref.shape[1:])

    @pl.when((j >= _MG_0) & (j < _GT_0))
    def _():
        sg_ref[...] = jax.nn.sigmoid(_dot(xn_ref[...], wmg_ref[...])).astype(BF16)

    @pl.when(j == _GT_0)
    def _():
        gt_ref[...] = jax.nn.sigmoid(_dot(xn_ref[...], wgt_ref[...]))

    @pl.when(j >= _KV_0)
    def _():
        if kv_feature_major:
            z = _dot_nt(wkv_ref[...], xn_ref[...])
        else:
            z = _dot_nt(xn_ref[...], wkv_ref[...])

        @pl.when(j < _KV_0 + _KVC_T)
        def _():
            kvc_ref[...] = z.reshape(kvc_ref.shape)

        @pl.when(j >= _KV_0 + _KVC_T)
        def _():
            kvw_ref[...] = z.reshape(kvw_ref.shape)


def _in_proj(x, g_mix, w, tm, seq_len=None):
    rows, d = x.shape
    feature_major = seq_len is not None

    def col(lo, n):
        return lambda i, j: (i, jnp.clip(j - lo, 0, n - 1))

    if feature_major:
        tiles_per_seq = seq_len // tm
        kvc_spec = pl.BlockSpec((1, IN_TN, tm), lambda i, j: (i // tiles_per_seq, jnp.clip(j - _KV_0, 0, _KVC_T - 1),
                                                              i % tiles_per_seq))
        kvw_spec = pl.BlockSpec((1, IN_TN, tm), lambda i, j: (i // tiles_per_seq, 0, i % tiles_per_seq))
        kvc_shape = jax.ShapeDtypeStruct((rows // seq_len, _KVC_T * IN_TN, seq_len), F32)
        kvw_shape = jax.ShapeDtypeStruct((rows // seq_len, IN_TN, seq_len), F32)
    else:
        kvc_spec = pl.BlockSpec((tm, IN_TN), col(_KV_0, _KVC_T))
        kvw_spec = pl.BlockSpec((tm, IN_TN), lambda i, j: (i, 0))
        kvc_shape = jax.ShapeDtypeStruct((rows, _KVC_T * IN_TN), F32)
        kvw_shape = jax.ShapeDtypeStruct((rows, IN_TN), F32)
    wkv_spec = pl.BlockSpec((IN_TN, d), lambda i, j: (jnp.clip(j - _KV_0, 0, _KV_T - 1), 0))
    n_slab = IN_TN // 128
    return pl.pallas_call(
        functools.partial(_inproj_kernel, kv_feature_major=feature_major),
        grid=(rows // tm, IN_TILES),
        in_specs=[
            pl.BlockSpec((tm, d), lambda i, j: (i, 0), pipeline_mode=pl.Buffered(1)),
            pl.BlockSpec((1, d), lambda i, j: (0, 0)),
            pl.BlockSpec((d, IN_TN), lambda i, j: (0, jnp.clip(j, 0, _Q_T - 1))),
            pl.BlockSpec((d, IN_TN), lambda i, j: (0, jnp.clip(j - _U_0, 0, _U_T - 1))),
            pl.BlockSpec((d, IN_TN), lambda i, j: (0, jnp.clip(j - _MG_0, 0, _MG_T - 1))),
            pl.BlockSpec((d, IN_TN), lambda i, j: (0, 0), pipeline_mode=pl.Buffered(1)),
            wkv_spec,
        ],
        out_specs=[
            pl.BlockSpec((tm, IN_TN), col(0, _Q_T)),
            pl.BlockSpec((n_slab, tm // S5_L, S5_L, 128), lambda i, j: (jnp.clip(j - _U_0, 0, _U_T - 1), i, 0, 0)),
            pl.BlockSpec((tm, IN_TN), col(_MG_0, _MG_T)),
            pl.BlockSpec((tm, IN_TN), lambda i, j: (i, 0)),
            kvc_spec, kvw_spec,
        ],
        out_shape=[
            jax.ShapeDtypeStruct((rows, _Q_T * IN_TN), BF16),
            jax.ShapeDtypeStruct((_U_T * n_slab, rows // S5_L, S5_L, 128), F32),
            jax.ShapeDtypeStruct((rows, _MG_T * IN_TN), BF16),
            jax.ShapeDtypeStruct((rows, IN_TN), F32),
            kvc_shape, kvw_shape,
        ],
        scratch_shapes=[pltpu.VMEM((tm, d), BF16)],
        compiler_params=pltpu.CompilerParams(
            dimension_semantics=("parallel", "arbitrary"), vmem_limit_bytes=VMEM_LIMIT),
        name="in_proj",
    )(x, g_mix.reshape(1, d), w['q'], w['u'], w['mg'], w['gt'], w['kv_t'])


def _pack_w_in(w_in, ssm_width):
    cuts = np.cumsum([Q_COLS, 6 * KV_COLS, 3 * N_HEADS, ssm_width]).tolist()
    wq, wkv, wgt, wu, wmg = jnp.split(w_in, cuts, axis=1)
    assert wq.shape[1] == _Q_T * IN_TN and wkv.shape[1] == _KV_T * IN_TN
    assert wu.shape[1] == _U_T * IN_TN and wmg.shape[1] == _MG_T * IN_TN and IN_TN == N_KV * 128
    d = w_in.shape[0]
    wgt = wgt.reshape(d, 3, N_KV, GROUP).transpose(0, 2, 1, 3).reshape(d, N_KV, 3 * GROUP)
    wgt = jnp.pad(wgt, ((0, 0), (0, 0), (0, 128 - 3 * GROUP))).reshape(d, IN_TN)
    return dict(q=wq.astype(BF16), u=wu.astype(BF16), mg=wmg.astype(BF16), gt=wgt.astype(BF16),
                kv_t=wkv.T.astype(BF16))


def _pool_kernel(*refs, npg, n_tables):
    refs = refs[n_tables:]
    prev_ref = refs[0]
    page_refs = refs[1:npg + 1]
    mk_ref, mv_ref, mprev_ref, out_ref = refs[npg + 1:]
    pages = [prev_ref[0].astype(BF16)] + [page_refs[j][0].astype(BF16) for j in range(npg)]
    mpg = mk_ref.shape[0] // PAGE
    nblk = mk_ref.shape[1]
    for seg in range(npg // mpg):
        x = jnp.concatenate(pages[1 + seg * mpg:1 + (seg + 1) * mpg], axis=1)
        for half, m_ref in ((0, mk_ref), (1, mv_ref)):
            rows = slice(half * KV_COLS, (half + 1) * KV_COLS)
            head = _dot(pages[seg * mpg][rows], mprev_ref[half])
            if seg == 0:
                head = jnp.where(pl.program_id(1) > 0, head, 0.0)
            out_ref[0, rows, seg * nblk:(seg + 1) * nblk] = _dot(x[rows], m_ref[...]) + head


def _pool_matrices(pool, npg):
    nblk = npg * CMP_PER_PAGE
    cols = []
    for r in range(CMP_PER_PAGE):
        start = CMP_STRIDE * (r - 1)
        lo, hi = max(start, 0), min(start + CMP_LEN, PAGE)
        cols.append(jnp.pad(pool[:, lo - start:hi - start], ((0, 0), (lo, PAGE - hi))))
    body = jnp.stack(cols, axis=-1)
    carry = jnp.pad(pool[:, :CMP_STRIDE], ((0, 0), (PAGE - CMP_STRIDE, 0)))
    first = jnp.asarray(np.arange(CMP_PER_PAGE) == 0, F32)
    same = jnp.eye(npg, dtype=F32)[None, :, None, :, None]
    nxt = jnp.eye(npg, k=1, dtype=F32)[None, :, None, :, None]
    m = same * body[:, None, :, None, :] + nxt * (carry[:, None, :, None, None] * first)
    m = m.reshape(2, npg * PAGE, nblk).astype(BF16)
    mprev = (carry[:, :, None] * jnp.asarray(np.arange(nblk) == 0, F32)).astype(BF16)
    return m[0], m[1], mprev


def _pool_pages(pages, page_index, tables, nseq, npages, mats, npg):
    assert npg % (mats[0].shape[0] // PAGE) == 0
    nstep = npages // npg
    nblk = npg * CMP_PER_PAGE
    mk, mv, mprev = mats

    def spec(off):
        return pl.BlockSpec((1, 2 * KV_COLS, PAGE),
                            lambda b, s, *t: page_index(b, jnp.maximum(s * npg + off, 0), *t))

    const = lambda shape: pl.BlockSpec(shape, lambda b, s, *t: (0,) * len(shape))
    return pl.pallas_call(
        functools.partial(_pool_kernel, npg=npg, n_tables=len(tables)),
        grid_spec=pltpu.PrefetchScalarGridSpec(
            num_scalar_prefetch=len(tables),
            grid=(nseq, nstep),
            in_specs=[spec(j) for j in range(-1, npg)] + [const(mk.shape), const(mv.shape), const(mprev.shape)],
            out_specs=pl.BlockSpec((1, 2 * KV_COLS, nblk), lambda b, s, *t: (b, 0, s)),
        ),
        out_shape=jax.ShapeDtypeStruct((nseq, 2 * KV_COLS, npages * CMP_PER_PAGE), F32),
        compiler_params=pltpu.CompilerParams(
            dimension_semantics=("parallel", "arbitrary"), vmem_limit_bytes=VMEM_LIMIT),
        name="cmp_pool",
    )(*tables, *([pages] * (npg + 1)), mk, mv, mprev)


def _cmp_finish_kernel(pooled_ref, poolw_ref, pe_ref, phi_ref, kc_ref, vc_ref, kcr_ref):
    bias = jnp.sum(poolw_ref[...] * pe_ref[...], axis=1, keepdims=True)
    x = _gelu(pooled_ref[0] + bias)
    kc = _dot3(phi_ref[0], x[:KV_COLS])
    kc_ref[0] = kc.astype(BF16)
    kcr_ref[0] = kc.T.astype(BF16)
    vc_ref[0] = _dot3(phi_ref[1], x[KV_COLS:]).astype(BF16)


def _cmp_finish(pooled, pool, pe, phi):
    nseq, _, n = pooled.shape
    poolw = jnp.concatenate([jnp.broadcast_to(pool[s][None, :], (KV_COLS, CMP_LEN)) for s in range(2)], axis=0)
    pe_t = jnp.concatenate([jnp.tile(pe[s].T, (N_KV, 1)) for s in range(2)], axis=0)
    phi_bd = jnp.stack([jnp.kron(jnp.eye(N_KV, dtype=F32), phi[s].T) for s in range(2)])
    spec = pl.BlockSpec((1, KV_COLS, n), lambda b: (b, 0, 0))
    return pl.pallas_call(
        _cmp_finish_kernel,
        grid=(nseq,),
        in_specs=[
            pl.BlockSpec((1, 2 * KV_COLS, n), lambda b: (b, 0, 0)),
            pl.BlockSpec((2 * KV_COLS, CMP_LEN), lambda b: (0, 0)),
            pl.BlockSpec((2 * KV_COLS, CMP_LEN), lambda b: (0, 0)),
            pl.BlockSpec((2, KV_COLS, KV_COLS), lambda b: (0, 0, 0)),
        ],
        out_specs=[spec, spec, pl.BlockSpec((1, n, KV_COLS), lambda b: (b, 0, 0))],
        out_shape=[jax.ShapeDtypeStruct((nseq, KV_COLS, n), BF16)] * 2
        + [jax.ShapeDtypeStruct((nseq, n, KV_COLS), BF16)],
        compiler_params=pltpu.CompilerParams(dimension_semantics=("parallel",)),
        name="cmp_finish",
    )(pooled, poolw, pe_t, phi_bd)


ATT_TQ = 512
ATT_TK = 512
NBLK_PAD = 128
POS_ROWS = 16
AUG = HEAD_DIM + POS_ROWS


def _attn_prompt_kernel(q_ref, kca_ref, vc_ref, ks_ref, vs_ref, kw_ref, vw_ref, gt_ref, ovt_ref, ktab_ref,
                        stab_ref, o_ref, ksa_ref, vsa_ref, kwa_ref, vwa_ref, qat_ref, m_ref, acc_ref,
                        *, tq, tk, n_sel, n_top):
    qi = pl.program_id(2)
    q0 = qi * tq
    seq = ksa_ref.shape[0]

    @pl.when(qi == 0)
    def _():
        zero = jnp.zeros((128 - HEAD_DIM, seq), F32)
        ones = jnp.where(_iota((POS_ROWS, seq), 0) == 0, 1.0, 0.0).astype(BF16)
        for src, dst, lo in ((ks_ref, ksa_ref, NBLK_PAD), (kw_ref, kwa_ref, 0)):
            k_rows = jnp.concatenate([src[0], zero], axis=0).T[:, :HEAD_DIM].astype(BF16)
            dst[:, lo:] = jnp.concatenate([k_rows, ktab_ref[:, NBLK_PAD:]], axis=1)
        ksa_ref[:, 0:NBLK_PAD] = ktab_ref[:, 0:NBLK_PAD]
        for src, dst in ((vs_ref, vsa_ref), (vw_ref, vwa_ref)):
            dst[0:HEAD_DIM] = src[0].astype(BF16)
            dst[HEAD_DIM:] = ones

    q_t = q_ref[...].astype(F32).T
    q_t = jnp.concatenate([q_t[n * HEAD_DIM:(n + 1) * HEAD_DIM] for n in range(GROUP)], axis=1).astype(BF16)
    stab = stab_ref[0]
    sp_t = jnp.concatenate([jnp.broadcast_to(stab[:, n:n + 1], (POS_ROWS, tq)) for n in range(GROUP)], axis=1)
    qat_ref[NBLK_PAD:NBLK_PAD + HEAD_DIM] = q_t
    qat_ref[NBLK_PAD + HEAD_DIM:] = sp_t.astype(BF16)
    trow = q0 + _iota((1, tq), 1)
    lanes = [slice(n * tq, (n + 1) * tq) for n in range(GROUP)]

    ncmp = kca_ref.shape[2]
    nidx = _iota((ncmp, 1), 0)
    heads = lambda a: jnp.concatenate([a] * GROUP, axis=1)
    valid = (nidx >= 1) & (CMP_STRIDE * nidx + (CMP_STRIDE - 1) <= trow)
    valid = heads(jnp.where(valid, 1.0, 0.0)) > 0.5
    s = jnp.where(valid, _dot(kca_ref[0, 0], qat_ref[NBLK_PAD:]), NEG)
    e = jnp.exp2(s - jnp.max(s, axis=0, keepdims=True))
    pc = jnp.where(valid, e / jnp.sum(e, axis=0, keepdims=True), 0.0)
    o_c = _dot(vc_ref[0], pc.astype(BF16))

    ph, plo = _split((pc[:, lanes[0]] + pc[:, lanes[1]]) + (pc[:, lanes[2]] + pc[:, lanes[3]]))
    imp = (_dot(ovt_ref[...], ph) + _dot(ovt_ref[...], plo))[0:n_sel]
    blk = _iota((n_sel, 1), 0)
    cur = _div_pow2(q0 + _iota((1, tq), 1), SEL_BLOCK)
    forced = (blk == 0) | (blk == cur) | (blk == cur - 1)
    imp = jnp.where(forced, FORCE_SCORE, imp)
    imp = jnp.where(blk <= cur, imp, -1.0)
    cnt = jnp.zeros((n_sel, tq), F32)
    for j in range(n_sel):
        vj = imp[j:j + 1, :]
        tie = jnp.where(blk > j, 1.0, 0.0)
        cnt = cnt + jnp.where(vj > imp, 1.0, jnp.where(vj == imp, tie, 0.0))
    mask_t = jnp.where((cnt < n_top) & (blk <= cur), 0.0, NEG)
    mask_t = jnp.concatenate([mask_t, jnp.full((NBLK_PAD - n_sel, tq), NEG, F32)], axis=0).astype(BF16)
    qat_ref[0:NBLK_PAD] = heads(mask_t)

    m_ref[...] = jnp.full_like(m_ref, NEG)
    acc_ref[...] = jnp.zeros_like(acc_ref)

    def scores(kt):
        return _dot(ksa_ref[pl.ds(pl.multiple_of(kt * tk, tk), tk), :], qat_ref[...])

    def update(kt, s):
        update_keys(pl.multiple_of(kt * tk, tk), tk, s, None)

    def update_keys(k0, nk, s, lane_sl):
        gather = (lambda ref: ref[...]) if lane_sl is None else (
            lambda ref: jnp.concatenate([ref[:, sl] for sl in lane_sl], axis=1))
        m_old = gather(m_ref)
        m_new = jnp.maximum(m_old, jnp.max(s, axis=0, keepdims=True))
        p = jnp.exp2(s - m_new).astype(BF16)
        acc_new = jnp.exp2(m_old - m_new) * gather(acc_ref) + _dot(vsa_ref[:, pl.ds(k0, nk)], p)
        if lane_sl is None:
            acc_ref[...] = acc_new
            m_ref[...] = m_new
        else:
            w = acc_new.shape[1] // len(lane_sl)
            for i, sl in enumerate(lane_sl):
                acc_ref[:, sl] = acc_new[:, i * w:(i + 1) * w]
                m_ref[:, sl] = m_new[:, i * w:(i + 1) * w]

    def pair(j, carry):
        s0, s1 = scores(2 * j), scores(2 * j + 1)
        update(2 * j, s0)
        update(2 * j + 1, s1)
        return carry

    lax.fori_loop(0, qi // 2, pair, 0)

    @pl.when(qi % 2 == 1)
    def _():
        update(qi - 1, scores(qi - 1))

    hq = tq // 2
    upper = [slice(n * tq + hq, (n + 1) * tq) for n in range(GROUP)]
    k_mid = pl.multiple_of(q0 + hq, hq)
    s_d0 = (_dot(ksa_ref[pl.ds(pl.multiple_of(q0, hq), hq), :], qat_ref[...])
            + heads(jnp.where(q0 + _iota((hq, 1), 0) <= trow, 0.0, NEG)))
    s_d1 = (_dot(ksa_ref[pl.ds(k_mid, hq), :], jnp.concatenate([qat_ref[:, sl] for sl in upper], axis=1))
            + jnp.concatenate([jnp.where(_iota((hq, 1), 0) <= _iota((1, hq), 1), 0.0, NEG)] * GROUP, axis=1))

    span = WINDOW + hq
    o_w_half = []
    for h in range(2):
        qh = q0 + h * hq
        w0 = pl.multiple_of(jnp.maximum(qh - WINDOW, 0), hq)
        dist = (qh + _iota((1, hq), 1)) - (w0 + _iota((span, 1), 0))
        qa_h = jnp.concatenate([qat_ref[NBLK_PAD:, n * tq + h * hq:n * tq + (h + 1) * hq] for n in range(GROUP)], axis=1)
        s_win = _dot(kwa_ref[pl.ds(w0, span), :], qa_h) + heads(jnp.where((dist >= 0) & (dist < WINDOW), 0.0, NEG))
        ow = _dot(vwa_ref[:, pl.ds(w0, span)],
                  jnp.exp2(s_win - jnp.max(s_win, axis=0, keepdims=True)).astype(BF16))
        o_w_half.append(ow[:HEAD_DIM] / ow[HEAD_DIM:HEAD_DIM + 1])
    o_w = jnp.concatenate([o_w_half[h][:, n * hq:(n + 1) * hq] for n in range(GROUP) for h in range(2)], axis=1)
    update_keys(pl.multiple_of(q0, hq), hq, s_d0, None)
    update_keys(k_mid, hq, s_d1, upper)
    acc = acc_ref[...]
    o_s = acc[:HEAD_DIM] / acc[HEAD_DIM:HEAD_DIM + 1]

    gt_t = gt_ref[...].T
    outs = []
    for n in range(GROUP):
        outs.append(gt_t[n:n + 1] * o_c[:, lanes[n]] + gt_t[GROUP + n:GROUP + n + 1] * o_s[:, lanes[n]]
                    + gt_t[2 * GROUP + n:2 * GROUP + n + 1] * o_w[:, lanes[n]])
    o_ref[...] = jnp.concatenate(outs, axis=0).T.astype(BF16)


def _overlap_t(n_blk_rows, n_cmp):
    s = np.arange(n_blk_rows)[:, None]
    n = np.arange(n_cmp)[None, :]
    r = SEL_BLOCK // CMP_STRIDE
    return jnp.asarray(((n >= r * s) & (n <= r * s + r)).astype(np.float32), dtype=BF16)


def _block_expand(n_blk_rows, n_keys):
    s = np.arange(n_blk_rows)[:, None]
    k = np.arange(n_keys)[None, :]
    return jnp.asarray((k // SEL_BLOCK == s).astype(np.float32), dtype=BF16)


def _attn_prompt(slopes, q, kc_rows, vc_t, kvc_t, kvw_t, gates, nbatch, seq):
    ncmp = kc_rows.shape[1]
    tq, tk = ATT_TQ, ATT_TK
    assert tk == tq and seq % tk == 0 and WINDOW % tq == 0 and WINDOW + tq <= seq
    nq = seq // tq
    n_sel = seq // SEL_BLOCK
    ovt = _overlap_t(NBLK_PAD, ncmp)
    ktab = jnp.concatenate([_block_expand(NBLK_PAD, seq), _position_rows(seq)], axis=0).T
    cmp_end = CMP_STRIDE * np.arange(ncmp) + (CMP_STRIDE - 1)
    kca = jnp.concatenate([kc_rows.reshape(nbatch, ncmp, N_KV, HEAD_DIM).transpose(0, 2, 1, 3),
                           jnp.broadcast_to(_position_rows(cmp_end).T, (nbatch, N_KV, ncmp, POS_ROWS))], axis=-1)
    stab = _slope_table(slopes).astype(F32).transpose(0, 2, 1)
    per_slot = KV_COLS // HEAD_DIM

    def kv_spec(slot):
        return pl.BlockSpec((1, HEAD_DIM, seq), lambda b, g, i: (b, slot * per_slot + g, 0))

    q_spec = pl.BlockSpec((tq, GROUP * HEAD_DIM), lambda b, g, i: (b * nq + i, g))
    return pl.pallas_call(
        functools.partial(_attn_prompt_kernel, tq=tq, tk=tk, n_sel=n_sel, n_top=min(SEL_TOP, n_sel)),
        grid=(nbatch, N_KV, nq),
        in_specs=[
            q_spec,
            pl.BlockSpec((1, 1, ncmp, AUG), lambda b, g, i: (b, g, 0, 0)),
            pl.BlockSpec((1, HEAD_DIM, ncmp), lambda b, g, i: (b, g, 0)),
            kv_spec(2), kv_spec(3), kv_spec(0), kv_spec(1),
            pl.BlockSpec((tq, 128), lambda b, g, i: (b * nq + i, g)),
            pl.BlockSpec((NBLK_PAD, ncmp), lambda b, g, i: (0, 0)),
            pl.BlockSpec((seq, NBLK_PAD + POS_ROWS), lambda b, g, i: (0, 0)),
            pl.BlockSpec((1, POS_ROWS, GROUP), lambda b, g, i: (g, 0, 0)),
        ],
        out_specs=q_spec,
        out_shape=jax.ShapeDtypeStruct(q.shape, BF16),
        scratch_shapes=[
            pltpu.VMEM((seq, NBLK_PAD + AUG), BF16),
            pltpu.VMEM((AUG, seq), BF16),
            pltpu.VMEM((seq, AUG), BF16),
            pltpu.VMEM((AUG, seq), BF16),
            pltpu.VMEM((NBLK_PAD + AUG, GROUP * tq), BF16),
            pltpu.VMEM((1, GROUP * tq), F32),
            pltpu.VMEM((AUG, GROUP * tq), F32),
        ],
        compiler_params=pltpu.CompilerParams(
            dimension_semantics=("parallel", "parallel", "arbitrary"), vmem_limit_bytes=VMEM_LIMIT),
        name="attn_prompt",
    )(q, kca, vc_t, kvc_t, kvc_t, kvw_t, kvw_t, gates, ovt, ktab, stab)


def _diag_blocks(o, maskbd):
    o = o * maskbd
    return (o[:, 0:64] + o[:, 64:128]) + (o[:, 128:192] + o[:, 192:256])


def _attn_dec1_kernel(qbd_ref, kc_ref, vc_ref, win_ref, kvnew_ref, gates_ref, slope_ref, ovt_ref,
                      maskbd_ref, opart_ref, sel_ref, *, past, tdec, n_blk, n_top):
    qb = qbd_ref[0]
    nrows = qb.shape[0]
    tcol = past + _mod_pow2(_iota((nrows, 1), 0), tdec)
    pref = float(past + tdec)
    slope = slope_ref[...]
    maskbd = maskbd_ref[...]

    ncmp = kc_ref.shape[2]
    nrow = _iota((1, ncmp), 1)
    cend = CMP_STRIDE * nrow + (CMP_STRIDE - 1)
    valid = (nrow >= 1) & (cend <= tcol)
    s = _dot(qb, kc_ref[0])
    s = jnp.where(valid, s + slope * (cend.astype(F32) - pref), NEG)
    p = jnp.where(valid, _softmax2_rows(s), 0.0)
    o_c = _diag_blocks(_dot_nt(p.astype(BF16), vc_ref[0]), maskbd)

    per_g = GROUP * tdec
    psum = jnp.concatenate(
        [sum(p[gi * per_g + n * tdec: gi * per_g + (n + 1) * tdec] for n in range(GROUP)) for gi in range(N_KV)],
        axis=0)
    imp = _dot_nt_hl(ovt_ref[...], psum)
    nb_pad = imp.shape[0]
    blk = _iota((nb_pad, 1), 0)
    cur = _div_pow2(past + _mod_pow2(_iota((1, N_KV * tdec), 1), tdec), SEL_BLOCK)
    forced = (blk == 0) | (blk == cur) | (blk == cur - 1)
    imp = jnp.where(forced, FORCE_SCORE, imp)
    imp = jnp.where(blk <= cur, imp, -1.0)
    imp = jnp.where(blk < n_blk, imp, -2.0)
    sel = jnp.zeros(imp.shape, F32)
    for _ in range(n_top):
        mx = jnp.max(imp, axis=0, keepdims=True)
        first = jnp.min(jnp.where(imp == mx, blk, nb_pad), axis=0, keepdims=True)
        pick = blk == first
        sel = jnp.where(pick, 1.0, sel)
        imp = jnp.where(pick, -jnp.inf, imp)
    sel_ref[0] = sel

    nbuf = win_ref.shape[2]
    knew = kvnew_ref[0]
    zpad = jnp.zeros((PAGE - tdec, KV_COLS), F32)
    kn = jnp.concatenate([knew[:, :KV_COLS], zpad], axis=0).astype(BF16)
    vn = jnp.concatenate([knew[:, KV_COLS:], zpad], axis=0).astype(BF16)
    sw = jnp.concatenate([_dot(qb, win_ref[0, :KV_COLS].astype(BF16)), _dot_nt(qb, kn)], axis=1)
    idx = _iota((1, nbuf + PAGE), 1)
    kposw = past - nbuf + idx
    dist = tcol - kposw
    validw = (dist >= 0) & (dist < WINDOW) & (kposw >= 0) & (idx < nbuf + tdec)
    sw = jnp.where(validw, sw + slope * (kposw.astype(F32) - pref), NEG)
    pw = _softmax2_rows(sw).astype(BF16)
    o_w = _dot_nt(pw[:, :nbuf], win_ref[0, KV_COLS:].astype(BF16)) + _dot(pw[:, nbuf:], vn)
    o_w = _diag_blocks(o_w, maskbd)

    gates = gates_ref[0]
    opart_ref[0] = gates[:, 0:1] * o_c + gates[:, 2:3] * o_w


def _attn_dec1(qbd, kc_t, vc_t, win_t, kvnew, gates_rows, slopecol, maskbd, past, tdec, n_blk):
    nseq, nrows, _ = qbd.shape
    ncmp = kc_t.shape[2]
    nb_pad = -(-n_blk // 8) * 8
    ovt = _overlap_t(nb_pad, ncmp)
    nbuf = win_t.shape[2]
    return pl.pallas_call(
        functools.partial(_attn_dec1_kernel, past=past, tdec=tdec, n_blk=n_blk, n_top=min(SEL_TOP, n_blk)),
        grid=(nseq,),
        in_specs=[
            pl.BlockSpec((1, nrows, KV_COLS), lambda b: (b, 0, 0)),
            pl.BlockSpec((1, KV_COLS, ncmp), lambda b: (b, 0, 0)),
            pl.BlockSpec((1, KV_COLS, ncmp), lambda b: (b, 0, 0)),
            pl.BlockSpec((1, 2 * KV_COLS, nbuf), lambda b: (b, 0, 0)),
            pl.BlockSpec((1, tdec, 2 * KV_COLS), lambda b: (b, 0, 0)),
            pl.BlockSpec((1, nrows, 4), lambda b: (b, 0, 0)),
            pl.BlockSpec((nrows, 1), lambda b: (0, 0)),
            pl.BlockSpec((nb_pad, ncmp), lambda b: (0, 0)),
            pl.BlockSpec((nrows, KV_COLS), lambda b: (0, 0)),
        ],
        out_specs=[
            pl.BlockSpec((1, nrows, HEAD_DIM), lambda b: (b, 0, 0)),
            pl.BlockSpec((1, nb_pad, N_KV * tdec), lambda b: (b, 0, 0)),
        ],
        out_shape=[
            jax.ShapeDtypeStruct((nseq, nrows, HEAD_DIM), F32),
            jax.ShapeDtypeStruct((nseq, nb_pad, N_KV * tdec), F32),
        ],
        compiler_params=pltpu.CompilerParams(dimension_semantics=("parallel",), vmem_limit_bytes=VMEM_LIMIT),
        name="attn_dec_select",
    )(qbd, kc_t, vc_t, win_t, kvnew, gates_rows, slopecol, ovt, maskbd)


def _attn_dec2_kernel(pt_ref, *refs, npg, past, tdec):
    page_refs = refs[:npg]
    (qbd_ref, selrow_ref, sellast_ref, e_ref, slope_ref, knew_ref, opart_ref, gates_ref, maskbd_ref,
     o_ref, m_ref, l_ref, acc_ref) = refs[npg:]
    step = pl.program_id(1)
    nkeys = npg * PAGE
    qb = qbd_ref[0]
    nrows = qb.shape[0]
    pref = float(past + tdec)
    slope = slope_ref[...]

    @pl.when(step == 0)
    def _():
        m_ref[...] = jnp.full_like(m_ref, NEG)
        l_ref[...] = jnp.zeros_like(l_ref)
        acc_ref[...] = jnp.zeros_like(acc_ref)

    def update(s, pv_fn):
        m_old = m_ref[...]
        m_new = jnp.maximum(m_old, jnp.max(s, axis=-1, keepdims=True))
        alpha = jnp.exp2(m_old - m_new)
        p = jnp.exp2(s - m_new)
        l_ref[...] = alpha * l_ref[...] + jnp.sum(p, axis=-1, keepdims=True)
        acc_ref[...] = alpha * acc_ref[...] + pv_fn(p.astype(BF16))
        m_ref[...] = m_new

    k_t = jnp.concatenate([page_refs[j][0, :KV_COLS].astype(BF16) for j in range(npg)], axis=1)
    v_t = jnp.concatenate([page_refs[j][0, KV_COLS:].astype(BF16) for j in range(npg)], axis=1)
    kpos = step * nkeys + _iota((1, nkeys), 1)
    chosen = _dot(selrow_ref[0, 0], e_ref[...])
    s = _dot(qb, k_t) + slope * (kpos.astype(F32) - pref) + jnp.where(chosen > 0.5, 0.0, NEG)
    update(s, lambda p: _dot_nt(p, v_t))

    @pl.when(step == pl.num_programs(1) - 1)
    def _():
        tcol = past + _mod_pow2(_iota((nrows, 1), 0), tdec)
        kn = knew_ref[0]
        kposn = past + _iota((1, PAGE), 1)
        sn = _dot_nt(qb, kn[:, :KV_COLS].astype(BF16))
        ok = (sellast_ref[0, 0][:, 0:1].astype(F32) > 0.5) & (kposn <= tcol)
        sn = sn + slope * (kposn.astype(F32) - pref) + jnp.where(ok, 0.0, NEG)
        update(sn, lambda p: _dot(p, kn[:, KV_COLS:].astype(BF16)))
        o_s = _diag_blocks(acc_ref[...] / l_ref[...], maskbd_ref[...])
        o_ref[0] = opart_ref[0] + gates_ref[0][:, 1:2] * o_s


def _attn_dec2(cache_t, page_table, qbd, selrows, slopecol, knew, opart, gates_rows, maskbd, past, tdec, npg):
    nseq, npages = page_table.shape
    nstep = npages // npg
    nrows = qbd.shape[1]
    blk_per_step = npg * PAGE // SEL_BLOCK
    emat = _block_expand(blk_per_step, npg * PAGE)
    page_specs = [
        pl.BlockSpec((1, 2 * KV_COLS, PAGE), lambda b, s, pt, j=j: (pt[b, s * npg + j], 1, 0))
        for j in range(npg)
    ]
    per_seq = lambda shape: pl.BlockSpec((1,) + shape, lambda b, s, pt: (b,) + (0,) * len(shape))
    const = lambda shape: pl.BlockSpec(shape, lambda b, s, pt: (0,) * len(shape))
    return pl.pallas_call(
        functools.partial(_attn_dec2_kernel, npg=npg, past=past, tdec=tdec),
        grid_spec=pltpu.PrefetchScalarGridSpec(
            num_scalar_prefetch=1,
            grid=(nseq, nstep),
            in_specs=page_specs + [
                per_seq((nrows, KV_COLS)),
                pl.BlockSpec((1, 1, nrows, blk_per_step), lambda b, s, pt: (b, s, 0, 0)),
                pl.BlockSpec((1, 1, nrows, blk_per_step), lambda b, s, pt: (b, nstep, 0, 0)),
                const((blk_per_step, npg * PAGE)),
                const((nrows, 1)),
                per_seq((PAGE, 2 * KV_COLS)),
                per_seq((nrows, HEAD_DIM)),
                per_seq((nrows, 4)),
                const((nrows, KV_COLS)),
            ],
            out_specs=per_seq((nrows, HEAD_DIM)),
            scratch_shapes=[
                pltpu.VMEM((nrows, 1), F32),
                pltpu.VMEM((nrows, 1), F32),
                pltpu.VMEM((nrows, KV_COLS), F32),
            ],
        ),
        out_shape=jax.ShapeDtypeStruct((nseq, nrows, HEAD_DIM), F32),
        compiler_params=pltpu.CompilerParams(
            dimension_semantics=("parallel", "arbitrary"), vmem_limit_bytes=VMEM_LIMIT),
        name="attn_dec_selected",
    )(page_table, *([cache_t] * npg), qbd, selrows, selrows, emat, slopecol, knew, opart, gates_rows, maskbd)


def _s5_weights(lam_re, lam_im, log_dt, b_re, b_im, c_re, c_im, d_skip):
    hp = lax.Precision.HIGHEST
    L = S5_L
    ng, p = lam_re.shape
    c = SSM_GROUP
    ns, gs = ng // S5_SLAB, S5_SLAB
    dt = jnp.exp(log_dt)[:, None]
    lr, li = lam_re, lam_im
    mag = jnp.exp(lr * dt)
    ar = mag * jnp.cos(li * dt)
    ai = mag * jnp.sin(li * dt)
    den = lr * lr + li * li
    fr = ((ar - 1.0) * lr + ai * li) / den
    fi = (ai * lr - (ar - 1.0) * li) / den
    bt_re, bt_im = b_re.transpose(0, 2, 1), b_im.transpose(0, 2, 1)
    bbr = fr[:, None, :] * bt_re - fi[:, None, :] * bt_im
    bbi = fr[:, None, :] * bt_im + fi[:, None, :] * bt_re
    j = jnp.arange(L + 1, dtype=F32)[:, None, None]
    pmag = jnp.exp(j * (lr * dt))
    pr = pmag * jnp.cos(j * (li * dt))
    pi = pmag * jnp.sin(j * (li * dt))
    abr = pr[:, :, None, :] * bbr - pi[:, :, None, :] * bbi
    abi = pr[:, :, None, :] * bbi + pi[:, :, None, :] * bbr
    kd = (jnp.einsum('jgkp,gcp->jgkc', abr, c_re, precision=hp)
          - jnp.einsum('jgkp,gcp->jgkc', abi, c_im, precision=hp))
    kd = kd.at[0].add(d_skip.reshape(ng, 1, c) * jnp.eye(c, dtype=F32))
    kc = kd[:L].reshape(L, ns, gs * c, c)
    abc = jnp.concatenate([abr[:L], abi[:L]], axis=-1).reshape(L, ns, gs * c, 2 * p)
    cr = c_re[None] * pr[1:, :, None, :] - c_im[None] * pi[1:, :, None, :]
    ci = -(c_re[None] * pi[1:, :, None, :] + c_im[None] * pr[1:, :, None, :])
    co = jnp.stack([cr, ci], axis=1).reshape(L, 2, ns, gs, c, p)
    al = jnp.stack([pr[L].reshape(-1), pi[L].reshape(-1)])
    return kc.astype(BF16), abc.astype(BF16), co.astype(BF16), al


def _s5_expanders():
    gs, c, p = S5_SLAB, SSM_GROUP, SSM_STATE
    lane = np.arange(gs * c)
    rep_c = (np.arange(c)[:, None] == lane[None, :] % c)
    same_g = (lane[:, None] // c == lane[None, :] // c)
    st = np.arange(2 * gs * p)
    rp = np.arange(2 * p)
    rep_p = (rp[:, None] // p == st[None, :] // (gs * p)) & (rp[:, None] % p == st[None, :] % p)
    g_in = (lane[:, None] // c == (st[None, :] // p) % gs)
    as_bf = lambda a: jnp.asarray(a.astype(np.float32), dtype=BF16)
    return as_bf(rep_c), as_bf(same_g), as_bf(rep_p), as_bf(g_in)


def _chunk_lanes(u_ref):
    return jnp.concatenate([u_ref[0, :, t, :] for t in range(S5_L)], axis=1)


def _s5_local_kernel(u_ref, abc_ref, rep_p_ref, g_in_ref, pre_ref, pim_ref, wp_ref):
    for s in range(S5_L):
        blk = _dot(abc_ref[S5_L - 1 - s, 0], rep_p_ref[...]).astype(BF16) * g_in_ref[...]
        wp_ref[128 * s:128 * (s + 1), :] = blk
    r = _dot(_chunk_lanes(u_ref).astype(BF16), wp_ref[...])
    half = r.shape[1] // 2
    pre_ref[...] = r[:, :half]
    pim_ref[...] = r[:, half:]


def _s5_scan_kernel(pre_ref, pim_ref, al_ref, h0r_ref, h0i_ref, hsr_ref, hsi_ref, hfr_ref, hfi_ref):
    nchunk = pre_ref.shape[1]
    ar = al_ref[0:1, :]
    ai = al_ref[1:2, :]

    def body(k, carry):
        cr, ci = carry
        hsr_ref[:, pl.ds(k, 1), :] = cr[:, None, :]
        hsi_ref[:, pl.ds(k, 1), :] = ci[:, None, :]
        xr = pre_ref[:, pl.ds(k, 1), :][:, 0, :]
        xi = pim_ref[:, pl.ds(k, 1), :][:, 0, :]
        return ar * cr - ai * ci + xr, ar * ci + ai * cr + xi

    cr, ci = lax.fori_loop(0, nchunk, body, (h0r_ref[...], h0i_ref[...]), unroll=min(nchunk, 4))
    hfr_ref[...] = cr
    hfi_ref[...] = ci


def _s5_out_kernel(u_ref, kc_ref, co_ref, rep_c_ref, same_g_ref, hsr_ref, hsi_ref, y_ref, wt_ref, wot_ref):
    rep_c = rep_c_ref[...]
    lag = [_dot(kc_ref[dl, 0], rep_c).astype(BF16) * same_g_ref[...] for dl in range(S5_L)]
    zero = jnp.zeros((128, 128), BF16)
    for s in range(S5_L):
        for t in range(S5_L):
            wt_ref[128 * s:128 * (s + 1), 128 * t:128 * (t + 1)] = lag[t - s] if t >= s else zero
    wot_ref[...] = jnp.zeros_like(wot_ref)
    c, p = SSM_GROUP, SSM_STATE
    for t in range(S5_L):
        for r in range(2):
            for g in range(S5_SLAB):
                col = (r * S5_SLAB + g) * p
                wot_ref[128 * t + c * g:128 * t + c * (g + 1), col:col + p] = co_ref[t, r, 0, g]
    hs = jnp.concatenate([hsr_ref[...], hsi_ref[...]], axis=1).astype(BF16)
    y = _dot(_chunk_lanes(u_ref).astype(BF16), wt_ref[...]) + _dot_nt(hs, wot_ref[...])
    for t in range(S5_L):
        y_ref[0, :, t, :] = y[:, 128 * t:128 * (t + 1)]


def _s5(u3, h0r, h0i, weights, nbatch):
    kc, abc, co, al = weights
    rep_c, same_g, rep_p, g_in = _s5_expanders()
    ns, nb = u3.shape[0], u3.shape[1]
    nchunk = nb // nbatch
    gp = al.shape[1]
    lb = gp // ns
    d = S5_L * 128
    cp = pltpu.CompilerParams(dimension_semantics=("parallel",), vmem_limit_bytes=VMEM_LIMIT)
    u_spec = pl.BlockSpec((1, nb, S5_L, 128), lambda j: (j, 0, 0, 0))
    st_spec = pl.BlockSpec((nb, lb), lambda j: (0, j))
    slab = lambda a: pl.BlockSpec((a.shape[0], 1) + a.shape[2:], lambda j: (0, j) + (0,) * (a.ndim - 2))
    const = lambda a: pl.BlockSpec(a.shape, lambda j: (0,) * a.ndim)
    pre, pim = pl.pallas_call(
        _s5_local_kernel,
        grid=(ns,),
        in_specs=[u_spec, slab(abc), const(rep_p), const(g_in)],
        out_specs=[st_spec] * 2,
        out_shape=[jax.ShapeDtypeStruct((nb, gp), F32)] * 2,
        scratch_shapes=[pltpu.VMEM((d, 2 * lb), BF16)],
        compiler_params=cp,
        name="s5_local",
    )(u3, abc, rep_p, g_in)

    ls = 2 * lb
    seq3 = pl.BlockSpec((nbatch, nchunk, ls), lambda j: (0, 0, j))
    row = lambda r: pl.BlockSpec((r, ls), lambda j: (0, j))
    hsr, hsi, hfr, hfi = pl.pallas_call(
        _s5_scan_kernel,
        grid=(gp // ls,),
        in_specs=[seq3, seq3, row(2), row(nbatch), row(nbatch)],
        out_specs=[seq3, seq3, row(nbatch), row(nbatch)],
        out_shape=[jax.ShapeDtypeStruct((nbatch, nchunk, gp), F32)] * 2 + [jax.ShapeDtypeStruct((nbatch, gp), F32)] * 2,
        compiler_params=cp,
        name="s5_scan",
    )(pre.reshape(nbatch, nchunk, gp), pim.reshape(nbatch, nchunk, gp), al, h0r, h0i)

    y3 = pl.pallas_call(
        _s5_out_kernel,
        grid=(ns,),
        in_specs=[u_spec, slab(kc),
                  pl.BlockSpec(co.shape[:2] + (1,) + co.shape[3:], lambda j: (0, 0, j, 0, 0, 0)),
                  const(rep_c), const(same_g), st_spec, st_spec],
        out_specs=u_spec,
        out_shape=jax.ShapeDtypeStruct(u3.shape, F32),
        scratch_shapes=[pltpu.VMEM((d, d), BF16), pltpu.VMEM((2 * lb, d), BF16)],
        compiler_params=cp,
        name="s5_out",
    )(u3, kc, co, rep_c, same_g, hsr.reshape(nb, gp), hsi.reshape(nb, gp))
    return y3, hfr, hfi


def _merge_kernel(o_ref, y_ref, wa_ref, wg1_ref, wg2_ref, sga_ref, sgb_ref, m_ref, gy_ref):
    @pl.when(pl.program_id(1) == 0)
    def _():
        y = jnp.concatenate([y_ref[s] for s in range(y_ref.shape[0])], axis=1)
        gy_ref[...] = _gelu(y).astype(BF16)

    ya = _dot(o_ref[...], wa_ref[...])
    gy = gy_ref[...]
    yb = _dot(gy, wg1_ref[...]) * jax.nn.sigmoid(_dot(gy, wg2_ref[...]))
    m = sga_ref[...].astype(F32) * ya + sgb_ref[...].astype(F32) * yb
    m_ref[...] = m.astype(BF16)


def _merge(o, y3, w_attn_out, w_glu, sg, tm, tn=2048):
    rows, d = o.shape[0], w_attn_out.shape[1]
    nj = d // tn
    kq, ks = w_attn_out.shape[0], w_glu.shape[0]
    ns = y3.shape[0]
    resident = pl.Buffered(1) if nj == 1 else None
    return pl.pallas_call(
        _merge_kernel,
        grid=(rows // tm, nj),
        in_specs=[
            pl.BlockSpec((tm, kq), lambda i, j: (i, 0)),
            pl.BlockSpec((ns, tm, 128), lambda i, j: (0, i, 0)),
            pl.BlockSpec((kq, tn), lambda i, j: (0, j), pipeline_mode=resident),
            pl.BlockSpec((ks, tn), lambda i, j: (0, j), pipeline_mode=resident),
            pl.BlockSpec((ks, tn), lambda i, j: (0, j + nj), pipeline_mode=resident),
            pl.BlockSpec((tm, tn), lambda i, j: (i, j)),
            pl.BlockSpec((tm, tn), lambda i, j: (i, j + nj)),
        ],
        out_specs=pl.BlockSpec((tm, tn), lambda i, j: (i, j)),
        out_shape=jax.ShapeDtypeStruct((rows, d), BF16),
        scratch_shapes=[pltpu.VMEM((tm, ks), BF16)],
        compiler_params=pltpu.CompilerParams(
            dimension_semantics=("parallel", "arbitrary"), vmem_limit_bytes=VMEM_LIMIT),
        name="merge",
    )(o, y3, w_attn_out, w_glu, w_glu, sg, sg)


def _outproj_kernel(m_ref, x_ref, w_ref, g_ref, x1_ref, h_ref):
    x1 = x_ref[...] + _dot(m_ref[...], w_ref[...])
    x1_ref[...] = x1
    r = lax.rsqrt(jnp.mean(x1 * x1, axis=-1, keepdims=True) + EPS)
    h_ref[...] = (x1 * r * g_ref[...]).astype(BF16)


def _outproj(m, x, w_out, g_ffn, tm):
    rows, d = x.shape
    row = pl.BlockSpec((tm, d), lambda i: (i, 0))
    return pl.pallas_call(
        _outproj_kernel,
        grid=(rows // tm,),
        in_specs=[row, row, pl.BlockSpec((d, d), lambda i: (0, 0), pipeline_mode=pl.Buffered(1)),
                  pl.BlockSpec((1, d), lambda i: (0, 0))],
        out_specs=[row, row],
        out_shape=[jax.ShapeDtypeStruct((rows, d), F32), jax.ShapeDtypeStruct((rows, d), BF16)],
        compiler_params=pltpu.CompilerParams(dimension_semantics=("parallel",), vmem_limit_bytes=VMEM_LIMIT),
        name="out_proj",
    )(m, x, w_out, g_ffn.reshape(1, d))


def _ffn_kernel(h_ref, x1_ref, wg_ref, wu_ref, wd_ref, gf_ref, y_ref, *rest):
    acc_ref = rest[-1]
    f = pl.program_id(1)

    @pl.when(f == 0)
    def _():
        acc_ref[...] = jnp.zeros_like(acc_ref)

    wg, wu, wd = wg_ref[...].astype(BF16), wu_ref[...].astype(BF16), wd_ref[...].astype(BF16)
    for copy_ref, w in zip(rest[:-1], (wg, wu, wd)):
        copy_ref[...] = w
    h = h_ref[...]
    a = _dot(h, wg)
    a = (a * jax.nn.sigmoid(a)) * _dot(h, wu)
    acc_ref[...] += _dot(a.astype(BF16), wd)

    @pl.when(f == pl.num_programs(1) - 1)
    def _():
        y = x1_ref[...] + acc_ref[...]
        r = lax.rsqrt(jnp.mean(y * y, axis=-1, keepdims=True) + EPS)
        y_ref[...] = y * r * gf_ref[...]


def _ffn(h, x1, w_gate, w_up, w_down, g_final, tm, tf=512):
    rows, d = x1.shape
    dff = w_gate.shape[1]
    emit = w_gate.dtype != BF16
    assert not emit or rows == tm
    row = pl.BlockSpec((tm, d), lambda i, f: (i, 0))
    w_specs = [pl.BlockSpec((d, tf), lambda i, f: (0, f)),
               pl.BlockSpec((d, tf), lambda i, f: (0, f)),
               pl.BlockSpec((tf, d), lambda i, f: (f, 0))]
    y_shape = jax.ShapeDtypeStruct((rows, d), F32)
    copies = [jax.ShapeDtypeStruct(w.shape, BF16) for w in (w_gate, w_up, w_down)] if emit else []
    out = pl.pallas_call(
        _ffn_kernel,
        grid=(rows // tm, dff // tf),
        in_specs=[row, row] + w_specs + [pl.BlockSpec((1, d), lambda i, f: (0, 0))],
        out_specs=[row] + (w_specs if emit else []),
        out_shape=[y_shape] + copies,
        scratch_shapes=[pltpu.VMEM((tm, d), F32)],
        compiler_params=pltpu.CompilerParams(
            dimension_semantics=("parallel", "arbitrary"), vmem_limit_bytes=VMEM_LIMIT),
        name="ffn",
    )(h, x1, w_gate, w_up, w_down, g_final.reshape(1, d))
    return tuple(out) if emit else out[0]


def _alibi_slopes():
    return jnp.exp2(-8.0 * jnp.arange(1, N_HEADS + 1, dtype=F32) / N_HEADS) * LOG2E


def _slope_table(slopes):
    s1 = slopes.astype(BF16)
    r1 = slopes - s1.astype(F32)
    s2 = r1.astype(BF16)
    s3 = (r1 - s2.astype(F32)).astype(BF16)
    tab = jnp.stack([s1, s1, s2, s2, s3, s3] + [jnp.zeros_like(s1)] * 10, axis=-1)
    return tab.reshape(N_KV, GROUP, 16)


def _position_rows(pos):
    pos = np.arange(pos) if np.isscalar(pos) else np.asarray(pos)
    hi, lo = (pos // 64) * 64, pos % 64
    assert pos.max() < 64 * 256
    rows = np.stack([hi, lo, hi, lo, hi, lo] + [np.zeros_like(pos)] * 10).astype(np.float32)
    return jnp.asarray(rows, dtype=BF16)


def _layer_weights(p, l):
    w_in = _pack_w_in(p['w_in'][l], p['ssm_d'].shape[-1])
    s5 = _s5_weights(*(p[n][l] for n in ('ssm_lam_re', 'ssm_lam_im', 'ssm_log_dt', 'ssm_b_re', 'ssm_b_im',
                                          'ssm_c_re', 'ssm_c_im', 'ssm_d')))
    cast = lambda n: p[n][l].astype(BF16)
    return dict(
        g_mix=p['g_mix'][l], w_in=w_in,
        pool=p['w_cmp_pool'][l], pe=p['w_cmp_pe'][l], phi=p['w_cmp_phi'][l],
        pool_mats=_pool_matrices(p['w_cmp_pool'][l], PROMPT_PAGES_PER_STEP),
        w_attn_out=cast('w_attn_out'), w_glu=cast('w_glu'), w_out=cast('w_out'), g_ffn=p['g_ffn'][l],
        ffn_w=tuple(p[n][l] for n in ('w_gate', 'w_up', 'w_down')), s5=s5)


def _tail(x, o, y3, sg, lw, g_final, tm):
    rows = x.shape[0]
    m = _merge(o, y3.reshape(y3.shape[0], rows, 128), lw['w_attn_out'], lw['w_glu'], sg, tm)
    x1, h = _outproj(m, x, lw['w_out'], lw['g_ffn'], tm)
    return _ffn(h, x1, *lw['ffn_w'], g_final, tm)


def _feature_major_rows(kv_t, t0):
    b, f, t = kv_t.shape
    return kv_t[:, :, t0:].reshape(b, f // KV_COLS, N_KV, HEAD_DIM, t - t0).transpose(0, 4, 1, 2, 3)


def _layer_prompt(x, lw, slopes, g_out):
    b, t, d = x.shape
    rows = b * t
    assert t % (PROMPT_PAGES_PER_STEP * PAGE) == 0
    q, u3, sg, gt, kvc_t, kvw_t = _in_proj(x.reshape(rows, d), lw['g_mix'], lw['w_in'], 1024, seq_len=t)

    pooled = _pool_pages(kvc_t, lambda bi, pg: (bi, 0, pg), (), b, t // PAGE, lw['pool_mats'],
                         PROMPT_PAGES_PER_STEP)
    _, vc_t, kc_rows = _cmp_finish(pooled, lw['pool'], lw['pe'], lw['phi'])
    o = _attn_prompt(slopes, q, kc_rows, vc_t, kvc_t, kvw_t, gt, b, t)

    gp = lw['s5'][3].shape[1]
    h0 = jnp.zeros((b, gp), F32)
    y3, hr, hi = _s5(u3, h0, h0, lw['s5'], b)

    y = _tail(x.reshape(rows, d), o, y3, sg, lw, g_out, 512).reshape(b, t, d)
    n_win = min(WINDOW, t)
    ng = gp // SSM_STATE
    return (y, _feature_major_rows(kvc_t, 0), _feature_major_rows(kvw_t, t - n_win),
            hr.reshape(b, ng, SSM_STATE), hi.reshape(b, ng, SSM_STATE))


def _layer_sample(x, cache, page_table, win_buf, h_re, h_im, lw, slopes, g_out):
    b, t, d = x.shape
    rows = b * t
    npages = page_table.shape[1]
    past = npages * cache.shape[1]
    n_buf = win_buf.shape[1]
    assert cache.shape[1] == PAGE and rows % S5_L == 0 and t == S5_L
    assert past % CMP_STRIDE == 0 and t < CMP_STRIDE and past % SEL_BLOCK == 0 and t <= SEL_BLOCK
    npg = DECODE_PAGES_PER_STEP
    assert npages % npg == 0
    q, u3, sg, gt, kvc, kvw = _in_proj(x.reshape(rows, d), lw['g_mix'], lw['w_in'], rows)

    cache_t = cache.transpose(0, 2, 3, 4, 1).reshape(cache.shape[0], 4 * KV_COLS, PAGE)
    win_t = win_buf.transpose(0, 2, 3, 4, 1).reshape(b, 2 * KV_COLS, n_buf)
    pooled = _pool_pages(cache_t, lambda bi, pg, pt: (pt[bi, pg], 0, 0), (page_table,), b, npages,
                         lw['pool_mats'], npg)
    kc_t, vc_t, _ = _cmp_finish(pooled, lw['pool'], lw['pe'], lw['phi'])

    nrows = N_HEADS * t
    eye = jnp.eye(N_KV, dtype=BF16)
    q5 = q.reshape(b, t, N_KV, GROUP, HEAD_DIM).transpose(0, 2, 3, 1, 4)
    qbd = (q5[:, :, :, :, None, :] * eye[None, :, None, None, :, None]).reshape(b, nrows, KV_COLS)
    maskbd = jnp.repeat(jnp.repeat(jnp.eye(N_KV, dtype=F32), GROUP * t, axis=0), HEAD_DIM, axis=1)
    slopecol = jnp.repeat(slopes, t).reshape(nrows, 1)
    g3 = gt.reshape(b, t, N_KV, 128)[..., :3 * GROUP].reshape(b, t, N_KV, 3, GROUP)
    g3 = g3.transpose(0, 2, 4, 1, 3).reshape(b, nrows, 3)
    gates_rows = jnp.pad(g3, ((0, 0), (0, 0), (0, 1)))
    n_blk = -(-(past + t) // SEL_BLOCK)

    opart, sel_t = _attn_dec1(qbd, kc_t, vc_t, win_t, kvw.reshape(b, t, 2 * KV_COLS), gates_rows, slopecol, maskbd,
                              past, t, n_blk)
    bps = npg * PAGE // SEL_BLOCK
    nstep = npages // npg
    nb_all = (nstep + 1) * bps
    sel = jnp.pad(sel_t[:, :n_blk], ((0, 0), (0, nb_all - n_blk), (0, 0)))
    sel = sel.reshape(b, nstep + 1, bps, N_KV, 1, t).transpose(0, 1, 3, 4, 5, 2)
    selrows = jnp.broadcast_to(sel, (b, nstep + 1, N_KV, GROUP, t, bps)).reshape(b, nstep + 1, nrows, bps)
    knew = jnp.pad(kvc.reshape(b, t, 4 * KV_COLS)[:, :, 2 * KV_COLS:], ((0, 0), (0, PAGE - t), (0, 0)))
    o_rows = _attn_dec2(cache_t, page_table, qbd, selrows.astype(BF16), slopecol, knew, opart, gates_rows,
                        maskbd, past, t, npg)
    o = (o_rows.reshape(b, N_KV, GROUP, t, HEAD_DIM).transpose(0, 3, 1, 2, 4).reshape(rows, Q_COLS).astype(BF16))

    gp = h_re.shape[1] * h_re.shape[2]
    y3, hr, hi = _s5(u3, h_re.reshape(b, gp), h_im.reshape(b, gp), lw['s5'], b)

    y, *ffn_w_bf16 = _tail(x.reshape(rows, d), o, y3, sg, lw, g_out, rows)
    y = y.reshape(b, t, d)
    win_all = jnp.concatenate([win_buf, kvw.reshape(b, t, 2, N_KV, HEAD_DIM)], axis=1)
    n_keep = min(WINDOW, n_buf + t)
    return (y, kvc.reshape(b, t, 4, N_KV, HEAD_DIM), win_all[:, n_buf + t - n_keep:], hr.reshape(h_re.shape),
            hi.reshape(h_im.shape), tuple(ffn_w_bf16))


def kernel(x_prompt, x_sample, cache_kv, state_win, state_ssm_re, state_ssm_im, page_table, g_mix, w_in, w_cmp_pe, w_cmp_pool, w_cmp_phi, w_attn_out, ssm_lam_re, ssm_lam_im, ssm_log_dt, ssm_b_re, ssm_b_im, ssm_c_re, ssm_c_im, ssm_d, w_glu, w_out, g_ffn, w_gate, w_up, w_down, g_final):
    depth = g_mix.shape[0]
    assert depth == 1, "final norm is fused into the last layer's FFN; one layer supported"
    params = dict(g_mix=g_mix, w_in=w_in, w_cmp_pe=w_cmp_pe, w_cmp_pool=w_cmp_pool, w_cmp_phi=w_cmp_phi,
                  w_attn_out=w_attn_out, ssm_lam_re=ssm_lam_re, ssm_lam_im=ssm_lam_im, ssm_log_dt=ssm_log_dt,
                  ssm_b_re=ssm_b_re, ssm_b_im=ssm_b_im, ssm_c_re=ssm_c_re, ssm_c_im=ssm_c_im, ssm_d=ssm_d,
                  w_glu=w_glu, w_out=w_out, g_ffn=g_ffn, w_gate=w_gate, w_up=w_up, w_down=w_down)
    slopes = _alibi_slopes()
    outs = [[] for _ in range(8)]
    xp, xs = x_prompt, x_sample
    for l in range(depth):
        lw = _layer_weights(params, l)
        xs, kvs, wins, hrs, his, ffn_w = _layer_sample(xs, cache_kv[l], page_table, state_win[l], state_ssm_re[l],
                                                       state_ssm_im[l], lw, slopes, g_final)
        xp, kvp, winp, hrp, hip = _layer_prompt(xp, dict(lw, ffn_w=ffn_w), slopes, g_final)
        for lst, v in zip(outs, (kvp, winp, hrp, hip, kvs, wins, hrs, his)):
            lst.append(v)
    st = [jnp.stack(v) for v in outs]
    return (xp, xs, st[0], st[1], st[2], st[3], st[4], st[5], st[6], st[7])
```
